```python
import jax, jax.numpy as jnp
from jax import lax
import numpy as np

D_MODEL = 1024
BATCH = 8
SEQ = 2048
DEPTH = 1
DEC_BATCH = 128
DEC_SEQ = 4
PAST_LEN = 8192
PAGE_SIZE = 128

POOL_WINDOWS = (2, 4, 8, 16)
N_POOL_GROUPS = len(POOL_WINDOWS)
POOL_GROUP = D_MODEL // 8
POOL_WIDTH = N_POOL_GROUPS * POOL_GROUP
POOL_BUF = max(POOL_WINDOWS) - 1
N_HEADS = 8
N_KV_HEADS = 2
HEAD_DIM = 64
GROUP = N_HEADS // N_KV_HEADS
Q_WIDTH = N_HEADS * HEAD_DIM
KV_WIDTH = N_KV_HEADS * HEAD_DIM
WINDOW = 128
ROPE_THETA = 10000.0
N_BRANCH = 2
IN_WIDTH = POOL_WIDTH + Q_WIDTH + 2 * KV_WIDTH + N_BRANCH * D_MODEL
D_FF = ((8 * D_MODEL // 3 + 127) // 128) * 128
ALPHA = (2.0 * DEPTH) ** 0.25
BETA = (8.0 * DEPTH) ** -0.25
LN_EPS = 1e-5
NEG_INF = -1e30

kernel_name = "hybrid_pool_swa_macaron_deepnorm_step"


def layer_norm(x, g, b):
    xf = x.astype(jnp.float32)
    mu = jnp.mean(xf, axis=-1, keepdims=True)
    var = jnp.mean(jnp.square(xf - mu), axis=-1, keepdims=True)
    y = (xf - mu) * lax.rsqrt(var + LN_EPS) * g.astype(jnp.float32) + b.astype(jnp.float32)
    return y.astype(x.dtype)


def swiglu(x, w1, w3, w2):
    return (jax.nn.silu(x @ w1) * (x @ w3)) @ w2


def ffn_half_step(x, w1, w3, w2, g, b):
    return layer_norm(ALPHA * x + 0.5 * swiglu(x, w1, w3, w2), g, b)


def rope(x, pos):
    half = HEAD_DIM // 2
    freqs = ROPE_THETA ** (-2.0 * jnp.arange(half, dtype=jnp.float32) / HEAD_DIM)
    ang = pos.astype(jnp.float32)[:, None] * freqs[None, :]
    cos = jnp.cos(ang)[:, None, :]
    sin = jnp.sin(ang)[:, None, :]
    xf = x.astype(jnp.float32)
    x1, x2 = xf[..., :half], xf[..., half:]
    return jnp.concatenate([x1 * cos - x2 * sin, x1 * sin + x2 * cos], axis=-1).astype(x.dtype)


def split_in(h, w_in):
    B, T, _ = h.shape
    z = h @ w_in
    o = 0
    u = z[..., o:o + POOL_WIDTH]; o += POOL_WIDTH
    q = z[..., o:o + Q_WIDTH].reshape(B, T, N_HEADS, HEAD_DIM); o += Q_WIDTH
    k = z[..., o:o + KV_WIDTH].reshape(B, T, N_KV_HEADS, HEAD_DIM); o += KV_WIDTH
    v = z[..., o:o + KV_WIDTH].reshape(B, T, N_KV_HEADS, HEAD_DIM); o += KV_WIDTH
    gates = z[..., o:].reshape(B, T, N_BRANCH, D_MODEL)
    return u, q, k, v, gates


def multiscale_pool(ctx, u, start_pos, w_grp, scale):
    B, T, P = u.shape
    C = ctx.shape[1]
    maxw = max(POOL_WINDOWS)
    full = jnp.concatenate([jnp.zeros((B, maxw, P), jnp.float32),
                            ctx.astype(jnp.float32), u.astype(jnp.float32)], axis=1)
    cs = jnp.cumsum(full, axis=1)
    lo = maxw + C
    cur = cs[:, lo:lo + T]
    pos = start_pos + jnp.arange(T, dtype=jnp.int32)
    uf = u.astype(jnp.float32)
    groups = []
    for g, w in enumerate(POOL_WINDOWS):
        sl = slice(g * POOL_GROUP, (g + 1) * POOL_GROUP)
        win_sum = cur[..., sl] - cs[:, lo - w:lo - w + T, sl]
        cnt = jnp.minimum(pos + 1, w).astype(jnp.float32)[None, :, None]
        groups.append(win_sum / cnt - uf[..., sl])
    z = jnp.stack(groups, axis=2).astype(u.dtype)
    z = jnp.einsum('btgc,gcd->btgd', z, w_grp).reshape(B, T, P)
    return z * scale


def sink_attention(qb, kb, vb, mask, sinks):
    s = jnp.einsum('bnqkgd,bnjkd->bnkgqj', qb, kb,
                   preferred_element_type=jnp.float32) * (HEAD_DIM ** -0.5)
    s = jnp.where(mask[None, :, None, None], s, NEG_INF)
    sink = sinks.astype(jnp.float32)[None, None, :, :, None]
    m = jnp.maximum(jnp.max(s, axis=-1), sink)
    p = jnp.exp(s - m[..., None])
    denom = jnp.sum(p, axis=-1) + jnp.exp(sink - m)
    probs = (p / denom[..., None]).astype(vb.dtype)
    return jnp.einsum('bnkgqj,bnjkd->bnqkgd', probs, vb)


def merge_out(pool_z, attn_o, gates, w_pool_out, w_attn_out, w_out):
    g = jax.nn.sigmoid(gates.astype(jnp.float32)).astype(pool_z.dtype)
    m = g[..., 0, :] * (pool_z @ w_pool_out) + g[..., 1, :] * (attn_o @ w_attn_out)
    return m @ w_out


def mixer_prompt(h, w_in, w_grp, pool_scale, sinks, w_pool_out, w_attn_out, w_out):
    B, S, _ = h.shape
    u, q, k, v, gates = split_in(h, w_in)
    pos = jnp.arange(S, dtype=jnp.int32)
    q = rope(q, pos)
    k = rope(k, pos)
    pool_z = multiscale_pool(jnp.zeros((B, 0, POOL_WIDTH), u.dtype), u, 0, w_grp, pool_scale)
    nb = S // WINDOW
    qb = q.reshape(B, nb, WINDOW, N_KV_HEADS, GROUP, HEAD_DIM)
    kb = k.reshape(B, nb, WINDOW, N_KV_HEADS, HEAD_DIM)
    vb = v.reshape(B, nb, WINDOW, N_KV_HEADS, HEAD_DIM)
    kk = jnp.concatenate([jnp.concatenate([jnp.zeros_like(kb[:, :1]), kb[:, :-1]], axis=1), kb], axis=2)
    vv = jnp.concatenate([jnp.concatenate([jnp.zeros_like(vb[:, :1]), vb[:, :-1]], axis=1), vb], axis=2)
    blk = jnp.arange(nb, dtype=jnp.int32)[:, None] * WINDOW
    qpos = blk + jnp.arange(WINDOW, dtype=jnp.int32)[None, :]
    kpos = blk - WINDOW + jnp.arange(2 * WINDOW, dtype=jnp.int32)[None, :]
    diff = qpos[:, :, None] - kpos[:, None, :]
    mask = (diff >= 0) & (diff <= WINDOW) & (kpos[:, None, :] >= 0)
    attn = sink_attention(qb, kk, vv, mask, sinks.reshape(N_KV_HEADS, GROUP))
    attn_o = attn.reshape(B, S, Q_WIDTH)
    m = merge_out(pool_z, attn_o, gates, w_pool_out, w_attn_out, w_out)
    wb = min(WINDOW, S)
    return m, u[:, S - POOL_BUF:], k[:, S - wb:], v[:, S - wb:]


def mixer_sample(h, pool_ctx, k_ctx, v_ctx, w_in, w_grp, pool_scale, sinks,
                 w_pool_out, w_attn_out, w_out):
    Bd, T, _ = h.shape
    wb = k_ctx.shape[1]
    u, q, k, v, gates = split_in(h, w_in)
    pos = PAST_LEN + jnp.arange(T, dtype=jnp.int32)
    q = rope(q, pos)
    k = rope(k, pos)
    pool_z = multiscale_pool(pool_ctx, u, PAST_LEN, w_grp, pool_scale)
    k_all = jnp.concatenate([k_ctx, k], axis=1)
    v_all = jnp.concatenate([v_ctx, v], axis=1)
    kpos = PAST_LEN - wb + jnp.arange(wb + T, dtype=jnp.int32)
    diff = pos[:, None] - kpos[None, :]
    mask = ((diff >= 0) & (diff <= WINDOW) & (kpos[None, :] >= 0))[None]
    qb = q.reshape(Bd, 1, T, N_KV_HEADS, GROUP, HEAD_DIM)
    attn = sink_attention(qb, k_all[:, None], v_all[:, None], mask, sinks.reshape(N_KV_HEADS, GROUP))
    attn_o = attn.reshape(Bd, T, Q_WIDTH)
    m = merge_out(pool_z, attn_o, gates, w_pool_out, w_attn_out, w_out)
    u_all = jnp.concatenate([pool_ctx, u], axis=1)
    return m, u_all[:, u_all.shape[1] - POOL_BUF:], k_all[:, T:], v_all[:, T:]


def setup_inputs(seed: int = 0) -> dict:
    key = jax.random.key(seed)
    ks = jax.random.split(key, 32)
    f32 = jnp.float32
    wb = min(WINDOW, PAST_LEN)

    def nrm(k, shape, scale):
        return jax.random.normal(k, shape, f32) * scale

    L = DEPTH
    return {
        "x_prompt": nrm(ks[0], (BATCH, SEQ, D_MODEL), 1.0),
        "x_sample": nrm(ks[1], (DEC_BATCH, DEC_SEQ, D_MODEL), 1.0),
        "cache_pool_u": nrm(ks[2], (L, DEC_BATCH, POOL_BUF, POOL_WIDTH), 1.0),
        "cache_k_win": nrm(ks[3], (L, DEC_BATCH, wb, N_KV_HEADS, HEAD_DIM), 1.0),
        "cache_v_win": nrm(ks[4], (L, DEC_BATCH, wb, N_KV_HEADS, HEAD_DIM), 1.0),
        "w_in": nrm(ks[5], (L, D_MODEL, IN_WIDTH), D_MODEL ** -0.5),
        "pool_w_grp": nrm(ks[6], (L, N_POOL_GROUPS, POOL_GROUP, POOL_GROUP), POOL_GROUP ** -0.5),
        "pool_scale": 1.0 + nrm(ks[7], (L, POOL_WIDTH), 0.1),
        "attn_sinks": nrm(ks[8], (L, N_HEADS), 0.5),
        "w_pool_out": nrm(ks[9], (L, POOL_WIDTH, D_MODEL), POOL_WIDTH ** -0.5),
        "w_attn_out": nrm(ks[10], (L, Q_WIDTH, D_MODEL), Q_WIDTH ** -0.5),
        "w_out": nrm(ks[11], (L, D_MODEL, D_MODEL), BETA * D_MODEL ** -0.5),
        "ffn1_w1": nrm(ks[12], (L, D_MODEL, D_FF), D_MODEL ** -0.5),
        "ffn1_w3": nrm(ks[13], (L, D_MODEL, D_FF), D_MODEL ** -0.5),
        "ffn1_w2": nrm(ks[14], (L, D_FF, D_MODEL), BETA * D_FF ** -0.5),
        "ffn2_w1": nrm(ks[15], (L, D_MODEL, D_FF), D_MODEL ** -0.5),
        "ffn2_w3": nrm(ks[16], (L, D_MODEL, D_FF), D_MODEL ** -0.5),
        "ffn2_w2": nrm(ks[17], (L, D_FF, D_MODEL), BETA * D_FF ** -0.5),
        "ln1_g": 1.0 + nrm(ks[18], (L, D_MODEL), 0.02),
        "ln1_b": nrm(ks[19], (L, D_MODEL), 0.02),
        "ln2_g": 1.0 + nrm(ks[20], (L, D_MODEL), 0.02),
        "ln2_b": nrm(ks[21], (L, D_MODEL), 0.02),
        "ln3_g": 1.0 + nrm(ks[22], (L, D_MODEL), 0.02),
        "ln3_b": nrm(ks[23], (L, D_MODEL), 0.02),
    }


def reference(x_prompt, x_sample, cache_pool_u, cache_k_win, cache_v_win,
              w_in, pool_w_grp, pool_scale, attn_sinks, w_pool_out, w_attn_out, w_out,
              ffn1_w1, ffn1_w3, ffn1_w2, ffn2_w1, ffn2_w3, ffn2_w2,
              ln1_g, ln1_b, ln2_g, ln2_b, ln3_g, ln3_b):
    xp, xs = x_prompt, x_sample
    pu_l, pk_l, pv_l, su_l, sk_l, sv_l = [], [], [], [], [], []
    for l in range(DEPTH):
        hp = ffn_half_step(xp, ffn1_w1[l], ffn1_w3[l], ffn1_w2[l], ln1_g[l], ln1_b[l])
        hs = ffn_half_step(xs, ffn1_w1[l], ffn1_w3[l], ffn1_w2[l], ln1_g[l], ln1_b[l])
        mp, pu, pk, pv = mixer_prompt(hp, w_in[l], pool_w_grp[l], pool_scale[l], attn_sinks[l],
                                      w_pool_out[l], w_attn_out[l], w_out[l])
        ms, su, sk, sv = mixer_sample(hs, cache_pool_u[l], cache_k_win[l], cache_v_win[l],
                                      w_in[l], pool_w_grp[l], pool_scale[l], attn_sinks[l],
                                      w_pool_out[l], w_attn_out[l], w_out[l])
        hp = layer_norm(ALPHA * hp + mp, ln2_g[l], ln2_b[l])
        hs = layer_norm(ALPHA * hs + ms, ln2_g[l], ln2_b[l])
        xp = ffn_half_step(hp, ffn2_w1[l], ffn2_w3[l], ffn2_w2[l], ln3_g[l], ln3_b[l])
        xs = ffn_half_step(hs, ffn2_w1[l], ffn2_w3[l], ffn2_w2[l], ln3_g[l], ln3_b[l])
        pu_l.append(pu); pk_l.append(pk); pv_l.append(pv)
        su_l.append(su); sk_l.append(sk); sv_l.append(sv)
    pool_u_prompt = jnp.stack(pu_l, axis=0)
    k_win_prompt = jnp.stack(pk_l, axis=0)
    v_win_prompt = jnp.stack(pv_l, axis=0)
    pool_u_sample = jnp.stack(su_l, axis=0)
    k_win_sample = jnp.stack(sk_l, axis=0)
    v_win_sample = jnp.stack(sv_l, axis=0)
    return (xp, xs, pool_u_prompt, k_win_prompt, v_win_prompt, pool_u_sample, k_win_sample, v_win_sample)
```

```python
import functools

import jax
import jax.numpy as jnp
import numpy as np
from jax import lax
from jax.experimental import pallas as pl
from jax.experimental.pallas import tpu as pltpu

D_MODEL = 1024
BATCH = 8
SEQ = 2048
DEC_BATCH = 128
DEC_SEQ = 4
PAST_LEN = 8192
POOL_WINDOWS = (2, 4, 8, 16)
POOL_GROUP = 128
POOL_WIDTH = 512
POOL_BUF = 15
N_HEADS = 8
N_KV_HEADS = 2
HEAD_DIM = 64
Q_WIDTH = 512
KV_WIDTH = 128
WINDOW = 128
ROPE_THETA = 10000.0
D_FF = 2816
DEPTH = 1
ALPHA = (2.0 * DEPTH) ** 0.25
LN_EPS = 1e-5
NEG_INF = -1e30
UQKV_WIDTH = POOL_WIDTH + Q_WIDTH + 2 * KV_WIDTH

LANES = 128
KEY_SPAN = 2 * WINDOW
VMEM_LIMIT_BYTES = 56 * 1024 * 1024

BF16 = jnp.bfloat16
F32 = jnp.float32


def _dot(a, b):
    return jnp.dot(a, b, preferred_element_type=F32)


def _layer_norm(y, g, b):
    mu = jnp.mean(y, axis=-1, keepdims=True)
    yc = y - mu
    var = jnp.mean(yc * yc, axis=-1, keepdims=True)
    return yc * lax.rsqrt(var + LN_EPS) * g + b


def _resident(shape):
    nd = len(shape)
    return pl.BlockSpec(shape, lambda *_: (0,) * nd, pipeline_mode=pl.Buffered(1))


def _ffn_kernel(x_ref, w1_ref, w3_ref, w2_ref, g_ref, b_ref, o_ref):
    x = x_ref[...]
    xb = x.astype(BF16)
    a = _dot(xb, w1_ref[...])
    b = _dot(xb, w3_ref[...])
    h = (a * jax.nn.sigmoid(a)) * b
    y = _dot(h.astype(BF16), w2_ref[...])
    o_ref[...] = _layer_norm(ALPHA * x + 0.5 * y, g_ref[...], b_ref[...])


def _ffn_half_step(x, w1, w3, w2, g, b, *, tm):
    rows = x.shape[0]
    row_spec = pl.BlockSpec((tm, D_MODEL), lambda i: (i, 0))
    return pl.pallas_call(
        _ffn_kernel,
        out_shape=jax.ShapeDtypeStruct((rows, D_MODEL), F32),
        grid=(rows // tm,),
        in_specs=[row_spec, _resident((D_MODEL, D_FF)), _resident((D_MODEL, D_FF)),
                  _resident((D_FF, D_MODEL)), _resident((1, D_MODEL)), _resident((1, D_MODEL))],
        out_specs=row_spec,
        compiler_params=pltpu.CompilerParams(dimension_semantics=("parallel",),
                                             vmem_limit_bytes=VMEM_LIMIT_BYTES),
        name="ffn_half_step",
    )(x, w1, w3, w2, g, b)


def _rope(x, cos, sin_signed, first_half):
    fwd = pltpu.roll(x, LANES - HEAD_DIM // 2, axis=1)
    bwd = pltpu.roll(x, HEAD_DIM // 2, axis=1)
    return x * cos + jnp.where(first_half, fwd, bwd) * sin_signed


def _in_proj_kernel(h_ref, w_ref, cos_ref, sin_ref, u_ref, q_ref, k_ref, v_ref):
    z = _dot(h_ref[...].astype(BF16), w_ref[...])
    cos = cos_ref[...]
    sin_signed = sin_ref[...]
    lane = lax.broadcasted_iota(jnp.int32, cos.shape, 1)
    first_half = (lane & (HEAD_DIM // 2)) == 0
    u_ref[...] = z[:, :POOL_WIDTH]
    for c in range(Q_WIDTH // LANES):
        lo = POOL_WIDTH + c * LANES
        qc = _rope(z[:, lo:lo + LANES], cos, sin_signed, first_half)
        q_ref[:, c * LANES:(c + 1) * LANES] = (qc * (HEAD_DIM ** -0.5)).astype(BF16)
    ko = POOL_WIDTH + Q_WIDTH
    k_ref[...] = _rope(z[:, ko:ko + KV_WIDTH], cos, sin_signed, first_half)
    v_ref[...] = z[:, ko + KV_WIDTH:ko + 2 * KV_WIDTH]


def _in_proj(h, w_uqkv, cos, sin_signed, *, tm):
    rows = h.shape[0]
    table_blocks = cos.shape[0] // tm

    def row(width):
        return pl.BlockSpec((tm, width), lambda i: (i, 0))

    table_spec = pl.BlockSpec((tm, LANES), lambda i: (i % table_blocks, 0))
    return pl.pallas_call(
        _in_proj_kernel,
        out_shape=(jax.ShapeDtypeStruct((rows, POOL_WIDTH), F32),
                   jax.ShapeDtypeStruct((rows, Q_WIDTH), BF16),
                   jax.ShapeDtypeStruct((rows, KV_WIDTH), F32),
                   jax.ShapeDtypeStruct((rows, KV_WIDTH), F32)),
        grid=(rows // tm,),
        in_specs=[row(D_MODEL), _resident((D_MODEL, UQKV_WIDTH)), table_spec, table_spec],
        out_specs=(row(POOL_WIDTH), row(Q_WIDTH), row(KV_WIDTH), row(KV_WIDTH)),
        compiler_params=pltpu.CompilerParams(dimension_semantics=("parallel",),
                                             vmem_limit_bytes=VMEM_LIMIT_BYTES),
        name="in_proj",
    )(h, w_uqkv, cos, sin_signed)


def _attend_kv_head(q_pairs, keys, keys_sw, vals, vals_sw, kh, bias, sinks_ref, rows_per_pair):
    low = lax.broadcasted_iota(jnp.int32, (1, LANES), 1) < HEAD_DIM
    if kh == 0:
        k_lo, k_hi, v_lo, v_hi = keys, keys_sw, vals, vals_sw
    else:
        k_lo, k_hi, v_lo, v_hi = keys_sw, keys, vals_sw, vals
    kcat = jnp.concatenate([jnp.where(low, k_lo, 0.0), jnp.where(low, 0.0, k_hi)], axis=0).astype(BF16)
    vcat = jnp.concatenate([jnp.where(low, v_lo, 0.0), jnp.where(low, 0.0, v_hi)], axis=0).astype(BF16)
    s = lax.dot_general(q_pairs, kcat, (((1,), (1,)), ((), ())), preferred_element_type=F32) + bias
    second_pair = lax.broadcasted_iota(jnp.int32, (s.shape[0], 1), 0) >= rows_per_pair
    probs, denoms = [], []
    for c in range(2):
        sc = s[:, c * KEY_SPAN:(c + 1) * KEY_SPAN]
        head = kh * 4 + c
        sink = jnp.where(second_pair, sinks_ref[head + 2], sinks_ref[head])
        m = jnp.maximum(jnp.max(sc, axis=-1, keepdims=True), sink)
        p = jnp.exp(sc - m)
        denoms.append(jnp.sum(p, axis=-1, keepdims=True) + jnp.exp(sink - m))
        probs.append(p.astype(BF16))
    o = _dot(jnp.concatenate(probs, axis=1), vcat)
    return o / jnp.where(low, denoms[0], denoms[1])


def _swap_halves(x):
    return pltpu.roll(x, HEAD_DIM, axis=1)


def _merge_out(h1, pool_in, attn_o, wg_ref, wgrp_ref, scale_ref, wpo_ref, wao_ref, wout_ref, g_ref, b_ref):
    gates = jax.nn.sigmoid(_dot(h1.astype(BF16), wg_ref[...]))
    zs = [_dot(pool_in[:, g * POOL_GROUP:(g + 1) * POOL_GROUP].astype(BF16), wgrp_ref[g])
          for g in range(len(POOL_WINDOWS))]
    pool_z = jnp.concatenate(zs, axis=1) * scale_ref[...]
    a = _dot(pool_z.astype(BF16), wpo_ref[...])
    b = _dot(attn_o.astype(BF16), wao_ref[...])
    m = gates[:, :D_MODEL] * a + gates[:, D_MODEL:] * b
    mo = _dot(m.astype(BF16), wout_ref[...])
    return _layer_norm(ALPHA * h1 + mo, g_ref[...], b_ref[...])


_MIXER_WEIGHT_SPECS = [
    _resident((D_MODEL, 2 * D_MODEL)),
    _resident((len(POOL_WINDOWS), POOL_GROUP, POOL_GROUP)),
    _resident((1, POOL_WIDTH)),
    _resident((POOL_WIDTH, D_MODEL)),
    _resident((Q_WIDTH, D_MODEL)),
    _resident((D_MODEL, D_MODEL)),
    _resident((1, D_MODEL)),
    _resident((1, D_MODEL)),
]


def _prompt_mixer_kernel(sinks_ref, h_ref, ucur_ref, uprev_ref, q_ref, kcur_ref, kprev_ref,
                         vcur_ref, vprev_ref, bias_ref,
                         wg_ref, wgrp_ref, scale_ref, wpo_ref, wao_ref, wout_ref, g_ref, b_ref,
                         o_ref, ubuf_ref, attn_ref, *, tq):
    i = pl.program_id(1)
    first_tile = i == 0
    pad = uprev_ref.shape[0]

    ubuf_ref[0:pad, :] = jnp.where(first_tile, 0.0, uprev_ref[...])
    ubuf_ref[pad:pad + tq, :] = ucur_ref[...]
    pos = i * tq + lax.broadcasted_iota(jnp.int32, (tq, 1), 0)
    pooled = []
    for g, w in enumerate(POOL_WINDOWS):
        cols = slice(g * POOL_GROUP, (g + 1) * POOL_GROUP)
        cur = ubuf_ref[pad:pad + tq, cols]
        acc = cur
        for j in range(1, w):
            acc = acc + ubuf_ref[pad - j:pad - j + tq, cols]
        cnt = jnp.minimum(pos + 1, w).astype(F32)
        pooled.append(acc / cnt - cur)
    pool_in = jnp.concatenate(pooled, axis=1)

    kall = jnp.concatenate([kprev_ref[...], kcur_ref[...]], axis=0)
    vall = jnp.concatenate([vprev_ref[...], vcur_ref[...]], axis=0)
    kall_sw = _swap_halves(kall)
    vall_sw = _swap_halves(vall)
    for blk in range(tq // WINDOW):
        r0 = blk * WINDOW
        span = slice(r0, r0 + KEY_SPAN)
        if blk == 0:
            bias = bias_ref[jnp.where(first_tile, 1, 0)]
        else:
            bias = bias_ref[0]
        for kh in range(N_KV_HEADS):
            c0 = 2 * kh * LANES
            q_pairs = jnp.concatenate([q_ref[r0:r0 + WINDOW, c0:c0 + LANES],
                                       q_ref[r0:r0 + WINDOW, c0 + LANES:c0 + 2 * LANES]], axis=0)
            o = _attend_kv_head(q_pairs, kall[span], kall_sw[span], vall[span], vall_sw[span],
                                kh, bias, sinks_ref, WINDOW)
            attn_ref[r0:r0 + WINDOW, c0:c0 + LANES] = o[:WINDOW]
            attn_ref[r0:r0 + WINDOW, c0 + LANES:c0 + 2 * LANES] = o[WINDOW:]

    o_ref[...] = _merge_out(h_ref[...], pool_in, attn_ref[...], wg_ref, wgrp_ref, scale_ref,
                            wpo_ref, wao_ref, wout_ref, g_ref, b_ref)


def _prompt_bias():
    r = np.arange(2 * WINDOW)[:, None] % WINDOW
    c = np.arange(2 * KEY_SPAN)[None, :] % KEY_SPAN
    valid = (r <= c) & (c <= r + WINDOW)
    first = valid & (c >= WINDOW)
    return np.where(np.stack([valid, first]), 0.0, NEG_INF).astype(np.float32)


def _prompt_mixer(h, u, q, k, v, sinks, weights, *, tq):
    tiles = SEQ // tq
    pad = 16
    blocks_per_tile = tq // WINDOW

    def cur(width):
        return pl.BlockSpec((tq, width), lambda b, i: (b * tiles + i, 0))

    def prev(rows, width):
        per_tile = tq // rows
        return pl.BlockSpec((rows, width),
                            lambda b, i: (jnp.maximum((b * tiles + i) * per_tile - 1, 0), 0))

    del blocks_per_tile
    return pl.pallas_call(
        functools.partial(_prompt_mixer_kernel, tq=tq),
        out_shape=jax.ShapeDtypeStruct((BATCH * SEQ, D_MODEL), F32),
        grid=(BATCH, tiles),
        in_specs=[pl.BlockSpec(memory_space=pltpu.SMEM),
                  cur(D_MODEL), cur(POOL_WIDTH), prev(pad, POOL_WIDTH), cur(Q_WIDTH),
                  cur(KV_WIDTH), prev(WINDOW, KV_WIDTH), cur(KV_WIDTH), prev(WINDOW, KV_WIDTH),
                  _resident((2, 2 * WINDOW, 2 * KEY_SPAN))] + _MIXER_WEIGHT_SPECS,
        out_specs=cur(D_MODEL),
        scratch_shapes=[pltpu.VMEM((pad + tq, POOL_WIDTH), F32), pltpu.VMEM((tq, Q_WIDTH), F32)],
        compiler_params=pltpu.CompilerParams(dimension_semantics=("parallel", "parallel"),
                                             vmem_limit_bytes=VMEM_LIMIT_BYTES),
        name="prompt_mixer",
    )(sinks, h, u, u, q, k, k, v, v, jnp.asarray(_prompt_bias()), *weights)


def _sample_mixer_kernel(sinks_ref, h_ref, u_ref, cu_ref, q_ref, k_ref, ck_ref, v_ref, cv_ref, bias_ref,
                         wg_ref, wgrp_ref, scale_ref, wpo_ref, wao_ref, wout_ref, g_ref, b_ref,
                         o_ref, ubuf_ref, pool_ref, attn_ref, *, bt):
    t = DEC_SEQ
    pad = 16
    bias = bias_ref[...]
    zeros_tail = jnp.zeros((KEY_SPAN - WINDOW - 2 * t, KV_WIDTH), F32)
    zeros_t = jnp.zeros((t, KV_WIDTH), F32)
    for b in range(bt):
        rows = slice(b * t, (b + 1) * t)
        ubuf_ref[pad - POOL_BUF:pad, :] = cu_ref[b]
        ubuf_ref[pad:pad + t, :] = u_ref[rows, :]
        for g, w in enumerate(POOL_WINDOWS):
            cols = slice(g * POOL_GROUP, (g + 1) * POOL_GROUP)
            cur = ubuf_ref[pad:pad + t, cols]
            acc = cur
            for j in range(1, w):
                acc = acc + ubuf_ref[pad - j:pad - j + t, cols]
            pool_ref[rows, cols] = acc / float(w) - cur
        keys = jnp.concatenate([ck_ref[b], k_ref[rows, :], zeros_t, zeros_tail], axis=0)
        vals = jnp.concatenate([cv_ref[b], v_ref[rows, :], zeros_t, zeros_tail], axis=0)
        keys_sw = _swap_halves(keys)
        vals_sw = _swap_halves(vals)
        for kh in range(N_KV_HEADS):
            c0 = 2 * kh * LANES
            q_pairs = jnp.concatenate([q_ref[rows, c0:c0 + LANES], q_ref[rows, c0 + LANES:c0 + 2 * LANES]], axis=0)
            o = _attend_kv_head(q_pairs, keys, keys_sw, vals, vals_sw, kh, bias, sinks_ref, t)
            attn_ref[rows, c0:c0 + LANES] = o[:t]
            attn_ref[rows, c0 + LANES:c0 + 2 * LANES] = o[t:]

    o_ref[...] = _merge_out(h_ref[...], pool_ref[...], attn_ref[...], wg_ref, wgrp_ref, scale_ref,
                            wpo_ref, wao_ref, wout_ref, g_ref, b_ref)


def _sample_bias():
    t = np.arange(2 * DEC_SEQ)[:, None] % DEC_SEQ
    j = np.arange(2 * KEY_SPAN)[None, :] % KEY_SPAN
    valid = (j < WINDOW + DEC_SEQ) & (t <= j) & (j <= t + WINDOW)
    return np.where(valid, 0.0, NEG_INF).astype(np.float32)


def _sample_mixer(h, u, q, k, v, cache_u, cache_k, cache_v, sinks, weights, *, bt):
    t = DEC_SEQ

    def rows(width):
        return pl.BlockSpec((bt * t, width), lambda i: (i, 0))

    def cache(n, width):
        return pl.BlockSpec((bt, n, width), lambda i: (i, 0, 0))

    return pl.pallas_call(
        functools.partial(_sample_mixer_kernel, bt=bt),
        out_shape=jax.ShapeDtypeStruct((DEC_BATCH * t, D_MODEL), F32),
        grid=(DEC_BATCH // bt,),
        in_specs=[pl.BlockSpec(memory_space=pltpu.SMEM),
                  rows(D_MODEL), rows(POOL_WIDTH), cache(POOL_BUF, POOL_WIDTH), rows(Q_WIDTH),
                  rows(KV_WIDTH), cache(WINDOW, KV_WIDTH), rows(KV_WIDTH), cache(WINDOW, KV_WIDTH),
                  _resident((2 * t, 2 * KEY_SPAN))] + _MIXER_WEIGHT_SPECS,
        out_specs=rows(D_MODEL),
        scratch_shapes=[pltpu.VMEM((16 + 8, POOL_WIDTH), F32), pltpu.VMEM((bt * t, POOL_WIDTH), F32),
                        pltpu.VMEM((bt * t, Q_WIDTH), F32)],
        compiler_params=pltpu.CompilerParams(dimension_semantics=("parallel",),
                                             vmem_limit_bytes=VMEM_LIMIT_BYTES),
        name="sample_mixer",
    )(sinks, h, u, cache_u, q, k, cache_k, v, cache_v, jnp.asarray(_sample_bias()), *weights)


def _rope_tables(pos):
    half = HEAD_DIM // 2
    freqs = ROPE_THETA ** (-2.0 * jnp.arange(half, dtype=F32) / HEAD_DIM)
    ang = pos.astype(F32)[:, None] * freqs[None, :]
    cos, sin = jnp.cos(ang), jnp.sin(ang)
    return jnp.tile(cos, (1, 4)), jnp.concatenate([-sin, sin, -sin, sin], axis=1)


def kernel(x_prompt, x_sample, cache_pool_u, cache_k_win, cache_v_win, w_in, pool_w_grp, pool_scale,
           attn_sinks, w_pool_out, w_attn_out, w_out, ffn1_w1, ffn1_w3, ffn1_w2, ffn2_w1, ffn2_w3,
           ffn2_w2, ln1_g, ln1_b, ln2_g, ln2_b, ln3_g, ln3_b):
    assert DEPTH == 1 and w_in.shape[0] == 1
    l = 0
    bf = lambda w: w.astype(BF16)
    vec = lambda p: p[l].reshape(1, -1)
    w_uqkv = bf(w_in[l][:, :UQKV_WIDTH])
    mixer_weights = (bf(w_in[l][:, UQKV_WIDTH:]), bf(pool_w_grp[l]), vec(pool_scale), bf(w_pool_out[l]),
                     bf(w_attn_out[l]), bf(w_out[l]), vec(ln2_g), vec(ln2_b))
    ffn1 = (bf(ffn1_w1[l]), bf(ffn1_w3[l]), bf(ffn1_w2[l]), vec(ln1_g), vec(ln1_b))
    ffn2 = (bf(ffn2_w1[l]), bf(ffn2_w3[l]), bf(ffn2_w2[l]), vec(ln3_g), vec(ln3_b))
    sinks = attn_sinks[l]

    cos_p, sin_p = _rope_tables(jnp.arange(SEQ, dtype=jnp.int32))
    pos_s = PAST_LEN + jnp.arange(DEC_SEQ, dtype=jnp.int32)
    cos_s, sin_s = (jnp.tile(tbl, (DEC_BATCH, 1)) for tbl in _rope_tables(pos_s))

    xp = x_prompt.reshape(BATCH * SEQ, D_MODEL)
    hp = _ffn_half_step(xp, *ffn1, tm=512)
    up, qp, kp, vp = _in_proj(hp, w_uqkv, cos_p, sin_p, tm=512)
    hp = _prompt_mixer(hp, up, qp, kp, vp, sinks, mixer_weights, tq=256)
    yp = _ffn_half_step(hp, *ffn2, tm=512).reshape(BATCH, SEQ, D_MODEL)

    rows_s = DEC_BATCH * DEC_SEQ
    xs = x_sample.reshape(rows_s, D_MODEL)
    hs = _ffn_half_step(xs, *ffn1, tm=rows_s)
    us, qs, ks, vs = _in_proj(hs, w_uqkv, cos_s, sin_s, tm=rows_s)
    cu = cache_pool_u[l]
    ck = cache_k_win[l].reshape(DEC_BATCH, WINDOW, KV_WIDTH)
    cv = cache_v_win[l].reshape(DEC_BATCH, WINDOW, KV_WIDTH)
    hs = _sample_mixer(hs, us, qs, ks, vs, cu, ck, cv, sinks, mixer_weights, bt=8)
    ys = _ffn_half_step(hs, *ffn2, tm=rows_s).reshape(DEC_BATCH, DEC_SEQ, D_MODEL)

    up3 = up.reshape(BATCH, SEQ, POOL_WIDTH)
    kp4 = kp.reshape(BATCH, SEQ, N_KV_HEADS, HEAD_DIM)
    vp4 = vp.reshape(BATCH, SEQ, N_KV_HEADS, HEAD_DIM)
    pool_u_prompt = up3[None, :, SEQ - POOL_BUF:]
    k_win_prompt = kp4[None, :, SEQ - WINDOW:]
    v_win_prompt = vp4[None, :, SEQ - WINDOW:]
    us3 = us.reshape(DEC_BATCH, DEC_SEQ, POOL_WIDTH)
    ks4 = ks.reshape(DEC_BATCH, DEC_SEQ, N_KV_HEADS, HEAD_DIM)
    vs4 = vs.reshape(DEC_BATCH, DEC_SEQ, N_KV_HEADS, HEAD_DIM)
    pool_u_sample = jnp.concatenate([cache_pool_u[l][:, DEC_SEQ:], us3], axis=1)[None]
    k_win_sample = jnp.concatenate([cache_k_win[l][:, DEC_SEQ:], ks4], axis=1)[None]
    v_win_sample = jnp.concatenate([cache_v_win[l][:, DEC_SEQ:], vs4], axis=1)[None]
    return (yp, ys, pool_u_prompt, k_win_prompt, v_win_prompt, pool_u_sample, k_win_sample, v_win_sample)
```

```python
import functools

import jax
import jax.numpy as jnp
import numpy as np
from jax import lax
from jax.experimental import pallas as pl
from jax.experimental.pallas import tpu as pltpu

D_MODEL = 1024
BATCH = 8
SEQ = 2048
DEC_BATCH = 128
DEC_SEQ = 4
PAST_LEN = 8192
POOL_WINDOWS = (2, 4, 8, 16)
POOL_GROUP = 128
POOL_WIDTH = 512
POOL_BUF = 15
N_HEADS = 8
N_KV_HEADS = 2
HEAD_DIM = 64
Q_WIDTH = 512
KV_WIDTH = 128
WINDOW = 128
ROPE_THETA = 10000.0
D_FF = 2816
DEPTH = 1
ALPHA = (2.0 * DEPTH) ** 0.25
LN_EPS = 1e-5
NEG_INF = -1e30
UQKV_WIDTH = POOL_WIDTH + Q_WIDTH + 2 * KV_WIDTH

LANES = 128
KEY_SPAN = 2 * WINDOW
VMEM_LIMIT_BYTES = 56 * 1024 * 1024

BF16 = jnp.bfloat16
F32 = jnp.float32


def _dot(a, b):
    return jnp.dot(a, b, preferred_element_type=F32)


def _layer_norm(y, g, b):
    mu = jnp.mean(y, axis=-1, keepdims=True)
    yc = y - mu
    var = jnp.mean(yc * yc, axis=-1, keepdims=True)
    return yc * lax.rsqrt(var + LN_EPS) * g + b


def _resident(shape):
    nd = len(shape)
    return pl.BlockSpec(shape, lambda *_: (0,) * nd, pipeline_mode=pl.Buffered(1))


ROW_TILE = 512
PROMPT_TILES = BATCH * SEQ // ROW_TILE
SAMPLE_ROWS = DEC_BATCH * DEC_SEQ
ROW_TILES = PROMPT_TILES + 1
STAGE_STEPS = 16


def _rope(x, cos, sin_signed, first_half):
    fwd = pltpu.roll(x, LANES - HEAD_DIM // 2, axis=1)
    bwd = pltpu.roll(x, HEAD_DIM // 2, axis=1)
    return x * cos + jnp.where(first_half, fwd, bwd) * sin_signed


def _stage_chunk(step, chunk_ref, scratch_ref):
    rows = chunk_ref.shape[0]
    r0 = pl.multiple_of(step * rows, rows)
    scratch_ref[pl.ds(r0, rows), :] = chunk_ref[...].astype(BF16)


def _swiglu_ln(x, w1_s, w3_s, w2_s, g_ref, b_ref):
    xb = x.astype(BF16)
    a = _dot(xb, w1_s[...])
    b = _dot(xb, w3_s[...])
    h = (a * jax.nn.sigmoid(a)) * b
    y = _dot(h.astype(BF16), w2_s[...])
    return _layer_norm(ALPHA * x + 0.5 * y, g_ref[...], b_ref[...])


def _ffn1_in_proj_kernel(xp_ref, xs_ref, w1c_ref, w3c_ref, w2c_ref, wuc_ref, g_ref, b_ref, cos_ref, sin_ref,
                         h_ref, u_ref, q_ref, k_ref, v_ref, w1_s, w3_s, w2_s, wu_s):
    step = pl.program_id(0)

    @pl.when(step < STAGE_STEPS)
    def _():
        _stage_chunk(step, w1c_ref, w1_s)
        _stage_chunk(step, w3c_ref, w3_s)
        _stage_chunk(step, w2c_ref, w2_s)
        _stage_chunk(step, wuc_ref, wu_s)

    @pl.when(step >= STAGE_STEPS)
    def _():
        is_sample = step == STAGE_STEPS + PROMPT_TILES
        x = jnp.where(is_sample, xs_ref[...], xp_ref[...])
        h1 = _swiglu_ln(x, w1_s, w3_s, w2_s, g_ref, b_ref)
        h_ref[...] = h1
        z = _dot(h1.astype(BF16), wu_s[...])
        cos = cos_ref[...]
        sin_signed = sin_ref[...]
        lane = lax.broadcasted_iota(jnp.int32, cos.shape, 1)
        first_half = (lane & (HEAD_DIM // 2)) == 0
        u_ref[...] = z[:, :POOL_WIDTH]
        for c in range(Q_WIDTH // LANES):
            lo = POOL_WIDTH + c * LANES
            qc = _rope(z[:, lo:lo + LANES], cos, sin_signed, first_half)
            q_ref[:, c * LANES:(c + 1) * LANES] = (qc * (HEAD_DIM ** -0.5)).astype(BF16)
        ko = POOL_WIDTH + Q_WIDTH
        k_ref[...] = _rope(z[:, ko:ko + KV_WIDTH], cos, sin_signed, first_half)
        v_ref[...] = z[:, ko + KV_WIDTH:ko + 2 * KV_WIDTH]


def _ffn2_kernel(xp_ref, xs_ref, w1c_ref, w3c_ref, w2c_ref, g_ref, b_ref, yp_ref, ys_ref, w1_s, w3_s, w2_s):
    step = pl.program_id(0)

    @pl.when(step < STAGE_STEPS)
    def _():
        _stage_chunk(step, w1c_ref, w1_s)
        _stage_chunk(step, w3c_ref, w3_s)
        _stage_chunk(step, w2c_ref, w2_s)

    @pl.when(jnp.logical_and(step >= STAGE_STEPS, step < STAGE_STEPS + PROMPT_TILES))
    def _():
        yp_ref[...] = _swiglu_ln(xp_ref[...], w1_s, w3_s, w2_s, g_ref, b_ref)

    @pl.when(step == STAGE_STEPS + PROMPT_TILES)
    def _():
        ys_ref[...] = _swiglu_ln(xs_ref[...], w1_s, w3_s, w2_s, g_ref, b_ref)


def _tile_index(step):
    return jnp.clip(step - STAGE_STEPS, 0, ROW_TILES - 1)


def _prompt_tile_spec(width):
    return pl.BlockSpec((ROW_TILE, width), lambda s: (jnp.minimum(_tile_index(s), PROMPT_TILES - 1), 0))


def _sample_tile_spec(width):
    return pl.BlockSpec((ROW_TILE, width), lambda s: (0, 0))


def _chunk_spec(rows, cols):
    return pl.BlockSpec((None, rows // STAGE_STEPS, cols), lambda s: (0, jnp.minimum(s, STAGE_STEPS - 1), 0))


def _ffn_weight_specs():
    return [_chunk_spec(D_MODEL, D_FF), _chunk_spec(D_MODEL, D_FF), _chunk_spec(D_FF, D_MODEL)]


def _ffn_weight_scratch():
    return [pltpu.VMEM((D_MODEL, D_FF), BF16), pltpu.VMEM((D_MODEL, D_FF), BF16), pltpu.VMEM((D_FF, D_MODEL), BF16)]


_SEQUENTIAL = pltpu.CompilerParams(dimension_semantics=("arbitrary",), vmem_limit_bytes=VMEM_LIMIT_BYTES)


def _ffn1_in_proj(xp, xs, w1, w3, w2, w_in, g, b, cos, sin_signed):
    rows = ROW_TILES * ROW_TILE
    tables_per_seq = SEQ // ROW_TILE

    def out(width):
        return pl.BlockSpec((ROW_TILE, width), lambda s: (_tile_index(s), 0))

    def table_index(s):
        t = _tile_index(s)
        return (jnp.where(t == PROMPT_TILES, tables_per_seq, t % tables_per_seq), 0)

    table_spec = pl.BlockSpec((ROW_TILE, LANES), table_index)
    return pl.pallas_call(
        _ffn1_in_proj_kernel,
        out_shape=(jax.ShapeDtypeStruct((rows, D_MODEL), F32),
                   jax.ShapeDtypeStruct((rows, POOL_WIDTH), F32),
                   jax.ShapeDtypeStruct((rows, Q_WIDTH), BF16),
                   jax.ShapeDtypeStruct((rows, KV_WIDTH), F32),
                   jax.ShapeDtypeStruct((rows, KV_WIDTH), F32)),
        grid=(STAGE_STEPS + ROW_TILES,),
        in_specs=[_prompt_tile_spec(D_MODEL), _sample_tile_spec(D_MODEL)] + _ffn_weight_specs()
                 + [_chunk_spec(D_MODEL, UQKV_WIDTH), _resident((1, D_MODEL)), _resident((1, D_MODEL)),
                    table_spec, table_spec],
        out_specs=(out(D_MODEL), out(POOL_WIDTH), out(Q_WIDTH), out(KV_WIDTH), out(KV_WIDTH)),
        scratch_shapes=_ffn_weight_scratch() + [pltpu.VMEM((D_MODEL, UQKV_WIDTH), BF16)],
        compiler_params=_SEQUENTIAL,
        name="ffn1_in_proj",
    )(xp, xs, w1, w3, w2, w_in, g, b, cos, sin_signed)


def _ffn2(hp, hs, w1, w3, w2, g, b):
    return pl.pallas_call(
        _ffn2_kernel,
        out_shape=(jax.ShapeDtypeStruct((BATCH * SEQ, D_MODEL), F32),
                   jax.ShapeDtypeStruct((SAMPLE_ROWS, D_MODEL), F32)),
        grid=(STAGE_STEPS + ROW_TILES,),
        in_specs=[_prompt_tile_spec(D_MODEL), _sample_tile_spec(D_MODEL)] + _ffn_weight_specs()
                 + [_resident((1, D_MODEL)), _resident((1, D_MODEL))],
        out_specs=(_prompt_tile_spec(D_MODEL), _sample_tile_spec(D_MODEL)),
        scratch_shapes=_ffn_weight_scratch(),
        compiler_params=_SEQUENTIAL,
        name="ffn2",
    )(hp, hs, w1, w3, w2, g, b)


def _attend_kv_head(q_pairs, keys, keys_sw, vals, vals_sw, kh, bias, sinks_ref, rows_per_pair):
    low = lax.broadcasted_iota(jnp.int32, (1, LANES), 1) < HEAD_DIM
    if kh == 0:
        k_lo, k_hi, v_lo, v_hi = keys, keys_sw, vals, vals_sw
    else:
        k_lo, k_hi, v_lo, v_hi = keys_sw, keys, vals_sw, vals
    kcat = jnp.concatenate([jnp.where(low, k_lo, 0.0), jnp.where(low, 0.0, k_hi)], axis=0).astype(BF16)
    vcat = jnp.concatenate([jnp.where(low, v_lo, 0.0), jnp.where(low, 0.0, v_hi)], axis=0).astype(BF16)
    s = lax.dot_general(q_pairs, kcat, (((1,), (1,)), ((), ())), preferred_element_type=F32) + bias
    second_pair = lax.broadcasted_iota(jnp.int32, (s.shape[0], 1), 0) >= rows_per_pair
    probs, denoms = [], []
    for c in range(2):
        sc = s[:, c * KEY_SPAN:(c + 1) * KEY_SPAN]
        head = kh * 4 + c
        sink = jnp.where(second_pair, sinks_ref[head + 2], sinks_ref[head])
        m = jnp.maximum(jnp.max(sc, axis=-1, keepdims=True), sink)
        p = jnp.exp(sc - m)
        denoms.append(jnp.sum(p, axis=-1, keepdims=True) + jnp.exp(sink - m))
        probs.append(p.astype(BF16))
    o = _dot(jnp.concatenate(probs, axis=1), vcat)
    return o / jnp.where(low, denoms[0], denoms[1])


def _swap_halves(x):
    return pltpu.roll(x, HEAD_DIM, axis=1)


def _merge_out(h1, pool_in, attn_o, wg_ref, wgrp_ref, scale_ref, wpo_ref, wao_ref, wout_ref, g_ref, b_ref):
    gates = jax.nn.sigmoid(_dot(h1.astype(BF16), wg_ref[...]))
    zs = [_dot(pool_in[:, g * POOL_GROUP:(g + 1) * POOL_GROUP].astype(BF16), wgrp_ref[g])
          for g in range(len(POOL_WINDOWS))]
    pool_z = jnp.concatenate(zs, axis=1) * scale_ref[...]
    a = _dot(pool_z.astype(BF16), wpo_ref[...])
    b = _dot(attn_o.astype(BF16), wao_ref[...])
    m = gates[:, :D_MODEL] * a + gates[:, D_MODEL:] * b
    mo = _dot(m.astype(BF16), wout_ref[...])
    return _layer_norm(ALPHA * h1 + mo, g_ref[...], b_ref[...])


_MIXER_WEIGHT_SPECS = [
    _resident((D_MODEL, 2 * D_MODEL)),
    _resident((len(POOL_WINDOWS), POOL_GROUP, POOL_GROUP)),
    _resident((1, POOL_WIDTH)),
    _resident((POOL_WIDTH, D_MODEL)),
    _resident((Q_WIDTH, D_MODEL)),
    _resident((D_MODEL, D_MODEL)),
    _resident((1, D_MODEL)),
    _resident((1, D_MODEL)),
]


def _prompt_mixer_kernel(sinks_ref, h_ref, ucur_ref, uprev_ref, q_ref, kcur_ref, kprev_ref,
                         vcur_ref, vprev_ref, bias_ref,
                         wg_ref, wgrp_ref, scale_ref, wpo_ref, wao_ref, wout_ref, g_ref, b_ref,
                         o_ref, ubuf_ref, attn_ref, *, tq):
    i = pl.program_id(1)
    first_tile = i == 0
    pad = uprev_ref.shape[0]

    ubuf_ref[0:pad, :] = jnp.where(first_tile, 0.0, uprev_ref[...])
    ubuf_ref[pad:pad + tq, :] = ucur_ref[...]
    pos = i * tq + lax.broadcasted_iota(jnp.int32, (tq, 1), 0)
    pooled = []
    for g, w in enumerate(POOL_WINDOWS):
        cols = slice(g * POOL_GROUP, (g + 1) * POOL_GROUP)
        cur = ubuf_ref[pad:pad + tq, cols]
        acc = cur
        for j in range(1, w):
            acc = acc + ubuf_ref[pad - j:pad - j + tq, cols]
        cnt = jnp.minimum(pos + 1, w).astype(F32)
        pooled.append(acc / cnt - cur)
    pool_in = jnp.concatenate(pooled, axis=1)

    kall = jnp.concatenate([kprev_ref[...], kcur_ref[...]], axis=0)
    vall = jnp.concatenate([vprev_ref[...], vcur_ref[...]], axis=0)
    kall_sw = _swap_halves(kall)
    vall_sw = _swap_halves(vall)
    for blk in range(tq // WINDOW):
        r0 = blk * WINDOW
        span = slice(r0, r0 + KEY_SPAN)
        if blk == 0:
            bias = bias_ref[jnp.where(first_tile, 1, 0)]
        else:
            bias = bias_ref[0]
        for kh in range(N_KV_HEADS):
            c0 = 2 * kh * LANES
            q_pairs = jnp.concatenate([q_ref[r0:r0 + WINDOW, c0:c0 + LANES],
                                       q_ref[r0:r0 + WINDOW, c0 + LANES:c0 + 2 * LANES]], axis=0)
            o = _attend_kv_head(q_pairs, kall[span], kall_sw[span], vall[span], vall_sw[span],
                                kh, bias, sinks_ref, WINDOW)
            attn_ref[r0:r0 + WINDOW, c0:c0 + LANES] = o[:WINDOW]
            attn_ref[r0:r0 + WINDOW, c0 + LANES:c0 + 2 * LANES] = o[WINDOW:]

    o_ref[...] = _merge_out(h_ref[...], pool_in, attn_ref[...], wg_ref, wgrp_ref, scale_ref,
                            wpo_ref, wao_ref, wout_ref, g_ref, b_ref)


def _prompt_bias():
    r = np.arange(2 * WINDOW)[:, None] % WINDOW
    c = np.arange(2 * KEY_SPAN)[None, :] % KEY_SPAN
    valid = (r <= c) & (c <= r + WINDOW)
    first = valid & (c >= WINDOW)
    return np.where(np.stack([valid, first]), 0.0, NEG_INF).astype(np.float32)


def _prompt_mixer(h, u, q, k, v, sinks, weights, *, tq):
    tiles = SEQ // tq
    pad = 16
    blocks_per_tile = tq // WINDOW

    def cur(width):
        return pl.BlockSpec((tq, width), lambda b, i: (b * tiles + i, 0))

    def prev(rows, width):
        per_tile = tq // rows
        return pl.BlockSpec((rows, width),
                            lambda b, i: (jnp.maximum((b * tiles + i) * per_tile - 1, 0), 0))

    del blocks_per_tile
    return pl.pallas_call(
        functools.partial(_prompt_mixer_kernel, tq=tq),
        out_shape=jax.ShapeDtypeStruct((BATCH * SEQ, D_MODEL), F32),
        grid=(BATCH, tiles),
        in_specs=[pl.BlockSpec(memory_space=pltpu.SMEM),
                  cur(D_MODEL), cur(POOL_WIDTH), prev(pad, POOL_WIDTH), cur(Q_WIDTH),
                  cur(KV_WIDTH), prev(WINDOW, KV_WIDTH), cur(KV_WIDTH), prev(WINDOW, KV_WIDTH),
                  _resident((2, 2 * WINDOW, 2 * KEY_SPAN))] + _MIXER_WEIGHT_SPECS,
        out_specs=cur(D_MODEL),
        scratch_shapes=[pltpu.VMEM((pad + tq, POOL_WIDTH), F32), pltpu.VMEM((tq, Q_WIDTH), F32)],
        compiler_params=pltpu.CompilerParams(dimension_semantics=("parallel", "parallel"),
                                             vmem_limit_bytes=VMEM_LIMIT_BYTES),
        name="prompt_mixer",
    )(sinks, h, u, u, q, k, k, v, v, jnp.asarray(_prompt_bias()), *weights)


def _sample_mixer_kernel(sinks_ref, h_ref, u_ref, cu_ref, q_ref, k_ref, ck_ref, v_ref, cv_ref, bias_ref,
                         wg_ref, wgrp_ref, scale_ref, wpo_ref, wao_ref, wout_ref, g_ref, b_ref,
                         o_ref, ubuf_ref, pool_ref, attn_ref, *, bt):
    t = DEC_SEQ
    pad = 16
    bias = bias_ref[...]
    zeros_tail = jnp.zeros((KEY_SPAN - WINDOW - 2 * t, KV_WIDTH), F32)
    zeros_t = jnp.zeros((t, KV_WIDTH), F32)
    for b in range(bt):
        rows = slice(b * t, (b + 1) * t)
        ubuf_ref[pad - POOL_BUF:pad, :] = cu_ref[b]
        ubuf_ref[pad:pad + t, :] = u_ref[rows, :]
        for g, w in enumerate(POOL_WINDOWS):
            cols = slice(g * POOL_GROUP, (g + 1) * POOL_GROUP)
            cur = ubuf_ref[pad:pad + t, cols]
            acc = cur
            for j in range(1, w):
                acc = acc + ubuf_ref[pad - j:pad - j + t, cols]
            pool_ref[rows, cols] = acc / float(w) - cur
        keys = jnp.concatenate([ck_ref[b], k_ref[rows, :], zeros_t, zeros_tail], axis=0)
        vals = jnp.concatenate([cv_ref[b], v_ref[rows, :], zeros_t, zeros_tail], axis=0)
        keys_sw = _swap_halves(keys)
        vals_sw = _swap_halves(vals)
        for kh in range(N_KV_HEADS):
            c0 = 2 * kh * LANES
            q_pairs = jnp.concatenate([q_ref[rows, c0:c0 + LANES], q_ref[rows, c0 + LANES:c0 + 2 * LANES]], axis=0)
            o = _attend_kv_head(q_pairs, keys, keys_sw, vals, vals_sw, kh, bias, sinks_ref, t)
            attn_ref[rows, c0:c0 + LANES] = o[:t]
            attn_ref[rows, c0 + LANES:c0 + 2 * LANES] = o[t:]

    o_ref[...] = _merge_out(h_ref[...], pool_ref[...], attn_ref[...], wg_ref, wgrp_ref, scale_ref,
                            wpo_ref, wao_ref, wout_ref, g_ref, b_ref)


def _sample_bias():
    t = np.arange(2 * DEC_SEQ)[:, None] % DEC_SEQ
    j = np.arange(2 * KEY_SPAN)[None, :] % KEY_SPAN
    valid = (j < WINDOW + DEC_SEQ) & (t <= j) & (j <= t + WINDOW)
    return np.where(valid, 0.0, NEG_INF).astype(np.float32)


def _sample_mixer(h, u, q, k, v, cache_u, cache_k, cache_v, sinks, weights, *, bt):
    t = DEC_SEQ

    first = BATCH * SEQ // (bt * t)

    def rows(width):
        return pl.BlockSpec((bt * t, width), lambda i: (first + i, 0))

    def cache(n, width):
        return pl.BlockSpec((bt, n, width), lambda i: (i, 0, 0))

    return pl.pallas_call(
        functools.partial(_sample_mixer_kernel, bt=bt),
        out_shape=jax.ShapeDtypeStruct((DEC_BATCH * t, D_MODEL), F32),
        grid=(DEC_BATCH // bt,),
        in_specs=[pl.BlockSpec(memory_space=pltpu.SMEM),
                  rows(D_MODEL), rows(POOL_WIDTH), cache(POOL_BUF, POOL_WIDTH), rows(Q_WIDTH),
                  rows(KV_WIDTH), cache(WINDOW, KV_WIDTH), rows(KV_WIDTH), cache(WINDOW, KV_WIDTH),
                  _resident((2 * t, 2 * KEY_SPAN))] + _MIXER_WEIGHT_SPECS,
        out_specs=pl.BlockSpec((bt * t, D_MODEL), lambda i: (i, 0)),
        scratch_shapes=[pltpu.VMEM((16 + 8, POOL_WIDTH), F32), pltpu.VMEM((bt * t, POOL_WIDTH), F32),
                        pltpu.VMEM((bt * t, Q_WIDTH), F32)],
        compiler_params=pltpu.CompilerParams(dimension_semantics=("parallel",),
                                             vmem_limit_bytes=VMEM_LIMIT_BYTES),
        name="sample_mixer",
    )(sinks, h, u, cache_u, q, k, cache_k, v, cache_v, jnp.asarray(_sample_bias()), *weights)


def _rope_tables(pos):
    half = HEAD_DIM // 2
    freqs = ROPE_THETA ** (-2.0 * jnp.arange(half, dtype=F32) / HEAD_DIM)
    ang = pos.astype(F32)[:, None] * freqs[None, :]
    cos, sin = jnp.cos(ang), jnp.sin(ang)
    return jnp.tile(cos, (1, 4)), jnp.concatenate([-sin, sin, -sin, sin], axis=1)


def kernel(x_prompt, x_sample, cache_pool_u, cache_k_win, cache_v_win, w_in, pool_w_grp, pool_scale,
           attn_sinks, w_pool_out, w_attn_out, w_out, ffn1_w1, ffn1_w3, ffn1_w2, ffn2_w1, ffn2_w3,
           ffn2_w2, ln1_g, ln1_b, ln2_g, ln2_b, ln3_g, ln3_b):
    assert DEPTH == 1 and w_in.shape[0] == 1
    l = 0
    bf = lambda w: w.astype(BF16)
    vec = lambda p: p[l].reshape(1, -1)
    mixer_weights = (bf(w_in[l][:, UQKV_WIDTH:]), bf(pool_w_grp[l]), vec(pool_scale), bf(w_pool_out[l]),
                     bf(w_attn_out[l]), bf(w_out[l]), vec(ln2_g), vec(ln2_b))
    sinks = attn_sinks[l]

    cos_p, sin_p = _rope_tables(jnp.arange(SEQ, dtype=jnp.int32))
    pos_s = PAST_LEN + jnp.arange(DEC_SEQ, dtype=jnp.int32)
    cos_s, sin_s = (jnp.tile(tbl, (DEC_BATCH, 1)) for tbl in _rope_tables(pos_s))
    cos = jnp.concatenate([cos_p, cos_s], axis=0)
    sin_signed = jnp.concatenate([sin_p, sin_s], axis=0)

    xp = x_prompt.reshape(BATCH * SEQ, D_MODEL)
    xs = x_sample.reshape(SAMPLE_ROWS, D_MODEL)
    h1, u, q, k, v = _ffn1_in_proj(xp, xs, ffn1_w1, ffn1_w3, ffn1_w2, w_in, vec(ln1_g), vec(ln1_b),
                                   cos, sin_signed)

    h2p = _prompt_mixer(h1, u, q, k, v, sinks, mixer_weights, tq=256)
    cu = cache_pool_u[l]
    ck = cache_k_win[l].reshape(DEC_BATCH, WINDOW, KV_WIDTH)
    cv = cache_v_win[l].reshape(DEC_BATCH, WINDOW, KV_WIDTH)
    h2s = _sample_mixer(h1, u, q, k, v, cu, ck, cv, sinks, mixer_weights, bt=8)

    yp, ys = _ffn2(h2p, h2s, ffn2_w1, ffn2_w3, ffn2_w2, vec(ln3_g), vec(ln3_b))
    yp = yp.reshape(BATCH, SEQ, D_MODEL)
    ys = ys.reshape(DEC_BATCH, DEC_SEQ, D_MODEL)

    n_p = BATCH * SEQ
    up3 = u[:n_p].reshape(BATCH, SEQ, POOL_WIDTH)
    kp4 = k[:n_p].reshape(BATCH, SEQ, N_KV_HEADS, HEAD_DIM)
    vp4 = v[:n_p].reshape(BATCH, SEQ, N_KV_HEADS, HEAD_DIM)
    pool_u_prompt = up3[None, :, SEQ - POOL_BUF:]
    k_win_prompt = kp4[None, :, SEQ - WINDOW:]
    v_win_prompt = vp4[None, :, SEQ - WINDOW:]
    us3 = u[n_p:].reshape(DEC_BATCH, DEC_SEQ, POOL_WIDTH)
    ks4 = k[n_p:].reshape(DEC_BATCH, DEC_SEQ, N_KV_HEADS, HEAD_DIM)
    vs4 = v[n_p:].reshape(DEC_BATCH, DEC_SEQ, N_KV_HEADS, HEAD_DIM)
    pool_u_sample = jnp.concatenate([cache_pool_u[l][:, DEC_SEQ:], us3], axis=1)[None]
    k_win_sample = jnp.concatenate([cache_k_win[l][:, DEC_SEQ:], ks4], axis=1)[None]
    v_win_sample = jnp.concatenate([cache_v_win[l][:, DEC_SEQ:], vs4], axis=1)[None]
    return (yp, ys, pool_u_prompt, k_win_prompt, v_win_prompt, pool_u_sample, k_win_sample, v_win_sample)
```

```python
import jax
import jax.numpy as jnp
import numpy as np
from jax import lax
from jax.experimental import pallas as pl
from jax.experimental.pallas import tpu as pltpu

D_MODEL = 1024
BATCH = 8
SEQ = 2048
DEC_BATCH = 128
DEC_SEQ = 4
PAST_LEN = 8192
POOL_WINDOWS = (2, 4, 8, 16)
POOL_GROUP = 128
POOL_WIDTH = 512
POOL_BUF = 15
N_HEADS = 8
N_KV_HEADS = 2
HEADS_PER_KV = N_HEADS // N_KV_HEADS
HEAD_DIM = 64
Q_WIDTH = 512
KV_WIDTH = 128
WINDOW = 128
ROPE_THETA = 10000.0
D_FF = 2816
DEPTH = 1
ALPHA = (2.0 * DEPTH) ** 0.25
LN_EPS = 1e-5
NEG_INF = -1e30
UQKV_WIDTH = POOL_WIDTH + Q_WIDTH + 2 * KV_WIDTH

LANES = 128
KEY_SPAN = 2 * WINDOW
VMEM_LIMIT_BYTES = 58 * 1024 * 1024

ROW_TILE = 512
TILES_PER_SEQ = SEQ // ROW_TILE
PROMPT_ROWS = BATCH * SEQ
PROMPT_TILES = PROMPT_ROWS // ROW_TILE
SAMPLE_ROWS = DEC_BATCH * DEC_SEQ
ROW_TILES = PROMPT_TILES + 1
POOL_PAD = 16
SEQ_GROUP = 16
SUB_GROUP = 8

BF16 = jnp.bfloat16
F32 = jnp.float32


def _dot(a, b):
    return jnp.dot(a, b, preferred_element_type=F32)


def _dot_nt(a, b):
    return lax.dot_general(a, b, (((1,), (1,)), ((), ())), preferred_element_type=F32)


def _layer_norm(y, g, b):
    mu = jnp.mean(y, axis=-1, keepdims=True)
    yc = y - mu
    var = jnp.mean(yc * yc, axis=-1, keepdims=True)
    return yc * lax.rsqrt(var + LN_EPS) * g + b


def _resident(shape):
    nd = len(shape)
    return pl.BlockSpec(shape, lambda *_: (0,) * nd, pipeline_mode=pl.Buffered(1))


def _swiglu_ln(x, w1_ref, w3_ref, w2_ref, g_ref, b_ref):
    xb = x.astype(BF16)
    a = _dot(xb, w1_ref[...])
    b = _dot(xb, w3_ref[...])
    h = (a * jax.nn.sigmoid(a)) * b
    y = _dot(h.astype(BF16), w2_ref[...])
    return _layer_norm(ALPHA * x + 0.5 * y, g_ref[...], b_ref[...])


def _rope(x, cos, sin_signed, first_half):
    fwd = pltpu.roll(x, LANES - HEAD_DIM // 2, axis=1)
    bwd = pltpu.roll(x, HEAD_DIM // 2, axis=1)
    return x * cos + jnp.where(first_half, fwd, bwd) * sin_signed


def _swap_halves(x):
    return pltpu.roll(x, HEAD_DIM, axis=1)


def _swap_halves_wide(x):
    return jnp.concatenate([_swap_halves(x[:, c:c + LANES]) for c in range(0, x.shape[1], LANES)], axis=1)


def _attend_kv_head(q_pairs, keys, keys_sw, vals, vals_sw, kh, bias, sinks_ref):
    low = lax.broadcasted_iota(jnp.int32, (1, LANES), 1) < HEAD_DIM
    if kh == 0:
        k_lo, k_hi, v_lo, v_hi = keys, keys_sw, vals, vals_sw
    else:
        k_lo, k_hi, v_lo, v_hi = keys_sw, keys, vals_sw, vals
    kcat = jnp.concatenate([jnp.where(low, k_lo, 0.0), jnp.where(low, 0.0, k_hi)], axis=0).astype(BF16)
    vcat = jnp.concatenate([jnp.where(low, v_lo, 0.0), jnp.where(low, 0.0, v_hi)], axis=0).astype(BF16)
    s = _dot_nt(q_pairs, kcat) + bias
    second_pair = lax.broadcasted_iota(jnp.int32, (s.shape[0], 1), 0) >= WINDOW
    probs, denoms = [], []
    for c in range(2):
        sc = s[:, c * KEY_SPAN:(c + 1) * KEY_SPAN]
        head = kh * HEADS_PER_KV + c
        sink = jnp.where(second_pair, sinks_ref[head + 2], sinks_ref[head])
        m = jnp.maximum(jnp.max(sc, axis=-1, keepdims=True), sink)
        p = jnp.exp(sc - m)
        denoms.append(jnp.sum(p, axis=-1, keepdims=True) + jnp.exp(sink - m))
        probs.append(p.astype(BF16))
    o = _dot(jnp.concatenate(probs, axis=1), vcat)
    return o / jnp.where(low, denoms[0], denoms[1])


def _prompt_context(tile, q_c, k_c, v_c, u_c, bias_ref, sinks_ref,
                    pool_ref, attn_ref, kt_ref, vt_ref, ulast_ref):
    seq_tile = (tile + TILES_PER_SEQ) % TILES_PER_SEQ
    first_tile = seq_tile == 0

    pos = seq_tile * ROW_TILE + lax.broadcasted_iota(jnp.int32, (ROW_TILE, 1), 0)
    pooled = []
    for g, w in enumerate(POOL_WINDOWS):
        cols = slice(g * POOL_GROUP, (g + 1) * POOL_GROUP)
        cur = u_c[POOL_PAD:POOL_PAD + ROW_TILE, cols]
        acc = cur
        for j in range(1, w):
            acc = acc + u_c[POOL_PAD - j:POOL_PAD - j + ROW_TILE, cols]
        cnt = jnp.minimum(pos + 1, w).astype(F32)
        pooled.append(acc / cnt - cur)
    pool_ref[...] = jnp.concatenate(pooled, axis=1).astype(BF16)

    kall = k_c[...]
    vall = v_c[...]
    kall_sw = _swap_halves(kall)
    vall_sw = _swap_halves(vall)
    for blk in range(ROW_TILE // WINDOW):
        r0 = blk * WINDOW
        span = slice(r0, r0 + KEY_SPAN)
        bias = bias_ref[jnp.where(first_tile, 1, 0)] if blk == 0 else bias_ref[0]
        for kh in range(N_KV_HEADS):
            c0 = 2 * kh * LANES
            q_pairs = jnp.concatenate([q_c[r0:r0 + WINDOW, c0:c0 + LANES],
                                       q_c[r0:r0 + WINDOW, c0 + LANES:c0 + 2 * LANES]], axis=0)
            o = _attend_kv_head(q_pairs, kall[span], kall_sw[span], vall[span], vall_sw[span],
                                kh, bias, sinks_ref)
            attn_ref[r0:r0 + WINDOW, c0:c0 + LANES] = o[:WINDOW].astype(BF16)
            attn_ref[r0:r0 + WINDOW, c0 + LANES:c0 + 2 * LANES] = o[WINDOW:].astype(BF16)

    @pl.when(seq_tile == TILES_PER_SEQ - 1)
    def _():
        kt_ref[0] = kall[ROW_TILE:ROW_TILE + WINDOW].T
        vt_ref[0] = vall[ROW_TILE:ROW_TILE + WINDOW].T
        ulast_ref[0] = u_c[ROW_TILE:ROW_TILE + POOL_PAD, :]


def _prompt_bias():
    r = np.arange(2 * WINDOW)[:, None] % WINDOW
    c = np.arange(2 * KEY_SPAN)[None, :] % KEY_SPAN
    valid = (r <= c) & (c <= r + WINDOW)
    first = valid & (c >= WINDOW)
    return np.where(np.stack([valid, first]), 0.0, NEG_INF).astype(np.float32)


def _front_kernel(sinks_ref, xp_ref, xs_ref, w1_ref, w3_ref, w2_ref, wu_ref, g_ref, b_ref, cos_ref, sin_ref,
                  bias_ref,
                  h_ref, pool_ref, attn_ref, us_ref, qs_ref, ks_ref, vs_ref, kt_ref, vt_ref, ulast_ref,
                  q_c, k_c, v_c, u_c):
    r = pl.program_id(0)

    @pl.when(r == 0)
    def _():
        q_c[...] = jnp.zeros(q_c.shape, q_c.dtype)
        k_c[...] = jnp.zeros(k_c.shape, k_c.dtype)
        v_c[...] = jnp.zeros(v_c.shape, v_c.dtype)
        u_c[...] = jnp.zeros(u_c.shape, u_c.dtype)

    def tile_step(x_ref, is_sample):
        _prompt_context(r - 1, q_c, k_c, v_c, u_c, bias_ref, sinks_ref,
                        pool_ref, attn_ref, kt_ref, vt_ref, ulast_ref)

        h1 = _swiglu_ln(x_ref[...], w1_ref, w3_ref, w2_ref, g_ref, b_ref)
        h_ref[...] = h1
        z = _dot(h1.astype(BF16), wu_ref[...])
        cos = cos_ref[...]
        sin_signed = sin_ref[...]
        lane = lax.broadcasted_iota(jnp.int32, cos.shape, 1)
        first_half = (lane & (HEAD_DIM // 2)) == 0
        u = z[:, :POOL_WIDTH]
        q_out, u_out, k_out, v_out = (qs_ref, us_ref, ks_ref, vs_ref) if is_sample else (q_c, None, None, None)
        for c in range(Q_WIDTH // LANES):
            lo = POOL_WIDTH + c * LANES
            qc = _rope(z[:, lo:lo + LANES], cos, sin_signed, first_half)
            q_out[:, c * LANES:(c + 1) * LANES] = (qc * (HEAD_DIM ** -0.5)).astype(BF16)
        ko = POOL_WIDTH + Q_WIDTH
        k = _rope(z[:, ko:ko + KV_WIDTH], cos, sin_signed, first_half)
        v = z[:, ko + KV_WIDTH:ko + 2 * KV_WIDTH]
        if is_sample:
            u_out[...] = u
            k_out[...] = k
            v_out[...] = v
        else:
            k_c[0:WINDOW, :] = k_c[ROW_TILE:ROW_TILE + WINDOW, :]
            v_c[0:WINDOW, :] = v_c[ROW_TILE:ROW_TILE + WINDOW, :]
            this_starts_seq = r % TILES_PER_SEQ == 0
            u_c[0:POOL_PAD, :] = jnp.where(this_starts_seq, 0.0, u_c[ROW_TILE:ROW_TILE + POOL_PAD, :])
            k_c[WINDOW:WINDOW + ROW_TILE, :] = k
            v_c[WINDOW:WINDOW + ROW_TILE, :] = v
            u_c[POOL_PAD:POOL_PAD + ROW_TILE, :] = u

    @pl.when(r < PROMPT_TILES)
    def _():
        tile_step(xp_ref, False)

    @pl.when(r == PROMPT_TILES)
    def _():
        tile_step(xs_ref, True)


def _front(xp, xs, w1, w3, w2, wu, g, b, cos, sin_signed, sinks):
    rows = ROW_TILES * ROW_TILE

    def lagged(width):
        return pl.BlockSpec((ROW_TILE, width), lambda r: (jnp.clip(r - 1, 0, PROMPT_TILES - 1), 0))

    def seq_of_lagged(shape):
        return pl.BlockSpec(shape, lambda r: (jnp.clip(r - 1, 0, PROMPT_TILES - 1) // TILES_PER_SEQ, 0, 0))

    def table_index(r):
        return (jnp.where(r == PROMPT_TILES, TILES_PER_SEQ, r % TILES_PER_SEQ), 0)

    table_spec = pl.BlockSpec((ROW_TILE, LANES), table_index)
    sample = lambda width: pl.BlockSpec((ROW_TILE, width), lambda r: (0, 0))
    return pl.pallas_call(
        _front_kernel,
        out_shape=(jax.ShapeDtypeStruct((rows, D_MODEL), F32),
                   jax.ShapeDtypeStruct((PROMPT_ROWS, POOL_WIDTH), BF16),
                   jax.ShapeDtypeStruct((PROMPT_ROWS, Q_WIDTH), BF16),
                   jax.ShapeDtypeStruct((SAMPLE_ROWS, POOL_WIDTH), F32),
                   jax.ShapeDtypeStruct((SAMPLE_ROWS, Q_WIDTH), BF16),
                   jax.ShapeDtypeStruct((SAMPLE_ROWS, KV_WIDTH), F32),
                   jax.ShapeDtypeStruct((SAMPLE_ROWS, KV_WIDTH), F32),
                   jax.ShapeDtypeStruct((BATCH, KV_WIDTH, WINDOW), F32),
                   jax.ShapeDtypeStruct((BATCH, KV_WIDTH, WINDOW), F32),
                   jax.ShapeDtypeStruct((BATCH, POOL_PAD, POOL_WIDTH), F32)),
        grid=(ROW_TILES,),
        in_specs=[pl.BlockSpec(memory_space=pltpu.SMEM),
                  pl.BlockSpec((ROW_TILE, D_MODEL), lambda r: (jnp.minimum(r, PROMPT_TILES - 1), 0)),
                  _resident((SAMPLE_ROWS, D_MODEL)),
                  _resident((D_MODEL, D_FF)), _resident((D_MODEL, D_FF)), _resident((D_FF, D_MODEL)),
                  _resident((D_MODEL, UQKV_WIDTH)), _resident((1, D_MODEL)), _resident((1, D_MODEL)),
                  table_spec, table_spec, _resident((2, 2 * WINDOW, 2 * KEY_SPAN))],
        out_specs=(pl.BlockSpec((ROW_TILE, D_MODEL), lambda r: (r, 0)),
                   lagged(POOL_WIDTH), lagged(Q_WIDTH),
                   sample(POOL_WIDTH), sample(Q_WIDTH), sample(KV_WIDTH), sample(KV_WIDTH),
                   seq_of_lagged((1, KV_WIDTH, WINDOW)), seq_of_lagged((1, KV_WIDTH, WINDOW)),
                   seq_of_lagged((1, POOL_PAD, POOL_WIDTH))),
        scratch_shapes=[pltpu.VMEM((ROW_TILE, Q_WIDTH), BF16),
                        pltpu.VMEM((WINDOW + ROW_TILE, KV_WIDTH), F32),
                        pltpu.VMEM((WINDOW + ROW_TILE, KV_WIDTH), F32),
                        pltpu.VMEM((POOL_PAD + ROW_TILE, POOL_WIDTH), F32)],
        compiler_params=pltpu.CompilerParams(dimension_semantics=("arbitrary",),
                                             vmem_limit_bytes=VMEM_LIMIT_BYTES),
        name="front",
    )(sinks, xp, xs, w1, w3, w2, wu, g, b, cos, sin_signed, jnp.asarray(_prompt_bias()))


def _sample_bias():
    row = np.arange(HEADS_PER_KV * DEC_SEQ * SUB_GROUP)
    row_t = (row // SUB_GROUP) % DEC_SEQ
    row_b = row % SUB_GROUP
    col = np.arange(SUB_GROUP * WINDOW)
    ok_c = (col[None, :] // WINDOW == row_b[:, None]) & (col[None, :] % WINDOW >= row_t[:, None])
    new = np.arange(DEC_SEQ * SUB_GROUP)
    ok_n = (new[None, :] % SUB_GROUP == row_b[:, None]) & (new[None, :] // SUB_GROUP <= row_t[:, None])
    to_bias = lambda ok: np.where(ok, 0.0, NEG_INF).astype(np.float32)
    return to_bias(ok_c), to_bias(ok_n)


def _sample_ctx_kernel(sinks_ref, q0, q1, q2, q3, k0, k1, k2, k3, v0, v1, v2, v3, u0, u1, u2, u3,
                       ckt_ref, cvt_ref, cu_ref, bias_c_ref, bias_n_ref,
                       attn_ref, pool_ref, kt_out, vt_out, pu_out, kbt_s, vbt_s):
    q_t = [q[...].astype(F32) for q in (q0, q1, q2, q3)]
    k_t = [k[...] for k in (k0, k1, k2, k3)]
    v_t = [v[...] for v in (v0, v1, v2, v3)]

    rows_u = [cu_ref[i] for i in range(POOL_BUF)] + [u[...] for u in (u0, u1, u2, u3)]
    for t in range(DEC_SEQ):
        pooled = []
        for g, w in enumerate(POOL_WINDOWS):
            cols = slice(g * POOL_GROUP, (g + 1) * POOL_GROUP)
            cur = rows_u[POOL_BUF + t][:, cols]
            acc = cur
            for j in range(1, w):
                acc = acc + rows_u[POOL_BUF + t - j][:, cols]
            pooled.append(acc / float(w) - cur)
        pool_ref[t] = jnp.concatenate(pooled, axis=1)
    for i in range(POOL_BUF):
        pu_out[i] = rows_u[i + DEC_SEQ]

    bias_c = bias_c_ref[...]
    bias_n = bias_n_ref[...]
    head_of_row = lax.broadcasted_iota(jnp.int32, (bias_c.shape[0], 1), 0) // (DEC_SEQ * SUB_GROUP)
    low = lax.broadcasted_iota(jnp.int32, (1, LANES), 1) < HEAD_DIM
    q_sw = [_swap_halves_wide(q) for q in q_t]
    k_sw = [_swap_halves(k) for k in k_t]
    v_sw = [_swap_halves(v) for v in v_t]
    for sub in range(SEQ_GROUP // SUB_GROUP):
        rows = slice(sub * SUB_GROUP, (sub + 1) * SUB_GROUP)
        for kh in range(N_KV_HEADS):
            def q_piece(t, head):
                src = q_t[t] if head % 2 == 0 else q_sw[t]
                chunk = head // 2
                return jnp.where(low, src[rows, chunk * LANES:(chunk + 1) * LANES], 0.0)

            def kv_first(c):
                return c if kh == 0 else jnp.concatenate([c[HEAD_DIM:], c[:HEAD_DIM]], axis=0)

            def kv_twice(c):
                part = c[kh * HEAD_DIM:(kh + 1) * HEAD_DIM]
                return jnp.concatenate([part, part], axis=0)

            lhs = jnp.concatenate([q_piece(t, kh * HEADS_PER_KV + g)
                                   for g in range(HEADS_PER_KV) for t in range(DEC_SEQ)], axis=0).astype(BF16)
            kcat = jnp.concatenate([kv_first(ckt_ref[sub * SUB_GROUP + b]) for b in range(SUB_GROUP)],
                                   axis=1).astype(BF16)
            vcat = jnp.concatenate([kv_twice(cvt_ref[sub * SUB_GROUP + b]) for b in range(SUB_GROUP)],
                                   axis=1).astype(BF16)
            knew = jnp.concatenate([(k_t[t] if kh == 0 else k_sw[t])[rows] for t in range(DEC_SEQ)],
                                   axis=0).astype(BF16)
            vnew = jnp.concatenate([(jnp.where(low, v_t[t], v_sw[t]) if kh == 0 else
                                     jnp.where(low, v_sw[t], v_t[t]))[rows] for t in range(DEC_SEQ)],
                                   axis=0).astype(BF16)
            s_c = _dot(lhs, kcat) + bias_c
            s_n = _dot_nt(lhs, knew) + bias_n
            sink = jnp.zeros(head_of_row.shape, F32)
            for g in range(HEADS_PER_KV):
                sink = jnp.where(head_of_row == g, sinks_ref[kh * HEADS_PER_KV + g], sink)
            m = jnp.maximum(jnp.maximum(jnp.max(s_c, axis=-1, keepdims=True),
                                        jnp.max(s_n, axis=-1, keepdims=True)), sink)
            p_c = jnp.exp(s_c - m)
            p_n = jnp.exp(s_n - m)
            denom = (jnp.sum(p_c, axis=-1, keepdims=True) + jnp.sum(p_n, axis=-1, keepdims=True)
                     + jnp.exp(sink - m))
            o = (_dot_nt(p_c.astype(BF16), vcat) + _dot(p_n.astype(BF16), vnew)) / denom
            for t in range(DEC_SEQ):
                for pair in range(HEADS_PER_KV // 2):
                    piece = lambda g: o[(g * DEC_SEQ + t) * SUB_GROUP:(g * DEC_SEQ + t + 1) * SUB_GROUP]
                    c0 = (kh * HEADS_PER_KV // 2 + pair) * LANES
                    attn_ref[t, rows, c0:c0 + LANES] = jnp.where(low, piece(2 * pair), piece(2 * pair + 1))

    zeros = jnp.zeros((LANES - DEC_SEQ * SEQ_GROUP, KV_WIDTH), F32)
    kbt_s[DEC_SEQ * SEQ_GROUP:, :] = zeros
    vbt_s[DEC_SEQ * SEQ_GROUP:, :] = zeros
    for t in range(DEC_SEQ):
        kbt_s[pl.ds(t, SEQ_GROUP, stride=DEC_SEQ), :] = k_t[t]
        vbt_s[pl.ds(t, SEQ_GROUP, stride=DEC_SEQ), :] = v_t[t]
    knew_t = kbt_s[...].T
    vnew_t = vbt_s[...].T
    keep = lax.broadcasted_iota(jnp.int32, (1, WINDOW), 1) < WINDOW - DEC_SEQ
    for b in range(SEQ_GROUP):
        shift_new = WINDOW - DEC_SEQ - DEC_SEQ * b
        kt_out[b] = jnp.where(keep, pltpu.roll(ckt_ref[b], WINDOW - DEC_SEQ, axis=1),
                              pltpu.roll(knew_t, shift_new, axis=1))
        vt_out[b] = jnp.where(keep, pltpu.roll(cvt_ref[b], WINDOW - DEC_SEQ, axis=1),
                              pltpu.roll(vnew_t, shift_new, axis=1))


def _sample_ctx(us, qs, ks, vs, ckt, cvt, cu, sinks):
    groups = DEC_BATCH // SEQ_GROUP

    def token_rows(t, width):
        return pl.BlockSpec((SEQ_GROUP, width), lambda i: (t * groups + i, 0))

    def per_token(width):
        return [token_rows(t, width) for t in range(DEC_SEQ)]

    cache_spec = pl.BlockSpec((SEQ_GROUP, KV_WIDTH, WINDOW), lambda i: (i, 0, 0))
    pool_rows_spec = pl.BlockSpec((POOL_BUF, SEQ_GROUP, POOL_WIDTH), lambda i: (0, i, 0))
    by_token = lambda width: pl.BlockSpec((DEC_SEQ, SEQ_GROUP, width), lambda i: (0, i, 0))
    bias_c, bias_n = _sample_bias()
    return pl.pallas_call(
        _sample_ctx_kernel,
        out_shape=(jax.ShapeDtypeStruct((DEC_SEQ, DEC_BATCH, Q_WIDTH), F32),
                   jax.ShapeDtypeStruct((DEC_SEQ, DEC_BATCH, POOL_WIDTH), F32),
                   jax.ShapeDtypeStruct((DEC_BATCH, KV_WIDTH, WINDOW), F32),
                   jax.ShapeDtypeStruct((DEC_BATCH, KV_WIDTH, WINDOW), F32),
                   jax.ShapeDtypeStruct((POOL_BUF, DEC_BATCH, POOL_WIDTH), F32)),
        grid=(groups,),
        in_specs=[pl.BlockSpec(memory_space=pltpu.SMEM)]
                 + per_token(Q_WIDTH) + per_token(KV_WIDTH) + per_token(KV_WIDTH) + per_token(POOL_WIDTH)
                 + [cache_spec, cache_spec, pool_rows_spec, _resident(bias_c.shape), _resident(bias_n.shape)],
        out_specs=(by_token(Q_WIDTH), by_token(POOL_WIDTH), cache_spec, cache_spec, pool_rows_spec),
        scratch_shapes=[pltpu.VMEM((LANES, KV_WIDTH), F32), pltpu.VMEM((LANES, KV_WIDTH), F32)],
        compiler_params=pltpu.CompilerParams(dimension_semantics=("parallel",),
                                             vmem_limit_bytes=VMEM_LIMIT_BYTES),
        name="sample_ctx",
    )(sinks, *([qs] * DEC_SEQ), *([ks] * DEC_SEQ), *([vs] * DEC_SEQ), *([us] * DEC_SEQ),
      ckt, cvt, cu, jnp.asarray(bias_c), jnp.asarray(bias_n))


def _back_kernel(h_ref, poolp_ref, attnp_ref, pools_ref, attns_ref,
                 wg_ref, wgrp_ref, scale_ref, wpo_ref, wao_ref, wout_ref, g2_ref, b2_ref,
                 w1_ref, w3_ref, w2_ref, g3_ref, b3_ref, yp_ref, ys_ref):
    r = pl.program_id(0)
    is_sample = r == PROMPT_TILES
    h1 = h_ref[...]
    pool_in = jnp.where(is_sample, pools_ref[...].astype(BF16), poolp_ref[...])
    attn_o = jnp.where(is_sample, attns_ref[...].astype(BF16), attnp_ref[...])
    gates = jax.nn.sigmoid(_dot(h1.astype(BF16), wg_ref[...]))
    zs = [_dot(pool_in[:, g * POOL_GROUP:(g + 1) * POOL_GROUP], wgrp_ref[g])
          for g in range(len(POOL_WINDOWS))]
    pool_z = jnp.concatenate(zs, axis=1) * scale_ref[...]
    a = _dot(pool_z.astype(BF16), wpo_ref[...])
    b = _dot(attn_o, wao_ref[...])
    m = gates[:, :D_MODEL] * a + gates[:, D_MODEL:] * b
    h2 = _layer_norm(ALPHA * h1 + _dot(m.astype(BF16), wout_ref[...]), g2_ref[...], b2_ref[...])
    y = _swiglu_ln(h2, w1_ref, w3_ref, w2_ref, g3_ref, b3_ref)

    @pl.when(r < PROMPT_TILES)
    def _():
        yp_ref[...] = y

    @pl.when(is_sample)
    def _():
        ys_ref[...] = y


def _back(h1, pool_p, attn_p, pool_s, attn_s, wg, wgrp, scale, wpo, wao, wout, g2, b2, w1, w3, w2, g3, b3):
    prompt = lambda width: pl.BlockSpec((ROW_TILE, width), lambda r: (jnp.minimum(r, PROMPT_TILES - 1), 0))
    return pl.pallas_call(
        _back_kernel,
        out_shape=(jax.ShapeDtypeStruct((PROMPT_ROWS, D_MODEL), F32),
                   jax.ShapeDtypeStruct((SAMPLE_ROWS, D_MODEL), F32)),
        grid=(ROW_TILES,),
        in_specs=[pl.BlockSpec((ROW_TILE, D_MODEL), lambda r: (r, 0)),
                  prompt(POOL_WIDTH), prompt(Q_WIDTH),
                  _resident((SAMPLE_ROWS, POOL_WIDTH)), _resident((SAMPLE_ROWS, Q_WIDTH)),
                  _resident((D_MODEL, 2 * D_MODEL)),
                  _resident((len(POOL_WINDOWS), POOL_GROUP, POOL_GROUP)), _resident((1, POOL_WIDTH)),
                  _resident((POOL_WIDTH, D_MODEL)), _resident((Q_WIDTH, D_MODEL)), _resident((D_MODEL, D_MODEL)),
                  _resident((1, D_MODEL)), _resident((1, D_MODEL)),
                  _resident((D_MODEL, D_FF)), _resident((D_MODEL, D_FF)), _resident((D_FF, D_MODEL)),
                  _resident((1, D_MODEL)), _resident((1, D_MODEL))],
        out_specs=(prompt(D_MODEL), pl.BlockSpec((ROW_TILE, D_MODEL), lambda r: (0, 0))),
        compiler_params=pltpu.CompilerParams(dimension_semantics=("arbitrary",),
                                             vmem_limit_bytes=VMEM_LIMIT_BYTES),
        name="back",
    )(h1, pool_p, attn_p, pool_s, attn_s, wg, wgrp, scale, wpo, wao, wout, g2, b2, w1, w3, w2, g3, b3)


def _rope_tables(pos):
    half = HEAD_DIM // 2
    freqs = ROPE_THETA ** (-2.0 * jnp.arange(half, dtype=F32) / HEAD_DIM)
    ang = pos.astype(F32)[:, None] * freqs[None, :]
    cos, sin = jnp.cos(ang), jnp.sin(ang)
    return jnp.tile(cos, (1, 4)), jnp.concatenate([-sin, sin, -sin, sin], axis=1)


def kernel(x_prompt, x_sample, cache_pool_u, cache_k_win, cache_v_win, w_in, pool_w_grp, pool_scale,
           attn_sinks, w_pool_out, w_attn_out, w_out, ffn1_w1, ffn1_w3, ffn1_w2, ffn2_w1, ffn2_w3,
           ffn2_w2, ln1_g, ln1_b, ln2_g, ln2_b, ln3_g, ln3_b):
    assert DEPTH == 1 and w_in.shape[0] == 1
    l = 0
    bf = lambda w: w.astype(BF16)
    vec = lambda p: p[l].reshape(1, -1)
    sinks = attn_sinks[l]

    pos = jnp.concatenate([jnp.arange(SEQ, dtype=jnp.int32),
                           jnp.repeat(PAST_LEN + jnp.arange(DEC_SEQ, dtype=jnp.int32), DEC_BATCH)])
    cos, sin_signed = _rope_tables(pos)

    xp = x_prompt.reshape(PROMPT_ROWS, D_MODEL)
    xs = jnp.transpose(x_sample, (1, 0, 2)).reshape(SAMPLE_ROWS, D_MODEL)
    (h1, pool_p, attn_p, us, qs, ks, vs, kt_last, vt_last, u_last) = _front(
        xp, xs, bf(ffn1_w1[l]), bf(ffn1_w3[l]), bf(ffn1_w2[l]), bf(w_in[l][:, :UQKV_WIDTH]),
        vec(ln1_g), vec(ln1_b), cos, sin_signed, sinks)

    to_t = lambda c: jnp.transpose(c[l], (0, 2, 3, 1)).reshape(DEC_BATCH, KV_WIDTH, WINDOW)
    cu = jnp.transpose(cache_pool_u[l], (1, 0, 2))
    attn_s, pool_s, kt_s, vt_s, pu_s = _sample_ctx(us, qs, ks, vs, to_t(cache_k_win), to_t(cache_v_win), cu, sinks)

    yp, ys = _back(h1, pool_p, attn_p, pool_s.reshape(SAMPLE_ROWS, POOL_WIDTH), attn_s.reshape(SAMPLE_ROWS, Q_WIDTH),
                   bf(w_in[l][:, UQKV_WIDTH:]), bf(pool_w_grp[l]), vec(pool_scale), bf(w_pool_out[l]),
                   bf(w_attn_out[l]), bf(w_out[l]), vec(ln2_g), vec(ln2_b),
                   bf(ffn2_w1[l]), bf(ffn2_w3[l]), bf(ffn2_w2[l]), vec(ln3_g), vec(ln3_b))
    yp = yp.reshape(BATCH, SEQ, D_MODEL)
    ys = jnp.transpose(ys.reshape(DEC_SEQ, DEC_BATCH, D_MODEL), (1, 0, 2))

    from_t = lambda c, n: jnp.transpose(c.reshape(n, N_KV_HEADS, HEAD_DIM, WINDOW), (0, 3, 1, 2))[None]
    pool_u_prompt = u_last[None, :, POOL_PAD - POOL_BUF:]
    pool_u_sample = jnp.transpose(pu_s, (1, 0, 2))[None]
    return (yp, ys, pool_u_prompt, from_t(kt_last, BATCH), from_t(vt_last, BATCH),
            pool_u_sample, from_t(kt_s, DEC_BATCH), from_t(vt_s, DEC_BATCH))
```

```python
import jax
import jax.numpy as jnp
import numpy as np
from jax import lax
from jax.experimental import pallas as pl
from jax.experimental.pallas import tpu as pltpu

D_MODEL = 1024
BATCH = 8
SEQ = 2048
DEC_BATCH = 128
DEC_SEQ = 4
PAST_LEN = 8192
POOL_WINDOWS = (2, 4, 8, 16)
POOL_GROUP = 128
POOL_WIDTH = 512
POOL_BUF = 15
N_HEADS = 8
N_KV_HEADS = 2
HEADS_PER_KV = N_HEADS // N_KV_HEADS
HEAD_DIM = 64
Q_WIDTH = 512
KV_WIDTH = 128
WINDOW = 128
ROPE_THETA = 10000.0
D_FF = 2816
DEPTH = 1
ALPHA = (2.0 * DEPTH) ** 0.25
LN_EPS = 1e-5
NEG_INF = -1e30
UQKV_WIDTH = POOL_WIDTH + Q_WIDTH + 2 * KV_WIDTH

LANES = 128
KEY_SPAN = 2 * WINDOW
VMEM_LIMIT_BYTES = 58 * 1024 * 1024

ROW_TILE = 512
TILES_PER_SEQ = SEQ // ROW_TILE
PROMPT_ROWS = BATCH * SEQ
PROMPT_TILES = PROMPT_ROWS // ROW_TILE
SAMPLE_ROWS = DEC_BATCH * DEC_SEQ
ROW_TILES = PROMPT_TILES + 1
POOL_PAD = 16
SEQ_GROUP = 16
SUB_GROUP = 8
FF_CHUNK = 256

BF16 = jnp.bfloat16
F32 = jnp.float32


def _dot(a, b):
    return jnp.dot(a, b, preferred_element_type=F32)


def _dot_nt(a, b):
    return lax.dot_general(a, b, (((1,), (1,)), ((), ())), preferred_element_type=F32)


def _layer_norm(y, g, b):
    mu = jnp.mean(y, axis=-1, keepdims=True)
    yc = y - mu
    var = jnp.mean(yc * yc, axis=-1, keepdims=True)
    return yc * lax.rsqrt(var + LN_EPS) * g + b


def _resident(shape):
    nd = len(shape)
    return pl.BlockSpec(shape, lambda *_: (0,) * nd, pipeline_mode=pl.Buffered(1))


def _swiglu(x, w1_ref, w3_ref, w2_ref, h_s, side_work=()):
    xb = x.astype(BF16)
    side = list(side_work)
    chunks = D_FF // FF_CHUNK
    for j in range(chunks):
        cols = slice(j * FF_CHUNK, (j + 1) * FF_CHUNK)
        a = _dot(xb, w1_ref[:, cols])
        b = _dot(xb, w3_ref[:, cols])
        h_s[:, cols] = ((a * jax.nn.sigmoid(a)) * b).astype(BF16)
        if j < len(side):
            side[j]()
    for work in side[chunks:]:
        work()
    return _dot(h_s[...], w2_ref[...])


def _swiglu_ln(x, w1_ref, w3_ref, w2_ref, g_ref, b_ref, h_s):
    return _layer_norm(ALPHA * x + 0.5 * _swiglu(x, w1_ref, w3_ref, w2_ref, h_s), g_ref[...], b_ref[...])


def _rope(x, cos, sin_signed, first_half):
    fwd = pltpu.roll(x, LANES - HEAD_DIM // 2, axis=1)
    bwd = pltpu.roll(x, HEAD_DIM // 2, axis=1)
    return x * cos + jnp.where(first_half, fwd, bwd) * sin_signed


def _swap_halves(x):
    return pltpu.roll(x, HEAD_DIM, axis=1)


def _swap_halves_wide(x):
    return jnp.concatenate([_swap_halves(x[:, c:c + LANES]) for c in range(0, x.shape[1], LANES)], axis=1)


def _attend_kv_head(q_pairs, keys, keys_sw, vals, vals_sw, kh, bias, sinks_ref):
    low = lax.broadcasted_iota(jnp.int32, (1, LANES), 1) < HEAD_DIM
    if kh == 0:
        k_lo, k_hi, v_lo, v_hi = keys, keys_sw, vals, vals_sw
    else:
        k_lo, k_hi, v_lo, v_hi = keys_sw, keys, vals_sw, vals
    kcat = jnp.concatenate([jnp.where(low, k_lo, 0.0), jnp.where(low, 0.0, k_hi)], axis=0).astype(BF16)
    vcat = jnp.concatenate([jnp.where(low, v_lo, 0.0), jnp.where(low, 0.0, v_hi)], axis=0).astype(BF16)
    s = _dot_nt(q_pairs, kcat) + bias
    second_pair = lax.broadcasted_iota(jnp.int32, (s.shape[0], 1), 0) >= WINDOW
    probs, denoms = [], []
    for c in range(2):
        sc = s[:, c * KEY_SPAN:(c + 1) * KEY_SPAN]
        head = kh * HEADS_PER_KV + c
        sink = jnp.where(second_pair, sinks_ref[head + 2], sinks_ref[head])
        m = jnp.maximum(jnp.max(sc, axis=-1, keepdims=True), sink)
        p = jnp.exp(sc - m)
        denoms.append(jnp.sum(p, axis=-1, keepdims=True) + jnp.exp(sink - m))
        probs.append(p.astype(BF16))
    o = _dot(jnp.concatenate(probs, axis=1), vcat)
    return o / jnp.where(low, denoms[0], denoms[1])


def _prompt_context_work(tile, q_c, k_c, v_c, u_c, bias_ref, sinks_ref,
                         pool_ref, attn_ref, kt_ref, vt_ref, ulast_ref):
    seq_tile = (tile + TILES_PER_SEQ) % TILES_PER_SEQ
    first_tile = seq_tile == 0

    def attend(blk, kh):
        r0 = blk * WINDOW
        span = slice(r0, r0 + KEY_SPAN)
        bias = bias_ref[jnp.where(first_tile, 1, 0)] if blk == 0 else bias_ref[0]
        c0 = 2 * kh * LANES
        keys, vals = k_c[span, :], v_c[span, :]
        q_pairs = jnp.concatenate([q_c[r0:r0 + WINDOW, c0:c0 + LANES],
                                   q_c[r0:r0 + WINDOW, c0 + LANES:c0 + 2 * LANES]], axis=0)
        o = _attend_kv_head(q_pairs, keys, _swap_halves(keys), vals, _swap_halves(vals), kh, bias, sinks_ref)
        attn_ref[r0:r0 + WINDOW, c0:c0 + LANES] = o[:WINDOW].astype(BF16)
        attn_ref[r0:r0 + WINDOW, c0 + LANES:c0 + 2 * LANES] = o[WINDOW:].astype(BF16)

    def pool(groups):
        pos = seq_tile * ROW_TILE + lax.broadcasted_iota(jnp.int32, (ROW_TILE, 1), 0)
        for g in groups:
            w = POOL_WINDOWS[g]
            cols = slice(g * POOL_GROUP, (g + 1) * POOL_GROUP)
            cur = u_c[POOL_PAD:POOL_PAD + ROW_TILE, cols]
            acc = cur
            for j in range(1, w):
                acc = acc + u_c[POOL_PAD - j:POOL_PAD - j + ROW_TILE, cols]
            cnt = jnp.minimum(pos + 1, w).astype(F32)
            pool_ref[:, cols] = (acc / cnt - cur).astype(BF16)

    def sequence_state():
        kt_ref[0] = k_c[ROW_TILE:ROW_TILE + WINDOW, :].T
        vt_ref[0] = v_c[ROW_TILE:ROW_TILE + WINDOW, :].T
        ulast_ref[0] = u_c[ROW_TILE:ROW_TILE + POOL_PAD, :]

    work = [lambda blk=blk, kh=kh: attend(blk, kh)
            for blk in range(ROW_TILE // WINDOW) for kh in range(N_KV_HEADS)]
    work += [lambda: pool((3,)), lambda: pool((0, 1, 2)), sequence_state]
    return work


def _prompt_bias():
    r = np.arange(2 * WINDOW)[:, None] % WINDOW
    c = np.arange(2 * KEY_SPAN)[None, :] % KEY_SPAN
    valid = (r <= c) & (c <= r + WINDOW)
    first = valid & (c >= WINDOW)
    return np.where(np.stack([valid, first]), 0.0, NEG_INF).astype(np.float32)


def _front_kernel(sinks_ref, xp_ref, xs_ref, w1_ref, w3_ref, w2_ref, wu_ref, g_ref, b_ref, cos_ref, sin_ref,
                  bias_ref,
                  h_ref, pool_ref, attn_ref, us_ref, qs_ref, ks_ref, vs_ref, kt_ref, vt_ref, ulast_ref,
                  z_c, q_c, k_c, v_c, u_c, h_s):
    r = pl.program_id(0)
    prev = r - 1

    @pl.when(r == 0)
    def _():
        for ref in (z_c, q_c, k_c, v_c, u_c):
            ref[...] = jnp.zeros(ref.shape, ref.dtype)

    def project_previous():
        h1 = _layer_norm(z_c[...], g_ref[...], b_ref[...])
        h_ref[...] = h1
        z = _dot(h1.astype(BF16), wu_ref[...])
        cos = cos_ref[...]
        sin_signed = sin_ref[...]
        lane = lax.broadcasted_iota(jnp.int32, cos.shape, 1)
        first_half = (lane & (HEAD_DIM // 2)) == 0
        q = []
        for c in range(Q_WIDTH // LANES):
            lo = POOL_WIDTH + c * LANES
            qc = _rope(z[:, lo:lo + LANES], cos, sin_signed, first_half)
            q.append((qc * (HEAD_DIM ** -0.5)).astype(BF16))
        ko = POOL_WIDTH + Q_WIDTH
        k = _rope(z[:, ko:ko + KV_WIDTH], cos, sin_signed, first_half)
        return z[:, :POOL_WIDTH], q, k, z[:, ko + KV_WIDTH:ko + 2 * KV_WIDTH]

    def tile_step(x_ref):
        projected = []

        def project():
            projected.extend(project_previous())

        def carry():
            u, q, k, v = projected
            for c, qc in enumerate(q):
                q_c[:, c * LANES:(c + 1) * LANES] = qc
            k_c[0:WINDOW, :] = k_c[ROW_TILE:ROW_TILE + WINDOW, :]
            v_c[0:WINDOW, :] = v_c[ROW_TILE:ROW_TILE + WINDOW, :]
            starts_seq = (prev + TILES_PER_SEQ) % TILES_PER_SEQ == 0
            u_c[0:POOL_PAD, :] = jnp.where(starts_seq, 0.0, u_c[ROW_TILE:ROW_TILE + POOL_PAD, :])
            k_c[WINDOW:WINDOW + ROW_TILE, :] = k
            v_c[WINDOW:WINDOW + ROW_TILE, :] = v
            u_c[POOL_PAD:POOL_PAD + ROW_TILE, :] = u

        context = _prompt_context_work(prev, q_c, k_c, v_c, u_c, bias_ref, sinks_ref,
                                       pool_ref, attn_ref, kt_ref, vt_ref, ulast_ref)
        x = x_ref[...]
        y = _swiglu(x, w1_ref, w3_ref, w2_ref, h_s, [project, carry] + context)
        z_c[...] = ALPHA * x + 0.5 * y

    @pl.when(r < PROMPT_TILES)
    def _():
        tile_step(xp_ref)

    @pl.when(r == PROMPT_TILES)
    def _():
        tile_step(xs_ref)

    @pl.when(r == ROW_TILES)
    def _():
        u, q, k, v = project_previous()
        us_ref[...] = u
        for c, qc in enumerate(q):
            qs_ref[:, c * LANES:(c + 1) * LANES] = qc
        ks_ref[...] = k
        vs_ref[...] = v


def _front(xp, xs, w1, w3, w2, wu, g, b, cos, sin_signed, sinks):
    rows = ROW_TILES * ROW_TILE

    def lagged(width):
        return pl.BlockSpec((ROW_TILE, width), lambda r: (jnp.clip(r - 1, 0, PROMPT_TILES - 1), 0))

    def seq_of_lagged(shape):
        return pl.BlockSpec(shape, lambda r: (jnp.clip(r - 1, 0, PROMPT_TILES - 1) // TILES_PER_SEQ, 0, 0))

    def table_index(r):
        p = jnp.clip(r - 1, 0, PROMPT_TILES)
        return (jnp.where(p == PROMPT_TILES, TILES_PER_SEQ, p % TILES_PER_SEQ), 0)

    table_spec = pl.BlockSpec((ROW_TILE, LANES), table_index)
    sample = lambda width: pl.BlockSpec((ROW_TILE, width), lambda r: (0, 0))
    return pl.pallas_call(
        _front_kernel,
        out_shape=(jax.ShapeDtypeStruct((rows, D_MODEL), F32),
                   jax.ShapeDtypeStruct((PROMPT_ROWS, POOL_WIDTH), BF16),
                   jax.ShapeDtypeStruct((PROMPT_ROWS, Q_WIDTH), BF16),
                   jax.ShapeDtypeStruct((SAMPLE_ROWS, POOL_WIDTH), F32),
                   jax.ShapeDtypeStruct((SAMPLE_ROWS, Q_WIDTH), BF16),
                   jax.ShapeDtypeStruct((SAMPLE_ROWS, KV_WIDTH), F32),
                   jax.ShapeDtypeStruct((SAMPLE_ROWS, KV_WIDTH), F32),
                   jax.ShapeDtypeStruct((BATCH, KV_WIDTH, WINDOW), F32),
                   jax.ShapeDtypeStruct((BATCH, KV_WIDTH, WINDOW), F32),
                   jax.ShapeDtypeStruct((BATCH, POOL_PAD, POOL_WIDTH), F32)),
        grid=(ROW_TILES + 1,),
        in_specs=[pl.BlockSpec(memory_space=pltpu.SMEM),
                  pl.BlockSpec((ROW_TILE, D_MODEL), lambda r: (jnp.minimum(r, PROMPT_TILES - 1), 0)),
                  _resident((SAMPLE_ROWS, D_MODEL)),
                  _resident((D_MODEL, D_FF)), _resident((D_MODEL, D_FF)), _resident((D_FF, D_MODEL)),
                  _resident((D_MODEL, UQKV_WIDTH)), _resident((1, D_MODEL)), _resident((1, D_MODEL)),
                  table_spec, table_spec, _resident((2, 2 * WINDOW, 2 * KEY_SPAN))],
        out_specs=(pl.BlockSpec((ROW_TILE, D_MODEL), lambda r: (jnp.clip(r - 1, 0, PROMPT_TILES), 0)),
                   lagged(POOL_WIDTH), lagged(Q_WIDTH),
                   sample(POOL_WIDTH), sample(Q_WIDTH), sample(KV_WIDTH), sample(KV_WIDTH),
                   seq_of_lagged((1, KV_WIDTH, WINDOW)), seq_of_lagged((1, KV_WIDTH, WINDOW)),
                   seq_of_lagged((1, POOL_PAD, POOL_WIDTH))),
        scratch_shapes=[pltpu.VMEM((ROW_TILE, D_MODEL), F32),
                        pltpu.VMEM((ROW_TILE, Q_WIDTH), BF16),
                        pltpu.VMEM((WINDOW + ROW_TILE, KV_WIDTH), F32),
                        pltpu.VMEM((WINDOW + ROW_TILE, KV_WIDTH), F32),
                        pltpu.VMEM((POOL_PAD + ROW_TILE, POOL_WIDTH), F32),
                        pltpu.VMEM((ROW_TILE, D_FF), BF16)],
        compiler_params=pltpu.CompilerParams(dimension_semantics=("arbitrary",),
                                             vmem_limit_bytes=VMEM_LIMIT_BYTES),
        name="front",
    )(sinks, xp, xs, w1, w3, w2, wu, g, b, cos, sin_signed, jnp.asarray(_prompt_bias()))


def _sample_bias():
    row = np.arange(HEADS_PER_KV * DEC_SEQ * SUB_GROUP)
    row_t = (row // SUB_GROUP) % DEC_SEQ
    row_b = row % SUB_GROUP
    col = np.arange(SUB_GROUP * WINDOW)
    ok_c = (col[None, :] // WINDOW == row_b[:, None]) & (col[None, :] % WINDOW >= row_t[:, None])
    new = np.arange(DEC_SEQ * SUB_GROUP)
    ok_n = (new[None, :] % SUB_GROUP == row_b[:, None]) & (new[None, :] // SUB_GROUP <= row_t[:, None])
    to_bias = lambda ok: np.where(ok, 0.0, NEG_INF).astype(np.float32)
    return to_bias(ok_c), to_bias(ok_n)


def _sample_ctx_kernel(sinks_ref, q0, q1, q2, q3, k0, k1, k2, k3, v0, v1, v2, v3, u0, u1, u2, u3,
                       ckt_ref, cvt_ref, cu_ref, bias_c_ref, bias_n_ref,
                       attn_ref, pool_ref, kt_out, vt_out, pu_out, kbt_s, vbt_s):
    q_t = [q[...].astype(F32) for q in (q0, q1, q2, q3)]
    k_t = [k[...] for k in (k0, k1, k2, k3)]
    v_t = [v[...] for v in (v0, v1, v2, v3)]

    rows_u = [cu_ref[i] for i in range(POOL_BUF)] + [u[...] for u in (u0, u1, u2, u3)]
    for t in range(DEC_SEQ):
        pooled = []
        for g, w in enumerate(POOL_WINDOWS):
            cols = slice(g * POOL_GROUP, (g + 1) * POOL_GROUP)
            cur = rows_u[POOL_BUF + t][:, cols]
            acc = cur
            for j in range(1, w):
                acc = acc + rows_u[POOL_BUF + t - j][:, cols]
            pooled.append(acc / float(w) - cur)
        pool_ref[t] = jnp.concatenate(pooled, axis=1)
    for i in range(POOL_BUF):
        pu_out[i] = rows_u[i + DEC_SEQ]

    bias_c = bias_c_ref[...]
    bias_n = bias_n_ref[...]
    head_of_row = lax.broadcasted_iota(jnp.int32, (bias_c.shape[0], 1), 0) // (DEC_SEQ * SUB_GROUP)
    low = lax.broadcasted_iota(jnp.int32, (1, LANES), 1) < HEAD_DIM
    q_sw = [_swap_halves_wide(q) for q in q_t]
    k_sw = [_swap_halves(k) for k in k_t]
    v_sw = [_swap_halves(v) for v in v_t]
    for sub in range(SEQ_GROUP // SUB_GROUP):
        rows = slice(sub * SUB_GROUP, (sub + 1) * SUB_GROUP)
        for kh in range(N_KV_HEADS):
            def q_piece(t, head):
                src = q_t[t] if head % 2 == 0 else q_sw[t]
                chunk = head // 2
                return jnp.where(low, src[rows, chunk * LANES:(chunk + 1) * LANES], 0.0)

            def kv_first(c):
                return c if kh == 0 else jnp.concatenate([c[HEAD_DIM:], c[:HEAD_DIM]], axis=0)

            def kv_twice(c):
                part = c[kh * HEAD_DIM:(kh + 1) * HEAD_DIM]
                return jnp.concatenate([part, part], axis=0)

            lhs = jnp.concatenate([q_piece(t, kh * HEADS_PER_KV + g)
                                   for g in range(HEADS_PER_KV) for t in range(DEC_SEQ)], axis=0).astype(BF16)
            kcat = jnp.concatenate([kv_first(ckt_ref[sub * SUB_GROUP + b]) for b in range(SUB_GROUP)],
                                   axis=1).astype(BF16)
            vcat = jnp.concatenate([kv_twice(cvt_ref[sub * SUB_GROUP + b]) for b in range(SUB_GROUP)],
                                   axis=1).astype(BF16)
            knew = jnp.concatenate([(k_t[t] if kh == 0 else k_sw[t])[rows] for t in range(DEC_SEQ)],
                                   axis=0).astype(BF16)
            vnew = jnp.concatenate([(jnp.where(low, v_t[t], v_sw[t]) if kh == 0 else
                                     jnp.where(low, v_sw[t], v_t[t]))[rows] for t in range(DEC_SEQ)],
                                   axis=0).astype(BF16)
            s_c = _dot(lhs, kcat) + bias_c
            s_n = _dot_nt(lhs, knew) + bias_n
            sink = jnp.zeros(head_of_row.shape, F32)
            for g in range(HEADS_PER_KV):
                sink = jnp.where(head_of_row == g, sinks_ref[kh * HEADS_PER_KV + g], sink)
            m = jnp.maximum(jnp.maximum(jnp.max(s_c, axis=-1, keepdims=True),
                                        jnp.max(s_n, axis=-1, keepdims=True)), sink)
            p_c = jnp.exp(s_c - m)
            p_n = jnp.exp(s_n - m)
            denom = (jnp.sum(p_c, axis=-1, keepdims=True) + jnp.sum(p_n, axis=-1, keepdims=True)
                     + jnp.exp(sink - m))
            o = (_dot_nt(p_c.astype(BF16), vcat) + _dot(p_n.astype(BF16), vnew)) / denom
            for t in range(DEC_SEQ):
                for pair in range(HEADS_PER_KV // 2):
                    piece = lambda g: o[(g * DEC_SEQ + t) * SUB_GROUP:(g * DEC_SEQ + t + 1) * SUB_GROUP]
                    c0 = (kh * HEADS_PER_KV // 2 + pair) * LANES
                    attn_ref[t, rows, c0:c0 + LANES] = jnp.where(low, piece(2 * pair), piece(2 * pair + 1))

    zeros = jnp.zeros((LANES - DEC_SEQ * SEQ_GROUP, KV_WIDTH), F32)
    kbt_s[DEC_SEQ * SEQ_GROUP:, :] = zeros
    vbt_s[DEC_SEQ * SEQ_GROUP:, :] = zeros
    for t in range(DEC_SEQ):
        kbt_s[pl.ds(t, SEQ_GROUP, stride=DEC_SEQ), :] = k_t[t]
        vbt_s[pl.ds(t, SEQ_GROUP, stride=DEC_SEQ), :] = v_t[t]
    knew_t = kbt_s[...].T
    vnew_t = vbt_s[...].T
    keep = lax.broadcasted_iota(jnp.int32, (1, WINDOW), 1) < WINDOW - DEC_SEQ
    for b in range(SEQ_GROUP):
        shift_new = WINDOW - DEC_SEQ - DEC_SEQ * b
        kt_out[b] = jnp.where(keep, pltpu.roll(ckt_ref[b], WINDOW - DEC_SEQ, axis=1),
                              pltpu.roll(knew_t, shift_new, axis=1))
        vt_out[b] = jnp.where(keep, pltpu.roll(cvt_ref[b], WINDOW - DEC_SEQ, axis=1),
                              pltpu.roll(vnew_t, shift_new, axis=1))


def _sample_ctx(us, qs, ks, vs, ckt, cvt, cu, sinks):
    groups = DEC_BATCH // SEQ_GROUP

    def token_rows(t, width):
        return pl.BlockSpec((SEQ_GROUP, width), lambda i: (t * groups + i, 0))

    def per_token(width):
        return [token_rows(t, width) for t in range(DEC_SEQ)]

    cache_spec = pl.BlockSpec((SEQ_GROUP, KV_WIDTH, WINDOW), lambda i: (i, 0, 0))
    pool_rows_spec = pl.BlockSpec((POOL_BUF, SEQ_GROUP, POOL_WIDTH), lambda i: (0, i, 0))
    by_token = lambda width: pl.BlockSpec((DEC_SEQ, SEQ_GROUP, width), lambda i: (0, i, 0))
    bias_c, bias_n = _sample_bias()
    return pl.pallas_call(
        _sample_ctx_kernel,
        out_shape=(jax.ShapeDtypeStruct((DEC_SEQ, DEC_BATCH, Q_WIDTH), F32),
                   jax.ShapeDtypeStruct((DEC_SEQ, DEC_BATCH, POOL_WIDTH), F32),
                   jax.ShapeDtypeStruct((DEC_BATCH, KV_WIDTH, WINDOW), F32),
                   jax.ShapeDtypeStruct((DEC_BATCH, KV_WIDTH, WINDOW), F32),
                   jax.ShapeDtypeStruct((POOL_BUF, DEC_BATCH, POOL_WIDTH), F32)),
        grid=(groups,),
        in_specs=[pl.BlockSpec(memory_space=pltpu.SMEM)]
                 + per_token(Q_WIDTH) + per_token(KV_WIDTH) + per_token(KV_WIDTH) + per_token(POOL_WIDTH)
                 + [cache_spec, cache_spec, pool_rows_spec, _resident(bias_c.shape), _resident(bias_n.shape)],
        out_specs=(by_token(Q_WIDTH), by_token(POOL_WIDTH), cache_spec, cache_spec, pool_rows_spec),
        scratch_shapes=[pltpu.VMEM((LANES, KV_WIDTH), F32), pltpu.VMEM((LANES, KV_WIDTH), F32)],
        compiler_params=pltpu.CompilerParams(dimension_semantics=("parallel",),
                                             vmem_limit_bytes=VMEM_LIMIT_BYTES),
        name="sample_ctx",
    )(sinks, *([qs] * DEC_SEQ), *([ks] * DEC_SEQ), *([vs] * DEC_SEQ), *([us] * DEC_SEQ),
      ckt, cvt, cu, jnp.asarray(bias_c), jnp.asarray(bias_n))


def _back_kernel(h_ref, poolp_ref, attnp_ref, pools_ref, attns_ref,
                 wg_ref, wgrp_ref, scale_ref, wpo_ref, wao_ref, wout_ref, g2_ref, b2_ref,
                 w1_ref, w3_ref, w2_ref, g3_ref, b3_ref, yp_ref, ys_ref, h_s):
    r = pl.program_id(0)
    is_sample = r == PROMPT_TILES
    h1 = h_ref[...]
    pool_in = jnp.where(is_sample, pools_ref[...].astype(BF16), poolp_ref[...])
    attn_o = jnp.where(is_sample, attns_ref[...].astype(BF16), attnp_ref[...])
    gates = jax.nn.sigmoid(_dot(h1.astype(BF16), wg_ref[...]))
    zs = [_dot(pool_in[:, g * POOL_GROUP:(g + 1) * POOL_GROUP], wgrp_ref[g])
          for g in range(len(POOL_WINDOWS))]
    pool_z = jnp.concatenate(zs, axis=1) * scale_ref[...]
    a = _dot(pool_z.astype(BF16), wpo_ref[...])
    b = _dot(attn_o, wao_ref[...])
    m = gates[:, :D_MODEL] * a + gates[:, D_MODEL:] * b
    h2 = _layer_norm(ALPHA * h1 + _dot(m.astype(BF16), wout_ref[...]), g2_ref[...], b2_ref[...])
    y = _swiglu_ln(h2, w1_ref, w3_ref, w2_ref, g3_ref, b3_ref, h_s)

    @pl.when(r < PROMPT_TILES)
    def _():
        yp_ref[...] = y

    @pl.when(is_sample)
    def _():
        ys_ref[...] = y


def _back(h1, pool_p, attn_p, pool_s, attn_s, wg, wgrp, scale, wpo, wao, wout, g2, b2, w1, w3, w2, g3, b3):
    prompt = lambda width: pl.BlockSpec((ROW_TILE, width), lambda r: (jnp.minimum(r, PROMPT_TILES - 1), 0))
    return pl.pallas_call(
        _back_kernel,
        out_shape=(jax.ShapeDtypeStruct((PROMPT_ROWS, D_MODEL), F32),
                   jax.ShapeDtypeStruct((SAMPLE_ROWS, D_MODEL), F32)),
        grid=(ROW_TILES,),
        in_specs=[pl.BlockSpec((ROW_TILE, D_MODEL), lambda r: (r, 0)),
                  prompt(POOL_WIDTH), prompt(Q_WIDTH),
                  _resident((SAMPLE_ROWS, POOL_WIDTH)), _resident((SAMPLE_ROWS, Q_WIDTH)),
                  _resident((D_MODEL, 2 * D_MODEL)),
                  _resident((len(POOL_WINDOWS), POOL_GROUP, POOL_GROUP)), _resident((1, POOL_WIDTH)),
                  _resident((POOL_WIDTH, D_MODEL)), _resident((Q_WIDTH, D_MODEL)), _resident((D_MODEL, D_MODEL)),
                  _resident((1, D_MODEL)), _resident((1, D_MODEL)),
                  _resident((D_MODEL, D_FF)), _resident((D_MODEL, D_FF)), _resident((D_FF, D_MODEL)),
                  _resident((1, D_MODEL)), _resident((1, D_MODEL))],
        out_specs=(prompt(D_MODEL), pl.BlockSpec((ROW_TILE, D_MODEL), lambda r: (0, 0))),
        scratch_shapes=[pltpu.VMEM((ROW_TILE, D_FF), BF16)],
        compiler_params=pltpu.CompilerParams(dimension_semantics=("arbitrary",),
                                             vmem_limit_bytes=VMEM_LIMIT_BYTES),
        name="back",
    )(h1, pool_p, attn_p, pool_s, attn_s, wg, wgrp, scale, wpo, wao, wout, g2, b2, w1, w3, w2, g3, b3)


def _rope_tables(pos):
    half = HEAD_DIM // 2
    freqs = ROPE_THETA ** (-2.0 * jnp.arange(half, dtype=F32) / HEAD_DIM)
    ang = pos.astype(F32)[:, None] * freqs[None, :]
    cos, sin = jnp.cos(ang), jnp.sin(ang)
    return jnp.tile(cos, (1, 4)), jnp.concatenate([-sin, sin, -sin, sin], axis=1)


def kernel(x_prompt, x_sample, cache_pool_u, cache_k_win, cache_v_win, w_in, pool_w_grp, pool_scale,
           attn_sinks, w_pool_out, w_attn_out, w_out, ffn1_w1, ffn1_w3, ffn1_w2, ffn2_w1, ffn2_w3,
           ffn2_w2, ln1_g, ln1_b, ln2_g, ln2_b, ln3_g, ln3_b):
    assert DEPTH == 1 and w_in.shape[0] == 1
    l = 0
    bf = lambda w: w.astype(BF16)
    vec = lambda p: p[l].reshape(1, -1)
    sinks = attn_sinks[l]

    pos = jnp.concatenate([jnp.arange(SEQ, dtype=jnp.int32),
                           jnp.repeat(PAST_LEN + jnp.arange(DEC_SEQ, dtype=jnp.int32), DEC_BATCH)])
    cos, sin_signed = _rope_tables(pos)

    xp = x_prompt.reshape(PROMPT_ROWS, D_MODEL)
    xs = jnp.transpose(x_sample, (1, 0, 2)).reshape(SAMPLE_ROWS, D_MODEL)
    (h1, pool_p, attn_p, us, qs, ks, vs, kt_last, vt_last, u_last) = _front(
        xp, xs, bf(ffn1_w1[l]), bf(ffn1_w3[l]), bf(ffn1_w2[l]), bf(w_in[l][:, :UQKV_WIDTH]),
        vec(ln1_g), vec(ln1_b), cos, sin_signed, sinks)

    to_t = lambda c: jnp.transpose(c[l], (0, 2, 3, 1)).reshape(DEC_BATCH, KV_WIDTH, WINDOW)
    cu = jnp.transpose(cache_pool_u[l], (1, 0, 2))
    attn_s, pool_s, kt_s, vt_s, pu_s = _sample_ctx(us, qs, ks, vs, to_t(cache_k_win), to_t(cache_v_win), cu, sinks)

    yp, ys = _back(h1, pool_p, attn_p, pool_s.reshape(SAMPLE_ROWS, POOL_WIDTH), attn_s.reshape(SAMPLE_ROWS, Q_WIDTH),
                   bf(w_in[l][:, UQKV_WIDTH:]), bf(pool_w_grp[l]), vec(pool_scale), bf(w_pool_out[l]),
                   bf(w_attn_out[l]), bf(w_out[l]), vec(ln2_g), vec(ln2_b),
                   bf(ffn2_w1[l]), bf(ffn2_w3[l]), bf(ffn2_w2[l]), vec(ln3_g), vec(ln3_b))
    yp = yp.reshape(BATCH, SEQ, D_MODEL)
    ys = jnp.transpose(ys.reshape(DEC_SEQ, DEC_BATCH, D_MODEL), (1, 0, 2))

    from_t = lambda c, n: jnp.transpose(c.reshape(n, N_KV_HEADS, HEAD_DIM, WINDOW), (0, 3, 1, 2))[None]
    pool_u_prompt = u_last[None, :, POOL_PAD - POOL_BUF:]
    pool_u_sample = jnp.transpose(pu_s, (1, 0, 2))[None]
    return (yp, ys, pool_u_prompt, from_t(kt_last, BATCH), from_t(vt_last, BATCH),
            pool_u_sample, from_t(kt_s, DEC_BATCH), from_t(vt_s, DEC_BATCH))
```

```python
import jax
import jax.numpy as jnp
import numpy as np
from jax import lax
from jax.experimental import pallas as pl
from jax.experimental.pallas import tpu as pltpu

D_MODEL = 1024
BATCH = 8
SEQ = 2048
DEC_BATCH = 128
DEC_SEQ = 4
PAST_LEN = 8192
POOL_WINDOWS = (2, 4, 8, 16)
POOL_GROUP = 128
POOL_WIDTH = 512
POOL_BUF = 15
N_HEADS = 8
N_KV_HEADS = 2
HEADS_PER_KV = N_HEADS // N_KV_HEADS
HEAD_DIM = 64
Q_WIDTH = 512
KV_WIDTH = 128
WINDOW = 128
ROPE_THETA = 10000.0
D_FF = 2816
DEPTH = 1
ALPHA = (2.0 * DEPTH) ** 0.25
LN_EPS = 1e-5
NEG_INF = -1e30
UQKV_WIDTH = POOL_WIDTH + Q_WIDTH + 2 * KV_WIDTH

LANES = 128
KEY_SPAN = 2 * WINDOW
VMEM_LIMIT_BYTES = 58 * 1024 * 1024

ROW_TILE = 512
TILES_PER_SEQ = SEQ // ROW_TILE
PROMPT_ROWS = BATCH * SEQ
PROMPT_TILES = PROMPT_ROWS // ROW_TILE
SAMPLE_ROWS = DEC_BATCH * DEC_SEQ
ROW_TILES = PROMPT_TILES + 1
POOL_PAD = 16
SEQ_GROUP = 16
SUB_GROUP = 8
FF_CHUNK = 256

BF16 = jnp.bfloat16
F32 = jnp.float32


def _dot(a, b):
    return jnp.dot(a, b, preferred_element_type=F32)


def _dot_nt(a, b):
    return lax.dot_general(a, b, (((1,), (1,)), ((), ())), preferred_element_type=F32)


def _layer_norm(y, g, b):
    mu = jnp.mean(y, axis=-1, keepdims=True)
    yc = y - mu
    var = jnp.mean(yc * yc, axis=-1, keepdims=True)
    return yc * lax.rsqrt(var + LN_EPS) * g + b


def _resident(shape):
    nd = len(shape)
    return pl.BlockSpec(shape, lambda *_: (0,) * nd, pipeline_mode=pl.Buffered(1))


FF_SLOTS = 2 * (D_FF // FF_CHUNK)


def _swiglu(x, w1_ref, w3_ref, w2_ref, h_s, side_work):
    assert all(0 <= slot <= FF_SLOTS for slot in side_work)
    run = lambda slot: side_work.get(slot, lambda: None)()
    xb = x.astype(BF16)
    for j in range(D_FF // FF_CHUNK):
        cols = slice(j * FF_CHUNK, (j + 1) * FF_CHUNK)
        a = _dot(xb, w1_ref[:, cols])
        run(2 * j)
        b = _dot(xb, w3_ref[:, cols])
        h_s[:, cols] = ((a * jax.nn.sigmoid(a)) * b).astype(BF16)
        run(2 * j + 1)
    run(FF_SLOTS)
    return _dot(h_s[...], w2_ref[...])


def _rope(x, cos, sin_signed, first_half):
    fwd = pltpu.roll(x, LANES - HEAD_DIM // 2, axis=1)
    bwd = pltpu.roll(x, HEAD_DIM // 2, axis=1)
    return x * cos + jnp.where(first_half, fwd, bwd) * sin_signed


def _swap_halves(x):
    return pltpu.roll(x, HEAD_DIM, axis=1)


def _swap_halves_wide(x):
    return jnp.concatenate([_swap_halves(x[:, c:c + LANES]) for c in range(0, x.shape[1], LANES)], axis=1)


def _lane_split(x, x_sw, kh):
    low = lax.broadcasted_iota(jnp.int32, (1, LANES), 1) < HEAD_DIM
    lo, hi = (x, x_sw) if kh == 0 else (x_sw, x)
    return jnp.concatenate([jnp.where(low, lo, 0.0), jnp.where(low, 0.0, hi)], axis=0).astype(BF16)


def _sink_softmax(q_pairs, keys, kh, bias, sinks_ref):
    s = _dot_nt(q_pairs, _lane_split(keys, _swap_halves(keys), kh)) + bias
    second_pair = lax.broadcasted_iota(jnp.int32, (s.shape[0], 1), 0) >= WINDOW
    probs, denoms = [], []
    for c in range(2):
        sc = s[:, c * KEY_SPAN:(c + 1) * KEY_SPAN]
        head = kh * HEADS_PER_KV + c
        sink = jnp.where(second_pair, sinks_ref[head + 2], sinks_ref[head])
        m = jnp.maximum(jnp.max(sc, axis=-1, keepdims=True), sink)
        p = jnp.exp(sc - m)
        denoms.append(jnp.sum(p, axis=-1, keepdims=True) + jnp.exp(sink - m))
        probs.append(p.astype(BF16))
    return jnp.concatenate(probs, axis=1), denoms


def _weighted_values(probs, denoms, vals, kh):
    low = lax.broadcasted_iota(jnp.int32, (1, LANES), 1) < HEAD_DIM
    o = _dot(probs, _lane_split(vals, _swap_halves(vals), kh))
    return o / jnp.where(low, denoms[0], denoms[1])


def _prompt_context_work(tile, q_c, k_c, v_c, u_c, bias_ref, sinks_ref,
                         pool_ref, attn_ref, kt_ref, vt_ref, ulast_ref):
    seq_tile = (tile + TILES_PER_SEQ) % TILES_PER_SEQ
    first_tile = seq_tile == 0
    softmaxed = {}

    def scores(unit, blk, kh):
        r0 = blk * WINDOW
        bias = bias_ref[jnp.where(first_tile, 1, 0)] if blk == 0 else bias_ref[0]
        c0 = 2 * kh * LANES
        q_pairs = jnp.concatenate([q_c[r0:r0 + WINDOW, c0:c0 + LANES],
                                   q_c[r0:r0 + WINDOW, c0 + LANES:c0 + 2 * LANES]], axis=0)
        softmaxed[unit] = _sink_softmax(q_pairs, k_c[r0:r0 + KEY_SPAN, :], kh, bias, sinks_ref)

    def values(unit, blk, kh):
        r0 = blk * WINDOW
        c0 = 2 * kh * LANES
        o = _weighted_values(*softmaxed.pop(unit), v_c[r0:r0 + KEY_SPAN, :], kh)
        attn_ref[r0:r0 + WINDOW, c0:c0 + LANES] = o[:WINDOW].astype(BF16)
        attn_ref[r0:r0 + WINDOW, c0 + LANES:c0 + 2 * LANES] = o[WINDOW:].astype(BF16)

    def pool(groups):
        pos = seq_tile * ROW_TILE + lax.broadcasted_iota(jnp.int32, (ROW_TILE, 1), 0)
        for g in groups:
            w = POOL_WINDOWS[g]
            cols = slice(g * POOL_GROUP, (g + 1) * POOL_GROUP)
            cur = u_c[POOL_PAD:POOL_PAD + ROW_TILE, cols]
            acc = cur
            for j in range(1, w):
                acc = acc + u_c[POOL_PAD - j:POOL_PAD - j + ROW_TILE, cols]
            cnt = jnp.minimum(pos + 1, w).astype(F32)
            pool_ref[:, cols] = (acc / cnt - cur).astype(BF16)

    def sequence_state():
        kt_ref[0] = k_c[ROW_TILE:ROW_TILE + WINDOW, :].T
        vt_ref[0] = v_c[ROW_TILE:ROW_TILE + WINDOW, :].T
        ulast_ref[0] = u_c[ROW_TILE:ROW_TILE + POOL_PAD, :]

    work = {"pool_wide": lambda: pool((3,)), "pool_narrow": lambda: pool((0, 1, 2)),
            "sequence_state": sequence_state}
    for blk in range(ROW_TILE // WINDOW):
        for kh in range(N_KV_HEADS):
            unit = blk * N_KV_HEADS + kh
            work["scores", unit] = lambda unit=unit, blk=blk, kh=kh: scores(unit, blk, kh)
            work["values", unit] = lambda unit=unit, blk=blk, kh=kh: values(unit, blk, kh)
    return work


def _prompt_bias():
    r = np.arange(2 * WINDOW)[:, None] % WINDOW
    c = np.arange(2 * KEY_SPAN)[None, :] % KEY_SPAN
    valid = (r <= c) & (c <= r + WINDOW)
    first = valid & (c >= WINDOW)
    return np.where(np.stack([valid, first]), 0.0, NEG_INF).astype(np.float32)


def _front_kernel(sinks_ref, xp_ref, xs_ref, w1_ref, w3_ref, w2_ref, wu_ref, g_ref, b_ref, cos_ref, sin_ref,
                  bias_ref,
                  h_ref, pool_ref, attn_ref, us_ref, qs_ref, ks_ref, vs_ref, kt_ref, vt_ref, ulast_ref,
                  z_c, q_c, k_c, v_c, u_c, h_s):
    r = pl.program_id(0)
    prev = r - 1

    @pl.when(r == 0)
    def _():
        for ref in (z_c, q_c, k_c, v_c, u_c):
            ref[...] = jnp.zeros(ref.shape, ref.dtype)

    def norm_previous():
        h1 = _layer_norm(z_c[...], g_ref[...], b_ref[...])
        h_ref[...] = h1
        return h1.astype(BF16)

    def project_previous(h1b):
        z = _dot(h1b, wu_ref[...])
        cos = cos_ref[...]
        sin_signed = sin_ref[...]
        lane = lax.broadcasted_iota(jnp.int32, cos.shape, 1)
        first_half = (lane & (HEAD_DIM // 2)) == 0
        q = []
        for c in range(Q_WIDTH // LANES):
            lo = POOL_WIDTH + c * LANES
            qc = _rope(z[:, lo:lo + LANES], cos, sin_signed, first_half)
            q.append((qc * (HEAD_DIM ** -0.5)).astype(BF16))
        ko = POOL_WIDTH + Q_WIDTH
        k = _rope(z[:, ko:ko + KV_WIDTH], cos, sin_signed, first_half)
        return z[:, :POOL_WIDTH], q, k, z[:, ko + KV_WIDTH:ko + 2 * KV_WIDTH]

    def tile_step(x_ref):
        normed, projected = [], []

        def norm():
            normed.append(norm_previous())

        def project():
            projected.extend(project_previous(normed[0]))

        def carry():
            u, q, k, v = projected
            for c, qc in enumerate(q):
                q_c[:, c * LANES:(c + 1) * LANES] = qc
            k_c[0:WINDOW, :] = k_c[ROW_TILE:ROW_TILE + WINDOW, :]
            v_c[0:WINDOW, :] = v_c[ROW_TILE:ROW_TILE + WINDOW, :]
            starts_seq = (prev + TILES_PER_SEQ) % TILES_PER_SEQ == 0
            u_c[0:POOL_PAD, :] = jnp.where(starts_seq, 0.0, u_c[ROW_TILE:ROW_TILE + POOL_PAD, :])
            k_c[WINDOW:WINDOW + ROW_TILE, :] = k
            v_c[WINDOW:WINDOW + ROW_TILE, :] = v
            u_c[POOL_PAD:POOL_PAD + ROW_TILE, :] = u

        context = _prompt_context_work(prev, q_c, k_c, v_c, u_c, bias_ref, sinks_ref,
                                       pool_ref, attn_ref, kt_ref, vt_ref, ulast_ref)
        side = {0: norm, 3: project, 4: carry}
        for unit in range(8):
            side[6 + 2 * unit] = context["scores", unit]
            side[7 + 2 * unit] = context["values", unit]

        def pooling_and_state():
            context["pool_wide"]()
            context["pool_narrow"]()
            context["sequence_state"]()

        side[FF_SLOTS] = pooling_and_state
        x = x_ref[...]
        y = _swiglu(x, w1_ref, w3_ref, w2_ref, h_s, side)
        z_c[...] = ALPHA * x + 0.5 * y

    @pl.when(r < PROMPT_TILES)
    def _():
        tile_step(xp_ref)

    @pl.when(r == PROMPT_TILES)
    def _():
        tile_step(xs_ref)

    @pl.when(r == ROW_TILES)
    def _():
        u, q, k, v = project_previous(norm_previous())
        us_ref[...] = u
        for c, qc in enumerate(q):
            qs_ref[:, c * LANES:(c + 1) * LANES] = qc
        ks_ref[...] = k
        vs_ref[...] = v


def _front(xp, xs, w1, w3, w2, wu, g, b, cos, sin_signed, sinks):
    rows = ROW_TILES * ROW_TILE

    def lagged(width):
        return pl.BlockSpec((ROW_TILE, width), lambda r: (jnp.clip(r - 1, 0, PROMPT_TILES - 1), 0))

    def seq_of_lagged(shape):
        return pl.BlockSpec(shape, lambda r: (jnp.clip(r - 1, 0, PROMPT_TILES - 1) // TILES_PER_SEQ, 0, 0))

    def table_index(r):
        p = jnp.clip(r - 1, 0, PROMPT_TILES)
        return (jnp.where(p == PROMPT_TILES, TILES_PER_SEQ, p % TILES_PER_SEQ), 0)

    table_spec = pl.BlockSpec((ROW_TILE, LANES), table_index)
    sample = lambda width: pl.BlockSpec((ROW_TILE, width), lambda r: (0, 0))
    return pl.pallas_call(
        _front_kernel,
        out_shape=(jax.ShapeDtypeStruct((rows, D_MODEL), F32),
                   jax.ShapeDtypeStruct((PROMPT_ROWS, POOL_WIDTH), BF16),
                   jax.ShapeDtypeStruct((PROMPT_ROWS, Q_WIDTH), BF16),
                   jax.ShapeDtypeStruct((SAMPLE_ROWS, POOL_WIDTH), F32),
                   jax.ShapeDtypeStruct((SAMPLE_ROWS, Q_WIDTH), BF16),
                   jax.ShapeDtypeStruct((SAMPLE_ROWS, KV_WIDTH), F32),
                   jax.ShapeDtypeStruct((SAMPLE_ROWS, KV_WIDTH), F32),
                   jax.ShapeDtypeStruct((BATCH, KV_WIDTH, WINDOW), F32),
                   jax.ShapeDtypeStruct((BATCH, KV_WIDTH, WINDOW), F32),
                   jax.ShapeDtypeStruct((BATCH, POOL_PAD, POOL_WIDTH), F32)),
        grid=(ROW_TILES + 1,),
        in_specs=[pl.BlockSpec(memory_space=pltpu.SMEM),
                  pl.BlockSpec((ROW_TILE, D_MODEL), lambda r: (jnp.minimum(r, PROMPT_TILES - 1), 0)),
                  _resident((SAMPLE_ROWS, D_MODEL)),
                  _resident((D_MODEL, D_FF)), _resident((D_MODEL, D_FF)), _resident((D_FF, D_MODEL)),
                  _resident((D_MODEL, UQKV_WIDTH)), _resident((1, D_MODEL)), _resident((1, D_MODEL)),
                  table_spec, table_spec, _resident((2, 2 * WINDOW, 2 * KEY_SPAN))],
        out_specs=(pl.BlockSpec((ROW_TILE, D_MODEL), lambda r: (jnp.clip(r - 1, 0, PROMPT_TILES), 0)),
                   lagged(POOL_WIDTH), lagged(Q_WIDTH),
                   sample(POOL_WIDTH), sample(Q_WIDTH), sample(KV_WIDTH), sample(KV_WIDTH),
                   seq_of_lagged((1, KV_WIDTH, WINDOW)), seq_of_lagged((1, KV_WIDTH, WINDOW)),
                   seq_of_lagged((1, POOL_PAD, POOL_WIDTH))),
        scratch_shapes=[pltpu.VMEM((ROW_TILE, D_MODEL), F32),
                        pltpu.VMEM((ROW_TILE, Q_WIDTH), BF16),
                        pltpu.VMEM((WINDOW + ROW_TILE, KV_WIDTH), F32),
                        pltpu.VMEM((WINDOW + ROW_TILE, KV_WIDTH), F32),
                        pltpu.VMEM((POOL_PAD + ROW_TILE, POOL_WIDTH), F32),
                        pltpu.VMEM((ROW_TILE, D_FF), BF16)],
        compiler_params=pltpu.CompilerParams(dimension_semantics=("arbitrary",),
                                             vmem_limit_bytes=VMEM_LIMIT_BYTES),
        name="front",
    )(sinks, xp, xs, w1, w3, w2, wu, g, b, cos, sin_signed, jnp.asarray(_prompt_bias()))


def _sample_bias():
    row = np.arange(HEADS_PER_KV * DEC_SEQ * SUB_GROUP)
    row_t = (row // SUB_GROUP) % DEC_SEQ
    row_b = row % SUB_GROUP
    col = np.arange(SUB_GROUP * WINDOW)
    ok_c = (col[None, :] // WINDOW == row_b[:, None]) & (col[None, :] % WINDOW >= row_t[:, None])
    new = np.arange(DEC_SEQ * SUB_GROUP)
    ok_n = (new[None, :] % SUB_GROUP == row_b[:, None]) & (new[None, :] // SUB_GROUP <= row_t[:, None])
    to_bias = lambda ok: np.where(ok, 0.0, NEG_INF).astype(np.float32)
    return to_bias(ok_c), to_bias(ok_n)


def _sample_ctx_kernel(sinks_ref, q0, q1, q2, q3, k0, k1, k2, k3, v0, v1, v2, v3, u0, u1, u2, u3,
                       ckt_ref, cvt_ref, cu_ref, bias_c_ref, bias_n_ref,
                       attn_ref, pool_ref, kt_out, vt_out, pu_out, kbt_s, vbt_s):
    q_t = [q[...].astype(F32) for q in (q0, q1, q2, q3)]
    k_t = [k[...] for k in (k0, k1, k2, k3)]
    v_t = [v[...] for v in (v0, v1, v2, v3)]

    rows_u = [cu_ref[i] for i in range(POOL_BUF)] + [u[...] for u in (u0, u1, u2, u3)]
    for t in range(DEC_SEQ):
        pooled = []
        for g, w in enumerate(POOL_WINDOWS):
            cols = slice(g * POOL_GROUP, (g + 1) * POOL_GROUP)
            cur = rows_u[POOL_BUF + t][:, cols]
            acc = cur
            for j in range(1, w):
                acc = acc + rows_u[POOL_BUF + t - j][:, cols]
            pooled.append(acc / float(w) - cur)
        pool_ref[t] = jnp.concatenate(pooled, axis=1)
    for i in range(POOL_BUF):
        pu_out[i] = rows_u[i + DEC_SEQ]

    bias_c = bias_c_ref[...]
    bias_n = bias_n_ref[...]
    head_of_row = lax.broadcasted_iota(jnp.int32, (bias_c.shape[0], 1), 0) // (DEC_SEQ * SUB_GROUP)
    low = lax.broadcasted_iota(jnp.int32, (1, LANES), 1) < HEAD_DIM
    q_sw = [_swap_halves_wide(q) for q in q_t]
    k_sw = [_swap_halves(k) for k in k_t]
    v_sw = [_swap_halves(v) for v in v_t]
    for sub in range(SEQ_GROUP // SUB_GROUP):
        rows = slice(sub * SUB_GROUP, (sub + 1) * SUB_GROUP)
        for kh in range(N_KV_HEADS):
            def q_piece(t, head):
                src = q_t[t] if head % 2 == 0 else q_sw[t]
                chunk = head // 2
                return jnp.where(low, src[rows, chunk * LANES:(chunk + 1) * LANES], 0.0)

            def kv_first(c):
                return c if kh == 0 else jnp.concatenate([c[HEAD_DIM:], c[:HEAD_DIM]], axis=0)

            def kv_twice(c):
                part = c[kh * HEAD_DIM:(kh + 1) * HEAD_DIM]
                return jnp.concatenate([part, part], axis=0)

            lhs = jnp.concatenate([q_piece(t, kh * HEADS_PER_KV + g)
                                   for g in range(HEADS_PER_KV) for t in range(DEC_SEQ)], axis=0).astype(BF16)
            kcat = jnp.concatenate([kv_first(ckt_ref[sub * SUB_GROUP + b]) for b in range(SUB_GROUP)],
                                   axis=1).astype(BF16)
            vcat = jnp.concatenate([kv_twice(cvt_ref[sub * SUB_GROUP + b]) for b in range(SUB_GROUP)],
                                   axis=1).astype(BF16)
            knew = jnp.concatenate([(k_t[t] if kh == 0 else k_sw[t])[rows] for t in range(DEC_SEQ)],
                                   axis=0).astype(BF16)
            vnew = jnp.concatenate([(jnp.where(low, v_t[t], v_sw[t]) if kh == 0 else
                                     jnp.where(low, v_sw[t], v_t[t]))[rows] for t in range(DEC_SEQ)],
                                   axis=0).astype(BF16)
            s_c = _dot(lhs, kcat) + bias_c
            s_n = _dot_nt(lhs, knew) + bias_n
            sink = jnp.zeros(head_of_row.shape, F32)
            for g in range(HEADS_PER_KV):
                sink = jnp.where(head_of_row == g, sinks_ref[kh * HEADS_PER_KV + g], sink)
            m = jnp.maximum(jnp.maximum(jnp.max(s_c, axis=-1, keepdims=True),
                                        jnp.max(s_n, axis=-1, keepdims=True)), sink)
            p_c = jnp.exp(s_c - m)
            p_n = jnp.exp(s_n - m)
            denom = (jnp.sum(p_c, axis=-1, keepdims=True) + jnp.sum(p_n, axis=-1, keepdims=True)
                     + jnp.exp(sink - m))
            o = (_dot_nt(p_c.astype(BF16), vcat) + _dot(p_n.astype(BF16), vnew)) / denom
            for t in range(DEC_SEQ):
                for pair in range(HEADS_PER_KV // 2):
                    piece = lambda g: o[(g * DEC_SEQ + t) * SUB_GROUP:(g * DEC_SEQ + t + 1) * SUB_GROUP]
                    c0 = (kh * HEADS_PER_KV // 2 + pair) * LANES
                    attn_ref[t, rows, c0:c0 + LANES] = jnp.where(low, piece(2 * pair), piece(2 * pair + 1))

    zeros = jnp.zeros((LANES - DEC_SEQ * SEQ_GROUP, KV_WIDTH), F32)
    kbt_s[DEC_SEQ * SEQ_GROUP:, :] = zeros
    vbt_s[DEC_SEQ * SEQ_GROUP:, :] = zeros
    for t in range(DEC_SEQ):
        kbt_s[pl.ds(t, SEQ_GROUP, stride=DEC_SEQ), :] = k_t[t]
        vbt_s[pl.ds(t, SEQ_GROUP, stride=DEC_SEQ), :] = v_t[t]
    knew_t = kbt_s[...].T
    vnew_t = vbt_s[...].T
    keep = lax.broadcasted_iota(jnp.int32, (1, WINDOW), 1) < WINDOW - DEC_SEQ
    for b in range(SEQ_GROUP):
        shift_new = WINDOW - DEC_SEQ - DEC_SEQ * b
        kt_out[b] = jnp.where(keep, pltpu.roll(ckt_ref[b], WINDOW - DEC_SEQ, axis=1),
                              pltpu.roll(knew_t, shift_new, axis=1))
        vt_out[b] = jnp.where(keep, pltpu.roll(cvt_ref[b], WINDOW - DEC_SEQ, axis=1),
                              pltpu.roll(vnew_t, shift_new, axis=1))


def _sample_ctx(us, qs, ks, vs, ckt, cvt, cu, sinks):
    groups = DEC_BATCH // SEQ_GROUP

    def token_rows(t, width):
        return pl.BlockSpec((SEQ_GROUP, width), lambda i: (t * groups + i, 0))

    def per_token(width):
        return [token_rows(t, width) for t in range(DEC_SEQ)]

    cache_spec = pl.BlockSpec((SEQ_GROUP, KV_WIDTH, WINDOW), lambda i: (i, 0, 0))
    pool_rows_spec = pl.BlockSpec((POOL_BUF, SEQ_GROUP, POOL_WIDTH), lambda i: (0, i, 0))
    by_token = lambda width: pl.BlockSpec((DEC_SEQ, SEQ_GROUP, width), lambda i: (0, i, 0))
    bias_c, bias_n = _sample_bias()
    return pl.pallas_call(
        _sample_ctx_kernel,
        out_shape=(jax.ShapeDtypeStruct((DEC_SEQ, DEC_BATCH, Q_WIDTH), F32),
                   jax.ShapeDtypeStruct((DEC_SEQ, DEC_BATCH, POOL_WIDTH), F32),
                   jax.ShapeDtypeStruct((DEC_BATCH, KV_WIDTH, WINDOW), F32),
                   jax.ShapeDtypeStruct((DEC_BATCH, KV_WIDTH, WINDOW), F32),
                   jax.ShapeDtypeStruct((POOL_BUF, DEC_BATCH, POOL_WIDTH), F32)),
        grid=(groups,),
        in_specs=[pl.BlockSpec(memory_space=pltpu.SMEM)]
                 + per_token(Q_WIDTH) + per_token(KV_WIDTH) + per_token(KV_WIDTH) + per_token(POOL_WIDTH)
                 + [cache_spec, cache_spec, pool_rows_spec, _resident(bias_c.shape), _resident(bias_n.shape)],
        out_specs=(by_token(Q_WIDTH), by_token(POOL_WIDTH), cache_spec, cache_spec, pool_rows_spec),
        scratch_shapes=[pltpu.VMEM((LANES, KV_WIDTH), F32), pltpu.VMEM((LANES, KV_WIDTH), F32)],
        compiler_params=pltpu.CompilerParams(dimension_semantics=("parallel",),
                                             vmem_limit_bytes=VMEM_LIMIT_BYTES),
        name="sample_ctx",
    )(sinks, *([qs] * DEC_SEQ), *([ks] * DEC_SEQ), *([vs] * DEC_SEQ), *([us] * DEC_SEQ),
      ckt, cvt, cu, jnp.asarray(bias_c), jnp.asarray(bias_n))


def _back_kernel(h_ref, poolp_ref, attnp_ref, pools_ref, attns_ref,
                 wg_ref, wgrp_ref, scale_ref, wpo_ref, wao_ref, wout_ref, g2_ref, b2_ref,
                 w1_ref, w3_ref, w2_ref, g3_ref, b3_ref, yp_ref, ys_ref, z2_c, z3_c, h_s):
    r = pl.program_id(0)

    @pl.when(r == 0)
    def _():
        z2_c[...] = jnp.zeros(z2_c.shape, z2_c.dtype)
        z3_c[...] = jnp.zeros(z3_c.shape, z3_c.dtype)

    @pl.when(r <= ROW_TILES)
    def _():
        is_sample = r >= PROMPT_TILES
        h1 = h_ref[...]
        pool_in = jnp.where(is_sample, pools_ref[...].astype(BF16), poolp_ref[...])
        attn_o = jnp.where(is_sample, attns_ref[...].astype(BF16), attnp_ref[...])
        zs = [_dot(pool_in[:, g * POOL_GROUP:(g + 1) * POOL_GROUP], wgrp_ref[g])
              for g in range(len(POOL_WINDOWS))]
        pool_z = jnp.concatenate(zs, axis=1) * scale_ref[...]
        a = _dot(pool_z.astype(BF16), wpo_ref[...])
        b = _dot(attn_o, wao_ref[...])

        yp_ref[...] = _layer_norm(z3_c[...], g3_ref[...], b3_ref[...])
        h2 = _layer_norm(z2_c[...], g2_ref[...], b2_ref[...])
        gated = []

        def gate():
            gates = jax.nn.sigmoid(_dot(h1.astype(BF16), wg_ref[...]))
            gated.append((gates[:, :D_MODEL] * a + gates[:, D_MODEL:] * b).astype(BF16))

        def project_out():
            z2_c[...] = ALPHA * h1 + _dot(gated[0], wout_ref[...])

        ff = _swiglu(h2, w1_ref, w3_ref, w2_ref, h_s, {1: gate, 9: project_out})
        z3_c[...] = ALPHA * h2 + 0.5 * ff

    @pl.when(r == ROW_TILES + 1)
    def _():
        ys_ref[...] = _layer_norm(z3_c[...], g3_ref[...], b3_ref[...])


def _back(h1, pool_p, attn_p, pool_s, attn_s, wg, wgrp, scale, wpo, wao, wout, g2, b2, w1, w3, w2, g3, b3):
    prompt = lambda width: pl.BlockSpec((ROW_TILE, width), lambda r: (jnp.minimum(r, PROMPT_TILES - 1), 0))
    lagged = pl.BlockSpec((ROW_TILE, D_MODEL), lambda r: (jnp.clip(r - 2, 0, PROMPT_TILES - 1), 0))
    return pl.pallas_call(
        _back_kernel,
        out_shape=(jax.ShapeDtypeStruct((PROMPT_ROWS, D_MODEL), F32),
                   jax.ShapeDtypeStruct((SAMPLE_ROWS, D_MODEL), F32)),
        grid=(ROW_TILES + 2,),
        in_specs=[pl.BlockSpec((ROW_TILE, D_MODEL), lambda r: (jnp.minimum(r, PROMPT_TILES), 0)),
                  prompt(POOL_WIDTH), prompt(Q_WIDTH),
                  _resident((SAMPLE_ROWS, POOL_WIDTH)), _resident((SAMPLE_ROWS, Q_WIDTH)),
                  _resident((D_MODEL, 2 * D_MODEL)),
                  _resident((len(POOL_WINDOWS), POOL_GROUP, POOL_GROUP)), _resident((1, POOL_WIDTH)),
                  _resident((POOL_WIDTH, D_MODEL)), _resident((Q_WIDTH, D_MODEL)), _resident((D_MODEL, D_MODEL)),
                  _resident((1, D_MODEL)), _resident((1, D_MODEL)),
                  _resident((D_MODEL, D_FF)), _resident((D_MODEL, D_FF)), _resident((D_FF, D_MODEL)),
                  _resident((1, D_MODEL)), _resident((1, D_MODEL))],
        out_specs=(lagged, pl.BlockSpec((ROW_TILE, D_MODEL), lambda r: (0, 0))),
        scratch_shapes=[pltpu.VMEM((ROW_TILE, D_MODEL), F32), pltpu.VMEM((ROW_TILE, D_MODEL), F32),
                        pltpu.VMEM((ROW_TILE, D_FF), BF16)],
        compiler_params=pltpu.CompilerParams(dimension_semantics=("arbitrary",),
                                             vmem_limit_bytes=VMEM_LIMIT_BYTES),
        name="back",
    )(h1, pool_p, attn_p, pool_s, attn_s, wg, wgrp, scale, wpo, wao, wout, g2, b2, w1, w3, w2, g3, b3)


def _rope_tables(pos):
    half = HEAD_DIM // 2
    freqs = ROPE_THETA ** (-2.0 * jnp.arange(half, dtype=F32) / HEAD_DIM)
    ang = pos.astype(F32)[:, None] * freqs[None, :]
    cos, sin = jnp.cos(ang), jnp.sin(ang)
    return jnp.tile(cos, (1, 4)), jnp.concatenate([-sin, sin, -sin, sin], axis=1)


def kernel(x_prompt, x_sample, cache_pool_u, cache_k_win, cache_v_win, w_in, pool_w_grp, pool_scale,
           attn_sinks, w_pool_out, w_attn_out, w_out, ffn1_w1, ffn1_w3, ffn1_w2, ffn2_w1, ffn2_w3,
           ffn2_w2, ln1_g, ln1_b, ln2_g, ln2_b, ln3_g, ln3_b):
    assert DEPTH == 1 and w_in.shape[0] == 1
    l = 0
    bf = lambda w: w.astype(BF16)
    vec = lambda p: p[l].reshape(1, -1)
    sinks = attn_sinks[l]

    pos = jnp.concatenate([jnp.arange(SEQ, dtype=jnp.int32),
                           jnp.repeat(PAST_LEN + jnp.arange(DEC_SEQ, dtype=jnp.int32), DEC_BATCH)])
    cos, sin_signed = _rope_tables(pos)

    xp = x_prompt.reshape(PROMPT_ROWS, D_MODEL)
    xs = jnp.transpose(x_sample, (1, 0, 2)).reshape(SAMPLE_ROWS, D_MODEL)
    (h1, pool_p, attn_p, us, qs, ks, vs, kt_last, vt_last, u_last) = _front(
        xp, xs, bf(ffn1_w1[l]), bf(ffn1_w3[l]), bf(ffn1_w2[l]), bf(w_in[l][:, :UQKV_WIDTH]),
        vec(ln1_g), vec(ln1_b), cos, sin_signed, sinks)

    to_t = lambda c: jnp.transpose(c[l], (0, 2, 3, 1)).reshape(DEC_BATCH, KV_WIDTH, WINDOW)
    cu = jnp.transpose(cache_pool_u[l], (1, 0, 2))
    attn_s, pool_s, kt_s, vt_s, pu_s = _sample_ctx(us, qs, ks, vs, to_t(cache_k_win), to_t(cache_v_win), cu, sinks)

    yp, ys = _back(h1, pool_p, attn_p, pool_s.reshape(SAMPLE_ROWS, POOL_WIDTH), attn_s.reshape(SAMPLE_ROWS, Q_WIDTH),
                   bf(w_in[l][:, UQKV_WIDTH:]), bf(pool_w_grp[l]), vec(pool_scale), bf(w_pool_out[l]),
                   bf(w_attn_out[l]), bf(w_out[l]), vec(ln2_g), vec(ln2_b),
                   bf(ffn2_w1[l]), bf(ffn2_w3[l]), bf(ffn2_w2[l]), vec(ln3_g), vec(ln3_b))
    yp = yp.reshape(BATCH, SEQ, D_MODEL)
    ys = jnp.transpose(ys.reshape(DEC_SEQ, DEC_BATCH, D_MODEL), (1, 0, 2))

    from_t = lambda c, n: jnp.transpose(c.reshape(n, N_KV_HEADS, HEAD_DIM, WINDOW), (0, 3, 1, 2))[None]
    pool_u_prompt = u_last[None, :, POOL_PAD - POOL_BUF:]
    pool_u_sample = jnp.transpose(pu_s, (1, 0, 2))[None]
    return (yp, ys, pool_u_prompt, from_t(kt_last, BATCH), from_t(vt_last, BATCH),
            pool_u_sample, from_t(kt_s, DEC_BATCH), from_t(vt_s, DEC_BATCH))
```

```python
import jax
import jax.numpy as jnp
import numpy as np
from jax import lax
from jax.experimental import pallas as pl
from jax.experimental.pallas import tpu as pltpu

D_MODEL = 1024
BATCH = 8
SEQ = 2048
DEC_BATCH = 128
DEC_SEQ = 4
PAST_LEN = 8192
POOL_WINDOWS = (2, 4, 8, 16)
POOL_GROUP = 128
POOL_WIDTH = 512
POOL_BUF = 15
N_HEADS = 8
N_KV_HEADS = 2
HEADS_PER_KV = N_HEADS // N_KV_HEADS
HEAD_DIM = 64
Q_WIDTH = 512
KV_WIDTH = 128
WINDOW = 128
ROPE_THETA = 10000.0
D_FF = 2816
DEPTH = 1
ALPHA = (2.0 * DEPTH) ** 0.25
LN_EPS = 1e-5
NEG_INF = -1e30
UQKV_WIDTH = POOL_WIDTH + Q_WIDTH + 2 * KV_WIDTH

LANES = 128
KEY_SPAN = 2 * WINDOW
VMEM_LIMIT_BYTES = 58 * 1024 * 1024

ROW_TILE = 512
TILES_PER_SEQ = SEQ // ROW_TILE
PROMPT_ROWS = BATCH * SEQ
PROMPT_TILES = PROMPT_ROWS // ROW_TILE
SAMPLE_ROWS = DEC_BATCH * DEC_SEQ
ROW_TILES = PROMPT_TILES + 1
POOL_PAD = 16
SEQ_GROUP = 16
SUB_GROUP = 8
FF_CHUNK = 256

BF16 = jnp.bfloat16
F32 = jnp.float32


def _dot(a, b):
    return jnp.dot(a, b, preferred_element_type=F32)


def _dot_nt(a, b):
    return lax.dot_general(a, b, (((1,), (1,)), ((), ())), preferred_element_type=F32)


def _layer_norm(y, g, b):
    mu = jnp.mean(y, axis=-1, keepdims=True)
    yc = y - mu
    var = jnp.mean(yc * yc, axis=-1, keepdims=True)
    return yc * lax.rsqrt(var + LN_EPS) * g + b


def _resident(shape):
    nd = len(shape)
    return pl.BlockSpec(shape, lambda *_: (0,) * nd, pipeline_mode=pl.Buffered(1))


def _stage_weights(jobs):
    uses, staged = {}, []
    for src, ring, sems, dst in jobs:
        slot = uses.get(id(ring), 0) % ring.shape[0]
        uses[id(ring)] = uses.get(id(ring), 0) + 1
        staged.append((pltpu.make_async_copy(src, ring.at[slot], sems.at[slot]), ring, slot, dst))
    in_flight = 2
    for copy, _, _, _ in staged[:in_flight]:
        copy.start()
    for i, (copy, ring, slot, dst) in enumerate(staged):
        copy.wait()
        dst[...] = ring[slot].astype(BF16)
        if i + in_flight < len(staged):
            staged[i + in_flight][0].start()


def _row_chunks(w_hbm, dst, ring, sems, col0=0):
    rows, cols = dst.shape
    step = ring.shape[1]
    assert rows % step == 0 and ring.shape[2] == cols
    return [(w_hbm.at[0, pl.ds(r0, step), pl.ds(col0, cols)], ring, sems, dst.at[pl.ds(r0, step), :])
            for r0 in range(0, rows, step)]


RING_FF = (2, 64, D_FF)
RING_MODEL = (2, 128, D_MODEL)
RING_UQKV = (2, 128, UQKV_WIDTH)
RING_GATE = (2, 64, 2 * D_MODEL)
RING_GROUP = (1, len(POOL_WINDOWS) * POOL_GROUP, POOL_GROUP)


def _ring_scratch(*rings):
    return ([pltpu.VMEM(ring, F32) for ring in rings]
            + [pltpu.SemaphoreType.DMA((ring[0],)) for ring in rings])


FF_SLOTS = 2 * (D_FF // FF_CHUNK)


def _swiglu(x, w1_ref, w3_ref, w2_ref, h_s, side_work):
    assert all(0 <= slot <= FF_SLOTS for slot in side_work)
    run = lambda slot: side_work.get(slot, lambda: None)()
    xb = x.astype(BF16)
    for j in range(D_FF // FF_CHUNK):
        cols = slice(j * FF_CHUNK, (j + 1) * FF_CHUNK)
        a = _dot(xb, w1_ref[:, cols])
        run(2 * j)
        b = _dot(xb, w3_ref[:, cols])
        h_s[:, cols] = ((a * jax.nn.sigmoid(a)) * b).astype(BF16)
        run(2 * j + 1)
    run(FF_SLOTS)
    return _dot(h_s[...], w2_ref[...])


def _rope(x, cos, sin_signed, first_half):
    fwd = pltpu.roll(x, LANES - HEAD_DIM // 2, axis=1)
    bwd = pltpu.roll(x, HEAD_DIM // 2, axis=1)
    return x * cos + jnp.where(first_half, fwd, bwd) * sin_signed


def _swap_halves(x):
    return pltpu.roll(x, HEAD_DIM, axis=1)


def _swap_halves_wide(x):
    return jnp.concatenate([_swap_halves(x[:, c:c + LANES]) for c in range(0, x.shape[1], LANES)], axis=1)


def _lane_split(x, x_sw, kh):
    low = lax.broadcasted_iota(jnp.int32, (1, LANES), 1) < HEAD_DIM
    lo, hi = (x, x_sw) if kh == 0 else (x_sw, x)
    return jnp.concatenate([jnp.where(low, lo, 0.0), jnp.where(low, 0.0, hi)], axis=0).astype(BF16)


def _sink_softmax(q_pairs, keys, kh, bias, sinks_ref):
    s = _dot_nt(q_pairs, _lane_split(keys, _swap_halves(keys), kh)) + bias
    second_pair = lax.broadcasted_iota(jnp.int32, (s.shape[0], 1), 0) >= WINDOW
    probs, denoms = [], []
    for c in range(2):
        sc = s[:, c * KEY_SPAN:(c + 1) * KEY_SPAN]
        head = kh * HEADS_PER_KV + c
        sink = jnp.where(second_pair, sinks_ref[head + 2], sinks_ref[head])
        m = jnp.maximum(jnp.max(sc, axis=-1, keepdims=True), sink)
        p = jnp.exp(sc - m)
        denoms.append(jnp.sum(p, axis=-1, keepdims=True) + jnp.exp(sink - m))
        probs.append(p.astype(BF16))
    return jnp.concatenate(probs, axis=1), denoms


def _weighted_values(probs, denoms, vals, kh):
    low = lax.broadcasted_iota(jnp.int32, (1, LANES), 1) < HEAD_DIM
    o = _dot(probs, _lane_split(vals, _swap_halves(vals), kh))
    return o / jnp.where(low, denoms[0], denoms[1])


def _prompt_context_work(tile, q_c, k_c, v_c, u_c, bias_ref, sinks_ref,
                         pool_ref, attn_ref, kt_ref, vt_ref, ulast_ref):
    seq_tile = (tile + TILES_PER_SEQ) % TILES_PER_SEQ
    first_tile = seq_tile == 0
    softmaxed = {}

    def scores(unit, blk, kh):
        r0 = blk * WINDOW
        bias = bias_ref[jnp.where(first_tile, 1, 0)] if blk == 0 else bias_ref[0]
        c0 = 2 * kh * LANES
        q_pairs = jnp.concatenate([q_c[r0:r0 + WINDOW, c0:c0 + LANES],
                                   q_c[r0:r0 + WINDOW, c0 + LANES:c0 + 2 * LANES]], axis=0)
        softmaxed[unit] = _sink_softmax(q_pairs, k_c[r0:r0 + KEY_SPAN, :], kh, bias, sinks_ref)

    def values(unit, blk, kh):
        r0 = blk * WINDOW
        c0 = 2 * kh * LANES
        o = _weighted_values(*softmaxed.pop(unit), v_c[r0:r0 + KEY_SPAN, :], kh)
        attn_ref[r0:r0 + WINDOW, c0:c0 + LANES] = o[:WINDOW].astype(BF16)
        attn_ref[r0:r0 + WINDOW, c0 + LANES:c0 + 2 * LANES] = o[WINDOW:].astype(BF16)

    def pool(groups):
        pos = seq_tile * ROW_TILE + lax.broadcasted_iota(jnp.int32, (ROW_TILE, 1), 0)
        for g in groups:
            w = POOL_WINDOWS[g]
            cols = slice(g * POOL_GROUP, (g + 1) * POOL_GROUP)
            cur = u_c[POOL_PAD:POOL_PAD + ROW_TILE, cols]
            acc = cur
            for j in range(1, w):
                acc = acc + u_c[POOL_PAD - j:POOL_PAD - j + ROW_TILE, cols]
            cnt = jnp.minimum(pos + 1, w).astype(F32)
            pool_ref[:, cols] = (acc / cnt - cur).astype(BF16)

    def sequence_state():
        kt_ref[0] = k_c[ROW_TILE:ROW_TILE + WINDOW, :].T
        vt_ref[0] = v_c[ROW_TILE:ROW_TILE + WINDOW, :].T
        ulast_ref[0] = u_c[ROW_TILE:ROW_TILE + POOL_PAD, :]

    work = {"pool_wide": lambda: pool((3,)), "pool_narrow": lambda: pool((0, 1, 2)),
            "sequence_state": sequence_state}
    for blk in range(ROW_TILE // WINDOW):
        for kh in range(N_KV_HEADS):
            unit = blk * N_KV_HEADS + kh
            work["scores", unit] = lambda unit=unit, blk=blk, kh=kh: scores(unit, blk, kh)
            work["values", unit] = lambda unit=unit, blk=blk, kh=kh: values(unit, blk, kh)
    return work


def _prompt_bias():
    r = np.arange(2 * WINDOW)[:, None] % WINDOW
    c = np.arange(2 * KEY_SPAN)[None, :] % KEY_SPAN
    valid = (r <= c) & (c <= r + WINDOW)
    first = valid & (c >= WINDOW)
    return np.where(np.stack([valid, first]), 0.0, NEG_INF).astype(np.float32)


def _front_kernel(sinks_ref, xp_ref, xs_ref, w1_hbm, w3_hbm, w2_hbm, win_hbm, g_ref, b_ref, rope_ref, bias_ref,
                  h_ref, pa_ref, us_ref, qs_ref, ks_ref, vs_ref, kt_ref, vt_ref, ulast_ref,
                  z_c, q_c, k_c, v_c, u_c, h_s, w1_ref, w3_ref, w2_ref, wu_ref,
                  ring_ff, ring_model, ring_uqkv, sem_ff, sem_model, sem_uqkv):
    r = pl.program_id(0)
    prev = r - 1
    pool_ref = pa_ref.at[:, 0:POOL_WIDTH]
    attn_ref = pa_ref.at[:, POOL_WIDTH:POOL_WIDTH + Q_WIDTH]

    @pl.when(r == 0)
    def _():
        for ref in (z_c, q_c, k_c, v_c, u_c):
            ref[...] = jnp.zeros(ref.shape, ref.dtype)
        _stage_weights(_row_chunks(w1_hbm, w1_ref, ring_ff, sem_ff)
                       + _row_chunks(w3_hbm, w3_ref, ring_ff, sem_ff)
                       + _row_chunks(win_hbm, wu_ref, ring_uqkv, sem_uqkv)
                       + _row_chunks(w2_hbm, w2_ref, ring_model, sem_model))

    def norm_previous():
        h1 = _layer_norm(z_c[...], g_ref[...], b_ref[...])
        h_ref[...] = h1
        return h1.astype(BF16)

    def project_previous(h1b):
        z = _dot(h1b, wu_ref[...])
        tile = jnp.clip(prev, 0, PROMPT_TILES)
        table = jnp.where(tile == PROMPT_TILES, TILES_PER_SEQ, tile % TILES_PER_SEQ)
        rows = pl.ds(pl.multiple_of(table * ROW_TILE, ROW_TILE), ROW_TILE)
        cos = rope_ref[rows, 0:LANES]
        sin_signed = rope_ref[rows, LANES:2 * LANES]
        lane = lax.broadcasted_iota(jnp.int32, cos.shape, 1)
        first_half = (lane & (HEAD_DIM // 2)) == 0
        q = []
        for c in range(Q_WIDTH // LANES):
            lo = POOL_WIDTH + c * LANES
            qc = _rope(z[:, lo:lo + LANES], cos, sin_signed, first_half)
            q.append((qc * (HEAD_DIM ** -0.5)).astype(BF16))
        ko = POOL_WIDTH + Q_WIDTH
        k = _rope(z[:, ko:ko + KV_WIDTH], cos, sin_signed, first_half)
        return z[:, :POOL_WIDTH], q, k, z[:, ko + KV_WIDTH:ko + 2 * KV_WIDTH]

    def tile_step(x_ref):
        normed, projected = [], []

        def norm():
            normed.append(norm_previous())

        def project():
            projected.extend(project_previous(normed[0]))

        def carry():
            u, q, k, v = projected
            for c, qc in enumerate(q):
                q_c[:, c * LANES:(c + 1) * LANES] = qc
            k_c[0:WINDOW, :] = k_c[ROW_TILE:ROW_TILE + WINDOW, :]
            v_c[0:WINDOW, :] = v_c[ROW_TILE:ROW_TILE + WINDOW, :]
            starts_seq = (prev + TILES_PER_SEQ) % TILES_PER_SEQ == 0
            u_c[0:POOL_PAD, :] = jnp.where(starts_seq, 0.0, u_c[ROW_TILE:ROW_TILE + POOL_PAD, :])
            k_c[WINDOW:WINDOW + ROW_TILE, :] = k
            v_c[WINDOW:WINDOW + ROW_TILE, :] = v
            u_c[POOL_PAD:POOL_PAD + ROW_TILE, :] = u

        context = _prompt_context_work(prev, q_c, k_c, v_c, u_c, bias_ref, sinks_ref,
                                       pool_ref, attn_ref, kt_ref, vt_ref, ulast_ref)
        side = {0: norm, 3: project, 4: carry}
        for unit in range(8):
            side[6 + 2 * unit] = context["scores", unit]
            side[7 + 2 * unit] = context["values", unit]

        def pooling_and_state():
            context["pool_wide"]()
            context["pool_narrow"]()
            context["sequence_state"]()

        side[FF_SLOTS] = pooling_and_state
        x = x_ref[...]
        y = _swiglu(x, w1_ref, w3_ref, w2_ref, h_s, side)
        z_c[...] = ALPHA * x + 0.5 * y

    @pl.when(r < PROMPT_TILES)
    def _():
        tile_step(xp_ref)

    @pl.when(r == PROMPT_TILES)
    def _():
        tile_step(xs_ref)

    @pl.when(r == ROW_TILES)
    def _():
        u, q, k, v = project_previous(norm_previous())
        us_ref[...] = u
        for c, qc in enumerate(q):
            qs_ref[:, c * LANES:(c + 1) * LANES] = qc
        ks_ref[...] = k
        vs_ref[...] = v


def _front(xp, xs, w1, w3, w2, w_in, g, b, rope, sinks):
    rows = ROW_TILES * ROW_TILE

    def lagged(width):
        return pl.BlockSpec((ROW_TILE, width), lambda r: (jnp.clip(r - 1, 0, PROMPT_TILES - 1), 0))

    def seq_of_lagged(shape):
        return pl.BlockSpec(shape, lambda r: (jnp.clip(r - 1, 0, PROMPT_TILES - 1) // TILES_PER_SEQ, 0, 0))

    sample = lambda width: pl.BlockSpec((ROW_TILE, width), lambda r: (0, 0))
    hbm = pl.BlockSpec(memory_space=pl.ANY)
    return pl.pallas_call(
        _front_kernel,
        out_shape=(jax.ShapeDtypeStruct((rows, D_MODEL), F32),
                   jax.ShapeDtypeStruct((PROMPT_ROWS, POOL_WIDTH + Q_WIDTH), BF16),
                   jax.ShapeDtypeStruct((SAMPLE_ROWS, POOL_WIDTH), F32),
                   jax.ShapeDtypeStruct((SAMPLE_ROWS, Q_WIDTH), BF16),
                   jax.ShapeDtypeStruct((SAMPLE_ROWS, KV_WIDTH), F32),
                   jax.ShapeDtypeStruct((SAMPLE_ROWS, KV_WIDTH), F32),
                   jax.ShapeDtypeStruct((BATCH, KV_WIDTH, WINDOW), F32),
                   jax.ShapeDtypeStruct((BATCH, KV_WIDTH, WINDOW), F32),
                   jax.ShapeDtypeStruct((BATCH, POOL_PAD, POOL_WIDTH), F32)),
        grid=(ROW_TILES + 1,),
        in_specs=[pl.BlockSpec(memory_space=pltpu.SMEM),
                  pl.BlockSpec((ROW_TILE, D_MODEL), lambda r: (jnp.minimum(r, PROMPT_TILES - 1), 0)),
                  _resident((SAMPLE_ROWS, D_MODEL)),
                  hbm, hbm, hbm, hbm, _resident((1, D_MODEL)), _resident((1, D_MODEL)),
                  _resident(rope.shape), _resident((2, 2 * WINDOW, 2 * KEY_SPAN))],
        out_specs=(pl.BlockSpec((ROW_TILE, D_MODEL), lambda r: (jnp.clip(r - 1, 0, PROMPT_TILES), 0)),
                   lagged(POOL_WIDTH + Q_WIDTH),
                   sample(POOL_WIDTH), sample(Q_WIDTH), sample(KV_WIDTH), sample(KV_WIDTH),
                   seq_of_lagged((1, KV_WIDTH, WINDOW)), seq_of_lagged((1, KV_WIDTH, WINDOW)),
                   seq_of_lagged((1, POOL_PAD, POOL_WIDTH))),
        scratch_shapes=[pltpu.VMEM((ROW_TILE, D_MODEL), F32),
                        pltpu.VMEM((ROW_TILE, Q_WIDTH), BF16),
                        pltpu.VMEM((WINDOW + ROW_TILE, KV_WIDTH), F32),
                        pltpu.VMEM((WINDOW + ROW_TILE, KV_WIDTH), F32),
                        pltpu.VMEM((POOL_PAD + ROW_TILE, POOL_WIDTH), F32),
                        pltpu.VMEM((ROW_TILE, D_FF), BF16),
                        pltpu.VMEM((D_MODEL, D_FF), BF16), pltpu.VMEM((D_MODEL, D_FF), BF16),
                        pltpu.VMEM((D_FF, D_MODEL), BF16), pltpu.VMEM((D_MODEL, UQKV_WIDTH), BF16)]
                       + _ring_scratch(RING_FF, RING_MODEL, RING_UQKV),
        compiler_params=pltpu.CompilerParams(dimension_semantics=("arbitrary",),
                                             vmem_limit_bytes=VMEM_LIMIT_BYTES),
        name="front",
    )(sinks, xp, xs, w1, w3, w2, w_in, g, b, rope, jnp.asarray(_prompt_bias()))


def _sample_bias():
    row = np.arange(HEADS_PER_KV * DEC_SEQ * SUB_GROUP)
    row_t = (row // SUB_GROUP) % DEC_SEQ
    row_b = row % SUB_GROUP
    col = np.arange(SUB_GROUP * WINDOW)
    ok_c = (col[None, :] // WINDOW == row_b[:, None]) & (col[None, :] % WINDOW >= row_t[:, None])
    new = np.arange(DEC_SEQ * SUB_GROUP)
    ok_n = (new[None, :] % SUB_GROUP == row_b[:, None]) & (new[None, :] // SUB_GROUP <= row_t[:, None])
    to_bias = lambda ok: np.where(ok, 0.0, NEG_INF).astype(np.float32)
    return to_bias(ok_c), to_bias(ok_n)


def _sample_ctx_kernel(sinks_ref, q0, q1, q2, q3, k0, k1, k2, k3, v0, v1, v2, v3, u0, u1, u2, u3,
                       ckt_ref, cvt_ref, cu_ref, bias_c_ref, bias_n_ref,
                       attn_ref, pool_ref, kt_out, vt_out, pu_out, kbt_s, vbt_s):
    q_t = [q[...].astype(F32) for q in (q0, q1, q2, q3)]
    k_t = [k[...] for k in (k0, k1, k2, k3)]
    v_t = [v[...] for v in (v0, v1, v2, v3)]

    rows_u = [cu_ref[i] for i in range(POOL_BUF)] + [u[...] for u in (u0, u1, u2, u3)]
    for t in range(DEC_SEQ):
        pooled = []
        for g, w in enumerate(POOL_WINDOWS):
            cols = slice(g * POOL_GROUP, (g + 1) * POOL_GROUP)
            cur = rows_u[POOL_BUF + t][:, cols]
            acc = cur
            for j in range(1, w):
                acc = acc + rows_u[POOL_BUF + t - j][:, cols]
            pooled.append(acc / float(w) - cur)
        pool_ref[t] = jnp.concatenate(pooled, axis=1)
    for i in range(POOL_BUF):
        pu_out[i] = rows_u[i + DEC_SEQ]

    bias_c = bias_c_ref[...]
    bias_n = bias_n_ref[...]
    head_of_row = lax.broadcasted_iota(jnp.int32, (bias_c.shape[0], 1), 0) // (DEC_SEQ * SUB_GROUP)
    low = lax.broadcasted_iota(jnp.int32, (1, LANES), 1) < HEAD_DIM
    q_sw = [_swap_halves_wide(q) for q in q_t]
    k_sw = [_swap_halves(k) for k in k_t]
    v_sw = [_swap_halves(v) for v in v_t]
    for sub in range(SEQ_GROUP // SUB_GROUP):
        rows = slice(sub * SUB_GROUP, (sub + 1) * SUB_GROUP)
        for kh in range(N_KV_HEADS):
            def q_piece(t, head):
                src = q_t[t] if head % 2 == 0 else q_sw[t]
                chunk = head // 2
                return jnp.where(low, src[rows, chunk * LANES:(chunk + 1) * LANES], 0.0)

            def kv_first(c):
                return c if kh == 0 else jnp.concatenate([c[HEAD_DIM:], c[:HEAD_DIM]], axis=0)

            def kv_twice(c):
                part = c[kh * HEAD_DIM:(kh + 1) * HEAD_DIM]
                return jnp.concatenate([part, part], axis=0)

            lhs = jnp.concatenate([q_piece(t, kh * HEADS_PER_KV + g)
                                   for g in range(HEADS_PER_KV) for t in range(DEC_SEQ)], axis=0).astype(BF16)
            kcat = jnp.concatenate([kv_first(ckt_ref[sub * SUB_GROUP + b]) for b in range(SUB_GROUP)],
                                   axis=1).astype(BF16)
            vcat = jnp.concatenate([kv_twice(cvt_ref[sub * SUB_GROUP + b]) for b in range(SUB_GROUP)],
                                   axis=1).astype(BF16)
            knew = jnp.concatenate([(k_t[t] if kh == 0 else k_sw[t])[rows] for t in range(DEC_SEQ)],
                                   axis=0).astype(BF16)
            vnew = jnp.concatenate([(jnp.where(low, v_t[t], v_sw[t]) if kh == 0 else
                                     jnp.where(low, v_sw[t], v_t[t]))[rows] for t in range(DEC_SEQ)],
                                   axis=0).astype(BF16)
            s_c = _dot(lhs, kcat) + bias_c
            s_n = _dot_nt(lhs, knew) + bias_n
            sink = jnp.zeros(head_of_row.shape, F32)
            for g in range(HEADS_PER_KV):
                sink = jnp.where(head_of_row == g, sinks_ref[kh * HEADS_PER_KV + g], sink)
            m = jnp.maximum(jnp.maximum(jnp.max(s_c, axis=-1, keepdims=True),
                                        jnp.max(s_n, axis=-1, keepdims=True)), sink)
            p_c = jnp.exp(s_c - m)
            p_n = jnp.exp(s_n - m)
            denom = (jnp.sum(p_c, axis=-1, keepdims=True) + jnp.sum(p_n, axis=-1, keepdims=True)
                     + jnp.exp(sink - m))
            o = (_dot_nt(p_c.astype(BF16), vcat) + _dot(p_n.astype(BF16), vnew)) / denom
            for t in range(DEC_SEQ):
                for pair in range(HEADS_PER_KV // 2):
                    piece = lambda g: o[(g * DEC_SEQ + t) * SUB_GROUP:(g * DEC_SEQ + t + 1) * SUB_GROUP]
                    c0 = (kh * HEADS_PER_KV // 2 + pair) * LANES
                    attn_ref[t, rows, c0:c0 + LANES] = jnp.where(low, piece(2 * pair), piece(2 * pair + 1))

    zeros = jnp.zeros((LANES - DEC_SEQ * SEQ_GROUP, KV_WIDTH), F32)
    kbt_s[DEC_SEQ * SEQ_GROUP:, :] = zeros
    vbt_s[DEC_SEQ * SEQ_GROUP:, :] = zeros
    for t in range(DEC_SEQ):
        kbt_s[pl.ds(t, SEQ_GROUP, stride=DEC_SEQ), :] = k_t[t]
        vbt_s[pl.ds(t, SEQ_GROUP, stride=DEC_SEQ), :] = v_t[t]
    knew_t = kbt_s[...].T
    vnew_t = vbt_s[...].T
    keep = lax.broadcasted_iota(jnp.int32, (1, WINDOW), 1) < WINDOW - DEC_SEQ
    for b in range(SEQ_GROUP):
        shift_new = WINDOW - DEC_SEQ - DEC_SEQ * b
        kt_out[b] = jnp.where(keep, pltpu.roll(ckt_ref[b], WINDOW - DEC_SEQ, axis=1),
                              pltpu.roll(knew_t, shift_new, axis=1))
        vt_out[b] = jnp.where(keep, pltpu.roll(cvt_ref[b], WINDOW - DEC_SEQ, axis=1),
                              pltpu.roll(vnew_t, shift_new, axis=1))


def _sample_ctx(us, qs, ks, vs, ckt, cvt, cu, sinks):
    groups = DEC_BATCH // SEQ_GROUP

    def token_rows(t, width):
        return pl.BlockSpec((SEQ_GROUP, width), lambda i: (t * groups + i, 0))

    def per_token(width):
        return [token_rows(t, width) for t in range(DEC_SEQ)]

    cache_spec = pl.BlockSpec((SEQ_GROUP, KV_WIDTH, WINDOW), lambda i: (i, 0, 0))
    pool_rows_spec = pl.BlockSpec((POOL_BUF, SEQ_GROUP, POOL_WIDTH), lambda i: (0, i, 0))
    by_token = lambda width: pl.BlockSpec((DEC_SEQ, SEQ_GROUP, width), lambda i: (0, i, 0))
    bias_c, bias_n = _sample_bias()
    return pl.pallas_call(
        _sample_ctx_kernel,
        out_shape=(jax.ShapeDtypeStruct((DEC_SEQ, DEC_BATCH, Q_WIDTH), F32),
                   jax.ShapeDtypeStruct((DEC_SEQ, DEC_BATCH, POOL_WIDTH), F32),
                   jax.ShapeDtypeStruct((DEC_BATCH, KV_WIDTH, WINDOW), F32),
                   jax.ShapeDtypeStruct((DEC_BATCH, KV_WIDTH, WINDOW), F32),
                   jax.ShapeDtypeStruct((POOL_BUF, DEC_BATCH, POOL_WIDTH), F32)),
        grid=(groups,),
        in_specs=[pl.BlockSpec(memory_space=pltpu.SMEM)]
                 + per_token(Q_WIDTH) + per_token(KV_WIDTH) + per_token(KV_WIDTH) + per_token(POOL_WIDTH)
                 + [cache_spec, cache_spec, pool_rows_spec, _resident(bias_c.shape), _resident(bias_n.shape)],
        out_specs=(by_token(Q_WIDTH), by_token(POOL_WIDTH), cache_spec, cache_spec, pool_rows_spec),
        scratch_shapes=[pltpu.VMEM((LANES, KV_WIDTH), F32), pltpu.VMEM((LANES, KV_WIDTH), F32)],
        compiler_params=pltpu.CompilerParams(dimension_semantics=("parallel",),
                                             vmem_limit_bytes=VMEM_LIMIT_BYTES),
        name="sample_ctx",
    )(sinks, *([qs] * DEC_SEQ), *([ks] * DEC_SEQ), *([vs] * DEC_SEQ), *([us] * DEC_SEQ),
      ckt, cvt, cu, jnp.asarray(bias_c), jnp.asarray(bias_n))


def _back_kernel(h_ref, pa_ref, pools_ref, attns_ref,
                 win_hbm, wgrp_hbm, scale_ref, wpo_hbm, wao_hbm, wout_hbm, g2_ref, b2_ref,
                 w1_hbm, w3_hbm, w2_hbm, g3_ref, b3_ref, yp_ref, ys_ref, z2_c, z3_c, h_s,
                 wg_ref, wgrp_ref, wpo_ref, wao_ref, wout_ref, w1_ref, w3_ref, w2_ref,
                 ring_ff, ring_model, ring_gate, ring_group, sem_ff, sem_model, sem_gate, sem_group):
    r = pl.program_id(0)

    @pl.when(r == 0)
    def _():
        z2_c[...] = jnp.zeros(z2_c.shape, z2_c.dtype)
        z3_c[...] = jnp.zeros(z3_c.shape, z3_c.dtype)
        _stage_weights(_row_chunks(wgrp_hbm, wgrp_ref, ring_group, sem_group)
                       + _row_chunks(wpo_hbm, wpo_ref, ring_model, sem_model)
                       + _row_chunks(wao_hbm, wao_ref, ring_model, sem_model)
                       + _row_chunks(w1_hbm, w1_ref, ring_ff, sem_ff)
                       + _row_chunks(w3_hbm, w3_ref, ring_ff, sem_ff)
                       + _row_chunks(win_hbm, wg_ref, ring_gate, sem_gate, col0=UQKV_WIDTH)
                       + _row_chunks(wout_hbm, wout_ref, ring_model, sem_model)
                       + _row_chunks(w2_hbm, w2_ref, ring_model, sem_model))

    @pl.when(r <= ROW_TILES)
    def _():
        is_sample = r >= PROMPT_TILES
        h1 = h_ref[...]
        pool_in = jnp.where(is_sample, pools_ref[...].astype(BF16), pa_ref[:, 0:POOL_WIDTH])
        attn_o = jnp.where(is_sample, attns_ref[...].astype(BF16), pa_ref[:, POOL_WIDTH:POOL_WIDTH + Q_WIDTH])
        zs = [_dot(pool_in[:, g * POOL_GROUP:(g + 1) * POOL_GROUP],
                   wgrp_ref[g * POOL_GROUP:(g + 1) * POOL_GROUP, :])
              for g in range(len(POOL_WINDOWS))]
        pool_z = jnp.concatenate(zs, axis=1) * scale_ref[...]
        a = _dot(pool_z.astype(BF16), wpo_ref[...])
        b = _dot(attn_o, wao_ref[...])

        yp_ref[...] = _layer_norm(z3_c[...], g3_ref[...], b3_ref[...])
        h2 = _layer_norm(z2_c[...], g2_ref[...], b2_ref[...])
        gated = []

        def gate():
            gates = jax.nn.sigmoid(_dot(h1.astype(BF16), wg_ref[...]))
            gated.append((gates[:, :D_MODEL] * a + gates[:, D_MODEL:] * b).astype(BF16))

        def project_out():
            z2_c[...] = ALPHA * h1 + _dot(gated[0], wout_ref[...])

        ff = _swiglu(h2, w1_ref, w3_ref, w2_ref, h_s, {1: gate, 9: project_out})
        z3_c[...] = ALPHA * h2 + 0.5 * ff

    @pl.when(r == ROW_TILES + 1)
    def _():
        ys_ref[...] = _layer_norm(z3_c[...], g3_ref[...], b3_ref[...])


def _back(h1, pool_attn, pool_s, attn_s, w_in, wgrp, scale, wpo, wao, wout, g2, b2, w1, w3, w2, g3, b3):
    lagged = pl.BlockSpec((ROW_TILE, D_MODEL), lambda r: (jnp.clip(r - 2, 0, PROMPT_TILES - 1), 0))
    hbm = pl.BlockSpec(memory_space=pl.ANY)
    return pl.pallas_call(
        _back_kernel,
        out_shape=(jax.ShapeDtypeStruct((PROMPT_ROWS, D_MODEL), F32),
                   jax.ShapeDtypeStruct((SAMPLE_ROWS, D_MODEL), F32)),
        grid=(ROW_TILES + 2,),
        in_specs=[pl.BlockSpec((ROW_TILE, D_MODEL), lambda r: (jnp.minimum(r, PROMPT_TILES), 0)),
                  pl.BlockSpec((ROW_TILE, POOL_WIDTH + Q_WIDTH), lambda r: (jnp.minimum(r, PROMPT_TILES - 1), 0)),
                  _resident((SAMPLE_ROWS, POOL_WIDTH)), _resident((SAMPLE_ROWS, Q_WIDTH)),
                  hbm, hbm, _resident((1, POOL_WIDTH)), hbm, hbm, hbm,
                  _resident((1, D_MODEL)), _resident((1, D_MODEL)),
                  hbm, hbm, hbm, _resident((1, D_MODEL)), _resident((1, D_MODEL))],
        out_specs=(lagged, pl.BlockSpec((ROW_TILE, D_MODEL), lambda r: (0, 0))),
        scratch_shapes=[pltpu.VMEM((ROW_TILE, D_MODEL), F32), pltpu.VMEM((ROW_TILE, D_MODEL), F32),
                        pltpu.VMEM((ROW_TILE, D_FF), BF16),
                        pltpu.VMEM((D_MODEL, 2 * D_MODEL), BF16),
                        pltpu.VMEM((len(POOL_WINDOWS) * POOL_GROUP, POOL_GROUP), BF16),
                        pltpu.VMEM((POOL_WIDTH, D_MODEL), BF16), pltpu.VMEM((Q_WIDTH, D_MODEL), BF16),
                        pltpu.VMEM((D_MODEL, D_MODEL), BF16),
                        pltpu.VMEM((D_MODEL, D_FF), BF16), pltpu.VMEM((D_MODEL, D_FF), BF16),
                        pltpu.VMEM((D_FF, D_MODEL), BF16)]
                       + _ring_scratch(RING_FF, RING_MODEL, RING_GATE, RING_GROUP),
        compiler_params=pltpu.CompilerParams(dimension_semantics=("arbitrary",),
                                             vmem_limit_bytes=VMEM_LIMIT_BYTES),
        name="back",
    )(h1, pool_attn, pool_s, attn_s, w_in, wgrp, scale, wpo, wao, wout, g2, b2, w1, w3, w2, g3, b3)


def _rope_tables(pos):
    half = HEAD_DIM // 2
    freqs = ROPE_THETA ** (-2.0 * jnp.arange(half, dtype=F32) / HEAD_DIM)
    ang = pos.astype(F32)[:, None] * freqs[None, :]
    cos, sin = jnp.cos(ang), jnp.sin(ang)
    return jnp.tile(cos, (1, 4)), jnp.concatenate([-sin, sin, -sin, sin], axis=1)


def kernel(x_prompt, x_sample, cache_pool_u, cache_k_win, cache_v_win, w_in, pool_w_grp, pool_scale,
           attn_sinks, w_pool_out, w_attn_out, w_out, ffn1_w1, ffn1_w3, ffn1_w2, ffn2_w1, ffn2_w3,
           ffn2_w2, ln1_g, ln1_b, ln2_g, ln2_b, ln3_g, ln3_b):
    assert DEPTH == 1 and w_in.shape[0] == 1
    l = 0
    vec = lambda p: p[l].reshape(1, -1)
    sinks = attn_sinks[l]

    pos = jnp.concatenate([jnp.arange(SEQ, dtype=jnp.int32),
                           jnp.repeat(PAST_LEN + jnp.arange(DEC_SEQ, dtype=jnp.int32), DEC_BATCH)])
    rope = jnp.concatenate(_rope_tables(pos), axis=1)

    xp = x_prompt.reshape(PROMPT_ROWS, D_MODEL)
    xs = jnp.transpose(x_sample, (1, 0, 2)).reshape(SAMPLE_ROWS, D_MODEL)
    (h1, pool_attn, us, qs, ks, vs, kt_last, vt_last, u_last) = _front(
        xp, xs, ffn1_w1, ffn1_w3, ffn1_w2, w_in, vec(ln1_g), vec(ln1_b), rope, sinks)

    to_t = lambda c: jnp.transpose(c[l], (0, 2, 3, 1)).reshape(DEC_BATCH, KV_WIDTH, WINDOW)
    cu = jnp.transpose(cache_pool_u[l], (1, 0, 2))
    attn_s, pool_s, kt_s, vt_s, pu_s = _sample_ctx(us, qs, ks, vs, to_t(cache_k_win), to_t(cache_v_win), cu, sinks)

    wgrp = pool_w_grp.reshape(DEPTH, len(POOL_WINDOWS) * POOL_GROUP, POOL_GROUP)
    yp, ys = _back(h1, pool_attn, pool_s.reshape(SAMPLE_ROWS, POOL_WIDTH), attn_s.reshape(SAMPLE_ROWS, Q_WIDTH),
                   w_in, wgrp, vec(pool_scale), w_pool_out, w_attn_out, w_out, vec(ln2_g), vec(ln2_b),
                   ffn2_w1, ffn2_w3, ffn2_w2, vec(ln3_g), vec(ln3_b))
    yp = yp.reshape(BATCH, SEQ, D_MODEL)
    ys = jnp.transpose(ys.reshape(DEC_SEQ, DEC_BATCH, D_MODEL), (1, 0, 2))

    from_t = lambda c, n: jnp.transpose(c.reshape(n, N_KV_HEADS, HEAD_DIM, WINDOW), (0, 3, 1, 2))[None]
    pool_u_prompt = u_last[None, :, POOL_PAD - POOL_BUF:]
    pool_u_sample = jnp.transpose(pu_s, (1, 0, 2))[None]
    return (yp, ys, pool_u_prompt, from_t(kt_last, BATCH), from_t(vt_last, BATCH),
            pool_u_sample, from_t(kt_s, DEC_BATCH), from_t(vt_s, DEC_BATCH))
```

```python
import jax
import jax.numpy as jnp
import numpy as np
from jax import lax
from jax.experimental import pallas as pl
from jax.experimental.pallas import tpu as pltpu

D_MODEL = 1024
BATCH = 8
SEQ = 2048
DEC_BATCH = 128
DEC_SEQ = 4
PAST_LEN = 8192
POOL_WINDOWS = (2, 4, 8, 16)
POOL_GROUP = 128
POOL_WIDTH = 512
POOL_BUF = 15
N_HEADS = 8
N_KV_HEADS = 2
HEADS_PER_KV = N_HEADS // N_KV_HEADS
HEAD_DIM = 64
Q_WIDTH = 512
KV_WIDTH = 128
WINDOW = 128
ROPE_THETA = 10000.0
D_FF = 2816
DEPTH = 1
ALPHA = (2.0 * DEPTH) ** 0.25
LN_EPS = 1e-5
NEG_INF = -1e30
UQKV_WIDTH = POOL_WIDTH + Q_WIDTH + 2 * KV_WIDTH

LANES = 128
KEY_SPAN = 2 * WINDOW
VMEM_LIMIT_BYTES = 61 * 1024 * 1024

ROW_TILE = 512
TILES_PER_SEQ = SEQ // ROW_TILE
PROMPT_ROWS = BATCH * SEQ
PROMPT_TILES = PROMPT_ROWS // ROW_TILE
SAMPLE_ROWS = DEC_BATCH * DEC_SEQ
ROW_TILES = PROMPT_TILES + 1
POOL_PAD = 16
SEQ_GROUP = 16
SUB_GROUP = 8
FF_CHUNK = 256
OUT_BLOCK = 256

BF16 = jnp.bfloat16
F32 = jnp.float32


def _dot(a, b):
    return jnp.dot(a, b, preferred_element_type=F32)


def _dot_nt(a, b):
    return lax.dot_general(a, b, (((1,), (1,)), ((), ())), preferred_element_type=F32)


def _layer_norm(y, g, b):
    mu = jnp.mean(y, axis=-1, keepdims=True)
    yc = y - mu
    var = jnp.mean(yc * yc, axis=-1, keepdims=True)
    return yc * lax.rsqrt(var + LN_EPS) * g + b


def _resident(shape):
    nd = len(shape)
    return pl.BlockSpec(shape, lambda *_: (0,) * nd, pipeline_mode=pl.Buffered(1))


def _stage_weights(jobs):
    uses, staged = {}, []
    for src, ring, dst in jobs:
        slot, sem = ring[uses.get(id(ring), 0) % len(ring)]
        uses[id(ring)] = uses.get(id(ring), 0) + 1
        staged.append((pltpu.make_async_copy(src, slot, sem), slot, dst))
    assert all(len(ring) >= min(STAGE_IN_FLIGHT, uses[id(ring)]) for _, ring, _ in jobs)
    for copy, _, _ in staged[:STAGE_IN_FLIGHT]:
        copy.start()
    for i, (copy, slot, dst) in enumerate(staged):
        copy.wait()
        dst[...] = slot[...].astype(BF16)
        if i + STAGE_IN_FLIGHT < len(staged):
            staged[i + STAGE_IN_FLIGHT][0].start()


def _ring(buf, sems, rows=None):
    if rows is None:
        return [(buf.at[i], sems.at[i]) for i in range(buf.shape[0])]
    return [(buf.at[pl.ds(i * rows, rows), :], sems.at[i]) for i in range(buf.shape[0] // rows)]


def _row_chunks(w_hbm, col0, dst, ring):
    rows, cols = dst.shape
    step, width = ring[0][0].shape
    assert rows % step == 0 and width == cols
    return [(w_hbm.at[0, pl.ds(r0, step), pl.ds(col0, cols)], ring, dst.at[pl.ds(r0, step), :])
            for r0 in range(0, rows, step)]


STAGE_IN_FLIGHT = 4
RING_FF = (STAGE_IN_FLIGHT, 64, D_FF)
MODEL_RING_ROWS = ROW_TILE // STAGE_IN_FLIGHT
RING_NARROW = (STAGE_IN_FLIGHT, 128, UQKV_WIDTH - D_MODEL)
RING_GROUP = (1, len(POOL_WINDOWS) * POOL_GROUP, POOL_GROUP)


def _ring_scratch(*rings):
    return ([pltpu.VMEM(ring, F32) for ring in rings]
            + [pltpu.SemaphoreType.DMA((ring[0],)) for ring in rings]
            + [pltpu.SemaphoreType.DMA((STAGE_IN_FLIGHT,))])


FF_SLOTS = 2 * (D_FF // FF_CHUNK)


def _swiglu_residual(x, w1_ref, w3_ref, w2_ref, h_s, side_work, out_ref):
    assert all(0 <= slot <= FF_SLOTS for slot in side_work)
    run = lambda slot: side_work.get(slot, lambda: None)()
    xb = x.astype(BF16)
    for j in range(D_FF // FF_CHUNK):
        cols = slice(j * FF_CHUNK, (j + 1) * FF_CHUNK)
        a = _dot(xb, w1_ref[:, cols])
        run(2 * j)
        b = _dot(xb, w3_ref[:, cols])
        h_s[:, cols] = ((a * jax.nn.sigmoid(a)) * b).astype(BF16)
        run(2 * j + 1)
    run(FF_SLOTS)
    h = h_s[...]
    for c0 in range(0, D_MODEL, OUT_BLOCK):
        cols = slice(c0, c0 + OUT_BLOCK)
        out_ref[:, cols] = ALPHA * x[:, cols] + 0.5 * _dot(h, w2_ref[:, cols])


def _rope(x, cos, sin_signed, first_half):
    fwd = pltpu.roll(x, LANES - HEAD_DIM // 2, axis=1)
    bwd = pltpu.roll(x, HEAD_DIM // 2, axis=1)
    return x * cos + jnp.where(first_half, fwd, bwd) * sin_signed


def _swap_halves(x):
    return pltpu.roll(x, HEAD_DIM, axis=1)


def _swap_halves_wide(x):
    return jnp.concatenate([_swap_halves(x[:, c:c + LANES]) for c in range(0, x.shape[1], LANES)], axis=1)


def _lane_split(x, x_sw, kh):
    low = lax.broadcasted_iota(jnp.int32, (1, LANES), 1) < HEAD_DIM
    lo, hi = (x, x_sw) if kh == 0 else (x_sw, x)
    return jnp.concatenate([jnp.where(low, lo, 0.0), jnp.where(low, 0.0, hi)], axis=0).astype(BF16)


def _sink_softmax(q_pairs, keys, kh, bias, sinks_ref):
    s = _dot_nt(q_pairs, _lane_split(keys, _swap_halves(keys), kh)) + bias
    second_pair = lax.broadcasted_iota(jnp.int32, (s.shape[0], 1), 0) >= WINDOW
    probs, denoms = [], []
    for c in range(2):
        sc = s[:, c * KEY_SPAN:(c + 1) * KEY_SPAN]
        head = kh * HEADS_PER_KV + c
        sink = jnp.where(second_pair, sinks_ref[head + 2], sinks_ref[head])
        m = jnp.maximum(jnp.max(sc, axis=-1, keepdims=True), sink)
        p = jnp.exp(sc - m)
        denoms.append(jnp.sum(p, axis=-1, keepdims=True) + jnp.exp(sink - m))
        probs.append(p.astype(BF16))
    return jnp.concatenate(probs, axis=1), denoms


def _weighted_values(probs, denoms, vals, kh):
    low = lax.broadcasted_iota(jnp.int32, (1, LANES), 1) < HEAD_DIM
    o = _dot(probs, _lane_split(vals, _swap_halves(vals), kh))
    return o / jnp.where(low, denoms[0], denoms[1])


def _prompt_context_work(tile, q_c, k_c, v_c, u_c, bias_ref, sinks_ref,
                         pool_ref, attn_ref, kt_ref, vt_ref, ulast_ref):
    seq_tile = (tile + TILES_PER_SEQ) % TILES_PER_SEQ
    first_tile = seq_tile == 0
    softmaxed = {}

    def scores(unit, blk, kh):
        r0 = blk * WINDOW
        bias = bias_ref[jnp.where(first_tile, 1, 0)] if blk == 0 else bias_ref[0]
        c0 = 2 * kh * LANES
        q_pairs = jnp.concatenate([q_c[r0:r0 + WINDOW, c0:c0 + LANES],
                                   q_c[r0:r0 + WINDOW, c0 + LANES:c0 + 2 * LANES]], axis=0)
        softmaxed[unit] = _sink_softmax(q_pairs, k_c[r0:r0 + KEY_SPAN, :], kh, bias, sinks_ref)

    def values(unit, blk, kh):
        r0 = blk * WINDOW
        c0 = 2 * kh * LANES
        o = _weighted_values(*softmaxed.pop(unit), v_c[r0:r0 + KEY_SPAN, :], kh)
        attn_ref[r0:r0 + WINDOW, c0:c0 + LANES] = o[:WINDOW].astype(BF16)
        attn_ref[r0:r0 + WINDOW, c0 + LANES:c0 + 2 * LANES] = o[WINDOW:].astype(BF16)

    def pool(groups):
        pos = seq_tile * ROW_TILE + lax.broadcasted_iota(jnp.int32, (ROW_TILE, 1), 0)
        for g in groups:
            w = POOL_WINDOWS[g]
            cols = slice(g * POOL_GROUP, (g + 1) * POOL_GROUP)
            cur = u_c[POOL_PAD:POOL_PAD + ROW_TILE, cols]
            acc = cur
            for j in range(1, w):
                acc = acc + u_c[POOL_PAD - j:POOL_PAD - j + ROW_TILE, cols]
            cnt = jnp.minimum(pos + 1, w).astype(F32)
            pool_ref[:, cols] = (acc / cnt - cur).astype(BF16)

    def sequence_state():
        kt_ref[0] = k_c[ROW_TILE:ROW_TILE + WINDOW, :].T
        vt_ref[0] = v_c[ROW_TILE:ROW_TILE + WINDOW, :].T
        ulast_ref[0] = u_c[ROW_TILE:ROW_TILE + POOL_PAD, :]

    work = {"pool_wide": lambda: pool((3,)), "pool_narrow": lambda: pool((0, 1, 2)),
            "sequence_state": sequence_state}
    for blk in range(ROW_TILE // WINDOW):
        for kh in range(N_KV_HEADS):
            unit = blk * N_KV_HEADS + kh
            work["scores", unit] = lambda unit=unit, blk=blk, kh=kh: scores(unit, blk, kh)
            work["values", unit] = lambda unit=unit, blk=blk, kh=kh: values(unit, blk, kh)
    return work


def _prompt_bias():
    r = np.arange(2 * WINDOW)[:, None] % WINDOW
    c = np.arange(2 * KEY_SPAN)[None, :] % KEY_SPAN
    valid = (r <= c) & (c <= r + WINDOW)
    first = valid & (c >= WINDOW)
    return np.where(np.stack([valid, first]), 0.0, NEG_INF).astype(np.float32)


def _front_kernel(sinks_ref, xp_ref, xs_ref, w1_hbm, w3_hbm, w2_hbm, win_hbm, g_ref, b_ref, rope_ref, bias_ref,
                  h_ref, pa_ref, us_ref, qs_ref, ks_ref, vs_ref, kt_ref, vt_ref, ulast_ref,
                  z_c, q_c, k_c, v_c, u_c, h_s, w1_ref, w3_ref, w2_ref, wu_ref,
                  buf_ff, buf_narrow, sem_ff, sem_narrow, sem_model):
    r = pl.program_id(0)
    prev = r - 1
    pool_ref = pa_ref.at[:, 0:POOL_WIDTH]
    attn_ref = pa_ref.at[:, POOL_WIDTH:POOL_WIDTH + Q_WIDTH]

    @pl.when(r == 0)
    def _():
        ring_ff = _ring(buf_ff, sem_ff)
        ring_narrow = _ring(buf_narrow, sem_narrow)
        ring_model = _ring(z_c, sem_model, rows=MODEL_RING_ROWS)
        _stage_weights(_row_chunks(w1_hbm, 0, w1_ref, ring_ff)
                       + _row_chunks(w3_hbm, 0, w3_ref, ring_ff)
                       + _row_chunks(win_hbm, 0, wu_ref.at[:, 0:D_MODEL], ring_model)
                       + _row_chunks(win_hbm, D_MODEL, wu_ref.at[:, D_MODEL:UQKV_WIDTH], ring_narrow)
                       + _row_chunks(w2_hbm, 0, w2_ref, ring_model))
        for ref in (z_c, q_c, k_c, v_c, u_c):
            ref[...] = jnp.zeros(ref.shape, ref.dtype)

    def norm_previous():
        h1 = _layer_norm(z_c[...], g_ref[...], b_ref[...])
        h_ref[...] = h1
        return h1.astype(BF16)

    def rope_tables():
        tile = jnp.clip(prev, 0, PROMPT_TILES)
        table = jnp.where(tile == PROMPT_TILES, TILES_PER_SEQ, tile % TILES_PER_SEQ)
        rows = pl.ds(pl.multiple_of(table * ROW_TILE, ROW_TILE), ROW_TILE)
        cos = rope_ref[rows, 0:LANES]
        lane = lax.broadcasted_iota(jnp.int32, cos.shape, 1)
        return cos, rope_ref[rows, LANES:2 * LANES], (lane & (HEAD_DIM // 2)) == 0

    def project_u(h1b):
        return _dot(h1b, wu_ref[:, 0:POOL_WIDTH])

    def project_q(h1b):
        z = _dot(h1b, wu_ref[:, POOL_WIDTH:POOL_WIDTH + Q_WIDTH])
        tables = rope_tables()
        return [(_rope(z[:, c:c + LANES], *tables) * (HEAD_DIM ** -0.5)).astype(BF16)
                for c in range(0, Q_WIDTH, LANES)]

    def project_kv(h1b):
        z = _dot(h1b, wu_ref[:, POOL_WIDTH + Q_WIDTH:UQKV_WIDTH])
        return _rope(z[:, :KV_WIDTH], *rope_tables()), z[:, KV_WIDTH:]

    def tile_step(x):
        normed = []

        def norm():
            normed.append(norm_previous())

        projected = []

        def project():
            projected.extend([project_u(normed[0]), project_q(normed[0]), *project_kv(normed[0])])

        def carry():
            u, q, k, v = projected
            starts_seq = (prev + TILES_PER_SEQ) % TILES_PER_SEQ == 0
            u_c[0:POOL_PAD, :] = jnp.where(starts_seq, 0.0, u_c[ROW_TILE:ROW_TILE + POOL_PAD, :])
            u_c[POOL_PAD:POOL_PAD + ROW_TILE, :] = u
            for c, qc in enumerate(q):
                q_c[:, c * LANES:(c + 1) * LANES] = qc
            k_c[0:WINDOW, :] = k_c[ROW_TILE:ROW_TILE + WINDOW, :]
            v_c[0:WINDOW, :] = v_c[ROW_TILE:ROW_TILE + WINDOW, :]
            k_c[WINDOW:WINDOW + ROW_TILE, :] = k
            v_c[WINDOW:WINDOW + ROW_TILE, :] = v

        side = {0: norm, FF_SLOTS - 3: project, FF_SLOTS - 1: carry}
        stages = context_stages()
        assert len(stages) < FF_SLOTS - 3
        side.update(enumerate(stages, start=1))
        _swiglu_residual(x, w1_ref, w3_ref, w2_ref, h_s, side, z_c)

    def context_stages():
        context = _prompt_context_work(r - 2, q_c, k_c, v_c, u_c, bias_ref, sinks_ref,
                                       pool_ref, attn_ref, kt_ref, vt_ref, ulast_ref)
        stages = [("scores", 0)]
        for unit in range(1, 8):
            stages += [("scores", unit), ("values", unit - 1)]
        stages.append(("values", 7))

        def pooling_and_state():
            context["pool_wide"]()
            context["pool_narrow"]()
            context["sequence_state"]()

        return [context[stage] for stage in stages] + [pooling_and_state]

    @pl.when(r <= PROMPT_TILES)
    def _():
        tile_step(jnp.where(r == PROMPT_TILES, xs_ref[...], xp_ref[...]))

    @pl.when(r == ROW_TILES)
    def _():
        for work in context_stages():
            work()
        h1b = norm_previous()
        us_ref[...] = project_u(h1b)
        for c, qc in enumerate(project_q(h1b)):
            qs_ref[:, c * LANES:(c + 1) * LANES] = qc
        ks_ref[...], vs_ref[...] = project_kv(h1b)


def _front(xp, xs, w1, w3, w2, w_in, g, b, rope, sinks):
    rows = ROW_TILES * ROW_TILE

    def lagged(width):
        return pl.BlockSpec((ROW_TILE, width), lambda r: (jnp.clip(r - 2, 0, PROMPT_TILES - 1), 0))

    def seq_of_lagged(shape):
        return pl.BlockSpec(shape, lambda r: (jnp.clip(r - 2, 0, PROMPT_TILES - 1) // TILES_PER_SEQ, 0, 0))

    sample = lambda width: pl.BlockSpec((ROW_TILE, width), lambda r: (0, 0))
    hbm = pl.BlockSpec(memory_space=pl.ANY)
    return pl.pallas_call(
        _front_kernel,
        out_shape=(jax.ShapeDtypeStruct((rows, D_MODEL), F32),
                   jax.ShapeDtypeStruct((PROMPT_ROWS, POOL_WIDTH + Q_WIDTH), BF16),
                   jax.ShapeDtypeStruct((SAMPLE_ROWS, POOL_WIDTH), F32),
                   jax.ShapeDtypeStruct((SAMPLE_ROWS, Q_WIDTH), BF16),
                   jax.ShapeDtypeStruct((SAMPLE_ROWS, KV_WIDTH), F32),
                   jax.ShapeDtypeStruct((SAMPLE_ROWS, KV_WIDTH), F32),
                   jax.ShapeDtypeStruct((BATCH, KV_WIDTH, WINDOW), F32),
                   jax.ShapeDtypeStruct((BATCH, KV_WIDTH, WINDOW), F32),
                   jax.ShapeDtypeStruct((BATCH, POOL_PAD, POOL_WIDTH), F32)),
        grid=(ROW_TILES + 1,),
        in_specs=[pl.BlockSpec(memory_space=pltpu.SMEM),
                  pl.BlockSpec((ROW_TILE, D_MODEL), lambda r: (jnp.minimum(r, PROMPT_TILES - 1), 0)),
                  _resident((SAMPLE_ROWS, D_MODEL)),
                  hbm, hbm, hbm, hbm, _resident((1, D_MODEL)), _resident((1, D_MODEL)),
                  _resident(rope.shape), _resident((2, 2 * WINDOW, 2 * KEY_SPAN))],
        out_specs=(pl.BlockSpec((ROW_TILE, D_MODEL), lambda r: (jnp.clip(r - 1, 0, PROMPT_TILES), 0)),
                   lagged(POOL_WIDTH + Q_WIDTH),
                   sample(POOL_WIDTH), sample(Q_WIDTH), sample(KV_WIDTH), sample(KV_WIDTH),
                   seq_of_lagged((1, KV_WIDTH, WINDOW)), seq_of_lagged((1, KV_WIDTH, WINDOW)),
                   seq_of_lagged((1, POOL_PAD, POOL_WIDTH))),
        scratch_shapes=[pltpu.VMEM((ROW_TILE, D_MODEL), F32),
                        pltpu.VMEM((ROW_TILE, Q_WIDTH), BF16),
                        pltpu.VMEM((WINDOW + ROW_TILE, KV_WIDTH), F32),
                        pltpu.VMEM((WINDOW + ROW_TILE, KV_WIDTH), F32),
                        pltpu.VMEM((POOL_PAD + ROW_TILE, POOL_WIDTH), F32),
                        pltpu.VMEM((ROW_TILE, D_FF), BF16),
                        pltpu.VMEM((D_MODEL, D_FF), BF16), pltpu.VMEM((D_MODEL, D_FF), BF16),
                        pltpu.VMEM((D_FF, D_MODEL), BF16), pltpu.VMEM((D_MODEL, UQKV_WIDTH), BF16)]
                       + _ring_scratch(RING_FF, RING_NARROW),
        compiler_params=pltpu.CompilerParams(dimension_semantics=("arbitrary",),
                                             vmem_limit_bytes=VMEM_LIMIT_BYTES),
        name="front",
    )(sinks, xp, xs, w1, w3, w2, w_in, g, b, rope, jnp.asarray(_prompt_bias()))


def _sample_bias():
    row = np.arange(HEADS_PER_KV * DEC_SEQ * SUB_GROUP)
    row_t = (row // SUB_GROUP) % DEC_SEQ
    row_b = row % SUB_GROUP
    col = np.arange(SUB_GROUP * WINDOW)
    ok_c = (col[None, :] // WINDOW == row_b[:, None]) & (col[None, :] % WINDOW >= row_t[:, None])
    new = np.arange(DEC_SEQ * SUB_GROUP)
    ok_n = (new[None, :] % SUB_GROUP == row_b[:, None]) & (new[None, :] // SUB_GROUP <= row_t[:, None])
    to_bias = lambda ok: np.where(ok, 0.0, NEG_INF).astype(np.float32)
    return to_bias(ok_c), to_bias(ok_n)


def _sample_ctx_kernel(sinks_ref, q0, q1, q2, q3, k0, k1, k2, k3, v0, v1, v2, v3, u0, u1, u2, u3,
                       ckt_ref, cvt_ref, cu_ref, bias_c_ref, bias_n_ref,
                       attn_ref, pool_ref, kt_out, vt_out, pu_out, kbt_s, vbt_s):
    q_t = [q[...].astype(F32) for q in (q0, q1, q2, q3)]
    k_t = [k[...] for k in (k0, k1, k2, k3)]
    v_t = [v[...] for v in (v0, v1, v2, v3)]

    rows_u = [cu_ref[i] for i in range(POOL_BUF)] + [u[...] for u in (u0, u1, u2, u3)]
    for t in range(DEC_SEQ):
        pooled = []
        for g, w in enumerate(POOL_WINDOWS):
            cols = slice(g * POOL_GROUP, (g + 1) * POOL_GROUP)
            cur = rows_u[POOL_BUF + t][:, cols]
            acc = cur
            for j in range(1, w):
                acc = acc + rows_u[POOL_BUF + t - j][:, cols]
            pooled.append(acc / float(w) - cur)
        pool_ref[t] = jnp.concatenate(pooled, axis=1)
    for i in range(POOL_BUF):
        pu_out[i] = rows_u[i + DEC_SEQ]

    bias_c = bias_c_ref[...]
    bias_n = bias_n_ref[...]
    head_of_row = lax.broadcasted_iota(jnp.int32, (bias_c.shape[0], 1), 0) // (DEC_SEQ * SUB_GROUP)
    low = lax.broadcasted_iota(jnp.int32, (1, LANES), 1) < HEAD_DIM
    q_sw = [_swap_halves_wide(q) for q in q_t]
    k_sw = [_swap_halves(k) for k in k_t]
    v_sw = [_swap_halves(v) for v in v_t]
    for sub in range(SEQ_GROUP // SUB_GROUP):
        rows = slice(sub * SUB_GROUP, (sub + 1) * SUB_GROUP)
        for kh in range(N_KV_HEADS):
            def q_piece(t, head):
                src = q_t[t] if head % 2 == 0 else q_sw[t]
                chunk = head // 2
                return jnp.where(low, src[rows, chunk * LANES:(chunk + 1) * LANES], 0.0)

            def kv_first(c):
                return c if kh == 0 else jnp.concatenate([c[HEAD_DIM:], c[:HEAD_DIM]], axis=0)

            def kv_twice(c):
                part = c[kh * HEAD_DIM:(kh + 1) * HEAD_DIM]
                return jnp.concatenate([part, part], axis=0)

            lhs = jnp.concatenate([q_piece(t, kh * HEADS_PER_KV + g)
                                   for g in range(HEADS_PER_KV) for t in range(DEC_SEQ)], axis=0).astype(BF16)
            kcat = jnp.concatenate([kv_first(ckt_ref[sub * SUB_GROUP + b]) for b in range(SUB_GROUP)],
                                   axis=1).astype(BF16)
            vcat = jnp.concatenate([kv_twice(cvt_ref[sub * SUB_GROUP + b]) for b in range(SUB_GROUP)],
                                   axis=1).astype(BF16)
            knew = jnp.concatenate([(k_t[t] if kh == 0 else k_sw[t])[rows] for t in range(DEC_SEQ)],
                                   axis=0).astype(BF16)
            vnew = jnp.concatenate([(jnp.where(low, v_t[t], v_sw[t]) if kh == 0 else
                                     jnp.where(low, v_sw[t], v_t[t]))[rows] for t in range(DEC_SEQ)],
                                   axis=0).astype(BF16)
            s_c = _dot(lhs, kcat) + bias_c
            s_n = _dot_nt(lhs, knew) + bias_n
            sink = jnp.zeros(head_of_row.shape, F32)
            for g in range(HEADS_PER_KV):
                sink = jnp.where(head_of_row == g, sinks_ref[kh * HEADS_PER_KV + g], sink)
            m = jnp.maximum(jnp.maximum(jnp.max(s_c, axis=-1, keepdims=True),
                                        jnp.max(s_n, axis=-1, keepdims=True)), sink)
            p_c = jnp.exp(s_c - m)
            p_n = jnp.exp(s_n - m)
            denom = (jnp.sum(p_c, axis=-1, keepdims=True) + jnp.sum(p_n, axis=-1, keepdims=True)
                     + jnp.exp(sink - m))
            o = (_dot_nt(p_c.astype(BF16), vcat) + _dot(p_n.astype(BF16), vnew)) / denom
            for t in range(DEC_SEQ):
                for pair in range(HEADS_PER_KV // 2):
                    piece = lambda g: o[(g * DEC_SEQ + t) * SUB_GROUP:(g * DEC_SEQ + t + 1) * SUB_GROUP]
                    c0 = (kh * HEADS_PER_KV // 2 + pair) * LANES
                    attn_ref[t, rows, c0:c0 + LANES] = jnp.where(low, piece(2 * pair), piece(2 * pair + 1))

    zeros = jnp.zeros((LANES - DEC_SEQ * SEQ_GROUP, KV_WIDTH), F32)
    kbt_s[DEC_SEQ * SEQ_GROUP:, :] = zeros
    vbt_s[DEC_SEQ * SEQ_GROUP:, :] = zeros
    for t in range(DEC_SEQ):
        kbt_s[pl.ds(t, SEQ_GROUP, stride=DEC_SEQ), :] = k_t[t]
        vbt_s[pl.ds(t, SEQ_GROUP, stride=DEC_SEQ), :] = v_t[t]
    knew_t = kbt_s[...].T
    vnew_t = vbt_s[...].T
    keep = lax.broadcasted_iota(jnp.int32, (1, WINDOW), 1) < WINDOW - DEC_SEQ
    for b in range(SEQ_GROUP):
        shift_new = WINDOW - DEC_SEQ - DEC_SEQ * b
        kt_out[b] = jnp.where(keep, pltpu.roll(ckt_ref[b], WINDOW - DEC_SEQ, axis=1),
                              pltpu.roll(knew_t, shift_new, axis=1))
        vt_out[b] = jnp.where(keep, pltpu.roll(cvt_ref[b], WINDOW - DEC_SEQ, axis=1),
                              pltpu.roll(vnew_t, shift_new, axis=1))


def _sample_ctx(us, qs, ks, vs, ckt, cvt, cu, sinks):
    groups = DEC_BATCH // SEQ_GROUP

    def token_rows(t, width):
        return pl.BlockSpec((SEQ_GROUP, width), lambda i: (t * groups + i, 0))

    def per_token(width):
        return [token_rows(t, width) for t in range(DEC_SEQ)]

    cache_spec = pl.BlockSpec((SEQ_GROUP, KV_WIDTH, WINDOW), lambda i: (i, 0, 0))
    pool_rows_spec = pl.BlockSpec((POOL_BUF, SEQ_GROUP, POOL_WIDTH), lambda i: (0, i, 0))
    by_token = lambda width: pl.BlockSpec((DEC_SEQ, SEQ_GROUP, width), lambda i: (0, i, 0))
    bias_c, bias_n = _sample_bias()
    return pl.pallas_call(
        _sample_ctx_kernel,
        out_shape=(jax.ShapeDtypeStruct((DEC_SEQ, DEC_BATCH, Q_WIDTH), F32),
                   jax.ShapeDtypeStruct((DEC_SEQ, DEC_BATCH, POOL_WIDTH), F32),
                   jax.ShapeDtypeStruct((DEC_BATCH, KV_WIDTH, WINDOW), F32),
                   jax.ShapeDtypeStruct((DEC_BATCH, KV_WIDTH, WINDOW), F32),
                   jax.ShapeDtypeStruct((POOL_BUF, DEC_BATCH, POOL_WIDTH), F32)),
        grid=(groups,),
        in_specs=[pl.BlockSpec(memory_space=pltpu.SMEM)]
                 + per_token(Q_WIDTH) + per_token(KV_WIDTH) + per_token(KV_WIDTH) + per_token(POOL_WIDTH)
                 + [cache_spec, cache_spec, pool_rows_spec, _resident(bias_c.shape), _resident(bias_n.shape)],
        out_specs=(by_token(Q_WIDTH), by_token(POOL_WIDTH), cache_spec, cache_spec, pool_rows_spec),
        scratch_shapes=[pltpu.VMEM((LANES, KV_WIDTH), F32), pltpu.VMEM((LANES, KV_WIDTH), F32)],
        compiler_params=pltpu.CompilerParams(dimension_semantics=("parallel",),
                                             vmem_limit_bytes=VMEM_LIMIT_BYTES),
        name="sample_ctx",
    )(sinks, *([qs] * DEC_SEQ), *([ks] * DEC_SEQ), *([vs] * DEC_SEQ), *([us] * DEC_SEQ),
      ckt, cvt, cu, jnp.asarray(bias_c), jnp.asarray(bias_n))


def _back_kernel(h_ref, pa_ref, pools_ref, attns_ref,
                 win_hbm, wgrp_hbm, scale_ref, wpo_hbm, wao_hbm, wout_hbm, g2_ref, b2_ref,
                 w1_hbm, w3_hbm, w2_hbm, g3_ref, b3_ref, yp_ref, ys_ref, z2_c, z3_c, h_s,
                 wg_ref, wgrp_ref, wpo_ref, wao_ref, wout_ref, w1_ref, w3_ref, w2_ref,
                 buf_ff, buf_group, sem_ff, sem_group, sem_model):
    r = pl.program_id(0)

    @pl.when(r == 0)
    def _():
        ring_ff = _ring(buf_ff, sem_ff)
        ring_group = _ring(buf_group, sem_group)
        ring_model = _ring(z2_c, sem_model, rows=MODEL_RING_ROWS)
        _stage_weights(_row_chunks(wgrp_hbm, 0, wgrp_ref, ring_group)
                       + _row_chunks(wpo_hbm, 0, wpo_ref, ring_model)
                       + _row_chunks(wao_hbm, 0, wao_ref, ring_model)
                       + _row_chunks(w1_hbm, 0, w1_ref, ring_ff)
                       + _row_chunks(w3_hbm, 0, w3_ref, ring_ff)
                       + _row_chunks(win_hbm, UQKV_WIDTH, wg_ref.at[:, 0:D_MODEL], ring_model)
                       + _row_chunks(win_hbm, UQKV_WIDTH + D_MODEL, wg_ref.at[:, D_MODEL:2 * D_MODEL], ring_model)
                       + _row_chunks(wout_hbm, 0, wout_ref, ring_model)
                       + _row_chunks(w2_hbm, 0, w2_ref, ring_model))
        z2_c[...] = jnp.zeros(z2_c.shape, z2_c.dtype)
        z3_c[...] = jnp.zeros(z3_c.shape, z3_c.dtype)

    @pl.when(r <= ROW_TILES)
    def _():
        is_sample = r >= PROMPT_TILES
        h1 = h_ref[...]
        pool_in = jnp.where(is_sample, pools_ref[...].astype(BF16), pa_ref[:, 0:POOL_WIDTH])
        attn_o = jnp.where(is_sample, attns_ref[...].astype(BF16), pa_ref[:, POOL_WIDTH:POOL_WIDTH + Q_WIDTH])
        zs = [_dot(pool_in[:, g * POOL_GROUP:(g + 1) * POOL_GROUP],
                   wgrp_ref[g * POOL_GROUP:(g + 1) * POOL_GROUP, :])
              for g in range(len(POOL_WINDOWS))]
        pool_z = jnp.concatenate(zs, axis=1) * scale_ref[...]
        a = _dot(pool_z.astype(BF16), wpo_ref[...])
        b = _dot(attn_o, wao_ref[...])

        yp_ref[...] = _layer_norm(z3_c[...], g3_ref[...], b3_ref[...])
        h2 = _layer_norm(z2_c[...], g2_ref[...], b2_ref[...])
        gated = []

        def gate():
            gates = jax.nn.sigmoid(_dot(h1.astype(BF16), wg_ref[...]))
            gated.append((gates[:, :D_MODEL] * a + gates[:, D_MODEL:] * b).astype(BF16))

        def project_out():
            z2_c[...] = ALPHA * h1 + _dot(gated[0], wout_ref[...])

        _swiglu_residual(h2, w1_ref, w3_ref, w2_ref, h_s, {1: gate, 9: project_out}, z3_c)

    @pl.when(r == ROW_TILES + 1)
    def _():
        ys_ref[...] = _layer_norm(z3_c[...], g3_ref[...], b3_ref[...])


def _back(h1, pool_attn, pool_s, attn_s, w_in, wgrp, scale, wpo, wao, wout, g2, b2, w1, w3, w2, g3, b3):
    lagged = pl.BlockSpec((ROW_TILE, D_MODEL), lambda r: (jnp.clip(r - 2, 0, PROMPT_TILES - 1), 0))
    hbm = pl.BlockSpec(memory_space=pl.ANY)
    return pl.pallas_call(
        _back_kernel,
        out_shape=(jax.ShapeDtypeStruct((PROMPT_ROWS, D_MODEL), F32),
                   jax.ShapeDtypeStruct((SAMPLE_ROWS, D_MODEL), F32)),
        grid=(ROW_TILES + 2,),
        in_specs=[pl.BlockSpec((ROW_TILE, D_MODEL), lambda r: (jnp.minimum(r, PROMPT_TILES), 0)),
                  pl.BlockSpec((ROW_TILE, POOL_WIDTH + Q_WIDTH), lambda r: (jnp.minimum(r, PROMPT_TILES - 1), 0)),
                  _resident((SAMPLE_ROWS, POOL_WIDTH)), _resident((SAMPLE_ROWS, Q_WIDTH)),
                  hbm, hbm, _resident((1, POOL_WIDTH)), hbm, hbm, hbm,
                  _resident((1, D_MODEL)), _resident((1, D_MODEL)),
                  hbm, hbm, hbm, _resident((1, D_MODEL)), _resident((1, D_MODEL))],
        out_specs=(lagged, pl.BlockSpec((ROW_TILE, D_MODEL), lambda r: (0, 0))),
        scratch_shapes=[pltpu.VMEM((ROW_TILE, D_MODEL), F32), pltpu.VMEM((ROW_TILE, D_MODEL), F32),
                        pltpu.VMEM((ROW_TILE, D_FF), BF16),
                        pltpu.VMEM((D_MODEL, 2 * D_MODEL), BF16),
                        pltpu.VMEM((len(POOL_WINDOWS) * POOL_GROUP, POOL_GROUP), BF16),
                        pltpu.VMEM((POOL_WIDTH, D_MODEL), BF16), pltpu.VMEM((Q_WIDTH, D_MODEL), BF16),
                        pltpu.VMEM((D_MODEL, D_MODEL), BF16),
                        pltpu.VMEM((D_MODEL, D_FF), BF16), pltpu.VMEM((D_MODEL, D_FF), BF16),
                        pltpu.VMEM((D_FF, D_MODEL), BF16)]
                       + _ring_scratch(RING_FF, RING_GROUP),
        compiler_params=pltpu.CompilerParams(dimension_semantics=("arbitrary",),
                                             vmem_limit_bytes=VMEM_LIMIT_BYTES),
        name="back",
    )(h1, pool_attn, pool_s, attn_s, w_in, wgrp, scale, wpo, wao, wout, g2, b2, w1, w3, w2, g3, b3)


def _rope_tables(pos):
    half = HEAD_DIM // 2
    freqs = ROPE_THETA ** (-2.0 * jnp.arange(half, dtype=F32) / HEAD_DIM)
    ang = pos.astype(F32)[:, None] * freqs[None, :]
    cos, sin = jnp.cos(ang), jnp.sin(ang)
    return jnp.tile(cos, (1, 4)), jnp.concatenate([-sin, sin, -sin, sin], axis=1)


def kernel(x_prompt, x_sample, cache_pool_u, cache_k_win, cache_v_win, w_in, pool_w_grp, pool_scale,
           attn_sinks, w_pool_out, w_attn_out, w_out, ffn1_w1, ffn1_w3, ffn1_w2, ffn2_w1, ffn2_w3,
           ffn2_w2, ln1_g, ln1_b, ln2_g, ln2_b, ln3_g, ln3_b):
    assert DEPTH == 1 and w_in.shape[0] == 1
    l = 0
    vec = lambda p: p[l].reshape(1, -1)
    sinks = attn_sinks[l]

    pos = jnp.concatenate([jnp.arange(SEQ, dtype=jnp.int32),
                           jnp.repeat(PAST_LEN + jnp.arange(DEC_SEQ, dtype=jnp.int32), DEC_BATCH)])
    rope = jnp.concatenate(_rope_tables(pos), axis=1)

    xp = x_prompt.reshape(PROMPT_ROWS, D_MODEL)
    xs = jnp.transpose(x_sample, (1, 0, 2)).reshape(SAMPLE_ROWS, D_MODEL)
    (h1, pool_attn, us, qs, ks, vs, kt_last, vt_last, u_last) = _front(
        xp, xs, ffn1_w1, ffn1_w3, ffn1_w2, w_in, vec(ln1_g), vec(ln1_b), rope, sinks)

    to_t = lambda c: jnp.transpose(c[l], (0, 2, 3, 1)).reshape(DEC_BATCH, KV_WIDTH, WINDOW)
    cu = jnp.transpose(cache_pool_u[l], (1, 0, 2))
    attn_s, pool_s, kt_s, vt_s, pu_s = _sample_ctx(us, qs, ks, vs, to_t(cache_k_win), to_t(cache_v_win), cu, sinks)

    wgrp = pool_w_grp.reshape(DEPTH, len(POOL_WINDOWS) * POOL_GROUP, POOL_GROUP)
    yp, ys = _back(h1, pool_attn, pool_s.reshape(SAMPLE_ROWS, POOL_WIDTH), attn_s.reshape(SAMPLE_ROWS, Q_WIDTH),
                   w_in, wgrp, vec(pool_scale), w_pool_out, w_attn_out, w_out, vec(ln2_g), vec(ln2_b),
                   ffn2_w1, ffn2_w3, ffn2_w2, vec(ln3_g), vec(ln3_b))
    yp = yp.reshape(BATCH, SEQ, D_MODEL)
    ys = jnp.transpose(ys.reshape(DEC_SEQ, DEC_BATCH, D_MODEL), (1, 0, 2))

    from_t = lambda c, n: jnp.transpose(c.reshape(n, N_KV_HEADS, HEAD_DIM, WINDOW), (0, 3, 1, 2))[None]
    pool_u_prompt = u_last[None, :, POOL_PAD - POOL_BUF:]
    pool_u_sample = jnp.transpose(pu_s, (1, 0, 2))[None]
    return (yp, ys, pool_u_prompt, from_t(kt_last, BATCH), from_t(vt_last, BATCH),
            pool_u_sample, from_t(kt_s, DEC_BATCH), from_t(vt_s, DEC_BATCH))
```

```python
import jax
import jax.numpy as jnp
import numpy as np
from jax import lax
from jax.experimental import pallas as pl
from jax.experimental.pallas import tpu as pltpu

D_MODEL = 1024
BATCH = 8
SEQ = 2048
DEC_BATCH = 128
DEC_SEQ = 4
PAST_LEN = 8192
POOL_WINDOWS = (2, 4, 8, 16)
POOL_GROUP = 128
POOL_WIDTH = 512
POOL_BUF = 15
N_HEADS = 8
N_KV_HEADS = 2
HEADS_PER_KV = N_HEADS // N_KV_HEADS
HEAD_DIM = 64
Q_WIDTH = 512
KV_WIDTH = 128
WINDOW = 128
ROPE_THETA = 10000.0
D_FF = 2816
DEPTH = 1
ALPHA = (2.0 * DEPTH) ** 0.25
LN_EPS = 1e-5
NEG_INF = -1e30
UQKV_WIDTH = POOL_WIDTH + Q_WIDTH + 2 * KV_WIDTH

LANES = 128
KEY_SPAN = 2 * WINDOW
VMEM_LIMIT_BYTES = 61 * 1024 * 1024

ROW_TILE = 512
TILES_PER_SEQ = SEQ // ROW_TILE
PROMPT_ROWS = BATCH * SEQ
PROMPT_TILES = PROMPT_ROWS // ROW_TILE
SAMPLE_ROWS = DEC_BATCH * DEC_SEQ
ROW_TILES = PROMPT_TILES + 1
POOL_PAD = 16
SEQ_GROUP = 16
SUB_GROUP = 8
FF_CHUNK = 256
OUT_BLOCK = 256

BF16 = jnp.bfloat16
F32 = jnp.float32


def _dot(a, b):
    return jnp.dot(a, b, preferred_element_type=F32)


def _dot_nt(a, b):
    return lax.dot_general(a, b, (((1,), (1,)), ((), ())), preferred_element_type=F32)


def _layer_norm(y, g, b):
    mu = jnp.mean(y, axis=-1, keepdims=True)
    yc = y - mu
    var = jnp.mean(yc * yc, axis=-1, keepdims=True)
    return yc * lax.rsqrt(var + LN_EPS) * g + b


def _resident(shape):
    nd = len(shape)
    return pl.BlockSpec(shape, lambda *_: (0,) * nd, pipeline_mode=pl.Buffered(1))


def _stage_weights(jobs):
    uses, staged = {}, []
    for src, ring, dst in jobs:
        slot, sem = ring[uses.get(id(ring), 0) % len(ring)]
        uses[id(ring)] = uses.get(id(ring), 0) + 1
        staged.append((pltpu.make_async_copy(src, slot, sem), slot, dst))
    assert all(len(ring) >= min(STAGE_IN_FLIGHT, uses[id(ring)]) for _, ring, _ in jobs)
    for copy, _, _ in staged[:STAGE_IN_FLIGHT]:
        copy.start()
    for i, (copy, slot, dst) in enumerate(staged):
        copy.wait()
        dst[...] = slot[...].astype(BF16)
        if i + STAGE_IN_FLIGHT < len(staged):
            staged[i + STAGE_IN_FLIGHT][0].start()


def _ring(buf, sems, rows=None):
    if rows is None:
        return [(buf.at[i], sems.at[i]) for i in range(buf.shape[0])]
    return [(buf.at[pl.ds(i * rows, rows), :], sems.at[i]) for i in range(buf.shape[0] // rows)]


def _row_chunks(w_hbm, col0, dst, ring):
    rows, cols = dst.shape
    step, width = ring[0][0].shape
    assert rows % step == 0 and width == cols
    return [(w_hbm.at[0, pl.ds(r0, step), pl.ds(col0, cols)], ring, dst.at[pl.ds(r0, step), :])
            for r0 in range(0, rows, step)]


STAGE_IN_FLIGHT = 4
RING_FF = (STAGE_IN_FLIGHT, 64, D_FF)
MODEL_RING_ROWS = ROW_TILE // STAGE_IN_FLIGHT
RING_NARROW = (STAGE_IN_FLIGHT, 128, UQKV_WIDTH - D_MODEL)
RING_GROUP = (1, len(POOL_WINDOWS) * POOL_GROUP, POOL_GROUP)


def _ring_scratch(*rings):
    return ([pltpu.VMEM(ring, F32) for ring in rings]
            + [pltpu.SemaphoreType.DMA((ring[0],)) for ring in rings]
            + [pltpu.SemaphoreType.DMA((STAGE_IN_FLIGHT,))])


FF_SLOTS = 2 * (D_FF // FF_CHUNK)


def _swiglu_residual(x, w1_ref, w3_ref, w2_ref, h_s, side_work, out_ref):
    assert all(0 <= slot <= FF_SLOTS for slot in side_work)
    run = lambda slot: side_work.get(slot, lambda: None)()
    xb = x.astype(BF16)
    for j in range(D_FF // FF_CHUNK):
        cols = slice(j * FF_CHUNK, (j + 1) * FF_CHUNK)
        a = _dot(xb, w1_ref[:, cols])
        run(2 * j)
        b = _dot(xb, w3_ref[:, cols])
        h_s[:, cols] = ((a * jax.nn.sigmoid(a)) * b).astype(BF16)
        run(2 * j + 1)
    run(FF_SLOTS)
    h = h_s[...]
    for c0 in range(0, D_MODEL, OUT_BLOCK):
        cols = slice(c0, c0 + OUT_BLOCK)
        out_ref[:, cols] = ALPHA * x[:, cols] + 0.5 * _dot(h, w2_ref[:, cols])


def _rope(x, cos, sin_signed, first_half):
    fwd = pltpu.roll(x, LANES - HEAD_DIM // 2, axis=1)
    bwd = pltpu.roll(x, HEAD_DIM // 2, axis=1)
    return x * cos + jnp.where(first_half, fwd, bwd) * sin_signed


def _swap_halves(x):
    return pltpu.roll(x, HEAD_DIM, axis=1)


def _swap_halves_wide(x):
    return jnp.concatenate([_swap_halves(x[:, c:c + LANES]) for c in range(0, x.shape[1], LANES)], axis=1)


def _lane_split(x, x_sw, kh):
    low = lax.broadcasted_iota(jnp.int32, (1, LANES), 1) < HEAD_DIM
    lo, hi = (x, x_sw) if kh == 0 else (x_sw, x)
    return jnp.concatenate([jnp.where(low, lo, 0.0), jnp.where(low, 0.0, hi)], axis=0).astype(BF16)


def _sink_softmax(q_pairs, keys, kh, bias, sinks_ref):
    s = _dot_nt(q_pairs, _lane_split(keys, _swap_halves(keys), kh)) + bias
    second_pair = lax.broadcasted_iota(jnp.int32, (s.shape[0], 1), 0) >= WINDOW
    probs, denoms = [], []
    for c in range(2):
        sc = s[:, c * KEY_SPAN:(c + 1) * KEY_SPAN]
        head = kh * HEADS_PER_KV + c
        sink = jnp.where(second_pair, sinks_ref[head + 2], sinks_ref[head])
        m = jnp.maximum(jnp.max(sc, axis=-1, keepdims=True), sink)
        p = jnp.exp(sc - m)
        denoms.append(jnp.sum(p, axis=-1, keepdims=True) + jnp.exp(sink - m))
        probs.append(p.astype(BF16))
    return jnp.concatenate(probs, axis=1), denoms


def _weighted_values(probs, denoms, vals, kh):
    low = lax.broadcasted_iota(jnp.int32, (1, LANES), 1) < HEAD_DIM
    o = _dot(probs, _lane_split(vals, _swap_halves(vals), kh))
    return o / jnp.where(low, denoms[0], denoms[1])


def _prompt_context_work(tile, q_c, k_c, v_c, u_c, bias_ref, sinks_ref,
                         pool_ref, attn_ref, kt_ref, vt_ref, ulast_ref):
    seq_tile = (tile + TILES_PER_SEQ) % TILES_PER_SEQ
    first_tile = seq_tile == 0
    softmaxed = {}

    def scores(unit, blk, kh):
        r0 = blk * WINDOW
        bias = bias_ref[jnp.where(first_tile, 1, 0)] if blk == 0 else bias_ref[0]
        c0 = 2 * kh * LANES
        q_pairs = jnp.concatenate([q_c[r0:r0 + WINDOW, c0:c0 + LANES],
                                   q_c[r0:r0 + WINDOW, c0 + LANES:c0 + 2 * LANES]], axis=0)
        softmaxed[unit] = _sink_softmax(q_pairs, k_c[r0:r0 + KEY_SPAN, :], kh, bias, sinks_ref)

    def values(unit, blk, kh):
        r0 = blk * WINDOW
        c0 = 2 * kh * LANES
        o = _weighted_values(*softmaxed.pop(unit), v_c[r0:r0 + KEY_SPAN, :], kh)
        attn_ref[r0:r0 + WINDOW, c0:c0 + LANES] = o[:WINDOW].astype(BF16)
        attn_ref[r0:r0 + WINDOW, c0 + LANES:c0 + 2 * LANES] = o[WINDOW:].astype(BF16)

    def pool(groups):
        pos = seq_tile * ROW_TILE + lax.broadcasted_iota(jnp.int32, (ROW_TILE, 1), 0)
        for g in groups:
            w = POOL_WINDOWS[g]
            cols = slice(g * POOL_GROUP, (g + 1) * POOL_GROUP)
            cur = u_c[POOL_PAD:POOL_PAD + ROW_TILE, cols]
            acc = cur
            for j in range(1, w):
                acc = acc + u_c[POOL_PAD - j:POOL_PAD - j + ROW_TILE, cols]
            cnt = jnp.minimum(pos + 1, w).astype(F32)
            pool_ref[:, cols] = (acc / cnt - cur).astype(BF16)

    def sequence_state():
        kt_ref[0] = k_c[ROW_TILE:ROW_TILE + WINDOW, :].T
        vt_ref[0] = v_c[ROW_TILE:ROW_TILE + WINDOW, :].T
        ulast_ref[0] = u_c[ROW_TILE:ROW_TILE + POOL_PAD, :]

    work = {"pool_wide": lambda: pool((3,)), "pool_narrow": lambda: pool((0, 1, 2)),
            "sequence_state": sequence_state}
    for blk in range(ROW_TILE // WINDOW):
        for kh in range(N_KV_HEADS):
            unit = blk * N_KV_HEADS + kh
            work["scores", unit] = lambda unit=unit, blk=blk, kh=kh: scores(unit, blk, kh)
            work["values", unit] = lambda unit=unit, blk=blk, kh=kh: values(unit, blk, kh)
    return work


def _prompt_bias():
    r = np.arange(2 * WINDOW)[:, None] % WINDOW
    c = np.arange(2 * KEY_SPAN)[None, :] % KEY_SPAN
    valid = (r <= c) & (c <= r + WINDOW)
    first = valid & (c >= WINDOW)
    return np.where(np.stack([valid, first]), 0.0, NEG_INF).astype(np.float32)


def _front_kernel(sinks_ref, xp_ref, xs_ref, w1_hbm, w3_hbm, w2_hbm, win_hbm, g_ref, b_ref, rope_ref, bias_ref,
                  h_ref, pa_ref, us_ref, qs_ref, ks_ref, vs_ref, kt_ref, vt_ref, ulast_ref,
                  z_c, q_c, k_c, v_c, u_c, h_s, w1_ref, w3_ref, w2_ref, wu_ref,
                  buf_ff, buf_narrow, sem_ff, sem_narrow, sem_model):
    r = pl.program_id(0) - 1
    prev = r - 1
    pool_ref = pa_ref.at[:, 0:POOL_WIDTH]
    attn_ref = pa_ref.at[:, POOL_WIDTH:POOL_WIDTH + Q_WIDTH]

    def stage_and_reset():
        ring_ff = _ring(buf_ff, sem_ff)
        ring_narrow = _ring(buf_narrow, sem_narrow)
        ring_model = _ring(z_c, sem_model, rows=MODEL_RING_ROWS)
        _stage_weights(_row_chunks(w1_hbm, 0, w1_ref, ring_ff)
                       + _row_chunks(w3_hbm, 0, w3_ref, ring_ff)
                       + _row_chunks(win_hbm, 0, wu_ref.at[:, 0:D_MODEL], ring_model)
                       + _row_chunks(win_hbm, D_MODEL, wu_ref.at[:, D_MODEL:UQKV_WIDTH], ring_narrow)
                       + _row_chunks(w2_hbm, 0, w2_ref, ring_model))
        for ref in (z_c, q_c, k_c, v_c, u_c):
            ref[...] = jnp.zeros(ref.shape, ref.dtype)

    def norm_previous():
        h1 = _layer_norm(z_c[...], g_ref[...], b_ref[...])
        h_ref[...] = h1
        return h1.astype(BF16)

    def rope_tables():
        tile = jnp.clip(prev, 0, PROMPT_TILES)
        table = jnp.where(tile == PROMPT_TILES, TILES_PER_SEQ, tile % TILES_PER_SEQ)
        rows = pl.ds(pl.multiple_of(table * ROW_TILE, ROW_TILE), ROW_TILE)
        cos = rope_ref[rows, 0:LANES]
        lane = lax.broadcasted_iota(jnp.int32, cos.shape, 1)
        return cos, rope_ref[rows, LANES:2 * LANES], (lane & (HEAD_DIM // 2)) == 0

    def project_u(h1b):
        return _dot(h1b, wu_ref[:, 0:POOL_WIDTH])

    def project_q(h1b):
        z = _dot(h1b, wu_ref[:, POOL_WIDTH:POOL_WIDTH + Q_WIDTH])
        tables = rope_tables()
        return [(_rope(z[:, c:c + LANES], *tables) * (HEAD_DIM ** -0.5)).astype(BF16)
                for c in range(0, Q_WIDTH, LANES)]

    def project_kv(h1b):
        z = _dot(h1b, wu_ref[:, POOL_WIDTH + Q_WIDTH:UQKV_WIDTH])
        return _rope(z[:, :KV_WIDTH], *rope_tables()), z[:, KV_WIDTH:]

    def tile_step(x):
        normed = []

        def norm():
            normed.append(norm_previous())

        projected = []

        def project():
            projected.extend([project_u(normed[0]), project_q(normed[0]), *project_kv(normed[0])])

        def carry():
            u, q, k, v = projected
            starts_seq = (prev + TILES_PER_SEQ) % TILES_PER_SEQ == 0
            u_c[0:POOL_PAD, :] = jnp.where(starts_seq, 0.0, u_c[ROW_TILE:ROW_TILE + POOL_PAD, :])
            u_c[POOL_PAD:POOL_PAD + ROW_TILE, :] = u
            for c, qc in enumerate(q):
                q_c[:, c * LANES:(c + 1) * LANES] = qc
            k_c[0:WINDOW, :] = k_c[ROW_TILE:ROW_TILE + WINDOW, :]
            v_c[0:WINDOW, :] = v_c[ROW_TILE:ROW_TILE + WINDOW, :]
            k_c[WINDOW:WINDOW + ROW_TILE, :] = k
            v_c[WINDOW:WINDOW + ROW_TILE, :] = v

        side = {0: norm, FF_SLOTS - 3: project, FF_SLOTS - 1: carry}
        stages = context_stages()
        assert len(stages) < FF_SLOTS - 3
        side.update(enumerate(stages, start=1))
        _swiglu_residual(x, w1_ref, w3_ref, w2_ref, h_s, side, z_c)

    def context_stages():
        context = _prompt_context_work(r - 2, q_c, k_c, v_c, u_c, bias_ref, sinks_ref,
                                       pool_ref, attn_ref, kt_ref, vt_ref, ulast_ref)
        stages = [("scores", 0)]
        for unit in range(1, 8):
            stages += [("scores", unit), ("values", unit - 1)]
        stages.append(("values", 7))

        def pooling_and_state():
            context["pool_wide"]()
            context["pool_narrow"]()
            context["sequence_state"]()

        return [context[stage] for stage in stages] + [pooling_and_state]

    @pl.when(jnp.logical_and(r >= 0, r <= PROMPT_TILES))
    def _():
        tile_step(jnp.where(r == PROMPT_TILES, xs_ref[...], xp_ref[...]))

    @pl.when(r == ROW_TILES)
    def _():
        for work in context_stages():
            work()
        h1b = norm_previous()
        us_ref[...] = project_u(h1b)
        for c, qc in enumerate(project_q(h1b)):
            qs_ref[:, c * LANES:(c + 1) * LANES] = qc
        ks_ref[...], vs_ref[...] = project_kv(h1b)

    pl.when(r < 0)(stage_and_reset)


def _front(xp, xs, w1, w3, w2, w_in, g, b, rope, sinks):
    rows = ROW_TILES * ROW_TILE

    def lagged(width):
        return pl.BlockSpec((ROW_TILE, width), lambda s: (jnp.clip(s - 3, 0, PROMPT_TILES - 1), 0))

    def seq_of_lagged(shape):
        return pl.BlockSpec(shape, lambda s: (jnp.clip(s - 3, 0, PROMPT_TILES - 1) // TILES_PER_SEQ, 0, 0))

    sample = lambda width: pl.BlockSpec((ROW_TILE, width), lambda r: (0, 0))
    hbm = pl.BlockSpec(memory_space=pl.ANY)
    return pl.pallas_call(
        _front_kernel,
        out_shape=(jax.ShapeDtypeStruct((rows, D_MODEL), F32),
                   jax.ShapeDtypeStruct((PROMPT_ROWS, POOL_WIDTH + Q_WIDTH), BF16),
                   jax.ShapeDtypeStruct((SAMPLE_ROWS, POOL_WIDTH), F32),
                   jax.ShapeDtypeStruct((SAMPLE_ROWS, Q_WIDTH), BF16),
                   jax.ShapeDtypeStruct((SAMPLE_ROWS, KV_WIDTH), F32),
                   jax.ShapeDtypeStruct((SAMPLE_ROWS, KV_WIDTH), F32),
                   jax.ShapeDtypeStruct((BATCH, KV_WIDTH, WINDOW), F32),
                   jax.ShapeDtypeStruct((BATCH, KV_WIDTH, WINDOW), F32),
                   jax.ShapeDtypeStruct((BATCH, POOL_PAD, POOL_WIDTH), F32)),
        grid=(ROW_TILES + 2,),
        in_specs=[pl.BlockSpec(memory_space=pltpu.SMEM),
                  pl.BlockSpec((ROW_TILE, D_MODEL), lambda s: (jnp.clip(s - 1, 0, PROMPT_TILES - 1), 0)),
                  _resident((SAMPLE_ROWS, D_MODEL)),
                  hbm, hbm, hbm, hbm, _resident((1, D_MODEL)), _resident((1, D_MODEL)),
                  _resident(rope.shape), _resident((2, 2 * WINDOW, 2 * KEY_SPAN))],
        out_specs=(pl.BlockSpec((ROW_TILE, D_MODEL), lambda s: (jnp.clip(s - 2, 0, PROMPT_TILES), 0)),
                   lagged(POOL_WIDTH + Q_WIDTH),
                   sample(POOL_WIDTH), sample(Q_WIDTH), sample(KV_WIDTH), sample(KV_WIDTH),
                   seq_of_lagged((1, KV_WIDTH, WINDOW)), seq_of_lagged((1, KV_WIDTH, WINDOW)),
                   seq_of_lagged((1, POOL_PAD, POOL_WIDTH))),
        scratch_shapes=[pltpu.VMEM((ROW_TILE, D_MODEL), F32),
                        pltpu.VMEM((ROW_TILE, Q_WIDTH), BF16),
                        pltpu.VMEM((WINDOW + ROW_TILE, KV_WIDTH), F32),
                        pltpu.VMEM((WINDOW + ROW_TILE, KV_WIDTH), F32),
                        pltpu.VMEM((POOL_PAD + ROW_TILE, POOL_WIDTH), F32),
                        pltpu.VMEM((ROW_TILE, D_FF), BF16),
                        pltpu.VMEM((D_MODEL, D_FF), BF16), pltpu.VMEM((D_MODEL, D_FF), BF16),
                        pltpu.VMEM((D_FF, D_MODEL), BF16), pltpu.VMEM((D_MODEL, UQKV_WIDTH), BF16)]
                       + _ring_scratch(RING_FF, RING_NARROW),
        compiler_params=pltpu.CompilerParams(dimension_semantics=("arbitrary",),
                                             vmem_limit_bytes=VMEM_LIMIT_BYTES),
        name="front",
    )(sinks, xp, xs, w1, w3, w2, w_in, g, b, rope, jnp.asarray(_prompt_bias()))


def _sample_bias():
    row = np.arange(HEADS_PER_KV * DEC_SEQ * SUB_GROUP)
    row_t = (row // SUB_GROUP) % DEC_SEQ
    row_b = row % SUB_GROUP
    col = np.arange(SUB_GROUP * WINDOW)
    ok_c = (col[None, :] // WINDOW == row_b[:, None]) & (col[None, :] % WINDOW >= row_t[:, None])
    new = np.arange(DEC_SEQ * SUB_GROUP)
    ok_n = (new[None, :] % SUB_GROUP == row_b[:, None]) & (new[None, :] // SUB_GROUP <= row_t[:, None])
    to_bias = lambda ok: np.where(ok, 0.0, NEG_INF).astype(np.float32)
    return to_bias(ok_c), to_bias(ok_n)


def _sample_ctx_kernel(sinks_ref, q0, q1, q2, q3, k0, k1, k2, k3, v0, v1, v2, v3, u0, u1, u2, u3,
                       ckt_ref, cvt_ref, cu_ref, bias_c_ref, bias_n_ref,
                       attn_ref, pool_ref, kt_out, vt_out, pu_out, kbt_s, vbt_s):
    q_t = [q[...].astype(F32) for q in (q0, q1, q2, q3)]
    k_t = [k[...] for k in (k0, k1, k2, k3)]
    v_t = [v[...] for v in (v0, v1, v2, v3)]

    rows_u = [cu_ref[i] for i in range(POOL_BUF)] + [u[...] for u in (u0, u1, u2, u3)]
    for t in range(DEC_SEQ):
        pooled = []
        for g, w in enumerate(POOL_WINDOWS):
            cols = slice(g * POOL_GROUP, (g + 1) * POOL_GROUP)
            cur = rows_u[POOL_BUF + t][:, cols]
            acc = cur
            for j in range(1, w):
                acc = acc + rows_u[POOL_BUF + t - j][:, cols]
            pooled.append(acc / float(w) - cur)
        pool_ref[t] = jnp.concatenate(pooled, axis=1)
    for i in range(POOL_BUF):
        pu_out[i] = rows_u[i + DEC_SEQ]

    bias_c = bias_c_ref[...]
    bias_n = bias_n_ref[...]
    head_of_row = lax.broadcasted_iota(jnp.int32, (bias_c.shape[0], 1), 0) // (DEC_SEQ * SUB_GROUP)
    low = lax.broadcasted_iota(jnp.int32, (1, LANES), 1) < HEAD_DIM
    q_sw = [_swap_halves_wide(q) for q in q_t]
    k_sw = [_swap_halves(k) for k in k_t]
    v_sw = [_swap_halves(v) for v in v_t]
    for sub in range(SEQ_GROUP // SUB_GROUP):
        rows = slice(sub * SUB_GROUP, (sub + 1) * SUB_GROUP)
        for kh in range(N_KV_HEADS):
            def q_piece(t, head):
                src = q_t[t] if head % 2 == 0 else q_sw[t]
                chunk = head // 2
                return jnp.where(low, src[rows, chunk * LANES:(chunk + 1) * LANES], 0.0)

            def kv_first(c):
                return c if kh == 0 else jnp.concatenate([c[HEAD_DIM:], c[:HEAD_DIM]], axis=0)

            def kv_twice(c):
                part = c[kh * HEAD_DIM:(kh + 1) * HEAD_DIM]
                return jnp.concatenate([part, part], axis=0)

            lhs = jnp.concatenate([q_piece(t, kh * HEADS_PER_KV + g)
                                   for g in range(HEADS_PER_KV) for t in range(DEC_SEQ)], axis=0).astype(BF16)
            kcat = jnp.concatenate([kv_first(ckt_ref[sub * SUB_GROUP + b]) for b in range(SUB_GROUP)],
                                   axis=1).astype(BF16)
            vcat = jnp.concatenate([kv_twice(cvt_ref[sub * SUB_GROUP + b]) for b in range(SUB_GROUP)],
                                   axis=1).astype(BF16)
            knew = jnp.concatenate([(k_t[t] if kh == 0 else k_sw[t])[rows] for t in range(DEC_SEQ)],
                                   axis=0).astype(BF16)
            vnew = jnp.concatenate([(jnp.where(low, v_t[t], v_sw[t]) if kh == 0 else
                                     jnp.where(low, v_sw[t], v_t[t]))[rows] for t in range(DEC_SEQ)],
                                   axis=0).astype(BF16)
            s_c = _dot(lhs, kcat) + bias_c
            s_n = _dot_nt(lhs, knew) + bias_n
            sink = jnp.zeros(head_of_row.shape, F32)
            for g in range(HEADS_PER_KV):
                sink = jnp.where(head_of_row == g, sinks_ref[kh * HEADS_PER_KV + g], sink)
            m = jnp.maximum(jnp.maximum(jnp.max(s_c, axis=-1, keepdims=True),
                                        jnp.max(s_n, axis=-1, keepdims=True)), sink)
            p_c = jnp.exp(s_c - m)
            p_n = jnp.exp(s_n - m)
            denom = (jnp.sum(p_c, axis=-1, keepdims=True) + jnp.sum(p_n, axis=-1, keepdims=True)
                     + jnp.exp(sink - m))
            o = (_dot_nt(p_c.astype(BF16), vcat) + _dot(p_n.astype(BF16), vnew)) / denom
            for t in range(DEC_SEQ):
                for pair in range(HEADS_PER_KV // 2):
                    piece = lambda g: o[(g * DEC_SEQ + t) * SUB_GROUP:(g * DEC_SEQ + t + 1) * SUB_GROUP]
                    c0 = (kh * HEADS_PER_KV // 2 + pair) * LANES
                    attn_ref[t, rows, c0:c0 + LANES] = jnp.where(low, piece(2 * pair), piece(2 * pair + 1))

    zeros = jnp.zeros((LANES - DEC_SEQ * SEQ_GROUP, KV_WIDTH), F32)
    kbt_s[DEC_SEQ * SEQ_GROUP:, :] = zeros
    vbt_s[DEC_SEQ * SEQ_GROUP:, :] = zeros
    for t in range(DEC_SEQ):
        kbt_s[pl.ds(t, SEQ_GROUP, stride=DEC_SEQ), :] = k_t[t]
        vbt_s[pl.ds(t, SEQ_GROUP, stride=DEC_SEQ), :] = v_t[t]
    knew_t = kbt_s[...].T
    vnew_t = vbt_s[...].T
    keep = lax.broadcasted_iota(jnp.int32, (1, WINDOW), 1) < WINDOW - DEC_SEQ
    for b in range(SEQ_GROUP):
        shift_new = WINDOW - DEC_SEQ - DEC_SEQ * b
        kt_out[b] = jnp.where(keep, pltpu.roll(ckt_ref[b], WINDOW - DEC_SEQ, axis=1),
                              pltpu.roll(knew_t, shift_new, axis=1))
        vt_out[b] = jnp.where(keep, pltpu.roll(cvt_ref[b], WINDOW - DEC_SEQ, axis=1),
                              pltpu.roll(vnew_t, shift_new, axis=1))


def _sample_ctx(us, qs, ks, vs, ckt, cvt, cu, sinks):
    groups = DEC_BATCH // SEQ_GROUP

    def token_rows(t, width):
        return pl.BlockSpec((SEQ_GROUP, width), lambda i: (t * groups + i, 0))

    def per_token(width):
        return [token_rows(t, width) for t in range(DEC_SEQ)]

    cache_spec = pl.BlockSpec((SEQ_GROUP, KV_WIDTH, WINDOW), lambda i: (i, 0, 0))
    pool_rows_spec = pl.BlockSpec((POOL_BUF, SEQ_GROUP, POOL_WIDTH), lambda i: (0, i, 0))
    by_token = lambda width: pl.BlockSpec((DEC_SEQ, SEQ_GROUP, width), lambda i: (0, i, 0))
    bias_c, bias_n = _sample_bias()
    return pl.pallas_call(
        _sample_ctx_kernel,
        out_shape=(jax.ShapeDtypeStruct((DEC_SEQ, DEC_BATCH, Q_WIDTH), F32),
                   jax.ShapeDtypeStruct((DEC_SEQ, DEC_BATCH, POOL_WIDTH), F32),
                   jax.ShapeDtypeStruct((DEC_BATCH, KV_WIDTH, WINDOW), F32),
                   jax.ShapeDtypeStruct((DEC_BATCH, KV_WIDTH, WINDOW), F32),
                   jax.ShapeDtypeStruct((POOL_BUF, DEC_BATCH, POOL_WIDTH), F32)),
        grid=(groups,),
        in_specs=[pl.BlockSpec(memory_space=pltpu.SMEM)]
                 + per_token(Q_WIDTH) + per_token(KV_WIDTH) + per_token(KV_WIDTH) + per_token(POOL_WIDTH)
                 + [cache_spec, cache_spec, pool_rows_spec, _resident(bias_c.shape), _resident(bias_n.shape)],
        out_specs=(by_token(Q_WIDTH), by_token(POOL_WIDTH), cache_spec, cache_spec, pool_rows_spec),
        scratch_shapes=[pltpu.VMEM((LANES, KV_WIDTH), F32), pltpu.VMEM((LANES, KV_WIDTH), F32)],
        compiler_params=pltpu.CompilerParams(dimension_semantics=("parallel",),
                                             vmem_limit_bytes=VMEM_LIMIT_BYTES),
        name="sample_ctx",
    )(sinks, *([qs] * DEC_SEQ), *([ks] * DEC_SEQ), *([vs] * DEC_SEQ), *([us] * DEC_SEQ),
      ckt, cvt, cu, jnp.asarray(bias_c), jnp.asarray(bias_n))


def _back_kernel(h_ref, pa_ref, pools_ref, attns_ref,
                 win_hbm, wgrp_hbm, scale_ref, wpo_hbm, wao_hbm, wout_hbm, g2_ref, b2_ref,
                 w1_hbm, w3_hbm, w2_hbm, g3_ref, b3_ref, yp_ref, ys_ref, z2_c, z3_c, h_s,
                 wg_ref, wgrp_ref, wpo_ref, wao_ref, wout_ref, w1_ref, w3_ref, w2_ref,
                 buf_ff, buf_group, sem_ff, sem_group, sem_model):
    r = pl.program_id(0) - 1

    def stage_and_reset():
        ring_ff = _ring(buf_ff, sem_ff)
        ring_group = _ring(buf_group, sem_group)
        ring_model = _ring(z2_c, sem_model, rows=MODEL_RING_ROWS)
        _stage_weights(_row_chunks(wgrp_hbm, 0, wgrp_ref, ring_group)
                       + _row_chunks(wpo_hbm, 0, wpo_ref, ring_model)
                       + _row_chunks(wao_hbm, 0, wao_ref, ring_model)
                       + _row_chunks(w1_hbm, 0, w1_ref, ring_ff)
                       + _row_chunks(w3_hbm, 0, w3_ref, ring_ff)
                       + _row_chunks(win_hbm, UQKV_WIDTH, wg_ref.at[:, 0:D_MODEL], ring_model)
                       + _row_chunks(win_hbm, UQKV_WIDTH + D_MODEL, wg_ref.at[:, D_MODEL:2 * D_MODEL], ring_model)
                       + _row_chunks(wout_hbm, 0, wout_ref, ring_model)
                       + _row_chunks(w2_hbm, 0, w2_ref, ring_model))
        z2_c[...] = jnp.zeros(z2_c.shape, z2_c.dtype)
        z3_c[...] = jnp.zeros(z3_c.shape, z3_c.dtype)

    @pl.when(jnp.logical_and(r >= 0, r <= ROW_TILES))
    def _():
        is_sample = r >= PROMPT_TILES
        h1 = h_ref[...]
        pool_in = jnp.where(is_sample, pools_ref[...].astype(BF16), pa_ref[:, 0:POOL_WIDTH])
        attn_o = jnp.where(is_sample, attns_ref[...].astype(BF16), pa_ref[:, POOL_WIDTH:POOL_WIDTH + Q_WIDTH])
        zs = [_dot(pool_in[:, g * POOL_GROUP:(g + 1) * POOL_GROUP],
                   wgrp_ref[g * POOL_GROUP:(g + 1) * POOL_GROUP, :])
              for g in range(len(POOL_WINDOWS))]
        pool_z = jnp.concatenate(zs, axis=1) * scale_ref[...]
        a = _dot(pool_z.astype(BF16), wpo_ref[...])
        b = _dot(attn_o, wao_ref[...])

        yp_ref[...] = _layer_norm(z3_c[...], g3_ref[...], b3_ref[...])
        h2 = _layer_norm(z2_c[...], g2_ref[...], b2_ref[...])
        gated = []

        def gate():
            gates = jax.nn.sigmoid(_dot(h1.astype(BF16), wg_ref[...]))
            gated.append((gates[:, :D_MODEL] * a + gates[:, D_MODEL:] * b).astype(BF16))

        def project_out():
            z2_c[...] = ALPHA * h1 + _dot(gated[0], wout_ref[...])

        _swiglu_residual(h2, w1_ref, w3_ref, w2_ref, h_s, {1: gate, 9: project_out}, z3_c)

    @pl.when(r == ROW_TILES + 1)
    def _():
        ys_ref[...] = _layer_norm(z3_c[...], g3_ref[...], b3_ref[...])

    pl.when(r < 0)(stage_and_reset)


def _back(h1, pool_attn, pool_s, attn_s, w_in, wgrp, scale, wpo, wao, wout, g2, b2, w1, w3, w2, g3, b3):
    lagged = pl.BlockSpec((ROW_TILE, D_MODEL), lambda s: (jnp.clip(s - 3, 0, PROMPT_TILES - 1), 0))
    hbm = pl.BlockSpec(memory_space=pl.ANY)
    return pl.pallas_call(
        _back_kernel,
        out_shape=(jax.ShapeDtypeStruct((PROMPT_ROWS, D_MODEL), F32),
                   jax.ShapeDtypeStruct((SAMPLE_ROWS, D_MODEL), F32)),
        grid=(ROW_TILES + 3,),
        in_specs=[pl.BlockSpec((ROW_TILE, D_MODEL), lambda s: (jnp.clip(s - 1, 0, PROMPT_TILES), 0)),
                  pl.BlockSpec((ROW_TILE, POOL_WIDTH + Q_WIDTH), lambda s: (jnp.clip(s - 1, 0, PROMPT_TILES - 1), 0)),
                  _resident((SAMPLE_ROWS, POOL_WIDTH)), _resident((SAMPLE_ROWS, Q_WIDTH)),
                  hbm, hbm, _resident((1, POOL_WIDTH)), hbm, hbm, hbm,
                  _resident((1, D_MODEL)), _resident((1, D_MODEL)),
                  hbm, hbm, hbm, _resident((1, D_MODEL)), _resident((1, D_MODEL))],
        out_specs=(lagged, pl.BlockSpec((ROW_TILE, D_MODEL), lambda r: (0, 0))),
        scratch_shapes=[pltpu.VMEM((ROW_TILE, D_MODEL), F32), pltpu.VMEM((ROW_TILE, D_MODEL), F32),
                        pltpu.VMEM((ROW_TILE, D_FF), BF16),
                        pltpu.VMEM((D_MODEL, 2 * D_MODEL), BF16),
                        pltpu.VMEM((len(POOL_WINDOWS) * POOL_GROUP, POOL_GROUP), BF16),
                        pltpu.VMEM((POOL_WIDTH, D_MODEL), BF16), pltpu.VMEM((Q_WIDTH, D_MODEL), BF16),
                        pltpu.VMEM((D_MODEL, D_MODEL), BF16),
                        pltpu.VMEM((D_MODEL, D_FF), BF16), pltpu.VMEM((D_MODEL, D_FF), BF16),
                        pltpu.VMEM((D_FF, D_MODEL), BF16)]
                       + _ring_scratch(RING_FF, RING_GROUP),
        compiler_params=pltpu.CompilerParams(dimension_semantics=("arbitrary",),
                                             vmem_limit_bytes=VMEM_LIMIT_BYTES),
        name="back",
    )(h1, pool_attn, pool_s, attn_s, w_in, wgrp, scale, wpo, wao, wout, g2, b2, w1, w3, w2, g3, b3)


def _rope_tables(pos):
    half = HEAD_DIM // 2
    freqs = ROPE_THETA ** (-2.0 * jnp.arange(half, dtype=F32) / HEAD_DIM)
    ang = pos.astype(F32)[:, None] * freqs[None, :]
    cos, sin = jnp.cos(ang), jnp.sin(ang)
    return jnp.tile(cos, (1, 4)), jnp.concatenate([-sin, sin, -sin, sin], axis=1)


def kernel(x_prompt, x_sample, cache_pool_u, cache_k_win, cache_v_win, w_in, pool_w_grp, pool_scale,
           attn_sinks, w_pool_out, w_attn_out, w_out, ffn1_w1, ffn1_w3, ffn1_w2, ffn2_w1, ffn2_w3,
           ffn2_w2, ln1_g, ln1_b, ln2_g, ln2_b, ln3_g, ln3_b):
    assert DEPTH == 1 and w_in.shape[0] == 1
    l = 0
    vec = lambda p: p[l].reshape(1, -1)
    sinks = attn_sinks[l]

    pos = jnp.concatenate([jnp.arange(SEQ, dtype=jnp.int32),
                           jnp.repeat(PAST_LEN + jnp.arange(DEC_SEQ, dtype=jnp.int32), DEC_BATCH)])
    rope = jnp.concatenate(_rope_tables(pos), axis=1)

    xp = x_prompt.reshape(PROMPT_ROWS, D_MODEL)
    xs = jnp.transpose(x_sample, (1, 0, 2)).reshape(SAMPLE_ROWS, D_MODEL)
    (h1, pool_attn, us, qs, ks, vs, kt_last, vt_last, u_last) = _front(
        xp, xs, ffn1_w1, ffn1_w3, ffn1_w2, w_in, vec(ln1_g), vec(ln1_b), rope, sinks)

    to_t = lambda c: jnp.transpose(c[l], (0, 2, 3, 1)).reshape(DEC_BATCH, KV_WIDTH, WINDOW)
    cu = jnp.transpose(cache_pool_u[l], (1, 0, 2))
    attn_s, pool_s, kt_s, vt_s, pu_s = _sample_ctx(us, qs, ks, vs, to_t(cache_k_win), to_t(cache_v_win), cu, sinks)

    wgrp = pool_w_grp.reshape(DEPTH, len(POOL_WINDOWS) * POOL_GROUP, POOL_GROUP)
    yp, ys = _back(h1, pool_attn, pool_s.reshape(SAMPLE_ROWS, POOL_WIDTH), attn_s.reshape(SAMPLE_ROWS, Q_WIDTH),
                   w_in, wgrp, vec(pool_scale), w_pool_out, w_attn_out, w_out, vec(ln2_g), vec(ln2_b),
                   ffn2_w1, ffn2_w3, ffn2_w2, vec(ln3_g), vec(ln3_b))
    yp = yp.reshape(BATCH, SEQ, D_MODEL)
    ys = jnp.transpose(ys.reshape(DEC_SEQ, DEC_BATCH, D_MODEL), (1, 0, 2))

    from_t = lambda c, n: jnp.transpose(c.reshape(n, N_KV_HEADS, HEAD_DIM, WINDOW), (0, 3, 1, 2))[None]
    pool_u_prompt = u_last[None, :, POOL_PAD - POOL_BUF:]
    pool_u_sample = jnp.transpose(pu_s, (1, 0, 2))[None]
    return (yp, ys, pool_u_prompt, from_t(kt_last, BATCH), from_t(vt_last, BATCH),
            pool_u_sample, from_t(kt_s, DEC_BATCH), from_t(vt_s, DEC_BATCH))
```

```python
import jax
import jax.numpy as jnp
import numpy as np
from jax import lax
from jax.experimental import pallas as pl
from jax.experimental.pallas import tpu as pltpu

D_MODEL = 1024
BATCH = 8
SEQ = 2048
DEC_BATCH = 128
DEC_SEQ = 4
PAST_LEN = 8192
POOL_WINDOWS = (2, 4, 8, 16)
POOL_GROUP = 128
POOL_WIDTH = 512
POOL_BUF = 15
N_HEADS = 8
N_KV_HEADS = 2
HEADS_PER_KV = N_HEADS // N_KV_HEADS
HEAD_DIM = 64
Q_WIDTH = 512
KV_WIDTH = 128
WINDOW = 128
ROPE_THETA = 10000.0
D_FF = 2816
DEPTH = 1
ALPHA = (2.0 * DEPTH) ** 0.25
LN_EPS = 1e-5
NEG_INF = -1e30
UQKV_WIDTH = POOL_WIDTH + Q_WIDTH + 2 * KV_WIDTH

LANES = 128
KEY_SPAN = 2 * WINDOW
VMEM_LIMIT_BYTES = 61 * 1024 * 1024

ROW_TILE = 512
TILES_PER_SEQ = SEQ // ROW_TILE
PROMPT_ROWS = BATCH * SEQ
PROMPT_TILES = PROMPT_ROWS // ROW_TILE
SAMPLE_ROWS = DEC_BATCH * DEC_SEQ
ROW_TILES = PROMPT_TILES + 1
POOL_PAD = 16
SEQ_GROUP = 16
SUB_GROUP = 8
FF_CHUNK = 256
OUT_BLOCK = 256

BF16 = jnp.bfloat16
F32 = jnp.float32


def _dot(a, b):
    return jnp.dot(a, b, preferred_element_type=F32)


def _dot_nt(a, b):
    return lax.dot_general(a, b, (((1,), (1,)), ((), ())), preferred_element_type=F32)


def _layer_norm(y, g, b):
    mu = jnp.mean(y, axis=-1, keepdims=True)
    yc = y - mu
    var = jnp.mean(yc * yc, axis=-1, keepdims=True)
    return yc * lax.rsqrt(var + LN_EPS) * g + b


def _resident(shape):
    nd = len(shape)
    return pl.BlockSpec(shape, lambda *_: (0,) * nd, pipeline_mode=pl.Buffered(1))


class _WeightStager:
    def __init__(self, jobs):
        uses, self.staged = {}, []
        for src, ring, dst in jobs:
            slot, sem = ring[uses.get(id(ring), 0) % len(ring)]
            uses[id(ring)] = uses.get(id(ring), 0) + 1
            self.staged.append((pltpu.make_async_copy(src, slot, sem), slot, dst))
        assert all(len(ring) >= min(STAGE_IN_FLIGHT, uses[id(ring)]) for _, ring, _ in jobs)
        self.done = 0
        for copy, _, _ in self.staged[:STAGE_IN_FLIGHT]:
            copy.start()

    def run(self, count=None):
        end = len(self.staged) if count is None else min(self.done + count, len(self.staged))
        for i in range(self.done, end):
            copy, slot, dst = self.staged[i]
            copy.wait()
            dst[...] = slot[...].astype(BF16)
            if i + STAGE_IN_FLIGHT < len(self.staged):
                self.staged[i + STAGE_IN_FLIGHT][0].start()
        self.done = end


def _ring(buf, sems, rows=None):
    if rows is None:
        return [(buf.at[i], sems.at[i]) for i in range(buf.shape[0])]
    return [(buf.at[pl.ds(i * rows, rows), :], sems.at[i]) for i in range(buf.shape[0] // rows)]


def _row_chunks(w_hbm, col0, dst, ring):
    rows, cols = dst.shape
    step, width = ring[0][0].shape
    assert rows % step == 0 and width == cols
    return [(w_hbm.at[0, pl.ds(r0, step), pl.ds(col0, cols)], ring, dst.at[pl.ds(r0, step), :])
            for r0 in range(0, rows, step)]


STAGE_IN_FLIGHT = 4
RING_FF = (STAGE_IN_FLIGHT, 64, D_FF)
MODEL_RING_ROWS = ROW_TILE // STAGE_IN_FLIGHT
RING_NARROW = (STAGE_IN_FLIGHT, 128, UQKV_WIDTH - D_MODEL)
RING_GROUP = (1, len(POOL_WINDOWS) * POOL_GROUP, POOL_GROUP)


def _ring_scratch(*rings):
    return ([pltpu.VMEM(ring, F32) for ring in rings]
            + [pltpu.SemaphoreType.DMA((ring[0],)) for ring in rings]
            + [pltpu.SemaphoreType.DMA((STAGE_IN_FLIGHT,))])


FF_SLOTS = 2 * (D_FF // FF_CHUNK)


def _swiglu_residual(x, w1_ref, w3_ref, w2_ref, h_s, side_work, out_ref):
    assert all(0 <= slot <= FF_SLOTS for slot in side_work)
    run = lambda slot: side_work.get(slot, lambda: None)()
    xb = x.astype(BF16)
    for j in range(D_FF // FF_CHUNK):
        cols = slice(j * FF_CHUNK, (j + 1) * FF_CHUNK)
        a = _dot(xb, w1_ref[:, cols])
        run(2 * j)
        b = _dot(xb, w3_ref[:, cols])
        h_s[:, cols] = ((a * jax.nn.sigmoid(a)) * b).astype(BF16)
        run(2 * j + 1)
    run(FF_SLOTS)
    h = h_s[...]
    for c0 in range(0, D_MODEL, OUT_BLOCK):
        cols = slice(c0, c0 + OUT_BLOCK)
        out_ref[:, cols] = ALPHA * x[:, cols] + 0.5 * _dot(h, w2_ref[:, cols])


def _rope(x, cos, sin_signed, first_half):
    fwd = pltpu.roll(x, LANES - HEAD_DIM // 2, axis=1)
    bwd = pltpu.roll(x, HEAD_DIM // 2, axis=1)
    return x * cos + jnp.where(first_half, fwd, bwd) * sin_signed


def _swap_halves(x):
    return pltpu.roll(x, HEAD_DIM, axis=1)


def _swap_halves_wide(x):
    return jnp.concatenate([_swap_halves(x[:, c:c + LANES]) for c in range(0, x.shape[1], LANES)], axis=1)


def _lane_split(x, x_sw, kh):
    low = lax.broadcasted_iota(jnp.int32, (1, LANES), 1) < HEAD_DIM
    lo, hi = (x, x_sw) if kh == 0 else (x_sw, x)
    return jnp.concatenate([jnp.where(low, lo, 0.0), jnp.where(low, 0.0, hi)], axis=0).astype(BF16)


def _sink_softmax(q_pairs, keys, kh, bias, sinks_ref):
    s = _dot_nt(q_pairs, _lane_split(keys, _swap_halves(keys), kh)) + bias
    second_pair = lax.broadcasted_iota(jnp.int32, (s.shape[0], 1), 0) >= WINDOW
    probs, denoms = [], []
    for c in range(2):
        sc = s[:, c * KEY_SPAN:(c + 1) * KEY_SPAN]
        head = kh * HEADS_PER_KV + c
        sink = jnp.where(second_pair, sinks_ref[head + 2], sinks_ref[head])
        m = jnp.maximum(jnp.max(sc, axis=-1, keepdims=True), sink)
        p = jnp.exp(sc - m)
        denoms.append(jnp.sum(p, axis=-1, keepdims=True) + jnp.exp(sink - m))
        probs.append(p.astype(BF16))
    return jnp.concatenate(probs, axis=1), denoms


def _weighted_values(probs, denoms, vals, kh):
    low = lax.broadcasted_iota(jnp.int32, (1, LANES), 1) < HEAD_DIM
    o = _dot(probs, _lane_split(vals, _swap_halves(vals), kh))
    return o / jnp.where(low, denoms[0], denoms[1])


def _prompt_context_work(tile, q_c, k_c, v_c, u_c, bias_ref, sinks_ref,
                         pool_ref, attn_ref, kt_ref, vt_ref, ulast_ref):
    seq_tile = (tile + TILES_PER_SEQ) % TILES_PER_SEQ
    first_tile = seq_tile == 0
    softmaxed = {}

    def scores(unit, blk, kh):
        r0 = blk * WINDOW
        bias = bias_ref[jnp.where(first_tile, 1, 0)] if blk == 0 else bias_ref[0]
        c0 = 2 * kh * LANES
        q_pairs = jnp.concatenate([q_c[r0:r0 + WINDOW, c0:c0 + LANES],
                                   q_c[r0:r0 + WINDOW, c0 + LANES:c0 + 2 * LANES]], axis=0)
        softmaxed[unit] = _sink_softmax(q_pairs, k_c[r0:r0 + KEY_SPAN, :], kh, bias, sinks_ref)

    def values(unit, blk, kh):
        r0 = blk * WINDOW
        c0 = 2 * kh * LANES
        o = _weighted_values(*softmaxed.pop(unit), v_c[r0:r0 + KEY_SPAN, :], kh)
        attn_ref[r0:r0 + WINDOW, c0:c0 + LANES] = o[:WINDOW].astype(BF16)
        attn_ref[r0:r0 + WINDOW, c0 + LANES:c0 + 2 * LANES] = o[WINDOW:].astype(BF16)

    def pool(groups):
        pos = seq_tile * ROW_TILE + lax.broadcasted_iota(jnp.int32, (ROW_TILE, 1), 0)
        for g in groups:
            w = POOL_WINDOWS[g]
            cols = slice(g * POOL_GROUP, (g + 1) * POOL_GROUP)
            cur = u_c[POOL_PAD:POOL_PAD + ROW_TILE, cols]
            acc = cur
            for j in range(1, w):
                acc = acc + u_c[POOL_PAD - j:POOL_PAD - j + ROW_TILE, cols]
            cnt = jnp.minimum(pos + 1, w).astype(F32)
            pool_ref[:, cols] = (acc / cnt - cur).astype(BF16)

    def sequence_state():
        kt_ref[0] = k_c[ROW_TILE:ROW_TILE + WINDOW, :].T
        vt_ref[0] = v_c[ROW_TILE:ROW_TILE + WINDOW, :].T
        ulast_ref[0] = u_c[ROW_TILE:ROW_TILE + POOL_PAD, :]

    work = {"pool_wide": lambda: pool((3,)), "pool_narrow": lambda: pool((0, 1, 2)),
            "sequence_state": sequence_state}
    for blk in range(ROW_TILE // WINDOW):
        for kh in range(N_KV_HEADS):
            unit = blk * N_KV_HEADS + kh
            work["scores", unit] = lambda unit=unit, blk=blk, kh=kh: scores(unit, blk, kh)
            work["values", unit] = lambda unit=unit, blk=blk, kh=kh: values(unit, blk, kh)
    return work


def _prompt_bias():
    r = np.arange(2 * WINDOW)[:, None] % WINDOW
    c = np.arange(2 * KEY_SPAN)[None, :] % KEY_SPAN
    valid = (r <= c) & (c <= r + WINDOW)
    first = valid & (c >= WINDOW)
    return np.where(np.stack([valid, first]), 0.0, NEG_INF).astype(np.float32)


def _front_kernel(sinks_ref, xp_ref, xs_ref, w1_hbm, w3_hbm, w2_hbm, win_hbm, g_ref, b_ref, rope_ref, bias_ref,
                  h_ref, pa_ref, us_ref, qs_ref, ks_ref, vs_ref, kt_ref, vt_ref, ulast_ref,
                  z_c, q_c, k_c, v_c, u_c, h_s, w1_ref, w3_ref, w2_ref, wu_ref,
                  buf_ff, buf_narrow, sem_ff, sem_narrow, sem_model):
    r = pl.program_id(0) - 1
    prev = r - 1
    pool_ref = pa_ref.at[:, 0:POOL_WIDTH]
    attn_ref = pa_ref.at[:, POOL_WIDTH:POOL_WIDTH + Q_WIDTH]

    def stage_and_reset():
        ring_ff = _ring(buf_ff, sem_ff)
        ring_narrow = _ring(buf_narrow, sem_narrow)
        ring_model = _ring(z_c, sem_model, rows=MODEL_RING_ROWS)
        _WeightStager(_row_chunks(w1_hbm, 0, w1_ref, ring_ff)
                      + _row_chunks(w3_hbm, 0, w3_ref, ring_ff)
                      + _row_chunks(win_hbm, 0, wu_ref.at[:, 0:D_MODEL], ring_model)
                      + _row_chunks(win_hbm, D_MODEL, wu_ref.at[:, D_MODEL:UQKV_WIDTH], ring_narrow)
                      + _row_chunks(w2_hbm, 0, w2_ref, ring_model)).run()
        for ref in (z_c, q_c, k_c, v_c, u_c):
            ref[...] = jnp.zeros(ref.shape, ref.dtype)

    def norm_previous():
        h1 = _layer_norm(z_c[...], g_ref[...], b_ref[...])
        h_ref[...] = h1
        return h1.astype(BF16)

    def rope_tables():
        tile = jnp.clip(prev, 0, PROMPT_TILES)
        table = jnp.where(tile == PROMPT_TILES, TILES_PER_SEQ, tile % TILES_PER_SEQ)
        rows = pl.ds(pl.multiple_of(table * ROW_TILE, ROW_TILE), ROW_TILE)
        cos = rope_ref[rows, 0:LANES]
        lane = lax.broadcasted_iota(jnp.int32, cos.shape, 1)
        return cos, rope_ref[rows, LANES:2 * LANES], (lane & (HEAD_DIM // 2)) == 0

    def project_u(h1b):
        return _dot(h1b, wu_ref[:, 0:POOL_WIDTH])

    def project_q(h1b):
        z = _dot(h1b, wu_ref[:, POOL_WIDTH:POOL_WIDTH + Q_WIDTH])
        tables = rope_tables()
        return [(_rope(z[:, c:c + LANES], *tables) * (HEAD_DIM ** -0.5)).astype(BF16)
                for c in range(0, Q_WIDTH, LANES)]

    def project_kv(h1b):
        z = _dot(h1b, wu_ref[:, POOL_WIDTH + Q_WIDTH:UQKV_WIDTH])
        return _rope(z[:, :KV_WIDTH], *rope_tables()), z[:, KV_WIDTH:]

    def tile_step(x):
        normed = []

        def norm():
            normed.append(norm_previous())

        projected = []

        def project():
            projected.extend([project_u(normed[0]), project_q(normed[0]), *project_kv(normed[0])])

        def carry():
            u, q, k, v = projected
            starts_seq = (prev + TILES_PER_SEQ) % TILES_PER_SEQ == 0
            u_c[0:POOL_PAD, :] = jnp.where(starts_seq, 0.0, u_c[ROW_TILE:ROW_TILE + POOL_PAD, :])
            u_c[POOL_PAD:POOL_PAD + ROW_TILE, :] = u
            for c, qc in enumerate(q):
                q_c[:, c * LANES:(c + 1) * LANES] = qc
            k_c[0:WINDOW, :] = k_c[ROW_TILE:ROW_TILE + WINDOW, :]
            v_c[0:WINDOW, :] = v_c[ROW_TILE:ROW_TILE + WINDOW, :]
            k_c[WINDOW:WINDOW + ROW_TILE, :] = k
            v_c[WINDOW:WINDOW + ROW_TILE, :] = v

        side = {0: norm, FF_SLOTS - 3: project, FF_SLOTS - 1: carry}
        stages = context_stages()
        assert len(stages) < FF_SLOTS - 3
        side.update(enumerate(stages, start=1))
        _swiglu_residual(x, w1_ref, w3_ref, w2_ref, h_s, side, z_c)

    def context_stages():
        context = _prompt_context_work(r - 2, q_c, k_c, v_c, u_c, bias_ref, sinks_ref,
                                       pool_ref, attn_ref, kt_ref, vt_ref, ulast_ref)
        stages = [("scores", 0)]
        for unit in range(1, 8):
            stages += [("scores", unit), ("values", unit - 1)]
        stages.append(("values", 7))

        def pooling_and_state():
            context["pool_wide"]()
            context["pool_narrow"]()
            context["sequence_state"]()

        return [context[stage] for stage in stages] + [pooling_and_state]

    @pl.when(jnp.logical_and(r >= 0, r <= PROMPT_TILES))
    def _():
        tile_step(jnp.where(r == PROMPT_TILES, xs_ref[...], xp_ref[...]))

    @pl.when(r == ROW_TILES)
    def _():
        for work in context_stages():
            work()
        h1b = norm_previous()
        us_ref[...] = project_u(h1b)
        for c, qc in enumerate(project_q(h1b)):
            qs_ref[:, c * LANES:(c + 1) * LANES] = qc
        ks_ref[...], vs_ref[...] = project_kv(h1b)

    pl.when(r < 0)(stage_and_reset)


def _front(xp, xs, w1, w3, w2, w_in, g, b, rope, sinks):
    rows = ROW_TILES * ROW_TILE

    def lagged(width):
        return pl.BlockSpec((ROW_TILE, width), lambda s: (jnp.clip(s - 3, 0, PROMPT_TILES - 1), 0))

    def seq_of_lagged(shape):
        return pl.BlockSpec(shape, lambda s: (jnp.clip(s - 3, 0, PROMPT_TILES - 1) // TILES_PER_SEQ, 0, 0))

    sample = lambda width: pl.BlockSpec((ROW_TILE, width), lambda r: (0, 0))
    hbm = pl.BlockSpec(memory_space=pl.ANY)
    return pl.pallas_call(
        _front_kernel,
        out_shape=(jax.ShapeDtypeStruct((rows, D_MODEL), F32),
                   jax.ShapeDtypeStruct((PROMPT_ROWS, POOL_WIDTH + Q_WIDTH), BF16),
                   jax.ShapeDtypeStruct((SAMPLE_ROWS, POOL_WIDTH), F32),
                   jax.ShapeDtypeStruct((SAMPLE_ROWS, Q_WIDTH), BF16),
                   jax.ShapeDtypeStruct((SAMPLE_ROWS, KV_WIDTH), F32),
                   jax.ShapeDtypeStruct((SAMPLE_ROWS, KV_WIDTH), F32),
                   jax.ShapeDtypeStruct((BATCH, KV_WIDTH, WINDOW), F32),
                   jax.ShapeDtypeStruct((BATCH, KV_WIDTH, WINDOW), F32),
                   jax.ShapeDtypeStruct((BATCH, POOL_PAD, POOL_WIDTH), F32)),
        grid=(ROW_TILES + 2,),
        in_specs=[pl.BlockSpec(memory_space=pltpu.SMEM),
                  pl.BlockSpec((ROW_TILE, D_MODEL), lambda s: (jnp.clip(s - 1, 0, PROMPT_TILES - 1), 0)),
                  _resident((SAMPLE_ROWS, D_MODEL)),
                  hbm, hbm, hbm, hbm, _resident((1, D_MODEL)), _resident((1, D_MODEL)),
                  _resident(rope.shape), _resident((2, 2 * WINDOW, 2 * KEY_SPAN))],
        out_specs=(pl.BlockSpec((ROW_TILE, D_MODEL), lambda s: (jnp.clip(s - 2, 0, PROMPT_TILES), 0)),
                   lagged(POOL_WIDTH + Q_WIDTH),
                   sample(POOL_WIDTH), sample(Q_WIDTH), sample(KV_WIDTH), sample(KV_WIDTH),
                   seq_of_lagged((1, KV_WIDTH, WINDOW)), seq_of_lagged((1, KV_WIDTH, WINDOW)),
                   seq_of_lagged((1, POOL_PAD, POOL_WIDTH))),
        scratch_shapes=[pltpu.VMEM((ROW_TILE, D_MODEL), F32),
                        pltpu.VMEM((ROW_TILE, Q_WIDTH), BF16),
                        pltpu.VMEM((WINDOW + ROW_TILE, KV_WIDTH), F32),
                        pltpu.VMEM((WINDOW + ROW_TILE, KV_WIDTH), F32),
                        pltpu.VMEM((POOL_PAD + ROW_TILE, POOL_WIDTH), F32),
                        pltpu.VMEM((ROW_TILE, D_FF), BF16),
                        pltpu.VMEM((D_MODEL, D_FF), BF16), pltpu.VMEM((D_MODEL, D_FF), BF16),
                        pltpu.VMEM((D_FF, D_MODEL), BF16), pltpu.VMEM((D_MODEL, UQKV_WIDTH), BF16)]
                       + _ring_scratch(RING_FF, RING_NARROW),
        compiler_params=pltpu.CompilerParams(dimension_semantics=("arbitrary",),
                                             vmem_limit_bytes=VMEM_LIMIT_BYTES),
        name="front",
    )(sinks, xp, xs, w1, w3, w2, w_in, g, b, rope, jnp.asarray(_prompt_bias()))


def _sample_bias():
    row = np.arange(HEADS_PER_KV * DEC_SEQ * SUB_GROUP)
    row_t = (row // SUB_GROUP) % DEC_SEQ
    row_b = row % SUB_GROUP
    col = np.arange(SUB_GROUP * WINDOW)
    ok_c = (col[None, :] // WINDOW == row_b[:, None]) & (col[None, :] % WINDOW >= row_t[:, None])
    new = np.arange(DEC_SEQ * SUB_GROUP)
    ok_n = (new[None, :] % SUB_GROUP == row_b[:, None]) & (new[None, :] // SUB_GROUP <= row_t[:, None])
    to_bias = lambda ok: np.where(ok, 0.0, NEG_INF).astype(np.float32)
    return to_bias(ok_c), to_bias(ok_n)


def _sample_ctx_kernel(sinks_ref, q0, q1, q2, q3, k0, k1, k2, k3, v0, v1, v2, v3, u0, u1, u2, u3,
                       ckt_ref, cvt_ref, cu_ref, bias_c_ref, bias_n_ref,
                       attn_ref, pool_ref, kt_out, vt_out, pu_out, kbt_s, vbt_s):
    q_t = [q[...].astype(F32) for q in (q0, q1, q2, q3)]
    k_t = [k[...] for k in (k0, k1, k2, k3)]
    v_t = [v[...] for v in (v0, v1, v2, v3)]

    rows_u = [cu_ref[i] for i in range(POOL_BUF)] + [u[...] for u in (u0, u1, u2, u3)]
    for t in range(DEC_SEQ):
        pooled = []
        for g, w in enumerate(POOL_WINDOWS):
            cols = slice(g * POOL_GROUP, (g + 1) * POOL_GROUP)
            cur = rows_u[POOL_BUF + t][:, cols]
            acc = cur
            for j in range(1, w):
                acc = acc + rows_u[POOL_BUF + t - j][:, cols]
            pooled.append(acc / float(w) - cur)
        pool_ref[t] = jnp.concatenate(pooled, axis=1)
    for i in range(POOL_BUF):
        pu_out[i] = rows_u[i + DEC_SEQ]

    bias_c = bias_c_ref[...]
    bias_n = bias_n_ref[...]
    head_of_row = lax.broadcasted_iota(jnp.int32, (bias_c.shape[0], 1), 0) // (DEC_SEQ * SUB_GROUP)
    low = lax.broadcasted_iota(jnp.int32, (1, LANES), 1) < HEAD_DIM
    q_sw = [_swap_halves_wide(q) for q in q_t]
    k_sw = [_swap_halves(k) for k in k_t]
    v_sw = [_swap_halves(v) for v in v_t]
    for sub in range(SEQ_GROUP // SUB_GROUP):
        rows = slice(sub * SUB_GROUP, (sub + 1) * SUB_GROUP)
        for kh in range(N_KV_HEADS):
            def q_piece(t, head):
                src = q_t[t] if head % 2 == 0 else q_sw[t]
                chunk = head // 2
                return jnp.where(low, src[rows, chunk * LANES:(chunk + 1) * LANES], 0.0)

            def kv_first(c):
                return c if kh == 0 else jnp.concatenate([c[HEAD_DIM:], c[:HEAD_DIM]], axis=0)

            def kv_twice(c):
                part = c[kh * HEAD_DIM:(kh + 1) * HEAD_DIM]
                return jnp.concatenate([part, part], axis=0)

            lhs = jnp.concatenate([q_piece(t, kh * HEADS_PER_KV + g)
                                   for g in range(HEADS_PER_KV) for t in range(DEC_SEQ)], axis=0).astype(BF16)
            kcat = jnp.concatenate([kv_first(ckt_ref[sub * SUB_GROUP + b]) for b in range(SUB_GROUP)],
                                   axis=1).astype(BF16)
            vcat = jnp.concatenate([kv_twice(cvt_ref[sub * SUB_GROUP + b]) for b in range(SUB_GROUP)],
                                   axis=1).astype(BF16)
            knew = jnp.concatenate([(k_t[t] if kh == 0 else k_sw[t])[rows] for t in range(DEC_SEQ)],
                                   axis=0).astype(BF16)
            vnew = jnp.concatenate([(jnp.where(low, v_t[t], v_sw[t]) if kh == 0 else
                                     jnp.where(low, v_sw[t], v_t[t]))[rows] for t in range(DEC_SEQ)],
                                   axis=0).astype(BF16)
            s_c = _dot(lhs, kcat) + bias_c
            s_n = _dot_nt(lhs, knew) + bias_n
            sink = jnp.zeros(head_of_row.shape, F32)
            for g in range(HEADS_PER_KV):
                sink = jnp.where(head_of_row == g, sinks_ref[kh * HEADS_PER_KV + g], sink)
            m = jnp.maximum(jnp.maximum(jnp.max(s_c, axis=-1, keepdims=True),
                                        jnp.max(s_n, axis=-1, keepdims=True)), sink)
            p_c = jnp.exp(s_c - m)
            p_n = jnp.exp(s_n - m)
            denom = (jnp.sum(p_c, axis=-1, keepdims=True) + jnp.sum(p_n, axis=-1, keepdims=True)
                     + jnp.exp(sink - m))
            o = (_dot_nt(p_c.astype(BF16), vcat) + _dot(p_n.astype(BF16), vnew)) / denom
            for t in range(DEC_SEQ):
                for pair in range(HEADS_PER_KV // 2):
                    piece = lambda g: o[(g * DEC_SEQ + t) * SUB_GROUP:(g * DEC_SEQ + t + 1) * SUB_GROUP]
                    c0 = (kh * HEADS_PER_KV // 2 + pair) * LANES
                    attn_ref[t, rows, c0:c0 + LANES] = jnp.where(low, piece(2 * pair), piece(2 * pair + 1))

    zeros = jnp.zeros((LANES - DEC_SEQ * SEQ_GROUP, KV_WIDTH), F32)
    kbt_s[DEC_SEQ * SEQ_GROUP:, :] = zeros
    vbt_s[DEC_SEQ * SEQ_GROUP:, :] = zeros
    for t in range(DEC_SEQ):
        kbt_s[pl.ds(t, SEQ_GROUP, stride=DEC_SEQ), :] = k_t[t]
        vbt_s[pl.ds(t, SEQ_GROUP, stride=DEC_SEQ), :] = v_t[t]
    knew_t = kbt_s[...].T
    vnew_t = vbt_s[...].T
    keep = lax.broadcasted_iota(jnp.int32, (1, WINDOW), 1) < WINDOW - DEC_SEQ
    for b in range(SEQ_GROUP):
        shift_new = WINDOW - DEC_SEQ - DEC_SEQ * b
        kt_out[b] = jnp.where(keep, pltpu.roll(ckt_ref[b], WINDOW - DEC_SEQ, axis=1),
                              pltpu.roll(knew_t, shift_new, axis=1))
        vt_out[b] = jnp.where(keep, pltpu.roll(cvt_ref[b], WINDOW - DEC_SEQ, axis=1),
                              pltpu.roll(vnew_t, shift_new, axis=1))


def _sample_ctx(us, qs, ks, vs, ckt, cvt, cu, sinks):
    groups = DEC_BATCH // SEQ_GROUP

    def token_rows(t, width):
        return pl.BlockSpec((SEQ_GROUP, width), lambda i: (t * groups + i, 0))

    def per_token(width):
        return [token_rows(t, width) for t in range(DEC_SEQ)]

    cache_spec = pl.BlockSpec((SEQ_GROUP, KV_WIDTH, WINDOW), lambda i: (i, 0, 0))
    pool_rows_spec = pl.BlockSpec((POOL_BUF, SEQ_GROUP, POOL_WIDTH), lambda i: (0, i, 0))
    by_token = lambda width: pl.BlockSpec((DEC_SEQ, SEQ_GROUP, width), lambda i: (0, i, 0))
    bias_c, bias_n = _sample_bias()
    return pl.pallas_call(
        _sample_ctx_kernel,
        out_shape=(jax.ShapeDtypeStruct((DEC_SEQ, DEC_BATCH, Q_WIDTH), F32),
                   jax.ShapeDtypeStruct((DEC_SEQ, DEC_BATCH, POOL_WIDTH), F32),
                   jax.ShapeDtypeStruct((DEC_BATCH, KV_WIDTH, WINDOW), F32),
                   jax.ShapeDtypeStruct((DEC_BATCH, KV_WIDTH, WINDOW), F32),
                   jax.ShapeDtypeStruct((POOL_BUF, DEC_BATCH, POOL_WIDTH), F32)),
        grid=(groups,),
        in_specs=[pl.BlockSpec(memory_space=pltpu.SMEM)]
                 + per_token(Q_WIDTH) + per_token(KV_WIDTH) + per_token(KV_WIDTH) + per_token(POOL_WIDTH)
                 + [cache_spec, cache_spec, pool_rows_spec, _resident(bias_c.shape), _resident(bias_n.shape)],
        out_specs=(by_token(Q_WIDTH), by_token(POOL_WIDTH), cache_spec, cache_spec, pool_rows_spec),
        scratch_shapes=[pltpu.VMEM((LANES, KV_WIDTH), F32), pltpu.VMEM((LANES, KV_WIDTH), F32)],
        compiler_params=pltpu.CompilerParams(dimension_semantics=("parallel",),
                                             vmem_limit_bytes=VMEM_LIMIT_BYTES),
        name="sample_ctx",
    )(sinks, *([qs] * DEC_SEQ), *([ks] * DEC_SEQ), *([vs] * DEC_SEQ), *([us] * DEC_SEQ),
      ckt, cvt, cu, jnp.asarray(bias_c), jnp.asarray(bias_n))


def _back_kernel(h_ref, pa_ref, pools_ref, attns_ref,
                 win_hbm, wgrp_hbm, scale_ref, wpo_hbm, wao_hbm, wout_hbm, g2_ref, b2_ref,
                 w1_hbm, w3_hbm, w2_hbm, g3_ref, b3_ref, yp_ref, ys_ref, z2_c, z3_c, h_s,
                 wg_ref, wgrp_ref, wpo_ref, wao_ref, wout_ref, w1_ref, w3_ref, w2_ref,
                 buf_ff, buf_group, sem_ff, sem_group, sem_model):
    t = pl.program_id(0)

    def branch_outputs():
        is_sample = t >= PROMPT_TILES
        pool_in = jnp.where(is_sample, pools_ref[...].astype(BF16), pa_ref[:, 0:POOL_WIDTH])
        attn_o = jnp.where(is_sample, attns_ref[...].astype(BF16), pa_ref[:, POOL_WIDTH:POOL_WIDTH + Q_WIDTH])
        zs = [_dot(pool_in[:, g * POOL_GROUP:(g + 1) * POOL_GROUP],
                   wgrp_ref[g * POOL_GROUP:(g + 1) * POOL_GROUP, :])
              for g in range(len(POOL_WINDOWS))]
        pool_z = jnp.concatenate(zs, axis=1) * scale_ref[...]
        return _dot(pool_z.astype(BF16), wpo_ref[...]), _dot(attn_o, wao_ref[...])

    def gate_and_merge(h1, a, b):
        gates = jax.nn.sigmoid(_dot(h1.astype(BF16), wg_ref[...]))
        return (gates[:, :D_MODEL] * a + gates[:, D_MODEL:] * b).astype(BF16)

    def project_out(h1, merged):
        z2_c[...] = ALPHA * h1 + _dot(merged, wout_ref[...])

    @pl.when(jnp.logical_and(t >= 1, t <= ROW_TILES))
    def _():
        h1 = h_ref[...]
        a, b = branch_outputs()
        yp_ref[...] = _layer_norm(z3_c[...], g3_ref[...], b3_ref[...])
        h2 = _layer_norm(z2_c[...], g2_ref[...], b2_ref[...])
        merged = []
        side = {1: lambda: merged.append(gate_and_merge(h1, a, b)), 9: lambda: project_out(h1, merged[0])}
        _swiglu_residual(h2, w1_ref, w3_ref, w2_ref, h_s, side, z3_c)

    @pl.when(t == ROW_TILES + 1)
    def _():
        ys_ref[...] = _layer_norm(z3_c[...], g3_ref[...], b3_ref[...])

    @pl.when(t == 0)
    def _():
        ring_ff = _ring(buf_ff, sem_ff)
        ring_group = _ring(buf_group, sem_group)
        ring_model = _ring(z3_c, sem_model, rows=MODEL_RING_ROWS)
        mixer = (_row_chunks(wgrp_hbm, 0, wgrp_ref, ring_group)
                 + _row_chunks(wpo_hbm, 0, wpo_ref, ring_model)
                 + _row_chunks(wao_hbm, 0, wao_ref, ring_model)
                 + _row_chunks(win_hbm, UQKV_WIDTH, wg_ref.at[:, 0:D_MODEL], ring_model)
                 + _row_chunks(win_hbm, UQKV_WIDTH + D_MODEL, wg_ref.at[:, D_MODEL:2 * D_MODEL], ring_model)
                 + _row_chunks(wout_hbm, 0, wout_ref, ring_model))
        swiglu = (_row_chunks(w1_hbm, 0, w1_ref, ring_ff) + _row_chunks(w3_hbm, 0, w3_ref, ring_ff)
                  + _row_chunks(w2_hbm, 0, w2_ref, ring_model))
        stager = _WeightStager(mixer + swiglu)
        stager.run(len(mixer))
        h1 = h_ref[...]
        a, b = branch_outputs()
        stager.run(len(swiglu) // 3)
        merged = gate_and_merge(h1, a, b)
        stager.run(len(swiglu) // 3)
        project_out(h1, merged)
        stager.run()
        z3_c[...] = jnp.zeros(z3_c.shape, z3_c.dtype)


def _back(h1, pool_attn, pool_s, attn_s, w_in, wgrp, scale, wpo, wao, wout, g2, b2, w1, w3, w2, g3, b3):
    lagged = pl.BlockSpec((ROW_TILE, D_MODEL), lambda t: (jnp.clip(t - 2, 0, PROMPT_TILES - 1), 0))
    hbm = pl.BlockSpec(memory_space=pl.ANY)
    return pl.pallas_call(
        _back_kernel,
        out_shape=(jax.ShapeDtypeStruct((PROMPT_ROWS, D_MODEL), F32),
                   jax.ShapeDtypeStruct((SAMPLE_ROWS, D_MODEL), F32)),
        grid=(ROW_TILES + 2,),
        in_specs=[pl.BlockSpec((ROW_TILE, D_MODEL), lambda t: (jnp.minimum(t, PROMPT_TILES), 0)),
                  pl.BlockSpec((ROW_TILE, POOL_WIDTH + Q_WIDTH), lambda t: (jnp.minimum(t, PROMPT_TILES - 1), 0)),
                  _resident((SAMPLE_ROWS, POOL_WIDTH)), _resident((SAMPLE_ROWS, Q_WIDTH)),
                  hbm, hbm, _resident((1, POOL_WIDTH)), hbm, hbm, hbm,
                  _resident((1, D_MODEL)), _resident((1, D_MODEL)),
                  hbm, hbm, hbm, _resident((1, D_MODEL)), _resident((1, D_MODEL))],
        out_specs=(lagged, pl.BlockSpec((ROW_TILE, D_MODEL), lambda r: (0, 0))),
        scratch_shapes=[pltpu.VMEM((ROW_TILE, D_MODEL), F32), pltpu.VMEM((ROW_TILE, D_MODEL), F32),
                        pltpu.VMEM((ROW_TILE, D_FF), BF16),
                        pltpu.VMEM((D_MODEL, 2 * D_MODEL), BF16),
                        pltpu.VMEM((len(POOL_WINDOWS) * POOL_GROUP, POOL_GROUP), BF16),
                        pltpu.VMEM((POOL_WIDTH, D_MODEL), BF16), pltpu.VMEM((Q_WIDTH, D_MODEL), BF16),
                        pltpu.VMEM((D_MODEL, D_MODEL), BF16),
                        pltpu.VMEM((D_MODEL, D_FF), BF16), pltpu.VMEM((D_MODEL, D_FF), BF16),
                        pltpu.VMEM((D_FF, D_MODEL), BF16)]
                       + _ring_scratch(RING_FF, RING_GROUP),
        compiler_params=pltpu.CompilerParams(dimension_semantics=("arbitrary",),
                                             vmem_limit_bytes=VMEM_LIMIT_BYTES),
        name="back",
    )(h1, pool_attn, pool_s, attn_s, w_in, wgrp, scale, wpo, wao, wout, g2, b2, w1, w3, w2, g3, b3)


def _rope_tables(pos):
    half = HEAD_DIM // 2
    freqs = ROPE_THETA ** (-2.0 * jnp.arange(half, dtype=F32) / HEAD_DIM)
    ang = pos.astype(F32)[:, None] * freqs[None, :]
    cos, sin = jnp.cos(ang), jnp.sin(ang)
    return jnp.tile(cos, (1, 4)), jnp.concatenate([-sin, sin, -sin, sin], axis=1)


def kernel(x_prompt, x_sample, cache_pool_u, cache_k_win, cache_v_win, w_in, pool_w_grp, pool_scale,
           attn_sinks, w_pool_out, w_attn_out, w_out, ffn1_w1, ffn1_w3, ffn1_w2, ffn2_w1, ffn2_w3,
           ffn2_w2, ln1_g, ln1_b, ln2_g, ln2_b, ln3_g, ln3_b):
    assert DEPTH == 1 and w_in.shape[0] == 1
    l = 0
    vec = lambda p: p[l].reshape(1, -1)
    sinks = attn_sinks[l]

    pos = jnp.concatenate([jnp.arange(SEQ, dtype=jnp.int32),
                           jnp.repeat(PAST_LEN + jnp.arange(DEC_SEQ, dtype=jnp.int32), DEC_BATCH)])
    rope = jnp.concatenate(_rope_tables(pos), axis=1)

    xp = x_prompt.reshape(PROMPT_ROWS, D_MODEL)
    xs = jnp.transpose(x_sample, (1, 0, 2)).reshape(SAMPLE_ROWS, D_MODEL)
    (h1, pool_attn, us, qs, ks, vs, kt_last, vt_last, u_last) = _front(
        xp, xs, ffn1_w1, ffn1_w3, ffn1_w2, w_in, vec(ln1_g), vec(ln1_b), rope, sinks)

    to_t = lambda c: jnp.transpose(c[l], (0, 2, 3, 1)).reshape(DEC_BATCH, KV_WIDTH, WINDOW)
    cu = jnp.transpose(cache_pool_u[l], (1, 0, 2))
    attn_s, pool_s, kt_s, vt_s, pu_s = _sample_ctx(us, qs, ks, vs, to_t(cache_k_win), to_t(cache_v_win), cu, sinks)

    wgrp = pool_w_grp.reshape(DEPTH, len(POOL_WINDOWS) * POOL_GROUP, POOL_GROUP)
    yp, ys = _back(h1, pool_attn, pool_s.reshape(SAMPLE_ROWS, POOL_WIDTH), attn_s.reshape(SAMPLE_ROWS, Q_WIDTH),
                   w_in, wgrp, vec(pool_scale), w_pool_out, w_attn_out, w_out, vec(ln2_g), vec(ln2_b),
                   ffn2_w1, ffn2_w3, ffn2_w2, vec(ln3_g), vec(ln3_b))
    yp = yp.reshape(BATCH, SEQ, D_MODEL)
    ys = jnp.transpose(ys.reshape(DEC_SEQ, DEC_BATCH, D_MODEL), (1, 0, 2))

    from_t = lambda c, n: jnp.transpose(c.reshape(n, N_KV_HEADS, HEAD_DIM, WINDOW), (0, 3, 1, 2))[None]
    pool_u_prompt = u_last[None, :, POOL_PAD - POOL_BUF:]
    pool_u_sample = jnp.transpose(pu_s, (1, 0, 2))[None]
    return (yp, ys, pool_u_prompt, from_t(kt_last, BATCH), from_t(vt_last, BATCH),
            pool_u_sample, from_t(kt_s, DEC_BATCH), from_t(vt_s, DEC_BATCH))
```

```python
import jax
import jax.numpy as jnp
import numpy as np
from jax import lax
from jax.experimental import pallas as pl
from jax.experimental.pallas import tpu as pltpu

D_MODEL = 1024
BATCH = 8
SEQ = 2048
DEC_BATCH = 128
DEC_SEQ = 4
PAST_LEN = 8192
POOL_WINDOWS = (2, 4, 8, 16)
POOL_GROUP = 128
POOL_WIDTH = 512
POOL_BUF = 15
N_HEADS = 8
N_KV_HEADS = 2
HEADS_PER_KV = N_HEADS // N_KV_HEADS
HEAD_DIM = 64
Q_WIDTH = 512
KV_WIDTH = 128
WINDOW = 128
ROPE_THETA = 10000.0
D_FF = 2816
DEPTH = 1
ALPHA = (2.0 * DEPTH) ** 0.25
LN_EPS = 1e-5
NEG_INF = -1e30
UQKV_WIDTH = POOL_WIDTH + Q_WIDTH + 2 * KV_WIDTH

LANES = 128
KEY_SPAN = 2 * WINDOW
VMEM_LIMIT_BYTES = 61 * 1024 * 1024

ROW_TILE = 512
TILES_PER_SEQ = SEQ // ROW_TILE
PROMPT_ROWS = BATCH * SEQ
PROMPT_TILES = PROMPT_ROWS // ROW_TILE
SAMPLE_ROWS = DEC_BATCH * DEC_SEQ
ROW_TILES = PROMPT_TILES + 1
POOL_PAD = 16
SEQ_GROUP = 16
SUB_GROUP = 8
FF_CHUNK = 256
OUT_BLOCK = 256

BF16 = jnp.bfloat16
F32 = jnp.float32


def _dot(a, b):
    return jnp.dot(a, b, preferred_element_type=F32)


def _dot_nt(a, b):
    return lax.dot_general(a, b, (((1,), (1,)), ((), ())), preferred_element_type=F32)


def _layer_norm(y, g, b):
    mu = jnp.mean(y, axis=-1, keepdims=True)
    yc = y - mu
    var = jnp.mean(yc * yc, axis=-1, keepdims=True)
    return yc * lax.rsqrt(var + LN_EPS) * g + b


def _resident(shape):
    nd = len(shape)
    return pl.BlockSpec(shape, lambda *_: (0,) * nd, pipeline_mode=pl.Buffered(1))


class _WeightStager:
    def __init__(self, jobs):
        uses, self.staged = {}, []
        for src, ring, dst in jobs:
            slot, sem = ring[uses.get(id(ring), 0) % len(ring)]
            uses[id(ring)] = uses.get(id(ring), 0) + 1
            self.staged.append((pltpu.make_async_copy(src, slot, sem), slot, dst))
        assert all(len(ring) >= min(STAGE_IN_FLIGHT, uses[id(ring)]) for _, ring, _ in jobs)
        self.done = 0
        for copy, _, _ in self.staged[:STAGE_IN_FLIGHT]:
            copy.start()

    def run(self, count=None):
        end = len(self.staged) if count is None else min(self.done + count, len(self.staged))
        for i in range(self.done, end):
            copy, slot, dst = self.staged[i]
            copy.wait()
            dst[...] = slot[...].astype(BF16)
            if i + STAGE_IN_FLIGHT < len(self.staged):
                self.staged[i + STAGE_IN_FLIGHT][0].start()
        self.done = end


def _ring(buf, sems, rows=None):
    if rows is None:
        return [(buf.at[i], sems.at[i]) for i in range(buf.shape[0])]
    return [(buf.at[pl.ds(i * rows, rows), :], sems.at[i]) for i in range(buf.shape[0] // rows)]


def _row_chunks(w_hbm, col0, dst, ring):
    rows, cols = dst.shape
    step, width = ring[0][0].shape
    assert rows % step == 0 and width == cols
    return [(w_hbm.at[0, pl.ds(r0, step), pl.ds(col0, cols)], ring, dst.at[pl.ds(r0, step), :])
            for r0 in range(0, rows, step)]


STAGE_IN_FLIGHT = 4
RING_FF = (STAGE_IN_FLIGHT, 64, D_FF)
MODEL_RING_ROWS = ROW_TILE // STAGE_IN_FLIGHT
RING_NARROW = (STAGE_IN_FLIGHT, 128, UQKV_WIDTH - D_MODEL)
RING_GROUP = (1, len(POOL_WINDOWS) * POOL_GROUP, POOL_GROUP)


def _ring_scratch(*rings):
    return ([pltpu.VMEM(ring, F32) for ring in rings]
            + [pltpu.SemaphoreType.DMA((ring[0],)) for ring in rings]
            + [pltpu.SemaphoreType.DMA((STAGE_IN_FLIGHT,))])


FF_SLOTS = 2 * (D_FF // FF_CHUNK)


def _swiglu_residual(x, w1_ref, w3_ref, w2_ref, h_s, side_work, out_ref):
    assert all(0 <= slot <= FF_SLOTS for slot in side_work)
    run = lambda slot: side_work.get(slot, lambda: None)()
    xb = x.astype(BF16)
    for j in range(D_FF // FF_CHUNK):
        cols = slice(j * FF_CHUNK, (j + 1) * FF_CHUNK)
        a = _dot(xb, w1_ref[:, cols])
        run(2 * j)
        b = _dot(xb, w3_ref[:, cols])
        h_s[:, cols] = ((a * jax.nn.sigmoid(a)) * b).astype(BF16)
        run(2 * j + 1)
    run(FF_SLOTS)
    h = h_s[...]
    for c0 in range(0, D_MODEL, OUT_BLOCK):
        cols = slice(c0, c0 + OUT_BLOCK)
        out_ref[:, cols] = ALPHA * x[:, cols] + 0.5 * _dot(h, w2_ref[:, cols])


def _rope(x, cos, sin_signed, first_half):
    fwd = pltpu.roll(x, LANES - HEAD_DIM // 2, axis=1)
    bwd = pltpu.roll(x, HEAD_DIM // 2, axis=1)
    return x * cos + jnp.where(first_half, fwd, bwd) * sin_signed


def _swap_halves(x):
    return pltpu.roll(x, HEAD_DIM, axis=1)


def _swap_halves_wide(x):
    return jnp.concatenate([_swap_halves(x[:, c:c + LANES]) for c in range(0, x.shape[1], LANES)], axis=1)


def _lane_split(x, x_sw, kh):
    low = lax.broadcasted_iota(jnp.int32, (1, LANES), 1) < HEAD_DIM
    lo, hi = (x, x_sw) if kh == 0 else (x_sw, x)
    return jnp.concatenate([jnp.where(low, lo, 0.0), jnp.where(low, 0.0, hi)], axis=0).astype(BF16)


def _sink_softmax(q_pairs, keys, kh, bias, sinks_ref):
    s = _dot_nt(q_pairs, _lane_split(keys, _swap_halves(keys), kh)) + bias
    second_pair = lax.broadcasted_iota(jnp.int32, (s.shape[0], 1), 0) >= WINDOW
    probs, denoms = [], []
    for c in range(2):
        sc = s[:, c * KEY_SPAN:(c + 1) * KEY_SPAN]
        head = kh * HEADS_PER_KV + c
        sink = jnp.where(second_pair, sinks_ref[head + 2], sinks_ref[head])
        m = jnp.maximum(jnp.max(sc, axis=-1, keepdims=True), sink)
        p = jnp.exp(sc - m)
        denoms.append(jnp.sum(p, axis=-1, keepdims=True) + jnp.exp(sink - m))
        probs.append(p.astype(BF16))
    return jnp.concatenate(probs, axis=1), denoms


def _weighted_values(probs, denoms, vals, kh):
    low = lax.broadcasted_iota(jnp.int32, (1, LANES), 1) < HEAD_DIM
    o = _dot(probs, _lane_split(vals, _swap_halves(vals), kh))
    return o / jnp.where(low, denoms[0], denoms[1])


def _prompt_context_work(tile, q_c, k_c, v_c, u_c, bias_ref, sinks_ref,
                         pool_ref, attn_ref, kt_ref, vt_ref, ulast_ref):
    seq_tile = (tile + TILES_PER_SEQ) % TILES_PER_SEQ
    first_tile = seq_tile == 0
    softmaxed = {}

    def scores(unit, blk, kh):
        r0 = blk * WINDOW
        bias = bias_ref[jnp.where(first_tile, 1, 0)] if blk == 0 else bias_ref[0]
        c0 = 2 * kh * LANES
        q_pairs = jnp.concatenate([q_c[r0:r0 + WINDOW, c0:c0 + LANES],
                                   q_c[r0:r0 + WINDOW, c0 + LANES:c0 + 2 * LANES]], axis=0)
        softmaxed[unit] = _sink_softmax(q_pairs, k_c[r0:r0 + KEY_SPAN, :], kh, bias, sinks_ref)

    def values(unit, blk, kh):
        r0 = blk * WINDOW
        c0 = 2 * kh * LANES
        o = _weighted_values(*softmaxed.pop(unit), v_c[r0:r0 + KEY_SPAN, :], kh)
        attn_ref[r0:r0 + WINDOW, c0:c0 + LANES] = o[:WINDOW].astype(BF16)
        attn_ref[r0:r0 + WINDOW, c0 + LANES:c0 + 2 * LANES] = o[WINDOW:].astype(BF16)

    def pool(groups):
        pos = seq_tile * ROW_TILE + lax.broadcasted_iota(jnp.int32, (ROW_TILE, 1), 0)
        for g in groups:
            w = POOL_WINDOWS[g]
            cols = slice(g * POOL_GROUP, (g + 1) * POOL_GROUP)
            cur = u_c[POOL_PAD:POOL_PAD + ROW_TILE, cols]
            acc = cur
            for j in range(1, w):
                acc = acc + u_c[POOL_PAD - j:POOL_PAD - j + ROW_TILE, cols]
            cnt = jnp.minimum(pos + 1, w).astype(F32)
            pool_ref[:, cols] = (acc / cnt - cur).astype(BF16)

    def sequence_state():
        kt_ref[0] = k_c[ROW_TILE:ROW_TILE + WINDOW, :].T
        vt_ref[0] = v_c[ROW_TILE:ROW_TILE + WINDOW, :].T
        ulast_ref[0] = u_c[ROW_TILE:ROW_TILE + POOL_PAD, :]

    work = {"pool_wide": lambda: pool((3,)), "pool_narrow": lambda: pool((0, 1, 2)),
            "sequence_state": sequence_state}
    for blk in range(ROW_TILE // WINDOW):
        for kh in range(N_KV_HEADS):
            unit = blk * N_KV_HEADS + kh
            work["scores", unit] = lambda unit=unit, blk=blk, kh=kh: scores(unit, blk, kh)
            work["values", unit] = lambda unit=unit, blk=blk, kh=kh: values(unit, blk, kh)
    return work


def _prompt_bias():
    r = np.arange(2 * WINDOW)[:, None] % WINDOW
    c = np.arange(2 * KEY_SPAN)[None, :] % KEY_SPAN
    valid = (r <= c) & (c <= r + WINDOW)
    first = valid & (c >= WINDOW)
    return np.where(np.stack([valid, first]), 0.0, NEG_INF).astype(np.float32)


def _front_kernel(sinks_ref, xp_ref, xs_ref, w1_hbm, w3_hbm, w2_hbm, win_hbm, g_ref, b_ref, freq_ref, bias_ref,
                  h_ref, pa_ref, us_ref, qs_ref, ks_ref, vs_ref, kt_ref, vt_ref, ulast_ref,
                  z_c, q_c, k_c, v_c, u_c, h_s, rope_ref, w1_ref, w3_ref, w2_ref, wu_ref,
                  buf_ff, buf_narrow, sem_ff, sem_narrow, sem_model):
    r = pl.program_id(0) - 1
    prev = r - 1
    pool_ref = pa_ref.at[:, 0:POOL_WIDTH]
    attn_ref = pa_ref.at[:, POOL_WIDTH:POOL_WIDTH + Q_WIDTH]

    def stage_and_reset():
        ring_ff = _ring(buf_ff, sem_ff)
        ring_narrow = _ring(buf_narrow, sem_narrow)
        ring_model = _ring(z_c, sem_model, rows=MODEL_RING_ROWS)
        stager = _WeightStager(_row_chunks(w1_hbm, 0, w1_ref, ring_ff)
                               + _row_chunks(w3_hbm, 0, w3_ref, ring_ff)
                               + _row_chunks(win_hbm, 0, wu_ref.at[:, 0:D_MODEL], ring_model)
                               + _row_chunks(win_hbm, D_MODEL, wu_ref.at[:, D_MODEL:UQKV_WIDTH], ring_narrow)
                               + _row_chunks(w2_hbm, 0, w2_ref, ring_model))
        freq = freq_ref[...]
        lane = lax.broadcasted_iota(jnp.int32, (1, LANES), 1)
        sign = jnp.where((lane & (HEAD_DIM // 2)) == 0, -1.0, 1.0)
        row = lax.broadcasted_iota(jnp.int32, (ROW_TILE, 1), 0)
        per_tile = -(-len(stager.staged) // (TILES_PER_SEQ + 1))
        for tile in range(TILES_PER_SEQ + 1):
            pos = tile * ROW_TILE + row if tile < TILES_PER_SEQ else PAST_LEN + row // DEC_BATCH
            ang = pos.astype(F32) * freq
            rows = slice(tile * ROW_TILE, (tile + 1) * ROW_TILE)
            rope_ref[rows, 0:LANES] = jnp.cos(ang)
            rope_ref[rows, LANES:2 * LANES] = jnp.sin(ang) * sign
            stager.run(per_tile)
        stager.run()
        for ref in (z_c, q_c, k_c, v_c, u_c):
            ref[...] = jnp.zeros(ref.shape, ref.dtype)

    def norm_previous():
        h1 = _layer_norm(z_c[...], g_ref[...], b_ref[...])
        h_ref[...] = h1
        return h1.astype(BF16)

    def rope_tables():
        tile = jnp.clip(prev, 0, PROMPT_TILES)
        table = jnp.where(tile == PROMPT_TILES, TILES_PER_SEQ, tile % TILES_PER_SEQ)
        rows = pl.ds(pl.multiple_of(table * ROW_TILE, ROW_TILE), ROW_TILE)
        cos = rope_ref[rows, 0:LANES]
        lane = lax.broadcasted_iota(jnp.int32, cos.shape, 1)
        return cos, rope_ref[rows, LANES:2 * LANES], (lane & (HEAD_DIM // 2)) == 0

    def project_u(h1b):
        return _dot(h1b, wu_ref[:, 0:POOL_WIDTH])

    def project_q(h1b):
        z = _dot(h1b, wu_ref[:, POOL_WIDTH:POOL_WIDTH + Q_WIDTH])
        tables = rope_tables()
        return [(_rope(z[:, c:c + LANES], *tables) * (HEAD_DIM ** -0.5)).astype(BF16)
                for c in range(0, Q_WIDTH, LANES)]

    def project_kv(h1b):
        z = _dot(h1b, wu_ref[:, POOL_WIDTH + Q_WIDTH:UQKV_WIDTH])
        return _rope(z[:, :KV_WIDTH], *rope_tables()), z[:, KV_WIDTH:]

    def tile_step(x):
        normed = []

        def norm():
            normed.append(norm_previous())

        projected = []

        def project():
            projected.extend([project_u(normed[0]), project_q(normed[0]), *project_kv(normed[0])])

        def carry():
            u, q, k, v = projected
            starts_seq = (prev + TILES_PER_SEQ) % TILES_PER_SEQ == 0
            u_c[0:POOL_PAD, :] = jnp.where(starts_seq, 0.0, u_c[ROW_TILE:ROW_TILE + POOL_PAD, :])
            u_c[POOL_PAD:POOL_PAD + ROW_TILE, :] = u
            for c, qc in enumerate(q):
                q_c[:, c * LANES:(c + 1) * LANES] = qc
            k_c[0:WINDOW, :] = k_c[ROW_TILE:ROW_TILE + WINDOW, :]
            v_c[0:WINDOW, :] = v_c[ROW_TILE:ROW_TILE + WINDOW, :]
            k_c[WINDOW:WINDOW + ROW_TILE, :] = k
            v_c[WINDOW:WINDOW + ROW_TILE, :] = v

        side = {0: norm, FF_SLOTS - 3: project, FF_SLOTS - 1: carry}
        stages = context_stages()
        assert len(stages) < FF_SLOTS - 3
        side.update(enumerate(stages, start=1))
        _swiglu_residual(x, w1_ref, w3_ref, w2_ref, h_s, side, z_c)

    def context_stages():
        context = _prompt_context_work(r - 2, q_c, k_c, v_c, u_c, bias_ref, sinks_ref,
                                       pool_ref, attn_ref, kt_ref, vt_ref, ulast_ref)
        stages = [("scores", 0)]
        for unit in range(1, 8):
            stages += [("scores", unit), ("values", unit - 1)]
        stages.append(("values", 7))

        def pooling_and_state():
            context["pool_wide"]()
            context["pool_narrow"]()
            context["sequence_state"]()

        return [context[stage] for stage in stages] + [pooling_and_state]

    @pl.when(jnp.logical_and(r >= 0, r <= PROMPT_TILES))
    def _():
        tile_step(jnp.where(r == PROMPT_TILES, xs_ref[...], xp_ref[...]))

    @pl.when(r == ROW_TILES)
    def _():
        for work in context_stages():
            work()
        h1b = norm_previous()
        us_ref[...] = project_u(h1b)
        for c, qc in enumerate(project_q(h1b)):
            qs_ref[:, c * LANES:(c + 1) * LANES] = qc
        ks_ref[...], vs_ref[...] = project_kv(h1b)

    pl.when(r < 0)(stage_and_reset)


def _front(xp, xs, w1, w3, w2, w_in, g, b, freq, sinks):
    rows = ROW_TILES * ROW_TILE

    def lagged(width):
        return pl.BlockSpec((ROW_TILE, width), lambda s: (jnp.clip(s - 3, 0, PROMPT_TILES - 1), 0))

    def seq_of_lagged(shape):
        return pl.BlockSpec(shape, lambda s: (jnp.clip(s - 3, 0, PROMPT_TILES - 1) // TILES_PER_SEQ, 0, 0))

    sample = lambda width: pl.BlockSpec((ROW_TILE, width), lambda r: (0, 0))
    hbm = pl.BlockSpec(memory_space=pl.ANY)
    return pl.pallas_call(
        _front_kernel,
        out_shape=(jax.ShapeDtypeStruct((rows, D_MODEL), F32),
                   jax.ShapeDtypeStruct((PROMPT_ROWS, POOL_WIDTH + Q_WIDTH), BF16),
                   jax.ShapeDtypeStruct((SAMPLE_ROWS, POOL_WIDTH), F32),
                   jax.ShapeDtypeStruct((SAMPLE_ROWS, Q_WIDTH), BF16),
                   jax.ShapeDtypeStruct((SAMPLE_ROWS, KV_WIDTH), F32),
                   jax.ShapeDtypeStruct((SAMPLE_ROWS, KV_WIDTH), F32),
                   jax.ShapeDtypeStruct((BATCH, KV_WIDTH, WINDOW), F32),
                   jax.ShapeDtypeStruct((BATCH, KV_WIDTH, WINDOW), F32),
                   jax.ShapeDtypeStruct((BATCH, POOL_PAD, POOL_WIDTH), F32)),
        grid=(ROW_TILES + 2,),
        in_specs=[pl.BlockSpec(memory_space=pltpu.SMEM),
                  pl.BlockSpec((ROW_TILE, D_MODEL), lambda s: (jnp.clip(s - 1, 0, PROMPT_TILES - 1), 0)),
                  _resident((SAMPLE_ROWS, D_MODEL)),
                  hbm, hbm, hbm, hbm, _resident((1, D_MODEL)), _resident((1, D_MODEL)),
                  _resident((1, LANES)), _resident((2, 2 * WINDOW, 2 * KEY_SPAN))],
        out_specs=(pl.BlockSpec((ROW_TILE, D_MODEL), lambda s: (jnp.clip(s - 2, 0, PROMPT_TILES), 0)),
                   lagged(POOL_WIDTH + Q_WIDTH),
                   sample(POOL_WIDTH), sample(Q_WIDTH), sample(KV_WIDTH), sample(KV_WIDTH),
                   seq_of_lagged((1, KV_WIDTH, WINDOW)), seq_of_lagged((1, KV_WIDTH, WINDOW)),
                   seq_of_lagged((1, POOL_PAD, POOL_WIDTH))),
        scratch_shapes=[pltpu.VMEM((ROW_TILE, D_MODEL), F32),
                        pltpu.VMEM((ROW_TILE, Q_WIDTH), BF16),
                        pltpu.VMEM((WINDOW + ROW_TILE, KV_WIDTH), F32),
                        pltpu.VMEM((WINDOW + ROW_TILE, KV_WIDTH), F32),
                        pltpu.VMEM((POOL_PAD + ROW_TILE, POOL_WIDTH), F32),
                        pltpu.VMEM((ROW_TILE, D_FF), BF16),
                        pltpu.VMEM(((TILES_PER_SEQ + 1) * ROW_TILE, 2 * LANES), F32),
                        pltpu.VMEM((D_MODEL, D_FF), BF16), pltpu.VMEM((D_MODEL, D_FF), BF16),
                        pltpu.VMEM((D_FF, D_MODEL), BF16), pltpu.VMEM((D_MODEL, UQKV_WIDTH), BF16)]
                       + _ring_scratch(RING_FF, RING_NARROW),
        compiler_params=pltpu.CompilerParams(dimension_semantics=("arbitrary",),
                                             vmem_limit_bytes=VMEM_LIMIT_BYTES),
        name="front",
    )(sinks, xp, xs, w1, w3, w2, w_in, g, b, freq, jnp.asarray(_prompt_bias()))


def _sample_bias():
    row = np.arange(HEADS_PER_KV * DEC_SEQ * SUB_GROUP)
    row_t = (row // SUB_GROUP) % DEC_SEQ
    row_b = row % SUB_GROUP
    col = np.arange(SUB_GROUP * WINDOW)
    ok_c = (col[None, :] // WINDOW == row_b[:, None]) & (col[None, :] % WINDOW >= row_t[:, None])
    new = np.arange(DEC_SEQ * SUB_GROUP)
    ok_n = (new[None, :] % SUB_GROUP == row_b[:, None]) & (new[None, :] // SUB_GROUP <= row_t[:, None])
    to_bias = lambda ok: np.where(ok, 0.0, NEG_INF).astype(np.float32)
    return to_bias(ok_c), to_bias(ok_n)


def _sample_ctx_kernel(sinks_ref, q0, q1, q2, q3, k0, k1, k2, k3, v0, v1, v2, v3, u0, u1, u2, u3,
                       ckt_ref, cvt_ref, cu_ref, bias_c_ref, bias_n_ref,
                       attn_ref, pool_ref, kt_out, vt_out, pu_out, kbt_s, vbt_s):
    q_t = [q[...].astype(F32) for q in (q0, q1, q2, q3)]
    k_t = [k[...] for k in (k0, k1, k2, k3)]
    v_t = [v[...] for v in (v0, v1, v2, v3)]

    rows_u = [cu_ref[i] for i in range(POOL_BUF)] + [u[...] for u in (u0, u1, u2, u3)]
    for t in range(DEC_SEQ):
        pooled = []
        for g, w in enumerate(POOL_WINDOWS):
            cols = slice(g * POOL_GROUP, (g + 1) * POOL_GROUP)
            cur = rows_u[POOL_BUF + t][:, cols]
            acc = cur
            for j in range(1, w):
                acc = acc + rows_u[POOL_BUF + t - j][:, cols]
            pooled.append(acc / float(w) - cur)
        pool_ref[t] = jnp.concatenate(pooled, axis=1)
    for i in range(POOL_BUF):
        pu_out[i] = rows_u[i + DEC_SEQ]

    bias_c = bias_c_ref[...]
    bias_n = bias_n_ref[...]
    head_of_row = lax.broadcasted_iota(jnp.int32, (bias_c.shape[0], 1), 0) // (DEC_SEQ * SUB_GROUP)
    low = lax.broadcasted_iota(jnp.int32, (1, LANES), 1) < HEAD_DIM
    q_sw = [_swap_halves_wide(q) for q in q_t]
    k_sw = [_swap_halves(k) for k in k_t]
    v_sw = [_swap_halves(v) for v in v_t]
    for sub in range(SEQ_GROUP // SUB_GROUP):
        rows = slice(sub * SUB_GROUP, (sub + 1) * SUB_GROUP)
        for kh in range(N_KV_HEADS):
            def q_piece(t, head):
                src = q_t[t] if head % 2 == 0 else q_sw[t]
                chunk = head // 2
                return jnp.where(low, src[rows, chunk * LANES:(chunk + 1) * LANES], 0.0)

            def kv_first(c):
                return c if kh == 0 else jnp.concatenate([c[HEAD_DIM:], c[:HEAD_DIM]], axis=0)

            def kv_twice(c):
                part = c[kh * HEAD_DIM:(kh + 1) * HEAD_DIM]
                return jnp.concatenate([part, part], axis=0)

            lhs = jnp.concatenate([q_piece(t, kh * HEADS_PER_KV + g)
                                   for g in range(HEADS_PER_KV) for t in range(DEC_SEQ)], axis=0).astype(BF16)
            kcat = jnp.concatenate([kv_first(ckt_ref[sub * SUB_GROUP + b]) for b in range(SUB_GROUP)],
                                   axis=1).astype(BF16)
            vcat = jnp.concatenate([kv_twice(cvt_ref[sub * SUB_GROUP + b]) for b in range(SUB_GROUP)],
                                   axis=1).astype(BF16)
            knew = jnp.concatenate([(k_t[t] if kh == 0 else k_sw[t])[rows] for t in range(DEC_SEQ)],
                                   axis=0).astype(BF16)
            vnew = jnp.concatenate([(jnp.where(low, v_t[t], v_sw[t]) if kh == 0 else
                                     jnp.where(low, v_sw[t], v_t[t]))[rows] for t in range(DEC_SEQ)],
                                   axis=0).astype(BF16)
            s_c = _dot(lhs, kcat) + bias_c
            s_n = _dot_nt(lhs, knew) + bias_n
            sink = jnp.zeros(head_of_row.shape, F32)
            for g in range(HEADS_PER_KV):
                sink = jnp.where(head_of_row == g, sinks_ref[kh * HEADS_PER_KV + g], sink)
            m = jnp.maximum(jnp.maximum(jnp.max(s_c, axis=-1, keepdims=True),
                                        jnp.max(s_n, axis=-1, keepdims=True)), sink)
            p_c = jnp.exp(s_c - m)
            p_n = jnp.exp(s_n - m)
            denom = (jnp.sum(p_c, axis=-1, keepdims=True) + jnp.sum(p_n, axis=-1, keepdims=True)
                     + jnp.exp(sink - m))
            o = (_dot_nt(p_c.astype(BF16), vcat) + _dot(p_n.astype(BF16), vnew)) / denom
            for t in range(DEC_SEQ):
                for pair in range(HEADS_PER_KV // 2):
                    piece = lambda g: o[(g * DEC_SEQ + t) * SUB_GROUP:(g * DEC_SEQ + t + 1) * SUB_GROUP]
                    c0 = (kh * HEADS_PER_KV // 2 + pair) * LANES
                    attn_ref[t, rows, c0:c0 + LANES] = jnp.where(low, piece(2 * pair), piece(2 * pair + 1))

    zeros = jnp.zeros((LANES - DEC_SEQ * SEQ_GROUP, KV_WIDTH), F32)
    kbt_s[DEC_SEQ * SEQ_GROUP:, :] = zeros
    vbt_s[DEC_SEQ * SEQ_GROUP:, :] = zeros
    for t in range(DEC_SEQ):
        kbt_s[pl.ds(t, SEQ_GROUP, stride=DEC_SEQ), :] = k_t[t]
        vbt_s[pl.ds(t, SEQ_GROUP, stride=DEC_SEQ), :] = v_t[t]
    knew_t = kbt_s[...].T
    vnew_t = vbt_s[...].T
    keep = lax.broadcasted_iota(jnp.int32, (1, WINDOW), 1) < WINDOW - DEC_SEQ
    for b in range(SEQ_GROUP):
        shift_new = WINDOW - DEC_SEQ - DEC_SEQ * b
        kt_out[b] = jnp.where(keep, pltpu.roll(ckt_ref[b], WINDOW - DEC_SEQ, axis=1),
                              pltpu.roll(knew_t, shift_new, axis=1))
        vt_out[b] = jnp.where(keep, pltpu.roll(cvt_ref[b], WINDOW - DEC_SEQ, axis=1),
                              pltpu.roll(vnew_t, shift_new, axis=1))


def _sample_ctx(us, qs, ks, vs, ckt, cvt, cu, sinks):
    groups = DEC_BATCH // SEQ_GROUP

    def token_rows(t, width):
        return pl.BlockSpec((SEQ_GROUP, width), lambda i: (t * groups + i, 0))

    def per_token(width):
        return [token_rows(t, width) for t in range(DEC_SEQ)]

    cache_spec = pl.BlockSpec((SEQ_GROUP, KV_WIDTH, WINDOW), lambda i: (i, 0, 0))
    pool_rows_spec = pl.BlockSpec((POOL_BUF, SEQ_GROUP, POOL_WIDTH), lambda i: (0, i, 0))
    by_token = lambda width: pl.BlockSpec((DEC_SEQ, SEQ_GROUP, width), lambda i: (0, i, 0))
    bias_c, bias_n = _sample_bias()
    return pl.pallas_call(
        _sample_ctx_kernel,
        out_shape=(jax.ShapeDtypeStruct((DEC_SEQ, DEC_BATCH, Q_WIDTH), F32),
                   jax.ShapeDtypeStruct((DEC_SEQ, DEC_BATCH, POOL_WIDTH), F32),
                   jax.ShapeDtypeStruct((DEC_BATCH, KV_WIDTH, WINDOW), F32),
                   jax.ShapeDtypeStruct((DEC_BATCH, KV_WIDTH, WINDOW), F32),
                   jax.ShapeDtypeStruct((POOL_BUF, DEC_BATCH, POOL_WIDTH), F32)),
        grid=(groups,),
        in_specs=[pl.BlockSpec(memory_space=pltpu.SMEM)]
                 + per_token(Q_WIDTH) + per_token(KV_WIDTH) + per_token(KV_WIDTH) + per_token(POOL_WIDTH)
                 + [cache_spec, cache_spec, pool_rows_spec, _resident(bias_c.shape), _resident(bias_n.shape)],
        out_specs=(by_token(Q_WIDTH), by_token(POOL_WIDTH), cache_spec, cache_spec, pool_rows_spec),
        scratch_shapes=[pltpu.VMEM((LANES, KV_WIDTH), F32), pltpu.VMEM((LANES, KV_WIDTH), F32)],
        compiler_params=pltpu.CompilerParams(dimension_semantics=("parallel",),
                                             vmem_limit_bytes=VMEM_LIMIT_BYTES),
        name="sample_ctx",
    )(sinks, *([qs] * DEC_SEQ), *([ks] * DEC_SEQ), *([vs] * DEC_SEQ), *([us] * DEC_SEQ),
      ckt, cvt, cu, jnp.asarray(bias_c), jnp.asarray(bias_n))


def _back_kernel(h_ref, pa_ref, pools_ref, attns_ref,
                 win_hbm, wgrp_hbm, scale_ref, wpo_hbm, wao_hbm, wout_hbm, g2_ref, b2_ref,
                 w1_hbm, w3_hbm, w2_hbm, g3_ref, b3_ref, yp_ref, ys_ref, z2_c, z3_c, h_s,
                 wg_ref, wgrp_ref, wpo_ref, wao_ref, wout_ref, w1_ref, w3_ref, w2_ref,
                 buf_ff, buf_group, sem_ff, sem_group, sem_model):
    t = pl.program_id(0)

    def branch_outputs():
        is_sample = t >= PROMPT_TILES
        pool_in = jnp.where(is_sample, pools_ref[...].astype(BF16), pa_ref[:, 0:POOL_WIDTH])
        attn_o = jnp.where(is_sample, attns_ref[...].astype(BF16), pa_ref[:, POOL_WIDTH:POOL_WIDTH + Q_WIDTH])
        zs = [_dot(pool_in[:, g * POOL_GROUP:(g + 1) * POOL_GROUP],
                   wgrp_ref[g * POOL_GROUP:(g + 1) * POOL_GROUP, :])
              for g in range(len(POOL_WINDOWS))]
        pool_z = jnp.concatenate(zs, axis=1) * scale_ref[...]
        return _dot(pool_z.astype(BF16), wpo_ref[...]), _dot(attn_o, wao_ref[...])

    def gate_and_merge(h1, a, b):
        gates = jax.nn.sigmoid(_dot(h1.astype(BF16), wg_ref[...]))
        return (gates[:, :D_MODEL] * a + gates[:, D_MODEL:] * b).astype(BF16)

    def project_out(h1, merged):
        z2_c[...] = ALPHA * h1 + _dot(merged, wout_ref[...])

    @pl.when(jnp.logical_and(t >= 1, t <= ROW_TILES))
    def _():
        h1 = h_ref[...]
        a, b = branch_outputs()
        yp_ref[...] = _layer_norm(z3_c[...], g3_ref[...], b3_ref[...])
        h2 = _layer_norm(z2_c[...], g2_ref[...], b2_ref[...])
        merged = []
        side = {1: lambda: merged.append(gate_and_merge(h1, a, b)), 9: lambda: project_out(h1, merged[0])}
        _swiglu_residual(h2, w1_ref, w3_ref, w2_ref, h_s, side, z3_c)

    @pl.when(t == ROW_TILES + 1)
    def _():
        ys_ref[...] = _layer_norm(z3_c[...], g3_ref[...], b3_ref[...])

    @pl.when(t == 0)
    def _():
        ring_ff = _ring(buf_ff, sem_ff)
        ring_group = _ring(buf_group, sem_group)
        ring_model = _ring(z3_c, sem_model, rows=MODEL_RING_ROWS)
        mixer = (_row_chunks(wgrp_hbm, 0, wgrp_ref, ring_group)
                 + _row_chunks(wpo_hbm, 0, wpo_ref, ring_model)
                 + _row_chunks(wao_hbm, 0, wao_ref, ring_model)
                 + _row_chunks(win_hbm, UQKV_WIDTH, wg_ref.at[:, 0:D_MODEL], ring_model)
                 + _row_chunks(win_hbm, UQKV_WIDTH + D_MODEL, wg_ref.at[:, D_MODEL:2 * D_MODEL], ring_model)
                 + _row_chunks(wout_hbm, 0, wout_ref, ring_model))
        swiglu = (_row_chunks(w1_hbm, 0, w1_ref, ring_ff) + _row_chunks(w3_hbm, 0, w3_ref, ring_ff)
                  + _row_chunks(w2_hbm, 0, w2_ref, ring_model))
        stager = _WeightStager(mixer + swiglu)
        stager.run(len(mixer))
        h1 = h_ref[...]
        a, b = branch_outputs()
        stager.run(len(swiglu) // 3)
        merged = gate_and_merge(h1, a, b)
        stager.run(len(swiglu) // 3)
        project_out(h1, merged)
        stager.run()
        z3_c[...] = jnp.zeros(z3_c.shape, z3_c.dtype)


def _back(h1, pool_attn, pool_s, attn_s, w_in, wgrp, scale, wpo, wao, wout, g2, b2, w1, w3, w2, g3, b3):
    lagged = pl.BlockSpec((ROW_TILE, D_MODEL), lambda t: (jnp.clip(t - 2, 0, PROMPT_TILES - 1), 0))
    hbm = pl.BlockSpec(memory_space=pl.ANY)
    return pl.pallas_call(
        _back_kernel,
        out_shape=(jax.ShapeDtypeStruct((PROMPT_ROWS, D_MODEL), F32),
                   jax.ShapeDtypeStruct((SAMPLE_ROWS, D_MODEL), F32)),
        grid=(ROW_TILES + 2,),
        in_specs=[pl.BlockSpec((ROW_TILE, D_MODEL), lambda t: (jnp.minimum(t, PROMPT_TILES), 0)),
                  pl.BlockSpec((ROW_TILE, POOL_WIDTH + Q_WIDTH), lambda t: (jnp.minimum(t, PROMPT_TILES - 1), 0)),
                  _resident((SAMPLE_ROWS, POOL_WIDTH)), _resident((SAMPLE_ROWS, Q_WIDTH)),
                  hbm, hbm, _resident((1, POOL_WIDTH)), hbm, hbm, hbm,
                  _resident((1, D_MODEL)), _resident((1, D_MODEL)),
                  hbm, hbm, hbm, _resident((1, D_MODEL)), _resident((1, D_MODEL))],
        out_specs=(lagged, pl.BlockSpec((ROW_TILE, D_MODEL), lambda r: (0, 0))),
        scratch_shapes=[pltpu.VMEM((ROW_TILE, D_MODEL), F32), pltpu.VMEM((ROW_TILE, D_MODEL), F32),
                        pltpu.VMEM((ROW_TILE, D_FF), BF16),
                        pltpu.VMEM((D_MODEL, 2 * D_MODEL), BF16),
                        pltpu.VMEM((len(POOL_WINDOWS) * POOL_GROUP, POOL_GROUP), BF16),
                        pltpu.VMEM((POOL_WIDTH, D_MODEL), BF16), pltpu.VMEM((Q_WIDTH, D_MODEL), BF16),
                        pltpu.VMEM((D_MODEL, D_MODEL), BF16),
                        pltpu.VMEM((D_MODEL, D_FF), BF16), pltpu.VMEM((D_MODEL, D_FF), BF16),
                        pltpu.VMEM((D_FF, D_MODEL), BF16)]
                       + _ring_scratch(RING_FF, RING_GROUP),
        compiler_params=pltpu.CompilerParams(dimension_semantics=("arbitrary",),
                                             vmem_limit_bytes=VMEM_LIMIT_BYTES),
        name="back",
    )(h1, pool_attn, pool_s, attn_s, w_in, wgrp, scale, wpo, wao, wout, g2, b2, w1, w3, w2, g3, b3)


def kernel(x_prompt, x_sample, cache_pool_u, cache_k_win, cache_v_win, w_in, pool_w_grp, pool_scale,
           attn_sinks, w_pool_out, w_attn_out, w_out, ffn1_w1, ffn1_w3, ffn1_w2, ffn2_w1, ffn2_w3,
           ffn2_w2, ln1_g, ln1_b, ln2_g, ln2_b, ln3_g, ln3_b):
    assert DEPTH == 1 and w_in.shape[0] == 1
    l = 0
    vec = lambda p: p[l].reshape(1, -1)
    sinks = attn_sinks[l]

    freq = jnp.tile(ROPE_THETA ** (-2.0 * jnp.arange(HEAD_DIM // 2, dtype=F32) / HEAD_DIM), 4).reshape(1, LANES)

    xp = x_prompt.reshape(PROMPT_ROWS, D_MODEL)
    xs = jnp.transpose(x_sample, (1, 0, 2)).reshape(SAMPLE_ROWS, D_MODEL)
    (h1, pool_attn, us, qs, ks, vs, kt_last, vt_last, u_last) = _front(
        xp, xs, ffn1_w1, ffn1_w3, ffn1_w2, w_in, vec(ln1_g), vec(ln1_b), freq, sinks)

    to_t = lambda c: jnp.transpose(c[l], (0, 2, 3, 1)).reshape(DEC_BATCH, KV_WIDTH, WINDOW)
    cu = jnp.transpose(cache_pool_u[l], (1, 0, 2))
    attn_s, pool_s, kt_s, vt_s, pu_s = _sample_ctx(us, qs, ks, vs, to_t(cache_k_win), to_t(cache_v_win), cu, sinks)

    wgrp = pool_w_grp.reshape(DEPTH, len(POOL_WINDOWS) * POOL_GROUP, POOL_GROUP)
    yp, ys = _back(h1, pool_attn, pool_s.reshape(SAMPLE_ROWS, POOL_WIDTH), attn_s.reshape(SAMPLE_ROWS, Q_WIDTH),
                   w_in, wgrp, vec(pool_scale), w_pool_out, w_attn_out, w_out, vec(ln2_g), vec(ln2_b),
                   ffn2_w1, ffn2_w3, ffn2_w2, vec(ln3_g), vec(ln3_b))
    yp = yp.reshape(BATCH, SEQ, D_MODEL)
    ys = jnp.transpose(ys.reshape(DEC_SEQ, DEC_BATCH, D_MODEL), (1, 0, 2))

    from_t = lambda c, n: jnp.transpose(c.reshape(n, N_KV_HEADS, HEAD_DIM, WINDOW), (0, 3, 1, 2))[None]
    pool_u_prompt = u_last[None, :, POOL_PAD - POOL_BUF:]
    pool_u_sample = jnp.transpose(pu_s, (1, 0, 2))[None]
    return (yp, ys, pool_u_prompt, from_t(kt_last, BATCH), from_t(vt_last, BATCH),
            pool_u_sample, from_t(kt_s, DEC_BATCH), from_t(vt_s, DEC_BATCH))
```

```python
import jax
import jax.numpy as jnp
import numpy as np
from jax import lax
from jax.experimental import pallas as pl
from jax.experimental.pallas import tpu as pltpu

D_MODEL = 1024
BATCH = 8
SEQ = 2048
DEC_BATCH = 128
DEC_SEQ = 4
PAST_LEN = 8192
POOL_WINDOWS = (2, 4, 8, 16)
POOL_GROUP = 128
POOL_WIDTH = 512
POOL_BUF = 15
N_HEADS = 8
N_KV_HEADS = 2
HEADS_PER_KV = N_HEADS // N_KV_HEADS
HEAD_DIM = 64
Q_WIDTH = 512
KV_WIDTH = 128
WINDOW = 128
ROPE_THETA = 10000.0
D_FF = 2816
DEPTH = 1
ALPHA = (2.0 * DEPTH) ** 0.25
LN_EPS = 1e-5
NEG_INF = -1e30
UQKV_WIDTH = POOL_WIDTH + Q_WIDTH + 2 * KV_WIDTH

LANES = 128
KEY_SPAN = 2 * WINDOW
VMEM_LIMIT_BYTES = 61 * 1024 * 1024

ROW_TILE = 512
TILES_PER_SEQ = SEQ // ROW_TILE
PROMPT_ROWS = BATCH * SEQ
PROMPT_TILES = PROMPT_ROWS // ROW_TILE
SAMPLE_ROWS = DEC_BATCH * DEC_SEQ
ROW_TILES = PROMPT_TILES + 1
POOL_PAD = 16
SEQ_GROUP = 32
SUB_GROUP = 8
FF_CHUNK = 256
OUT_BLOCK = 256

BF16 = jnp.bfloat16
F32 = jnp.float32


def _dot(a, b):
    return jnp.dot(a, b, preferred_element_type=F32)


def _dot_nt(a, b):
    return lax.dot_general(a, b, (((1,), (1,)), ((), ())), preferred_element_type=F32)


def _layer_norm(y, g, b):
    mu = jnp.mean(y, axis=-1, keepdims=True)
    yc = y - mu
    var = jnp.mean(yc * yc, axis=-1, keepdims=True)
    return yc * lax.rsqrt(var + LN_EPS) * g + b


def _resident(shape):
    nd = len(shape)
    return pl.BlockSpec(shape, lambda *_: (0,) * nd, pipeline_mode=pl.Buffered(1))


class _WeightStager:
    def __init__(self, jobs):
        uses, self.staged = {}, []
        for src, ring, dst in jobs:
            slot, sem = ring[uses.get(id(ring), 0) % len(ring)]
            uses[id(ring)] = uses.get(id(ring), 0) + 1
            self.staged.append((pltpu.make_async_copy(src, slot, sem), slot, dst))
        assert all(len(ring) >= min(STAGE_IN_FLIGHT, uses[id(ring)]) for _, ring, _ in jobs)
        self.done = 0
        for copy, _, _ in self.staged[:STAGE_IN_FLIGHT]:
            copy.start()

    def run(self, count=None):
        end = len(self.staged) if count is None else min(self.done + count, len(self.staged))
        for i in range(self.done, end):
            copy, slot, dst = self.staged[i]
            copy.wait()
            dst[...] = slot[...].astype(BF16)
            if i + STAGE_IN_FLIGHT < len(self.staged):
                self.staged[i + STAGE_IN_FLIGHT][0].start()
        self.done = end


def _ring(buf, sems, rows=None):
    if rows is None:
        return [(buf.at[i], sems.at[i]) for i in range(buf.shape[0])]
    return [(buf.at[pl.ds(i * rows, rows), :], sems.at[i]) for i in range(buf.shape[0] // rows)]


def _row_chunks(w_hbm, col0, dst, ring):
    rows, cols = dst.shape
    step, width = ring[0][0].shape
    assert rows % step == 0 and width == cols
    return [(w_hbm.at[0, pl.ds(r0, step), pl.ds(col0, cols)], ring, dst.at[pl.ds(r0, step), :])
            for r0 in range(0, rows, step)]


STAGE_IN_FLIGHT = 4
RING_FF = (STAGE_IN_FLIGHT, 64, D_FF)
MODEL_RING_ROWS = ROW_TILE // STAGE_IN_FLIGHT
RING_NARROW = (STAGE_IN_FLIGHT, 128, UQKV_WIDTH - D_MODEL)
RING_GROUP = (1, len(POOL_WINDOWS) * POOL_GROUP, POOL_GROUP)


def _ring_scratch(*rings):
    return ([pltpu.VMEM(ring, F32) for ring in rings]
            + [pltpu.SemaphoreType.DMA((ring[0],)) for ring in rings]
            + [pltpu.SemaphoreType.DMA((STAGE_IN_FLIGHT,))])


FF_SLOTS = 2 * (D_FF // FF_CHUNK)


def _swiglu_residual(x, w1_ref, w3_ref, w2_ref, h_s, side_work, out_ref):
    assert all(0 <= slot <= FF_SLOTS for slot in side_work)
    run = lambda slot: side_work.get(slot, lambda: None)()
    xb = x.astype(BF16)
    for j in range(D_FF // FF_CHUNK):
        cols = slice(j * FF_CHUNK, (j + 1) * FF_CHUNK)
        a = _dot(xb, w1_ref[:, cols])
        run(2 * j)
        b = _dot(xb, w3_ref[:, cols])
        h_s[:, cols] = ((a * jax.nn.sigmoid(a)) * b).astype(BF16)
        run(2 * j + 1)
    run(FF_SLOTS)
    h = h_s[...]
    for c0 in range(0, D_MODEL, OUT_BLOCK):
        cols = slice(c0, c0 + OUT_BLOCK)
        out_ref[:, cols] = ALPHA * x[:, cols] + 0.5 * _dot(h, w2_ref[:, cols])


def _rope(x, cos, sin_signed, first_half):
    fwd = pltpu.roll(x, LANES - HEAD_DIM // 2, axis=1)
    bwd = pltpu.roll(x, HEAD_DIM // 2, axis=1)
    return x * cos + jnp.where(first_half, fwd, bwd) * sin_signed


def _swap_halves(x):
    return pltpu.roll(x, HEAD_DIM, axis=1)


def _swap_halves_wide(x):
    return jnp.concatenate([_swap_halves(x[:, c:c + LANES]) for c in range(0, x.shape[1], LANES)], axis=1)


def _lane_split(x, x_sw, kh):
    low = lax.broadcasted_iota(jnp.int32, (1, LANES), 1) < HEAD_DIM
    lo, hi = (x, x_sw) if kh == 0 else (x_sw, x)
    return jnp.concatenate([jnp.where(low, lo, 0.0), jnp.where(low, 0.0, hi)], axis=0).astype(BF16)


def _sink_softmax(q_pairs, keys, kh, bias, sinks_ref):
    s = _dot_nt(q_pairs, _lane_split(keys, _swap_halves(keys), kh)) + bias
    second_pair = lax.broadcasted_iota(jnp.int32, (s.shape[0], 1), 0) >= WINDOW
    probs, denoms = [], []
    for c in range(2):
        sc = s[:, c * KEY_SPAN:(c + 1) * KEY_SPAN]
        head = kh * HEADS_PER_KV + c
        sink = jnp.where(second_pair, sinks_ref[head + 2], sinks_ref[head])
        m = jnp.maximum(jnp.max(sc, axis=-1, keepdims=True), sink)
        p = jnp.exp(sc - m)
        denoms.append(jnp.sum(p, axis=-1, keepdims=True) + jnp.exp(sink - m))
        probs.append(p.astype(BF16))
    return jnp.concatenate(probs, axis=1), denoms


def _weighted_values(probs, denoms, vals, kh):
    low = lax.broadcasted_iota(jnp.int32, (1, LANES), 1) < HEAD_DIM
    o = _dot(probs, _lane_split(vals, _swap_halves(vals), kh))
    return o / jnp.where(low, denoms[0], denoms[1])


def _prompt_context_work(tile, q_c, k_c, v_c, u_c, bias_ref, sinks_ref,
                         pool_ref, attn_ref, kt_ref, vt_ref, ulast_ref):
    seq_tile = (tile + TILES_PER_SEQ) % TILES_PER_SEQ
    first_tile = seq_tile == 0
    softmaxed = {}

    def scores(unit, blk, kh):
        r0 = blk * WINDOW
        bias = bias_ref[jnp.where(first_tile, 1, 0)] if blk == 0 else bias_ref[0]
        c0 = 2 * kh * LANES
        q_pairs = jnp.concatenate([q_c[r0:r0 + WINDOW, c0:c0 + LANES],
                                   q_c[r0:r0 + WINDOW, c0 + LANES:c0 + 2 * LANES]], axis=0)
        softmaxed[unit] = _sink_softmax(q_pairs, k_c[r0:r0 + KEY_SPAN, :], kh, bias, sinks_ref)

    def values(unit, blk, kh):
        r0 = blk * WINDOW
        c0 = 2 * kh * LANES
        o = _weighted_values(*softmaxed.pop(unit), v_c[r0:r0 + KEY_SPAN, :], kh)
        attn_ref[r0:r0 + WINDOW, c0:c0 + LANES] = o[:WINDOW].astype(BF16)
        attn_ref[r0:r0 + WINDOW, c0 + LANES:c0 + 2 * LANES] = o[WINDOW:].astype(BF16)

    def pool(groups):
        pos = seq_tile * ROW_TILE + lax.broadcasted_iota(jnp.int32, (ROW_TILE, 1), 0)
        for g in groups:
            w = POOL_WINDOWS[g]
            cols = slice(g * POOL_GROUP, (g + 1) * POOL_GROUP)
            cur = u_c[POOL_PAD:POOL_PAD + ROW_TILE, cols]
            acc = cur
            for j in range(1, w):
                acc = acc + u_c[POOL_PAD - j:POOL_PAD - j + ROW_TILE, cols]
            cnt = jnp.minimum(pos + 1, w).astype(F32)
            pool_ref[:, cols] = (acc / cnt - cur).astype(BF16)

    def sequence_state():
        kt_ref[0] = k_c[ROW_TILE:ROW_TILE + WINDOW, :].T
        vt_ref[0] = v_c[ROW_TILE:ROW_TILE + WINDOW, :].T
        ulast_ref[0] = u_c[ROW_TILE:ROW_TILE + POOL_PAD, :]

    work = {"pool_wide": lambda: pool((3,)), "pool_narrow": lambda: pool((0, 1, 2)),
            "sequence_state": sequence_state}
    for blk in range(ROW_TILE // WINDOW):
        for kh in range(N_KV_HEADS):
            unit = blk * N_KV_HEADS + kh
            work["scores", unit] = lambda unit=unit, blk=blk, kh=kh: scores(unit, blk, kh)
            work["values", unit] = lambda unit=unit, blk=blk, kh=kh: values(unit, blk, kh)
    return work


def _prompt_bias():
    r = np.arange(2 * WINDOW)[:, None] % WINDOW
    c = np.arange(2 * KEY_SPAN)[None, :] % KEY_SPAN
    valid = (r <= c) & (c <= r + WINDOW)
    first = valid & (c >= WINDOW)
    return np.where(np.stack([valid, first]), 0.0, NEG_INF).astype(np.float32)


def _front_kernel(sinks_ref, xp_ref, xs_ref, w1_hbm, w3_hbm, w2_hbm, win_hbm, g_ref, b_ref, freq_ref, bias_ref,
                  h_ref, pa_ref, us_ref, qs_ref, ks_ref, vs_ref, kt_ref, vt_ref, ulast_ref,
                  z_c, q_c, k_c, v_c, u_c, h_s, rope_ref, w1_ref, w3_ref, w2_ref, wu_ref,
                  buf_ff, buf_narrow, sem_ff, sem_narrow, sem_model):
    r = pl.program_id(0) - 1
    prev = r - 1
    pool_ref = pa_ref.at[:, 0:POOL_WIDTH]
    attn_ref = pa_ref.at[:, POOL_WIDTH:POOL_WIDTH + Q_WIDTH]

    def stage_and_reset():
        ring_ff = _ring(buf_ff, sem_ff)
        ring_narrow = _ring(buf_narrow, sem_narrow)
        ring_model = _ring(z_c, sem_model, rows=MODEL_RING_ROWS)
        stager = _WeightStager(_row_chunks(w1_hbm, 0, w1_ref, ring_ff)
                               + _row_chunks(w3_hbm, 0, w3_ref, ring_ff)
                               + _row_chunks(win_hbm, 0, wu_ref.at[:, 0:D_MODEL], ring_model)
                               + _row_chunks(win_hbm, D_MODEL, wu_ref.at[:, D_MODEL:UQKV_WIDTH], ring_narrow)
                               + _row_chunks(w2_hbm, 0, w2_ref, ring_model))
        freq = freq_ref[...]
        lane = lax.broadcasted_iota(jnp.int32, (1, LANES), 1)
        sign = jnp.where((lane & (HEAD_DIM // 2)) == 0, -1.0, 1.0)
        row = lax.broadcasted_iota(jnp.int32, (ROW_TILE, 1), 0)
        per_tile = -(-len(stager.staged) // (TILES_PER_SEQ + 1))
        for tile in range(TILES_PER_SEQ + 1):
            pos = tile * ROW_TILE + row if tile < TILES_PER_SEQ else PAST_LEN + row // DEC_BATCH
            ang = pos.astype(F32) * freq
            rows = slice(tile * ROW_TILE, (tile + 1) * ROW_TILE)
            rope_ref[rows, 0:LANES] = jnp.cos(ang)
            rope_ref[rows, LANES:2 * LANES] = jnp.sin(ang) * sign
            stager.run(per_tile)
        stager.run()
        for ref in (z_c, q_c, k_c, v_c, u_c):
            ref[...] = jnp.zeros(ref.shape, ref.dtype)

    def norm_previous():
        h1 = _layer_norm(z_c[...], g_ref[...], b_ref[...])
        h_ref[...] = h1
        return h1.astype(BF16)

    def rope_tables():
        tile = jnp.clip(prev, 0, PROMPT_TILES)
        table = jnp.where(tile == PROMPT_TILES, TILES_PER_SEQ, tile % TILES_PER_SEQ)
        rows = pl.ds(pl.multiple_of(table * ROW_TILE, ROW_TILE), ROW_TILE)
        cos = rope_ref[rows, 0:LANES]
        lane = lax.broadcasted_iota(jnp.int32, cos.shape, 1)
        return cos, rope_ref[rows, LANES:2 * LANES], (lane & (HEAD_DIM // 2)) == 0

    def project_u(h1b):
        return _dot(h1b, wu_ref[:, 0:POOL_WIDTH])

    def project_q(h1b):
        z = _dot(h1b, wu_ref[:, POOL_WIDTH:POOL_WIDTH + Q_WIDTH])
        tables = rope_tables()
        return [(_rope(z[:, c:c + LANES], *tables) * (HEAD_DIM ** -0.5)).astype(BF16)
                for c in range(0, Q_WIDTH, LANES)]

    def project_kv(h1b):
        z = _dot(h1b, wu_ref[:, POOL_WIDTH + Q_WIDTH:UQKV_WIDTH])
        return _rope(z[:, :KV_WIDTH], *rope_tables()), z[:, KV_WIDTH:]

    def tile_step(x):
        normed = []

        def norm():
            normed.append(norm_previous())

        projected = []

        def project():
            projected.extend([project_u(normed[0]), project_q(normed[0]), *project_kv(normed[0])])

        def carry():
            u, q, k, v = projected
            starts_seq = (prev + TILES_PER_SEQ) % TILES_PER_SEQ == 0
            u_c[0:POOL_PAD, :] = jnp.where(starts_seq, 0.0, u_c[ROW_TILE:ROW_TILE + POOL_PAD, :])
            u_c[POOL_PAD:POOL_PAD + ROW_TILE, :] = u
            for c, qc in enumerate(q):
                q_c[:, c * LANES:(c + 1) * LANES] = qc
            k_c[0:WINDOW, :] = k_c[ROW_TILE:ROW_TILE + WINDOW, :]
            v_c[0:WINDOW, :] = v_c[ROW_TILE:ROW_TILE + WINDOW, :]
            k_c[WINDOW:WINDOW + ROW_TILE, :] = k
            v_c[WINDOW:WINDOW + ROW_TILE, :] = v

        side = {0: norm, FF_SLOTS - 3: project, FF_SLOTS - 1: carry}
        stages = context_stages()
        assert len(stages) < FF_SLOTS - 3
        side.update(enumerate(stages, start=1))
        _swiglu_residual(x, w1_ref, w3_ref, w2_ref, h_s, side, z_c)

    def context_stages():
        context = _prompt_context_work(r - 2, q_c, k_c, v_c, u_c, bias_ref, sinks_ref,
                                       pool_ref, attn_ref, kt_ref, vt_ref, ulast_ref)
        stages = [("scores", 0)]
        for unit in range(1, 8):
            stages += [("scores", unit), ("values", unit - 1)]
        stages.append(("values", 7))

        def pooling_and_state():
            context["pool_wide"]()
            context["pool_narrow"]()
            context["sequence_state"]()

        return [context[stage] for stage in stages] + [pooling_and_state]

    @pl.when(jnp.logical_and(r >= 0, r <= PROMPT_TILES))
    def _():
        tile_step(jnp.where(r == PROMPT_TILES, xs_ref[...], xp_ref[...]))

    @pl.when(r == ROW_TILES)
    def _():
        for work in context_stages():
            work()
        h1b = norm_previous()
        us_ref[...] = project_u(h1b)
        for c, qc in enumerate(project_q(h1b)):
            qs_ref[:, c * LANES:(c + 1) * LANES] = qc
        ks_ref[...], vs_ref[...] = project_kv(h1b)

    pl.when(r < 0)(stage_and_reset)


def _front(xp, xs, w1, w3, w2, w_in, g, b, freq, sinks):
    rows = ROW_TILES * ROW_TILE

    def lagged(width):
        return pl.BlockSpec((ROW_TILE, width), lambda s: (jnp.clip(s - 3, 0, PROMPT_TILES - 1), 0))

    def seq_of_lagged(shape):
        return pl.BlockSpec(shape, lambda s: (jnp.clip(s - 3, 0, PROMPT_TILES - 1) // TILES_PER_SEQ, 0, 0))

    sample = lambda width: pl.BlockSpec((ROW_TILE, width), lambda r: (0, 0))
    hbm = pl.BlockSpec(memory_space=pl.ANY)
    return pl.pallas_call(
        _front_kernel,
        out_shape=(jax.ShapeDtypeStruct((rows, D_MODEL), F32),
                   jax.ShapeDtypeStruct((PROMPT_ROWS, POOL_WIDTH + Q_WIDTH), BF16),
                   jax.ShapeDtypeStruct((SAMPLE_ROWS, POOL_WIDTH), F32),
                   jax.ShapeDtypeStruct((SAMPLE_ROWS, Q_WIDTH), BF16),
                   jax.ShapeDtypeStruct((SAMPLE_ROWS, KV_WIDTH), F32),
                   jax.ShapeDtypeStruct((SAMPLE_ROWS, KV_WIDTH), F32),
                   jax.ShapeDtypeStruct((BATCH, KV_WIDTH, WINDOW), F32),
                   jax.ShapeDtypeStruct((BATCH, KV_WIDTH, WINDOW), F32),
                   jax.ShapeDtypeStruct((BATCH, POOL_PAD, POOL_WIDTH), F32)),
        grid=(ROW_TILES + 2,),
        in_specs=[pl.BlockSpec(memory_space=pltpu.SMEM),
                  pl.BlockSpec((ROW_TILE, D_MODEL), lambda s: (jnp.clip(s - 1, 0, PROMPT_TILES - 1), 0)),
                  _resident((SAMPLE_ROWS, D_MODEL)),
                  hbm, hbm, hbm, hbm, _resident((1, D_MODEL)), _resident((1, D_MODEL)),
                  _resident((1, LANES)), _resident((2, 2 * WINDOW, 2 * KEY_SPAN))],
        out_specs=(pl.BlockSpec((ROW_TILE, D_MODEL), lambda s: (jnp.clip(s - 2, 0, PROMPT_TILES), 0)),
                   lagged(POOL_WIDTH + Q_WIDTH),
                   sample(POOL_WIDTH), sample(Q_WIDTH), sample(KV_WIDTH), sample(KV_WIDTH),
                   seq_of_lagged((1, KV_WIDTH, WINDOW)), seq_of_lagged((1, KV_WIDTH, WINDOW)),
                   seq_of_lagged((1, POOL_PAD, POOL_WIDTH))),
        scratch_shapes=[pltpu.VMEM((ROW_TILE, D_MODEL), F32),
                        pltpu.VMEM((ROW_TILE, Q_WIDTH), BF16),
                        pltpu.VMEM((WINDOW + ROW_TILE, KV_WIDTH), F32),
                        pltpu.VMEM((WINDOW + ROW_TILE, KV_WIDTH), F32),
                        pltpu.VMEM((POOL_PAD + ROW_TILE, POOL_WIDTH), F32),
                        pltpu.VMEM((ROW_TILE, D_FF), BF16),
                        pltpu.VMEM(((TILES_PER_SEQ + 1) * ROW_TILE, 2 * LANES), F32),
                        pltpu.VMEM((D_MODEL, D_FF), BF16), pltpu.VMEM((D_MODEL, D_FF), BF16),
                        pltpu.VMEM((D_FF, D_MODEL), BF16), pltpu.VMEM((D_MODEL, UQKV_WIDTH), BF16)]
                       + _ring_scratch(RING_FF, RING_NARROW),
        compiler_params=pltpu.CompilerParams(dimension_semantics=("arbitrary",),
                                             vmem_limit_bytes=VMEM_LIMIT_BYTES),
        name="front",
    )(sinks, xp, xs, w1, w3, w2, w_in, g, b, freq, jnp.asarray(_prompt_bias()))


def _sample_bias():
    row = np.arange(HEADS_PER_KV * DEC_SEQ * SUB_GROUP)
    row_t = (row // SUB_GROUP) % DEC_SEQ
    row_b = row % SUB_GROUP
    col = np.arange(SUB_GROUP * WINDOW)
    ok_c = (col[None, :] // WINDOW == row_b[:, None]) & (col[None, :] % WINDOW >= row_t[:, None])
    new = np.arange(DEC_SEQ * SUB_GROUP)
    ok_n = (new[None, :] % SUB_GROUP == row_b[:, None]) & (new[None, :] // SUB_GROUP <= row_t[:, None])
    to_bias = lambda ok: np.where(ok, 0.0, NEG_INF).astype(np.float32)
    return to_bias(ok_c), to_bias(ok_n)


def _sample_ctx_kernel(sinks_ref, q0, q1, q2, q3, k0, k1, k2, k3, v0, v1, v2, v3, u0, u1, u2, u3,
                       ckt_ref, cvt_ref, cu_ref, bias_c_ref, bias_n_ref,
                       attn_ref, pool_ref, kt_out, vt_out, pu_out, kbt_s, vbt_s):
    q_t = [q[...].astype(F32) for q in (q0, q1, q2, q3)]
    k_t = [k[...] for k in (k0, k1, k2, k3)]
    v_t = [v[...] for v in (v0, v1, v2, v3)]

    rows_u = [cu_ref[i] for i in range(POOL_BUF)] + [u[...] for u in (u0, u1, u2, u3)]
    for t in range(DEC_SEQ):
        pooled = []
        for g, w in enumerate(POOL_WINDOWS):
            cols = slice(g * POOL_GROUP, (g + 1) * POOL_GROUP)
            cur = rows_u[POOL_BUF + t][:, cols]
            acc = cur
            for j in range(1, w):
                acc = acc + rows_u[POOL_BUF + t - j][:, cols]
            pooled.append(acc / float(w) - cur)
        pool_ref[t] = jnp.concatenate(pooled, axis=1)
    for i in range(POOL_BUF):
        pu_out[i] = rows_u[i + DEC_SEQ]

    bias_c = bias_c_ref[...]
    bias_n = bias_n_ref[...]
    head_of_row = lax.broadcasted_iota(jnp.int32, (bias_c.shape[0], 1), 0) // (DEC_SEQ * SUB_GROUP)
    low = lax.broadcasted_iota(jnp.int32, (1, LANES), 1) < HEAD_DIM
    q_sw = [_swap_halves_wide(q) for q in q_t]
    k_sw = [_swap_halves(k) for k in k_t]
    v_sw = [_swap_halves(v) for v in v_t]
    for sub in range(SEQ_GROUP // SUB_GROUP):
        rows = slice(sub * SUB_GROUP, (sub + 1) * SUB_GROUP)
        for kh in range(N_KV_HEADS):
            def q_piece(t, head):
                src = q_t[t] if head % 2 == 0 else q_sw[t]
                chunk = head // 2
                return jnp.where(low, src[rows, chunk * LANES:(chunk + 1) * LANES], 0.0)

            def kv_first(c):
                return c if kh == 0 else jnp.concatenate([c[HEAD_DIM:], c[:HEAD_DIM]], axis=0)

            def kv_twice(c):
                part = c[kh * HEAD_DIM:(kh + 1) * HEAD_DIM]
                return jnp.concatenate([part, part], axis=0)

            lhs = jnp.concatenate([q_piece(t, kh * HEADS_PER_KV + g)
                                   for g in range(HEADS_PER_KV) for t in range(DEC_SEQ)], axis=0).astype(BF16)
            kcat = jnp.concatenate([kv_first(ckt_ref[sub * SUB_GROUP + b]) for b in range(SUB_GROUP)],
                                   axis=1).astype(BF16)
            vcat = jnp.concatenate([kv_twice(cvt_ref[sub * SUB_GROUP + b]) for b in range(SUB_GROUP)],
                                   axis=1).astype(BF16)
            knew = jnp.concatenate([(k_t[t] if kh == 0 else k_sw[t])[rows] for t in range(DEC_SEQ)],
                                   axis=0).astype(BF16)
            vnew = jnp.concatenate([(jnp.where(low, v_t[t], v_sw[t]) if kh == 0 else
                                     jnp.where(low, v_sw[t], v_t[t]))[rows] for t in range(DEC_SEQ)],
                                   axis=0).astype(BF16)
            s_c = _dot(lhs, kcat) + bias_c
            s_n = _dot_nt(lhs, knew) + bias_n
            sink = jnp.zeros(head_of_row.shape, F32)
            for g in range(HEADS_PER_KV):
                sink = jnp.where(head_of_row == g, sinks_ref[kh * HEADS_PER_KV + g], sink)
            m = jnp.maximum(jnp.maximum(jnp.max(s_c, axis=-1, keepdims=True),
                                        jnp.max(s_n, axis=-1, keepdims=True)), sink)
            p_c = jnp.exp(s_c - m)
            p_n = jnp.exp(s_n - m)
            denom = (jnp.sum(p_c, axis=-1, keepdims=True) + jnp.sum(p_n, axis=-1, keepdims=True)
                     + jnp.exp(sink - m))
            o = (_dot_nt(p_c.astype(BF16), vcat) + _dot(p_n.astype(BF16), vnew)) / denom
            for t in range(DEC_SEQ):
                for pair in range(HEADS_PER_KV // 2):
                    piece = lambda g: o[(g * DEC_SEQ + t) * SUB_GROUP:(g * DEC_SEQ + t + 1) * SUB_GROUP]
                    c0 = (kh * HEADS_PER_KV // 2 + pair) * LANES
                    attn_ref[t, rows, c0:c0 + LANES] = jnp.where(low, piece(2 * pair), piece(2 * pair + 1))

    if DEC_SEQ * SEQ_GROUP < LANES:
        zeros = jnp.zeros((LANES - DEC_SEQ * SEQ_GROUP, KV_WIDTH), F32)
        kbt_s[DEC_SEQ * SEQ_GROUP:, :] = zeros
        vbt_s[DEC_SEQ * SEQ_GROUP:, :] = zeros
    for t in range(DEC_SEQ):
        kbt_s[pl.ds(t, SEQ_GROUP, stride=DEC_SEQ), :] = k_t[t]
        vbt_s[pl.ds(t, SEQ_GROUP, stride=DEC_SEQ), :] = v_t[t]
    knew_t = kbt_s[...].T
    vnew_t = vbt_s[...].T
    keep = lax.broadcasted_iota(jnp.int32, (1, WINDOW), 1) < WINDOW - DEC_SEQ
    for b in range(SEQ_GROUP):
        shift_new = WINDOW - DEC_SEQ - DEC_SEQ * b
        kt_out[b] = jnp.where(keep, pltpu.roll(ckt_ref[b], WINDOW - DEC_SEQ, axis=1),
                              pltpu.roll(knew_t, shift_new, axis=1))
        vt_out[b] = jnp.where(keep, pltpu.roll(cvt_ref[b], WINDOW - DEC_SEQ, axis=1),
                              pltpu.roll(vnew_t, shift_new, axis=1))


def _sample_ctx(us, qs, ks, vs, ckt, cvt, cu, sinks):
    groups = DEC_BATCH // SEQ_GROUP

    def token_rows(t, width):
        return pl.BlockSpec((SEQ_GROUP, width), lambda i: (t * groups + i, 0))

    def per_token(width):
        return [token_rows(t, width) for t in range(DEC_SEQ)]

    cache_spec = pl.BlockSpec((SEQ_GROUP, KV_WIDTH, WINDOW), lambda i: (i, 0, 0))
    pool_rows_spec = pl.BlockSpec((POOL_BUF, SEQ_GROUP, POOL_WIDTH), lambda i: (0, i, 0))
    by_token = lambda width: pl.BlockSpec((DEC_SEQ, SEQ_GROUP, width), lambda i: (0, i, 0))
    bias_c, bias_n = _sample_bias()
    return pl.pallas_call(
        _sample_ctx_kernel,
        out_shape=(jax.ShapeDtypeStruct((DEC_SEQ, DEC_BATCH, Q_WIDTH), F32),
                   jax.ShapeDtypeStruct((DEC_SEQ, DEC_BATCH, POOL_WIDTH), F32),
                   jax.ShapeDtypeStruct((DEC_BATCH, KV_WIDTH, WINDOW), F32),
                   jax.ShapeDtypeStruct((DEC_BATCH, KV_WIDTH, WINDOW), F32),
                   jax.ShapeDtypeStruct((POOL_BUF, DEC_BATCH, POOL_WIDTH), F32)),
        grid=(groups,),
        in_specs=[pl.BlockSpec(memory_space=pltpu.SMEM)]
                 + per_token(Q_WIDTH) + per_token(KV_WIDTH) + per_token(KV_WIDTH) + per_token(POOL_WIDTH)
                 + [cache_spec, cache_spec, pool_rows_spec, _resident(bias_c.shape), _resident(bias_n.shape)],
        out_specs=(by_token(Q_WIDTH), by_token(POOL_WIDTH), cache_spec, cache_spec, pool_rows_spec),
        scratch_shapes=[pltpu.VMEM((LANES, KV_WIDTH), F32), pltpu.VMEM((LANES, KV_WIDTH), F32)],
        compiler_params=pltpu.CompilerParams(dimension_semantics=("parallel",),
                                             vmem_limit_bytes=VMEM_LIMIT_BYTES),
        name="sample_ctx",
    )(sinks, *([qs] * DEC_SEQ), *([ks] * DEC_SEQ), *([vs] * DEC_SEQ), *([us] * DEC_SEQ),
      ckt, cvt, cu, jnp.asarray(bias_c), jnp.asarray(bias_n))


def _back_kernel(h_ref, pa_ref, pools_ref, attns_ref,
                 win_hbm, wgrp_hbm, scale_ref, wpo_hbm, wao_hbm, wout_hbm, g2_ref, b2_ref,
                 w1_hbm, w3_hbm, w2_hbm, g3_ref, b3_ref, yp_ref, ys_ref, z2_c, z3_c, h_s,
                 wg_ref, wgrp_ref, wpo_ref, wao_ref, wout_ref, w1_ref, w3_ref, w2_ref,
                 buf_ff, buf_group, sem_ff, sem_group, sem_model):
    t = pl.program_id(0)

    def branch_outputs():
        is_sample = t >= PROMPT_TILES
        pool_in = jnp.where(is_sample, pools_ref[...].astype(BF16), pa_ref[:, 0:POOL_WIDTH])
        attn_o = jnp.where(is_sample, attns_ref[...].astype(BF16), pa_ref[:, POOL_WIDTH:POOL_WIDTH + Q_WIDTH])
        zs = [_dot(pool_in[:, g * POOL_GROUP:(g + 1) * POOL_GROUP],
                   wgrp_ref[g * POOL_GROUP:(g + 1) * POOL_GROUP, :])
              for g in range(len(POOL_WINDOWS))]
        pool_z = jnp.concatenate(zs, axis=1) * scale_ref[...]
        return _dot(pool_z.astype(BF16), wpo_ref[...]), _dot(attn_o, wao_ref[...])

    def gate_and_merge(h1, a, b):
        gates = jax.nn.sigmoid(_dot(h1.astype(BF16), wg_ref[...]))
        return (gates[:, :D_MODEL] * a + gates[:, D_MODEL:] * b).astype(BF16)

    def project_out(h1, merged):
        z2_c[...] = ALPHA * h1 + _dot(merged, wout_ref[...])

    @pl.when(jnp.logical_and(t >= 1, t <= ROW_TILES))
    def _():
        h1 = h_ref[...]
        a, b = branch_outputs()
        yp_ref[...] = _layer_norm(z3_c[...], g3_ref[...], b3_ref[...])
        h2 = _layer_norm(z2_c[...], g2_ref[...], b2_ref[...])
        merged = []
        side = {1: lambda: merged.append(gate_and_merge(h1, a, b)), 9: lambda: project_out(h1, merged[0])}
        _swiglu_residual(h2, w1_ref, w3_ref, w2_ref, h_s, side, z3_c)

    @pl.when(t == ROW_TILES + 1)
    def _():
        ys_ref[...] = _layer_norm(z3_c[...], g3_ref[...], b3_ref[...])

    @pl.when(t == 0)
    def _():
        ring_ff = _ring(buf_ff, sem_ff)
        ring_group = _ring(buf_group, sem_group)
        ring_model = _ring(z3_c, sem_model, rows=MODEL_RING_ROWS)
        mixer = (_row_chunks(wgrp_hbm, 0, wgrp_ref, ring_group)
                 + _row_chunks(wpo_hbm, 0, wpo_ref, ring_model)
                 + _row_chunks(wao_hbm, 0, wao_ref, ring_model)
                 + _row_chunks(win_hbm, UQKV_WIDTH, wg_ref.at[:, 0:D_MODEL], ring_model)
                 + _row_chunks(win_hbm, UQKV_WIDTH + D_MODEL, wg_ref.at[:, D_MODEL:2 * D_MODEL], ring_model)
                 + _row_chunks(wout_hbm, 0, wout_ref, ring_model))
        swiglu = (_row_chunks(w1_hbm, 0, w1_ref, ring_ff) + _row_chunks(w3_hbm, 0, w3_ref, ring_ff)
                  + _row_chunks(w2_hbm, 0, w2_ref, ring_model))
        stager = _WeightStager(mixer + swiglu)
        stager.run(len(mixer))
        h1 = h_ref[...]
        a, b = branch_outputs()
        stager.run(len(swiglu) // 3)
        merged = gate_and_merge(h1, a, b)
        stager.run(len(swiglu) // 3)
        project_out(h1, merged)
        stager.run()
        z3_c[...] = jnp.zeros(z3_c.shape, z3_c.dtype)


def _back(h1, pool_attn, pool_s, attn_s, w_in, wgrp, scale, wpo, wao, wout, g2, b2, w1, w3, w2, g3, b3):
    lagged = pl.BlockSpec((ROW_TILE, D_MODEL), lambda t: (jnp.clip(t - 2, 0, PROMPT_TILES - 1), 0))
    hbm = pl.BlockSpec(memory_space=pl.ANY)
    return pl.pallas_call(
        _back_kernel,
        out_shape=(jax.ShapeDtypeStruct((PROMPT_ROWS, D_MODEL), F32),
                   jax.ShapeDtypeStruct((SAMPLE_ROWS, D_MODEL), F32)),
        grid=(ROW_TILES + 2,),
        in_specs=[pl.BlockSpec((ROW_TILE, D_MODEL), lambda t: (jnp.minimum(t, PROMPT_TILES), 0)),
                  pl.BlockSpec((ROW_TILE, POOL_WIDTH + Q_WIDTH), lambda t: (jnp.minimum(t, PROMPT_TILES - 1), 0)),
                  _resident((SAMPLE_ROWS, POOL_WIDTH)), _resident((SAMPLE_ROWS, Q_WIDTH)),
                  hbm, hbm, _resident((1, POOL_WIDTH)), hbm, hbm, hbm,
                  _resident((1, D_MODEL)), _resident((1, D_MODEL)),
                  hbm, hbm, hbm, _resident((1, D_MODEL)), _resident((1, D_MODEL))],
        out_specs=(lagged, pl.BlockSpec((ROW_TILE, D_MODEL), lambda r: (0, 0))),
        scratch_shapes=[pltpu.VMEM((ROW_TILE, D_MODEL), F32), pltpu.VMEM((ROW_TILE, D_MODEL), F32),
                        pltpu.VMEM((ROW_TILE, D_FF), BF16),
                        pltpu.VMEM((D_MODEL, 2 * D_MODEL), BF16),
                        pltpu.VMEM((len(POOL_WINDOWS) * POOL_GROUP, POOL_GROUP), BF16),
                        pltpu.VMEM((POOL_WIDTH, D_MODEL), BF16), pltpu.VMEM((Q_WIDTH, D_MODEL), BF16),
                        pltpu.VMEM((D_MODEL, D_MODEL), BF16),
                        pltpu.VMEM((D_MODEL, D_FF), BF16), pltpu.VMEM((D_MODEL, D_FF), BF16),
                        pltpu.VMEM((D_FF, D_MODEL), BF16)]
                       + _ring_scratch(RING_FF, RING_GROUP),
        compiler_params=pltpu.CompilerParams(dimension_semantics=("arbitrary",),
                                             vmem_limit_bytes=VMEM_LIMIT_BYTES),
        name="back",
    )(h1, pool_attn, pool_s, attn_s, w_in, wgrp, scale, wpo, wao, wout, g2, b2, w1, w3, w2, g3, b3)


def kernel(x_prompt, x_sample, cache_pool_u, cache_k_win, cache_v_win, w_in, pool_w_grp, pool_scale,
           attn_sinks, w_pool_out, w_attn_out, w_out, ffn1_w1, ffn1_w3, ffn1_w2, ffn2_w1, ffn2_w3,
           ffn2_w2, ln1_g, ln1_b, ln2_g, ln2_b, ln3_g, ln3_b):
    assert DEPTH == 1 and w_in.shape[0] == 1
    l = 0
    vec = lambda p: p[l].reshape(1, -1)
    sinks = attn_sinks[l]

    freq = jnp.tile(ROPE_THETA ** (-2.0 * jnp.arange(HEAD_DIM // 2, dtype=F32) / HEAD_DIM), 4).reshape(1, LANES)

    xp = x_prompt.reshape(PROMPT_ROWS, D_MODEL)
    xs = jnp.transpose(x_sample, (1, 0, 2)).reshape(SAMPLE_ROWS, D_MODEL)
    (h1, pool_attn, us, qs, ks, vs, kt_last, vt_last, u_last) = _front(
        xp, xs, ffn1_w1, ffn1_w3, ffn1_w2, w_in, vec(ln1_g), vec(ln1_b), freq, sinks)

    to_t = lambda c: jnp.transpose(c[l], (0, 2, 3, 1)).reshape(DEC_BATCH, KV_WIDTH, WINDOW)
    cu = jnp.transpose(cache_pool_u[l], (1, 0, 2))
    attn_s, pool_s, kt_s, vt_s, pu_s = _sample_ctx(us, qs, ks, vs, to_t(cache_k_win), to_t(cache_v_win), cu, sinks)

    wgrp = pool_w_grp.reshape(DEPTH, len(POOL_WINDOWS) * POOL_GROUP, POOL_GROUP)
    yp, ys = _back(h1, pool_attn, pool_s.reshape(SAMPLE_ROWS, POOL_WIDTH), attn_s.reshape(SAMPLE_ROWS, Q_WIDTH),
                   w_in, wgrp, vec(pool_scale), w_pool_out, w_attn_out, w_out, vec(ln2_g), vec(ln2_b),
                   ffn2_w1, ffn2_w3, ffn2_w2, vec(ln3_g), vec(ln3_b))
    yp = yp.reshape(BATCH, SEQ, D_MODEL)
    ys = jnp.transpose(ys.reshape(DEC_SEQ, DEC_BATCH, D_MODEL), (1, 0, 2))

    from_t = lambda c, n: jnp.transpose(c.reshape(n, N_KV_HEADS, HEAD_DIM, WINDOW), (0, 3, 1, 2))[None]
    pool_u_prompt = u_last[None, :, POOL_PAD - POOL_BUF:]
    pool_u_sample = jnp.transpose(pu_s, (1, 0, 2))[None]
    return (yp, ys, pool_u_prompt, from_t(kt_last, BATCH), from_t(vt_last, BATCH),
            pool_u_sample, from_t(kt_s, DEC_BATCH), from_t(vt_s, DEC_BATCH))
```

```python
import jax
import jax.numpy as jnp
import numpy as np
from jax import lax
from jax.experimental import pallas as pl
from jax.experimental.pallas import tpu as pltpu

D_MODEL = 1024
BATCH = 8
SEQ = 2048
DEC_BATCH = 128
DEC_SEQ = 4
PAST_LEN = 8192
POOL_WINDOWS = (2, 4, 8, 16)
POOL_GROUP = 128
POOL_WIDTH = 512
POOL_BUF = 15
N_HEADS = 8
N_KV_HEADS = 2
HEADS_PER_KV = N_HEADS // N_KV_HEADS
HEAD_DIM = 64
Q_WIDTH = 512
KV_WIDTH = 128
WINDOW = 128
ROPE_THETA = 10000.0
D_FF = 2816
DEPTH = 1
ALPHA = (2.0 * DEPTH) ** 0.25
LN_EPS = 1e-5
NEG_INF = -1e30
UQKV_WIDTH = POOL_WIDTH + Q_WIDTH + 2 * KV_WIDTH

LANES = 128
KEY_SPAN = 2 * WINDOW
VMEM_LIMIT_BYTES = 61 * 1024 * 1024

ROW_TILE = 512
TILES_PER_SEQ = SEQ // ROW_TILE
PROMPT_ROWS = BATCH * SEQ
PROMPT_TILES = PROMPT_ROWS // ROW_TILE
SAMPLE_ROWS = DEC_BATCH * DEC_SEQ
ROW_TILES = PROMPT_TILES + 1
POOL_PAD = 16
SEQ_GROUP = 32
SUB_GROUP = 8
FF_CHUNK = 256
OUT_BLOCK = 256

BF16 = jnp.bfloat16
F32 = jnp.float32


def _dot(a, b):
    return jnp.dot(a, b, preferred_element_type=F32)


def _dot_nt(a, b):
    return lax.dot_general(a, b, (((1,), (1,)), ((), ())), preferred_element_type=F32)


def _layer_norm(y, g, b):
    mu = jnp.mean(y, axis=-1, keepdims=True)
    yc = y - mu
    var = jnp.mean(yc * yc, axis=-1, keepdims=True)
    return yc * lax.rsqrt(var + LN_EPS) * g + b


def _resident(shape):
    nd = len(shape)
    return pl.BlockSpec(shape, lambda *_: (0,) * nd, pipeline_mode=pl.Buffered(1))


class _WeightStager:
    def __init__(self, jobs):
        uses, self.staged = {}, []
        for src, ring, dst in jobs:
            slot, sem = ring[uses.get(id(ring), 0) % len(ring)]
            uses[id(ring)] = uses.get(id(ring), 0) + 1
            self.staged.append((pltpu.make_async_copy(src, slot, sem), slot, dst))
        assert all(len(ring) >= min(STAGE_IN_FLIGHT, uses[id(ring)]) for _, ring, _ in jobs)
        self.done = 0
        for copy, _, _ in self.staged[:STAGE_IN_FLIGHT]:
            copy.start()

    def run(self, count=None):
        end = len(self.staged) if count is None else min(self.done + count, len(self.staged))
        for i in range(self.done, end):
            copy, slot, dst = self.staged[i]
            copy.wait()
            dst[...] = slot[...].astype(BF16)
            if i + STAGE_IN_FLIGHT < len(self.staged):
                self.staged[i + STAGE_IN_FLIGHT][0].start()
        self.done = end


def _ring(buf, sems, rows=None):
    if rows is None:
        return [(buf.at[i], sems.at[i]) for i in range(buf.shape[0])]
    return [(buf.at[pl.ds(i * rows, rows), :], sems.at[i]) for i in range(buf.shape[0] // rows)]


def _row_chunks(w_hbm, col0, dst, ring):
    rows, cols = dst.shape
    step, width = ring[0][0].shape
    assert rows % step == 0 and width == cols
    return [(w_hbm.at[0, pl.ds(r0, step), pl.ds(col0, cols)], ring, dst.at[pl.ds(r0, step), :])
            for r0 in range(0, rows, step)]


STAGE_IN_FLIGHT = 4
RING_FF = (STAGE_IN_FLIGHT, 64, D_FF)
MODEL_RING_ROWS = ROW_TILE // STAGE_IN_FLIGHT
RING_NARROW = (STAGE_IN_FLIGHT, 128, UQKV_WIDTH - D_MODEL)
RING_GROUP = (1, len(POOL_WINDOWS) * POOL_GROUP, POOL_GROUP)


def _ring_scratch(*rings):
    return ([pltpu.VMEM(ring, F32) for ring in rings]
            + [pltpu.SemaphoreType.DMA((ring[0],)) for ring in rings]
            + [pltpu.SemaphoreType.DMA((STAGE_IN_FLIGHT,))])


FF_SLOTS = 2 * (D_FF // FF_CHUNK)


def _swiglu_residual(x, w1_ref, w3_ref, w2_ref, h_s, side_work, out_ref):
    assert all(0 <= slot <= FF_SLOTS for slot in side_work)
    run = lambda slot: side_work.get(slot, lambda: None)()
    xb = x.astype(BF16)
    for j in range(D_FF // FF_CHUNK):
        cols = slice(j * FF_CHUNK, (j + 1) * FF_CHUNK)
        a = _dot(xb, w1_ref[:, cols])
        run(2 * j)
        b = _dot(xb, w3_ref[:, cols])
        h_s[:, cols] = ((a * jax.nn.sigmoid(a)) * b).astype(BF16)
        run(2 * j + 1)
    run(FF_SLOTS)
    h = h_s[...]
    for c0 in range(0, D_MODEL, OUT_BLOCK):
        cols = slice(c0, c0 + OUT_BLOCK)
        out_ref[:, cols] = ALPHA * x[:, cols] + 0.5 * _dot(h, w2_ref[:, cols])


def _rope(x, cos, sin_signed, first_half):
    fwd = pltpu.roll(x, LANES - HEAD_DIM // 2, axis=1)
    bwd = pltpu.roll(x, HEAD_DIM // 2, axis=1)
    return x * cos + jnp.where(first_half, fwd, bwd) * sin_signed


def _swap_halves(x):
    return pltpu.roll(x, HEAD_DIM, axis=1)


def _swap_halves_wide(x):
    return jnp.concatenate([_swap_halves(x[:, c:c + LANES]) for c in range(0, x.shape[1], LANES)], axis=1)


def _lane_split(x, x_sw, kh):
    low = lax.broadcasted_iota(jnp.int32, (1, LANES), 1) < HEAD_DIM
    lo, hi = (x, x_sw) if kh == 0 else (x_sw, x)
    return jnp.concatenate([jnp.where(low, lo, 0.0), jnp.where(low, 0.0, hi)], axis=0).astype(BF16)


def _sink_softmax(q_pairs, keys, kh, bias, sinks_ref):
    s = _dot_nt(q_pairs, _lane_split(keys, _swap_halves(keys), kh)) + bias
    second_pair = lax.broadcasted_iota(jnp.int32, (s.shape[0], 1), 0) >= WINDOW
    probs, denoms = [], []
    for c in range(2):
        sc = s[:, c * KEY_SPAN:(c + 1) * KEY_SPAN]
        head = kh * HEADS_PER_KV + c
        sink = jnp.where(second_pair, sinks_ref[head + 2], sinks_ref[head])
        m = jnp.maximum(jnp.max(sc, axis=-1, keepdims=True), sink)
        p = jnp.exp(sc - m)
        denoms.append(jnp.sum(p, axis=-1, keepdims=True) + jnp.exp(sink - m))
        probs.append(p.astype(BF16))
    return jnp.concatenate(probs, axis=1), denoms


def _weighted_values(probs, denoms, vals, kh):
    low = lax.broadcasted_iota(jnp.int32, (1, LANES), 1) < HEAD_DIM
    o = _dot(probs, _lane_split(vals, _swap_halves(vals), kh))
    return o / jnp.where(low, denoms[0], denoms[1])


def _prompt_context_work(tile, q_c, k_c, v_c, u_c, bias_ref, sinks_ref,
                         pool_ref, attn_ref, kt_ref, vt_ref, ulast_ref):
    seq_tile = (tile + TILES_PER_SEQ) % TILES_PER_SEQ
    first_tile = seq_tile == 0
    softmaxed = {}

    def scores(unit, blk, kh):
        r0 = blk * WINDOW
        bias = bias_ref[jnp.where(first_tile, 1, 0)] if blk == 0 else bias_ref[0]
        c0 = 2 * kh * LANES
        q_pairs = jnp.concatenate([q_c[r0:r0 + WINDOW, c0:c0 + LANES],
                                   q_c[r0:r0 + WINDOW, c0 + LANES:c0 + 2 * LANES]], axis=0)
        softmaxed[unit] = _sink_softmax(q_pairs, k_c[r0:r0 + KEY_SPAN, :], kh, bias, sinks_ref)

    def values(unit, blk, kh):
        r0 = blk * WINDOW
        c0 = 2 * kh * LANES
        o = _weighted_values(*softmaxed.pop(unit), v_c[r0:r0 + KEY_SPAN, :], kh)
        attn_ref[r0:r0 + WINDOW, c0:c0 + LANES] = o[:WINDOW].astype(BF16)
        attn_ref[r0:r0 + WINDOW, c0 + LANES:c0 + 2 * LANES] = o[WINDOW:].astype(BF16)

    def pool(groups):
        pos = seq_tile * ROW_TILE + lax.broadcasted_iota(jnp.int32, (ROW_TILE, 1), 0)
        for g in groups:
            w = POOL_WINDOWS[g]
            cols = slice(g * POOL_GROUP, (g + 1) * POOL_GROUP)
            rows = u_c[:, cols]
            acc, span = rows, 1
            while span < w:
                acc = acc + pltpu.roll(acc, span, axis=0)
                span *= 2
            cur = rows[POOL_PAD:]
            cnt = jnp.minimum(pos + 1, w).astype(F32)
            pool_ref[:, cols] = (acc[POOL_PAD:] / cnt - cur).astype(BF16)

    def sequence_state():
        kt_ref[0] = k_c[ROW_TILE:ROW_TILE + WINDOW, :].T
        vt_ref[0] = v_c[ROW_TILE:ROW_TILE + WINDOW, :].T
        ulast_ref[0] = u_c[ROW_TILE:ROW_TILE + POOL_PAD, :]

    work = {"pool_wide": lambda: pool((3,)), "pool_narrow": lambda: pool((0, 1, 2)),
            "sequence_state": sequence_state}
    for blk in range(ROW_TILE // WINDOW):
        for kh in range(N_KV_HEADS):
            unit = blk * N_KV_HEADS + kh
            work["scores", unit] = lambda unit=unit, blk=blk, kh=kh: scores(unit, blk, kh)
            work["values", unit] = lambda unit=unit, blk=blk, kh=kh: values(unit, blk, kh)
    return work


def _prompt_bias():
    r = np.arange(2 * WINDOW)[:, None] % WINDOW
    c = np.arange(2 * KEY_SPAN)[None, :] % KEY_SPAN
    valid = (r <= c) & (c <= r + WINDOW)
    first = valid & (c >= WINDOW)
    return np.where(np.stack([valid, first]), 0.0, NEG_INF).astype(np.float32)


def _front_kernel(sinks_ref, xp_ref, xs_ref, w1_hbm, w3_hbm, w2_hbm, win_hbm, g_ref, b_ref, freq_ref, bias_ref,
                  h_ref, pa_ref, us_ref, qs_ref, ks_ref, vs_ref, kt_ref, vt_ref, ulast_ref,
                  z_c, q_c, k_c, v_c, u_c, h_s, rope_ref, w1_ref, w3_ref, w2_ref, wu_ref,
                  buf_ff, buf_narrow, sem_ff, sem_narrow, sem_model):
    r = pl.program_id(0) - 1
    prev = r - 1
    pool_ref = pa_ref.at[:, 0:POOL_WIDTH]
    attn_ref = pa_ref.at[:, POOL_WIDTH:POOL_WIDTH + Q_WIDTH]

    def stage_and_reset():
        ring_ff = _ring(buf_ff, sem_ff)
        ring_narrow = _ring(buf_narrow, sem_narrow)
        ring_model = _ring(z_c, sem_model, rows=MODEL_RING_ROWS)
        stager = _WeightStager(_row_chunks(w1_hbm, 0, w1_ref, ring_ff)
                               + _row_chunks(w3_hbm, 0, w3_ref, ring_ff)
                               + _row_chunks(win_hbm, 0, wu_ref.at[:, 0:D_MODEL], ring_model)
                               + _row_chunks(win_hbm, D_MODEL, wu_ref.at[:, D_MODEL:UQKV_WIDTH], ring_narrow)
                               + _row_chunks(w2_hbm, 0, w2_ref, ring_model))
        freq = freq_ref[...]
        lane = lax.broadcasted_iota(jnp.int32, (1, LANES), 1)
        sign = jnp.where((lane & (HEAD_DIM // 2)) == 0, -1.0, 1.0)
        row = lax.broadcasted_iota(jnp.int32, (ROW_TILE, 1), 0)
        per_tile = -(-len(stager.staged) // (TILES_PER_SEQ + 1))
        for tile in range(TILES_PER_SEQ + 1):
            pos = tile * ROW_TILE + row if tile < TILES_PER_SEQ else PAST_LEN + row // DEC_BATCH
            ang = pos.astype(F32) * freq
            rows = slice(tile * ROW_TILE, (tile + 1) * ROW_TILE)
            rope_ref[rows, 0:LANES] = jnp.cos(ang)
            rope_ref[rows, LANES:2 * LANES] = jnp.sin(ang) * sign
            stager.run(per_tile)
        stager.run()
        for ref in (z_c, q_c, k_c, v_c, u_c):
            ref[...] = jnp.zeros(ref.shape, ref.dtype)

    def norm_previous():
        h1 = _layer_norm(z_c[...], g_ref[...], b_ref[...])
        h_ref[...] = h1
        return h1.astype(BF16)

    def rope_tables():
        tile = jnp.clip(prev, 0, PROMPT_TILES)
        table = jnp.where(tile == PROMPT_TILES, TILES_PER_SEQ, tile % TILES_PER_SEQ)
        rows = pl.ds(pl.multiple_of(table * ROW_TILE, ROW_TILE), ROW_TILE)
        cos = rope_ref[rows, 0:LANES]
        lane = lax.broadcasted_iota(jnp.int32, cos.shape, 1)
        return cos, rope_ref[rows, LANES:2 * LANES], (lane & (HEAD_DIM // 2)) == 0

    def project_u(h1b):
        return _dot(h1b, wu_ref[:, 0:POOL_WIDTH])

    def project_q(h1b):
        z = _dot(h1b, wu_ref[:, POOL_WIDTH:POOL_WIDTH + Q_WIDTH])
        tables = rope_tables()
        return [(_rope(z[:, c:c + LANES], *tables) * (HEAD_DIM ** -0.5)).astype(BF16)
                for c in range(0, Q_WIDTH, LANES)]

    def project_kv(h1b):
        z = _dot(h1b, wu_ref[:, POOL_WIDTH + Q_WIDTH:UQKV_WIDTH])
        return _rope(z[:, :KV_WIDTH], *rope_tables()), z[:, KV_WIDTH:]

    def tile_step(x):
        normed = []

        def norm():
            normed.append(norm_previous())

        projected = []

        def project():
            projected.extend([project_u(normed[0]), project_q(normed[0]), *project_kv(normed[0])])

        def carry():
            u, q, k, v = projected
            starts_seq = (prev + TILES_PER_SEQ) % TILES_PER_SEQ == 0
            u_c[0:POOL_PAD, :] = jnp.where(starts_seq, 0.0, u_c[ROW_TILE:ROW_TILE + POOL_PAD, :])
            u_c[POOL_PAD:POOL_PAD + ROW_TILE, :] = u
            for c, qc in enumerate(q):
                q_c[:, c * LANES:(c + 1) * LANES] = qc
            k_c[0:WINDOW, :] = k_c[ROW_TILE:ROW_TILE + WINDOW, :]
            v_c[0:WINDOW, :] = v_c[ROW_TILE:ROW_TILE + WINDOW, :]
            k_c[WINDOW:WINDOW + ROW_TILE, :] = k
            v_c[WINDOW:WINDOW + ROW_TILE, :] = v

        side = {0: norm, FF_SLOTS - 3: project, FF_SLOTS - 1: carry}
        stages = context_stages()
        assert len(stages) < FF_SLOTS - 3
        side.update(enumerate(stages, start=1))
        _swiglu_residual(x, w1_ref, w3_ref, w2_ref, h_s, side, z_c)

    def context_stages():
        context = _prompt_context_work(r - 2, q_c, k_c, v_c, u_c, bias_ref, sinks_ref,
                                       pool_ref, attn_ref, kt_ref, vt_ref, ulast_ref)
        stages = [("scores", 0)]
        for unit in range(1, 8):
            stages += [("scores", unit), ("values", unit - 1)]
        stages.append(("values", 7))

        def pooling_and_state():
            context["pool_wide"]()
            context["pool_narrow"]()
            context["sequence_state"]()

        return [context[stage] for stage in stages] + [pooling_and_state]

    @pl.when(jnp.logical_and(r >= 0, r <= PROMPT_TILES))
    def _():
        tile_step(jnp.where(r == PROMPT_TILES, xs_ref[...], xp_ref[...]))

    @pl.when(r == ROW_TILES)
    def _():
        for work in context_stages():
            work()
        h1b = norm_previous()
        us_ref[...] = project_u(h1b)
        for c, qc in enumerate(project_q(h1b)):
            qs_ref[:, c * LANES:(c + 1) * LANES] = qc
        ks_ref[...], vs_ref[...] = project_kv(h1b)

    pl.when(r < 0)(stage_and_reset)


def _front(xp, xs, w1, w3, w2, w_in, g, b, freq, sinks):
    rows = ROW_TILES * ROW_TILE

    def lagged(width):
        return pl.BlockSpec((ROW_TILE, width), lambda s: (jnp.clip(s - 3, 0, PROMPT_TILES - 1), 0))

    def seq_of_lagged(shape):
        return pl.BlockSpec(shape, lambda s: (jnp.clip(s - 3, 0, PROMPT_TILES - 1) // TILES_PER_SEQ, 0, 0))

    sample = lambda width: pl.BlockSpec((ROW_TILE, width), lambda r: (0, 0))
    hbm = pl.BlockSpec(memory_space=pl.ANY)
    return pl.pallas_call(
        _front_kernel,
        out_shape=(jax.ShapeDtypeStruct((rows, D_MODEL), F32),
                   jax.ShapeDtypeStruct((PROMPT_ROWS, POOL_WIDTH + Q_WIDTH), BF16),
                   jax.ShapeDtypeStruct((SAMPLE_ROWS, POOL_WIDTH), F32),
                   jax.ShapeDtypeStruct((SAMPLE_ROWS, Q_WIDTH), BF16),
                   jax.ShapeDtypeStruct((SAMPLE_ROWS, KV_WIDTH), F32),
                   jax.ShapeDtypeStruct((SAMPLE_ROWS, KV_WIDTH), F32),
                   jax.ShapeDtypeStruct((BATCH, KV_WIDTH, WINDOW), F32),
                   jax.ShapeDtypeStruct((BATCH, KV_WIDTH, WINDOW), F32),
                   jax.ShapeDtypeStruct((BATCH, POOL_PAD, POOL_WIDTH), F32)),
        grid=(ROW_TILES + 2,),
        in_specs=[pl.BlockSpec(memory_space=pltpu.SMEM),
                  pl.BlockSpec((ROW_TILE, D_MODEL), lambda s: (jnp.clip(s - 1, 0, PROMPT_TILES - 1), 0)),
                  _resident((SAMPLE_ROWS, D_MODEL)),
                  hbm, hbm, hbm, hbm, _resident((1, D_MODEL)), _resident((1, D_MODEL)),
                  _resident((1, LANES)), _resident((2, 2 * WINDOW, 2 * KEY_SPAN))],
        out_specs=(pl.BlockSpec((ROW_TILE, D_MODEL), lambda s: (jnp.clip(s - 2, 0, PROMPT_TILES), 0)),
                   lagged(POOL_WIDTH + Q_WIDTH),
                   sample(POOL_WIDTH), sample(Q_WIDTH), sample(KV_WIDTH), sample(KV_WIDTH),
                   seq_of_lagged((1, KV_WIDTH, WINDOW)), seq_of_lagged((1, KV_WIDTH, WINDOW)),
                   seq_of_lagged((1, POOL_PAD, POOL_WIDTH))),
        scratch_shapes=[pltpu.VMEM((ROW_TILE, D_MODEL), F32),
                        pltpu.VMEM((ROW_TILE, Q_WIDTH), BF16),
                        pltpu.VMEM((WINDOW + ROW_TILE, KV_WIDTH), F32),
                        pltpu.VMEM((WINDOW + ROW_TILE, KV_WIDTH), F32),
                        pltpu.VMEM((POOL_PAD + ROW_TILE, POOL_WIDTH), F32),
                        pltpu.VMEM((ROW_TILE, D_FF), BF16),
                        pltpu.VMEM(((TILES_PER_SEQ + 1) * ROW_TILE, 2 * LANES), F32),
                        pltpu.VMEM((D_MODEL, D_FF), BF16), pltpu.VMEM((D_MODEL, D_FF), BF16),
                        pltpu.VMEM((D_FF, D_MODEL), BF16), pltpu.VMEM((D_MODEL, UQKV_WIDTH), BF16)]
                       + _ring_scratch(RING_FF, RING_NARROW),
        compiler_params=pltpu.CompilerParams(dimension_semantics=("arbitrary",),
                                             vmem_limit_bytes=VMEM_LIMIT_BYTES),
        name="front",
    )(sinks, xp, xs, w1, w3, w2, w_in, g, b, freq, jnp.asarray(_prompt_bias()))


def _sample_bias():
    row = np.arange(HEADS_PER_KV * DEC_SEQ * SUB_GROUP)
    row_t = (row // SUB_GROUP) % DEC_SEQ
    row_b = row % SUB_GROUP
    col = np.arange(SUB_GROUP * WINDOW)
    ok_c = (col[None, :] // WINDOW == row_b[:, None]) & (col[None, :] % WINDOW >= row_t[:, None])
    new = np.arange(DEC_SEQ * SUB_GROUP)
    ok_n = (new[None, :] % SUB_GROUP == row_b[:, None]) & (new[None, :] // SUB_GROUP <= row_t[:, None])
    to_bias = lambda ok: np.where(ok, 0.0, NEG_INF).astype(np.float32)
    return to_bias(ok_c), to_bias(ok_n)


def _sample_ctx_kernel(sinks_ref, q0, q1, q2, q3, k0, k1, k2, k3, v0, v1, v2, v3, u0, u1, u2, u3,
                       ckt_ref, cvt_ref, cu_ref, bias_c_ref, bias_n_ref,
                       attn_ref, pool_ref, kt_out, vt_out, pu_out, kbt_s, vbt_s):
    q_t = [q[...].astype(F32) for q in (q0, q1, q2, q3)]
    k_t = [k[...] for k in (k0, k1, k2, k3)]
    v_t = [v[...] for v in (v0, v1, v2, v3)]

    rows_u = [cu_ref[i] for i in range(POOL_BUF)] + [u[...] for u in (u0, u1, u2, u3)]
    for t in range(DEC_SEQ):
        pooled = []
        for g, w in enumerate(POOL_WINDOWS):
            cols = slice(g * POOL_GROUP, (g + 1) * POOL_GROUP)
            cur = rows_u[POOL_BUF + t][:, cols]
            acc = cur
            for j in range(1, w):
                acc = acc + rows_u[POOL_BUF + t - j][:, cols]
            pooled.append(acc / float(w) - cur)
        pool_ref[t] = jnp.concatenate(pooled, axis=1)
    for i in range(POOL_BUF):
        pu_out[i] = rows_u[i + DEC_SEQ]

    bias_c = bias_c_ref[...]
    bias_n = bias_n_ref[...]
    head_of_row = lax.broadcasted_iota(jnp.int32, (bias_c.shape[0], 1), 0) // (DEC_SEQ * SUB_GROUP)
    low = lax.broadcasted_iota(jnp.int32, (1, LANES), 1) < HEAD_DIM
    q_sw = [_swap_halves_wide(q) for q in q_t]
    k_sw = [_swap_halves(k) for k in k_t]
    v_sw = [_swap_halves(v) for v in v_t]
    for sub in range(SEQ_GROUP // SUB_GROUP):
        rows = slice(sub * SUB_GROUP, (sub + 1) * SUB_GROUP)
        for kh in range(N_KV_HEADS):
            def q_piece(t, head):
                src = q_t[t] if head % 2 == 0 else q_sw[t]
                chunk = head // 2
                return jnp.where(low, src[rows, chunk * LANES:(chunk + 1) * LANES], 0.0)

            def kv_first(c):
                return c if kh == 0 else jnp.concatenate([c[HEAD_DIM:], c[:HEAD_DIM]], axis=0)

            def kv_twice(c):
                part = c[kh * HEAD_DIM:(kh + 1) * HEAD_DIM]
                return jnp.concatenate([part, part], axis=0)

            lhs = jnp.concatenate([q_piece(t, kh * HEADS_PER_KV + g)
                                   for g in range(HEADS_PER_KV) for t in range(DEC_SEQ)], axis=0).astype(BF16)
            kcat = jnp.concatenate([kv_first(ckt_ref[sub * SUB_GROUP + b]) for b in range(SUB_GROUP)],
                                   axis=1).astype(BF16)
            vcat = jnp.concatenate([kv_twice(cvt_ref[sub * SUB_GROUP + b]) for b in range(SUB_GROUP)],
                                   axis=1).astype(BF16)
            knew = jnp.concatenate([(k_t[t] if kh == 0 else k_sw[t])[rows] for t in range(DEC_SEQ)],
                                   axis=0).astype(BF16)
            vnew = jnp.concatenate([(jnp.where(low, v_t[t], v_sw[t]) if kh == 0 else
                                     jnp.where(low, v_sw[t], v_t[t]))[rows] for t in range(DEC_SEQ)],
                                   axis=0).astype(BF16)
            s_c = _dot(lhs, kcat) + bias_c
            s_n = _dot_nt(lhs, knew) + bias_n
            sink = jnp.zeros(head_of_row.shape, F32)
            for g in range(HEADS_PER_KV):
                sink = jnp.where(head_of_row == g, sinks_ref[kh * HEADS_PER_KV + g], sink)
            m = jnp.maximum(jnp.maximum(jnp.max(s_c, axis=-1, keepdims=True),
                                        jnp.max(s_n, axis=-1, keepdims=True)), sink)
            p_c = jnp.exp(s_c - m)
            p_n = jnp.exp(s_n - m)
            denom = (jnp.sum(p_c, axis=-1, keepdims=True) + jnp.sum(p_n, axis=-1, keepdims=True)
                     + jnp.exp(sink - m))
            o = (_dot_nt(p_c.astype(BF16), vcat) + _dot(p_n.astype(BF16), vnew)) / denom
            for t in range(DEC_SEQ):
                for pair in range(HEADS_PER_KV // 2):
                    piece = lambda g: o[(g * DEC_SEQ + t) * SUB_GROUP:(g * DEC_SEQ + t + 1) * SUB_GROUP]
                    c0 = (kh * HEADS_PER_KV // 2 + pair) * LANES
                    attn_ref[t, rows, c0:c0 + LANES] = jnp.where(low, piece(2 * pair), piece(2 * pair + 1))

    if DEC_SEQ * SEQ_GROUP < LANES:
        zeros = jnp.zeros((LANES - DEC_SEQ * SEQ_GROUP, KV_WIDTH), F32)
        kbt_s[DEC_SEQ * SEQ_GROUP:, :] = zeros
        vbt_s[DEC_SEQ * SEQ_GROUP:, :] = zeros
    for t in range(DEC_SEQ):
        kbt_s[pl.ds(t, SEQ_GROUP, stride=DEC_SEQ), :] = k_t[t]
        vbt_s[pl.ds(t, SEQ_GROUP, stride=DEC_SEQ), :] = v_t[t]
    knew_t = kbt_s[...].T
    vnew_t = vbt_s[...].T
    keep = lax.broadcasted_iota(jnp.int32, (1, WINDOW), 1) < WINDOW - DEC_SEQ
    for b in range(SEQ_GROUP):
        shift_new = WINDOW - DEC_SEQ - DEC_SEQ * b
        kt_out[b] = jnp.where(keep, pltpu.roll(ckt_ref[b], WINDOW - DEC_SEQ, axis=1),
                              pltpu.roll(knew_t, shift_new, axis=1))
        vt_out[b] = jnp.where(keep, pltpu.roll(cvt_ref[b], WINDOW - DEC_SEQ, axis=1),
                              pltpu.roll(vnew_t, shift_new, axis=1))


def _sample_ctx(us, qs, ks, vs, ckt, cvt, cu, sinks):
    groups = DEC_BATCH // SEQ_GROUP

    def token_rows(t, width):
        return pl.BlockSpec((SEQ_GROUP, width), lambda i: (t * groups + i, 0))

    def per_token(width):
        return [token_rows(t, width) for t in range(DEC_SEQ)]

    cache_spec = pl.BlockSpec((SEQ_GROUP, KV_WIDTH, WINDOW), lambda i: (i, 0, 0))
    pool_rows_spec = pl.BlockSpec((POOL_BUF, SEQ_GROUP, POOL_WIDTH), lambda i: (0, i, 0))
    by_token = lambda width: pl.BlockSpec((DEC_SEQ, SEQ_GROUP, width), lambda i: (0, i, 0))
    bias_c, bias_n = _sample_bias()
    return pl.pallas_call(
        _sample_ctx_kernel,
        out_shape=(jax.ShapeDtypeStruct((DEC_SEQ, DEC_BATCH, Q_WIDTH), F32),
                   jax.ShapeDtypeStruct((DEC_SEQ, DEC_BATCH, POOL_WIDTH), F32),
                   jax.ShapeDtypeStruct((DEC_BATCH, KV_WIDTH, WINDOW), F32),
                   jax.ShapeDtypeStruct((DEC_BATCH, KV_WIDTH, WINDOW), F32),
                   jax.ShapeDtypeStruct((POOL_BUF, DEC_BATCH, POOL_WIDTH), F32)),
        grid=(groups,),
        in_specs=[pl.BlockSpec(memory_space=pltpu.SMEM)]
                 + per_token(Q_WIDTH) + per_token(KV_WIDTH) + per_token(KV_WIDTH) + per_token(POOL_WIDTH)
                 + [cache_spec, cache_spec, pool_rows_spec, _resident(bias_c.shape), _resident(bias_n.shape)],
        out_specs=(by_token(Q_WIDTH), by_token(POOL_WIDTH), cache_spec, cache_spec, pool_rows_spec),
        scratch_shapes=[pltpu.VMEM((LANES, KV_WIDTH), F32), pltpu.VMEM((LANES, KV_WIDTH), F32)],
        compiler_params=pltpu.CompilerParams(dimension_semantics=("parallel",),
                                             vmem_limit_bytes=VMEM_LIMIT_BYTES),
        name="sample_ctx",
    )(sinks, *([qs] * DEC_SEQ), *([ks] * DEC_SEQ), *([vs] * DEC_SEQ), *([us] * DEC_SEQ),
      ckt, cvt, cu, jnp.asarray(bias_c), jnp.asarray(bias_n))


def _back_kernel(h_ref, pa_ref, pools_ref, attns_ref,
                 win_hbm, wgrp_hbm, scale_ref, wpo_hbm, wao_hbm, wout_hbm, g2_ref, b2_ref,
                 w1_hbm, w3_hbm, w2_hbm, g3_ref, b3_ref, yp_ref, ys_ref, z2_c, z3_c, h_s,
                 wg_ref, wgrp_ref, wpo_ref, wao_ref, wout_ref, w1_ref, w3_ref, w2_ref,
                 buf_ff, buf_group, sem_ff, sem_group, sem_model):
    t = pl.program_id(0)

    def branch_outputs():
        is_sample = t >= PROMPT_TILES
        pool_in = jnp.where(is_sample, pools_ref[...].astype(BF16), pa_ref[:, 0:POOL_WIDTH])
        attn_o = jnp.where(is_sample, attns_ref[...].astype(BF16), pa_ref[:, POOL_WIDTH:POOL_WIDTH + Q_WIDTH])
        zs = [_dot(pool_in[:, g * POOL_GROUP:(g + 1) * POOL_GROUP],
                   wgrp_ref[g * POOL_GROUP:(g + 1) * POOL_GROUP, :])
              for g in range(len(POOL_WINDOWS))]
        pool_z = jnp.concatenate(zs, axis=1) * scale_ref[...]
        return _dot(pool_z.astype(BF16), wpo_ref[...]), _dot(attn_o, wao_ref[...])

    def gate_and_merge(h1, a, b):
        gates = jax.nn.sigmoid(_dot(h1.astype(BF16), wg_ref[...]))
        return (gates[:, :D_MODEL] * a + gates[:, D_MODEL:] * b).astype(BF16)

    def project_out(h1, merged):
        z2_c[...] = ALPHA * h1 + _dot(merged, wout_ref[...])

    @pl.when(jnp.logical_and(t >= 1, t <= ROW_TILES))
    def _():
        h1 = h_ref[...]
        a, b = branch_outputs()
        yp_ref[...] = _layer_norm(z3_c[...], g3_ref[...], b3_ref[...])
        h2 = _layer_norm(z2_c[...], g2_ref[...], b2_ref[...])
        merged = []
        side = {1: lambda: merged.append(gate_and_merge(h1, a, b)), 9: lambda: project_out(h1, merged[0])}
        _swiglu_residual(h2, w1_ref, w3_ref, w2_ref, h_s, side, z3_c)

    @pl.when(t == ROW_TILES + 1)
    def _():
        ys_ref[...] = _layer_norm(z3_c[...], g3_ref[...], b3_ref[...])

    @pl.when(t == 0)
    def _():
        ring_ff = _ring(buf_ff, sem_ff)
        ring_group = _ring(buf_group, sem_group)
        ring_model = _ring(z3_c, sem_model, rows=MODEL_RING_ROWS)
        mixer = (_row_chunks(wgrp_hbm, 0, wgrp_ref, ring_group)
                 + _row_chunks(wpo_hbm, 0, wpo_ref, ring_model)
                 + _row_chunks(wao_hbm, 0, wao_ref, ring_model)
                 + _row_chunks(win_hbm, UQKV_WIDTH, wg_ref.at[:, 0:D_MODEL], ring_model)
                 + _row_chunks(win_hbm, UQKV_WIDTH + D_MODEL, wg_ref.at[:, D_MODEL:2 * D_MODEL], ring_model)
                 + _row_chunks(wout_hbm, 0, wout_ref, ring_model))
        swiglu = (_row_chunks(w1_hbm, 0, w1_ref, ring_ff) + _row_chunks(w3_hbm, 0, w3_ref, ring_ff)
                  + _row_chunks(w2_hbm, 0, w2_ref, ring_model))
        stager = _WeightStager(mixer + swiglu)
        stager.run(len(mixer))
        h1 = h_ref[...]
        a, b = branch_outputs()
        stager.run(len(swiglu) // 3)
        merged = gate_and_merge(h1, a, b)
        stager.run(len(swiglu) // 3)
        project_out(h1, merged)
        stager.run()
        z3_c[...] = jnp.zeros(z3_c.shape, z3_c.dtype)


def _back(h1, pool_attn, pool_s, attn_s, w_in, wgrp, scale, wpo, wao, wout, g2, b2, w1, w3, w2, g3, b3):
    lagged = pl.BlockSpec((ROW_TILE, D_MODEL), lambda t: (jnp.clip(t - 2, 0, PROMPT_TILES - 1), 0))
    hbm = pl.BlockSpec(memory_space=pl.ANY)
    return pl.pallas_call(
        _back_kernel,
        out_shape=(jax.ShapeDtypeStruct((PROMPT_ROWS, D_MODEL), F32),
                   jax.ShapeDtypeStruct((SAMPLE_ROWS, D_MODEL), F32)),
        grid=(ROW_TILES + 2,),
        in_specs=[pl.BlockSpec((ROW_TILE, D_MODEL), lambda t: (jnp.minimum(t, PROMPT_TILES), 0)),
                  pl.BlockSpec((ROW_TILE, POOL_WIDTH + Q_WIDTH), lambda t: (jnp.minimum(t, PROMPT_TILES - 1), 0)),
                  _resident((SAMPLE_ROWS, POOL_WIDTH)), _resident((SAMPLE_ROWS, Q_WIDTH)),
                  hbm, hbm, _resident((1, POOL_WIDTH)), hbm, hbm, hbm,
                  _resident((1, D_MODEL)), _resident((1, D_MODEL)),
                  hbm, hbm, hbm, _resident((1, D_MODEL)), _resident((1, D_MODEL))],
        out_specs=(lagged, pl.BlockSpec((ROW_TILE, D_MODEL), lambda r: (0, 0))),
        scratch_shapes=[pltpu.VMEM((ROW_TILE, D_MODEL), F32), pltpu.VMEM((ROW_TILE, D_MODEL), F32),
                        pltpu.VMEM((ROW_TILE, D_FF), BF16),
                        pltpu.VMEM((D_MODEL, 2 * D_MODEL), BF16),
                        pltpu.VMEM((len(POOL_WINDOWS) * POOL_GROUP, POOL_GROUP), BF16),
                        pltpu.VMEM((POOL_WIDTH, D_MODEL), BF16), pltpu.VMEM((Q_WIDTH, D_MODEL), BF16),
                        pltpu.VMEM((D_MODEL, D_MODEL), BF16),
                        pltpu.VMEM((D_MODEL, D_FF), BF16), pltpu.VMEM((D_MODEL, D_FF), BF16),
                        pltpu.VMEM((D_FF, D_MODEL), BF16)]
                       + _ring_scratch(RING_FF, RING_GROUP),
        compiler_params=pltpu.CompilerParams(dimension_semantics=("arbitrary",),
                                             vmem_limit_bytes=VMEM_LIMIT_BYTES),
        name="back",
    )(h1, pool_attn, pool_s, attn_s, w_in, wgrp, scale, wpo, wao, wout, g2, b2, w1, w3, w2, g3, b3)


def kernel(x_prompt, x_sample, cache_pool_u, cache_k_win, cache_v_win, w_in, pool_w_grp, pool_scale,
           attn_sinks, w_pool_out, w_attn_out, w_out, ffn1_w1, ffn1_w3, ffn1_w2, ffn2_w1, ffn2_w3,
           ffn2_w2, ln1_g, ln1_b, ln2_g, ln2_b, ln3_g, ln3_b):
    assert DEPTH == 1 and w_in.shape[0] == 1
    l = 0
    vec = lambda p: p[l].reshape(1, -1)
    sinks = attn_sinks[l]

    freq = jnp.tile(ROPE_THETA ** (-2.0 * jnp.arange(HEAD_DIM // 2, dtype=F32) / HEAD_DIM), 4).reshape(1, LANES)

    xp = x_prompt.reshape(PROMPT_ROWS, D_MODEL)
    xs = jnp.transpose(x_sample, (1, 0, 2)).reshape(SAMPLE_ROWS, D_MODEL)
    (h1, pool_attn, us, qs, ks, vs, kt_last, vt_last, u_last) = _front(
        xp, xs, ffn1_w1, ffn1_w3, ffn1_w2, w_in, vec(ln1_g), vec(ln1_b), freq, sinks)

    to_t = lambda c: jnp.transpose(c[l], (0, 2, 3, 1)).reshape(DEC_BATCH, KV_WIDTH, WINDOW)
    cu = jnp.transpose(cache_pool_u[l], (1, 0, 2))
    attn_s, pool_s, kt_s, vt_s, pu_s = _sample_ctx(us, qs, ks, vs, to_t(cache_k_win), to_t(cache_v_win), cu, sinks)

    wgrp = pool_w_grp.reshape(DEPTH, len(POOL_WINDOWS) * POOL_GROUP, POOL_GROUP)
    yp, ys = _back(h1, pool_attn, pool_s.reshape(SAMPLE_ROWS, POOL_WIDTH), attn_s.reshape(SAMPLE_ROWS, Q_WIDTH),
                   w_in, wgrp, vec(pool_scale), w_pool_out, w_attn_out, w_out, vec(ln2_g), vec(ln2_b),
                   ffn2_w1, ffn2_w3, ffn2_w2, vec(ln3_g), vec(ln3_b))
    yp = yp.reshape(BATCH, SEQ, D_MODEL)
    ys = jnp.transpose(ys.reshape(DEC_SEQ, DEC_BATCH, D_MODEL), (1, 0, 2))

    from_t = lambda c, n: jnp.transpose(c.reshape(n, N_KV_HEADS, HEAD_DIM, WINDOW), (0, 3, 1, 2))[None]
    pool_u_prompt = u_last[None, :, POOL_PAD - POOL_BUF:]
    pool_u_sample = jnp.transpose(pu_s, (1, 0, 2))[None]
    return (yp, ys, pool_u_prompt, from_t(kt_last, BATCH), from_t(vt_last, BATCH),
            pool_u_sample, from_t(kt_s, DEC_BATCH), from_t(vt_s, DEC_BATCH))
```

```python
import jax
import jax.numpy as jnp
import numpy as np
from jax import lax
from jax.experimental import pallas as pl
from jax.experimental.pallas import tpu as pltpu

D_MODEL = 1024
BATCH = 8
SEQ = 2048
DEC_BATCH = 128
DEC_SEQ = 4
PAST_LEN = 8192
POOL_WINDOWS = (2, 4, 8, 16)
POOL_GROUP = 128
POOL_WIDTH = 512
POOL_BUF = 15
N_HEADS = 8
N_KV_HEADS = 2
HEADS_PER_KV = N_HEADS // N_KV_HEADS
HEAD_DIM = 64
Q_WIDTH = 512
KV_WIDTH = 128
WINDOW = 128
ROPE_THETA = 10000.0
D_FF = 2816
DEPTH = 1
ALPHA = (2.0 * DEPTH) ** 0.25
LN_EPS = 1e-5
NEG_INF = -1e30
UQKV_WIDTH = POOL_WIDTH + Q_WIDTH + 2 * KV_WIDTH

LANES = 128
KEY_SPAN = 2 * WINDOW
VMEM_LIMIT_BYTES = 61 * 1024 * 1024

ROW_TILE = 512
TILES_PER_SEQ = SEQ // ROW_TILE
PROMPT_ROWS = BATCH * SEQ
PROMPT_TILES = PROMPT_ROWS // ROW_TILE
SAMPLE_ROWS = DEC_BATCH * DEC_SEQ
ROW_TILES = PROMPT_TILES + 1
POOL_PAD = 16
SEQ_GROUP = 32
SUB_GROUP = 8
FF_CHUNK = 256
OUT_BLOCK = 256

BF16 = jnp.bfloat16
F32 = jnp.float32


def _dot(a, b):
    return jnp.dot(a, b, preferred_element_type=F32)


def _dot_nt(a, b):
    return lax.dot_general(a, b, (((1,), (1,)), ((), ())), preferred_element_type=F32)


def _layer_norm(y, g, b):
    mu = jnp.mean(y, axis=-1, keepdims=True)
    yc = y - mu
    var = jnp.mean(yc * yc, axis=-1, keepdims=True)
    return yc * lax.rsqrt(var + LN_EPS) * g + b


def _resident(shape):
    nd = len(shape)
    return pl.BlockSpec(shape, lambda *_: (0,) * nd, pipeline_mode=pl.Buffered(1))


class _WeightStager:
    def __init__(self, jobs):
        uses, self.staged = {}, []
        for src, ring, dst in jobs:
            slot, sem = ring[uses.get(id(ring), 0) % len(ring)]
            uses[id(ring)] = uses.get(id(ring), 0) + 1
            self.staged.append((pltpu.make_async_copy(src, slot, sem), slot, dst))
        assert all(len(ring) >= min(STAGE_IN_FLIGHT, uses[id(ring)]) for _, ring, _ in jobs)
        self.done = 0
        for copy, _, _ in self.staged[:STAGE_IN_FLIGHT]:
            copy.start()

    def run(self, count=None):
        end = len(self.staged) if count is None else min(self.done + count, len(self.staged))
        for i in range(self.done, end):
            copy, slot, dst = self.staged[i]
            copy.wait()
            dst[...] = slot[...].astype(BF16)
            if i + STAGE_IN_FLIGHT < len(self.staged):
                self.staged[i + STAGE_IN_FLIGHT][0].start()
        self.done = end


def _ring(buf, sems, rows=None):
    if rows is None:
        return [(buf.at[i], sems.at[i]) for i in range(buf.shape[0])]
    return [(buf.at[pl.ds(i * rows, rows), :], sems.at[i]) for i in range(buf.shape[0] // rows)]


def _row_chunks(w_hbm, col0, dst, ring):
    rows, cols = dst.shape
    step, width = ring[0][0].shape
    assert rows % step == 0 and width == cols
    return [(w_hbm.at[0, pl.ds(r0, step), pl.ds(col0, cols)], ring, dst.at[pl.ds(r0, step), :])
            for r0 in range(0, rows, step)]


STAGE_IN_FLIGHT = 4
RING_FF = (STAGE_IN_FLIGHT, 64, D_FF)
MODEL_RING_ROWS = ROW_TILE // STAGE_IN_FLIGHT
RING_NARROW = (STAGE_IN_FLIGHT, 128, UQKV_WIDTH - D_MODEL)
RING_GROUP = (1, len(POOL_WINDOWS) * POOL_GROUP, POOL_GROUP)


def _ring_scratch(*rings):
    return ([pltpu.VMEM(ring, F32) for ring in rings]
            + [pltpu.SemaphoreType.DMA((ring[0],)) for ring in rings]
            + [pltpu.SemaphoreType.DMA((STAGE_IN_FLIGHT,))])


FF_SLOTS = 2 * (D_FF // FF_CHUNK)


def _swiglu_residual(x, w1_ref, w3_ref, w2_ref, h_s, side_work, out_ref):
    assert all(0 <= slot <= FF_SLOTS for slot in side_work)
    run = lambda slot: side_work.get(slot, lambda: None)()
    xb = x.astype(BF16)
    for j in range(D_FF // FF_CHUNK):
        cols = slice(j * FF_CHUNK, (j + 1) * FF_CHUNK)
        a = _dot(xb, w1_ref[:, cols])
        run(2 * j)
        b = _dot(xb, w3_ref[:, cols])
        h_s[:, cols] = ((a * jax.nn.sigmoid(a)) * b).astype(BF16)
        run(2 * j + 1)
    run(FF_SLOTS)
    h = h_s[...]
    for c0 in range(0, D_MODEL, OUT_BLOCK):
        cols = slice(c0, c0 + OUT_BLOCK)
        out_ref[:, cols] = ALPHA * x[:, cols] + 0.5 * _dot(h, w2_ref[:, cols])


def _rope(x, cos, sin_signed, first_half):
    fwd = pltpu.roll(x, LANES - HEAD_DIM // 2, axis=1)
    bwd = pltpu.roll(x, HEAD_DIM // 2, axis=1)
    return x * cos + jnp.where(first_half, fwd, bwd) * sin_signed


def _swap_halves(x):
    return pltpu.roll(x, HEAD_DIM, axis=1)


def _swap_halves_wide(x):
    return jnp.concatenate([_swap_halves(x[:, c:c + LANES]) for c in range(0, x.shape[1], LANES)], axis=1)


def _lane_split(x, x_sw, kh):
    low = lax.broadcasted_iota(jnp.int32, (1, LANES), 1) < HEAD_DIM
    lo, hi = (x, x_sw) if kh == 0 else (x_sw, x)
    return jnp.concatenate([jnp.where(low, lo, 0.0), jnp.where(low, 0.0, hi)], axis=0).astype(BF16)


def _sink_softmax(q_pairs, keys, kh, bias, sinks_ref):
    s = _dot_nt(q_pairs, _lane_split(keys, _swap_halves(keys), kh)) + bias
    second_pair = lax.broadcasted_iota(jnp.int32, (s.shape[0], 1), 0) >= WINDOW
    probs, denoms = [], []
    for c in range(2):
        sc = s[:, c * KEY_SPAN:(c + 1) * KEY_SPAN]
        head = kh * HEADS_PER_KV + c
        sink = jnp.where(second_pair, sinks_ref[head + 2], sinks_ref[head])
        m = jnp.maximum(jnp.max(sc, axis=-1, keepdims=True), sink)
        p = jnp.exp(sc - m)
        denoms.append(jnp.sum(p, axis=-1, keepdims=True) + jnp.exp(sink - m))
        probs.append(p.astype(BF16))
    return jnp.concatenate(probs, axis=1), denoms


def _weighted_values(probs, denoms, vals, kh):
    low = lax.broadcasted_iota(jnp.int32, (1, LANES), 1) < HEAD_DIM
    o = _dot(probs, _lane_split(vals, _swap_halves(vals), kh))
    return o / jnp.where(low, denoms[0], denoms[1])


def _prompt_context_work(tile, q_c, k_c, v_c, u_c, bias_ref, sinks_ref,
                         pool_ref, attn_ref, kt_ref, vt_ref, ulast_ref):
    seq_tile = (tile + TILES_PER_SEQ) % TILES_PER_SEQ
    first_tile = seq_tile == 0
    softmaxed = {}

    def scores(unit, blk, kh):
        r0 = blk * WINDOW
        bias = bias_ref[jnp.where(first_tile, 1, 0)] if blk == 0 else bias_ref[0]
        c0 = 2 * kh * LANES
        q_pairs = jnp.concatenate([q_c[r0:r0 + WINDOW, c0:c0 + LANES],
                                   q_c[r0:r0 + WINDOW, c0 + LANES:c0 + 2 * LANES]], axis=0)
        softmaxed[unit] = _sink_softmax(q_pairs, k_c[r0:r0 + KEY_SPAN, :], kh, bias, sinks_ref)

    def values(unit, blk, kh):
        r0 = blk * WINDOW
        c0 = 2 * kh * LANES
        o = _weighted_values(*softmaxed.pop(unit), v_c[r0:r0 + KEY_SPAN, :], kh)
        attn_ref[r0:r0 + WINDOW, c0:c0 + LANES] = o[:WINDOW].astype(BF16)
        attn_ref[r0:r0 + WINDOW, c0 + LANES:c0 + 2 * LANES] = o[WINDOW:].astype(BF16)

    def pool(groups):
        pos = seq_tile * ROW_TILE + lax.broadcasted_iota(jnp.int32, (ROW_TILE, 1), 0)
        for g in groups:
            w = POOL_WINDOWS[g]
            cols = slice(g * POOL_GROUP, (g + 1) * POOL_GROUP)
            rows = u_c[:, cols]
            acc, span = rows, 1
            while span < w:
                acc = acc + pltpu.roll(acc, span, axis=0)
                span *= 2
            cur = rows[POOL_PAD:]
            cnt = jnp.minimum(pos + 1, w).astype(F32)
            pool_ref[:, cols] = (acc[POOL_PAD:] / cnt - cur).astype(BF16)

    def sequence_state():
        kt_ref[0] = k_c[ROW_TILE:ROW_TILE + WINDOW, :].T
        vt_ref[0] = v_c[ROW_TILE:ROW_TILE + WINDOW, :].T
        ulast_ref[0] = u_c[ROW_TILE:ROW_TILE + POOL_PAD, :]

    work = {"pool_wide": lambda: pool((3,)), "pool_narrow": lambda: pool((0, 1, 2)),
            "sequence_state": sequence_state}
    for blk in range(ROW_TILE // WINDOW):
        for kh in range(N_KV_HEADS):
            unit = blk * N_KV_HEADS + kh
            work["scores", unit] = lambda unit=unit, blk=blk, kh=kh: scores(unit, blk, kh)
            work["values", unit] = lambda unit=unit, blk=blk, kh=kh: values(unit, blk, kh)
    return work


def _prompt_bias():
    r = np.arange(2 * WINDOW)[:, None] % WINDOW
    c = np.arange(2 * KEY_SPAN)[None, :] % KEY_SPAN
    valid = (r <= c) & (c <= r + WINDOW)
    first = valid & (c >= WINDOW)
    return np.where(np.stack([valid, first]), 0.0, NEG_INF).astype(np.float32)


def _front_kernel(sinks_ref, xp_ref, xs_ref, w1_hbm, w3_hbm, w2_hbm, win_hbm, g_ref, b_ref, freq_ref, bias_ref,
                  h_ref, pa_ref, us_ref, qs_ref, ks_ref, vs_ref, kt_ref, vt_ref, ulast_ref,
                  z_c, q_c, k_c, v_c, u_c, h_s, rope_ref, w1_ref, w3_ref, w2_ref, wu_ref,
                  buf_ff, buf_narrow, sem_ff, sem_narrow, sem_model):
    r = pl.program_id(0) - 1
    prev = r - 1
    pool_ref = pa_ref.at[:, 0:POOL_WIDTH]
    attn_ref = pa_ref.at[:, POOL_WIDTH:POOL_WIDTH + Q_WIDTH]

    def stage_and_reset():
        ring_ff = _ring(buf_ff, sem_ff)
        ring_narrow = _ring(buf_narrow, sem_narrow)
        ring_model = _ring(z_c, sem_model, rows=MODEL_RING_ROWS)
        stager = _WeightStager(_row_chunks(w1_hbm, 0, w1_ref, ring_ff)
                               + _row_chunks(w3_hbm, 0, w3_ref, ring_ff)
                               + _row_chunks(win_hbm, 0, wu_ref.at[:, 0:D_MODEL], ring_model)
                               + _row_chunks(win_hbm, D_MODEL, wu_ref.at[:, D_MODEL:UQKV_WIDTH], ring_narrow)
                               + _row_chunks(w2_hbm, 0, w2_ref, ring_model))
        freq = freq_ref[...]
        lane = lax.broadcasted_iota(jnp.int32, (1, LANES), 1)
        sign = jnp.where((lane & (HEAD_DIM // 2)) == 0, -1.0, 1.0)
        row = lax.broadcasted_iota(jnp.int32, (ROW_TILE, 1), 0)
        per_tile = -(-len(stager.staged) // (TILES_PER_SEQ + 1))
        for tile in range(TILES_PER_SEQ + 1):
            pos = tile * ROW_TILE + row if tile < TILES_PER_SEQ else PAST_LEN + row // DEC_BATCH
            ang = pos.astype(F32) * freq
            rows = slice(tile * ROW_TILE, (tile + 1) * ROW_TILE)
            rope_ref[rows, 0:LANES] = jnp.cos(ang)
            rope_ref[rows, LANES:2 * LANES] = jnp.sin(ang) * sign
            stager.run(per_tile)
        stager.run()
        for ref in (z_c, q_c, k_c, v_c, u_c):
            ref[...] = jnp.zeros(ref.shape, ref.dtype)

    def norm_previous():
        h1 = _layer_norm(z_c[...], g_ref[...], b_ref[...])
        h_ref[...] = h1
        return h1.astype(BF16)

    def rope_tables():
        tile = jnp.clip(prev, 0, PROMPT_TILES)
        table = jnp.where(tile == PROMPT_TILES, TILES_PER_SEQ, tile % TILES_PER_SEQ)
        rows = pl.ds(pl.multiple_of(table * ROW_TILE, ROW_TILE), ROW_TILE)
        cos = rope_ref[rows, 0:LANES]
        lane = lax.broadcasted_iota(jnp.int32, cos.shape, 1)
        return cos, rope_ref[rows, LANES:2 * LANES], (lane & (HEAD_DIM // 2)) == 0

    def project_u(h1b):
        return _dot(h1b, wu_ref[:, 0:POOL_WIDTH])

    def project_q(h1b):
        z = _dot(h1b, wu_ref[:, POOL_WIDTH:POOL_WIDTH + Q_WIDTH])
        tables = rope_tables()
        return [(_rope(z[:, c:c + LANES], *tables) * (HEAD_DIM ** -0.5)).astype(BF16)
                for c in range(0, Q_WIDTH, LANES)]

    def project_kv(h1b):
        z = _dot(h1b, wu_ref[:, POOL_WIDTH + Q_WIDTH:UQKV_WIDTH])
        return _rope(z[:, :KV_WIDTH], *rope_tables()), z[:, KV_WIDTH:]

    def tile_step(x):
        normed = []

        def norm():
            normed.append(norm_previous())

        projected = []

        def project():
            projected.extend([project_u(normed[0]), project_q(normed[0]), *project_kv(normed[0])])

        def carry():
            u, q, k, v = projected
            starts_seq = (prev + TILES_PER_SEQ) % TILES_PER_SEQ == 0
            u_c[0:POOL_PAD, :] = jnp.where(starts_seq, 0.0, u_c[ROW_TILE:ROW_TILE + POOL_PAD, :])
            u_c[POOL_PAD:POOL_PAD + ROW_TILE, :] = u
            for c, qc in enumerate(q):
                q_c[:, c * LANES:(c + 1) * LANES] = qc
            k_c[0:WINDOW, :] = k_c[ROW_TILE:ROW_TILE + WINDOW, :]
            v_c[0:WINDOW, :] = v_c[ROW_TILE:ROW_TILE + WINDOW, :]
            k_c[WINDOW:WINDOW + ROW_TILE, :] = k
            v_c[WINDOW:WINDOW + ROW_TILE, :] = v

        side = {0: norm, FF_SLOTS - 3: project, FF_SLOTS - 1: carry}
        stages = context_stages()
        assert len(stages) < FF_SLOTS - 3
        side.update(enumerate(stages, start=1))
        _swiglu_residual(x, w1_ref, w3_ref, w2_ref, h_s, side, z_c)

    def context_stages():
        context = _prompt_context_work(r - 2, q_c, k_c, v_c, u_c, bias_ref, sinks_ref,
                                       pool_ref, attn_ref, kt_ref, vt_ref, ulast_ref)
        stages = [("scores", 0)]
        for unit in range(1, 8):
            stages += [("scores", unit), ("values", unit - 1)]
        stages.append(("values", 7))

        def pooling_and_state():
            context["pool_wide"]()
            context["pool_narrow"]()
            context["sequence_state"]()

        return [context[stage] for stage in stages] + [pooling_and_state]

    @pl.when(jnp.logical_and(r >= 0, r <= PROMPT_TILES))
    def _():
        tile_step(jnp.where(r == PROMPT_TILES, xs_ref[...], xp_ref[...]))

    @pl.when(r == ROW_TILES)
    def _():
        for work in context_stages():
            work()
        h1b = norm_previous()
        us_ref[...] = project_u(h1b)
        for c, qc in enumerate(project_q(h1b)):
            qs_ref[:, c * LANES:(c + 1) * LANES] = qc
        ks_ref[...], vs_ref[...] = project_kv(h1b)

    pl.when(r < 0)(stage_and_reset)


def _front(xp, xs, w1, w3, w2, w_in, g, b, freq, sinks):
    rows = ROW_TILES * ROW_TILE

    def lagged(width):
        return pl.BlockSpec((ROW_TILE, width), lambda s: (jnp.clip(s - 3, 0, PROMPT_TILES - 1), 0))

    def seq_of_lagged(shape):
        return pl.BlockSpec(shape, lambda s: (jnp.clip(s - 3, 0, PROMPT_TILES - 1) // TILES_PER_SEQ, 0, 0))

    sample = lambda width: pl.BlockSpec((ROW_TILE, width), lambda r: (0, 0))
    hbm = pl.BlockSpec(memory_space=pl.ANY)
    return pl.pallas_call(
        _front_kernel,
        out_shape=(jax.ShapeDtypeStruct((rows, D_MODEL), F32),
                   jax.ShapeDtypeStruct((PROMPT_ROWS, POOL_WIDTH + Q_WIDTH), BF16),
                   jax.ShapeDtypeStruct((SAMPLE_ROWS, POOL_WIDTH), F32),
                   jax.ShapeDtypeStruct((SAMPLE_ROWS, Q_WIDTH), BF16),
                   jax.ShapeDtypeStruct((SAMPLE_ROWS, KV_WIDTH), F32),
                   jax.ShapeDtypeStruct((SAMPLE_ROWS, KV_WIDTH), F32),
                   jax.ShapeDtypeStruct((BATCH, KV_WIDTH, WINDOW), F32),
                   jax.ShapeDtypeStruct((BATCH, KV_WIDTH, WINDOW), F32),
                   jax.ShapeDtypeStruct((BATCH, POOL_PAD, POOL_WIDTH), F32)),
        grid=(ROW_TILES + 2,),
        in_specs=[pl.BlockSpec(memory_space=pltpu.SMEM),
                  pl.BlockSpec((ROW_TILE, D_MODEL), lambda s: (jnp.clip(s - 1, 0, PROMPT_TILES - 1), 0)),
                  _resident((SAMPLE_ROWS, D_MODEL)),
                  hbm, hbm, hbm, hbm, _resident((1, D_MODEL)), _resident((1, D_MODEL)),
                  _resident((1, LANES)), _resident((2, 2 * WINDOW, 2 * KEY_SPAN))],
        out_specs=(pl.BlockSpec((ROW_TILE, D_MODEL), lambda s: (jnp.clip(s - 2, 0, PROMPT_TILES), 0)),
                   lagged(POOL_WIDTH + Q_WIDTH),
                   sample(POOL_WIDTH), sample(Q_WIDTH), sample(KV_WIDTH), sample(KV_WIDTH),
                   seq_of_lagged((1, KV_WIDTH, WINDOW)), seq_of_lagged((1, KV_WIDTH, WINDOW)),
                   seq_of_lagged((1, POOL_PAD, POOL_WIDTH))),
        scratch_shapes=[pltpu.VMEM((ROW_TILE, D_MODEL), F32),
                        pltpu.VMEM((ROW_TILE, Q_WIDTH), BF16),
                        pltpu.VMEM((WINDOW + ROW_TILE, KV_WIDTH), F32),
                        pltpu.VMEM((WINDOW + ROW_TILE, KV_WIDTH), F32),
                        pltpu.VMEM((POOL_PAD + ROW_TILE, POOL_WIDTH), F32),
                        pltpu.VMEM((ROW_TILE, D_FF), BF16),
                        pltpu.VMEM(((TILES_PER_SEQ + 1) * ROW_TILE, 2 * LANES), F32),
                        pltpu.VMEM((D_MODEL, D_FF), BF16), pltpu.VMEM((D_MODEL, D_FF), BF16),
                        pltpu.VMEM((D_FF, D_MODEL), BF16), pltpu.VMEM((D_MODEL, UQKV_WIDTH), BF16)]
                       + _ring_scratch(RING_FF, RING_NARROW),
        compiler_params=pltpu.CompilerParams(dimension_semantics=("arbitrary",),
                                             vmem_limit_bytes=VMEM_LIMIT_BYTES),
        name="front",
    )(sinks, xp, xs, w1, w3, w2, w_in, g, b, freq, jnp.asarray(_prompt_bias()))


def _sample_bias():
    row = np.arange(HEADS_PER_KV * DEC_SEQ * SUB_GROUP)
    row_t = (row // SUB_GROUP) % DEC_SEQ
    row_b = row % SUB_GROUP
    col = np.arange(SUB_GROUP * WINDOW)
    ok_c = (col[None, :] // WINDOW == row_b[:, None]) & (col[None, :] % WINDOW >= row_t[:, None])
    new = np.arange(DEC_SEQ * SUB_GROUP)
    ok_n = (new[None, :] % SUB_GROUP == row_b[:, None]) & (new[None, :] // SUB_GROUP <= row_t[:, None])
    to_bias = lambda ok: np.where(ok, 0.0, NEG_INF).astype(np.float32)
    return to_bias(ok_c), to_bias(ok_n)


def _sample_ctx_kernel(sinks_ref, q0, q1, q2, q3, k0, k1, k2, k3, v0, v1, v2, v3, u0, u1, u2, u3,
                       ckt_ref, cvt_ref, cu_ref, bias_c_ref, bias_n_ref,
                       attn_ref, pool_ref, kt_out, vt_out, pu_out, kbt_s, vbt_s):
    q_t = [q[...].astype(F32) for q in (q0, q1, q2, q3)]
    k_t = [k[...] for k in (k0, k1, k2, k3)]
    v_t = [v[...] for v in (v0, v1, v2, v3)]

    rows_u = [cu_ref[i] for i in range(POOL_BUF)] + [u[...] for u in (u0, u1, u2, u3)]
    for t in range(DEC_SEQ):
        pooled = []
        for g, w in enumerate(POOL_WINDOWS):
            cols = slice(g * POOL_GROUP, (g + 1) * POOL_GROUP)
            cur = rows_u[POOL_BUF + t][:, cols]
            acc = cur
            for j in range(1, w):
                acc = acc + rows_u[POOL_BUF + t - j][:, cols]
            pooled.append(acc / float(w) - cur)
        pool_ref[t] = jnp.concatenate(pooled, axis=1)
    for i in range(POOL_BUF):
        pu_out[i] = rows_u[i + DEC_SEQ]

    bias_c = bias_c_ref[...]
    bias_n = bias_n_ref[...]
    head_of_row = lax.broadcasted_iota(jnp.int32, (bias_c.shape[0], 1), 0) // (DEC_SEQ * SUB_GROUP)
    low = lax.broadcasted_iota(jnp.int32, (1, LANES), 1) < HEAD_DIM
    q_sw = [_swap_halves_wide(q) for q in q_t]
    k_sw = [_swap_halves(k) for k in k_t]
    v_sw = [_swap_halves(v) for v in v_t]
    for sub in range(SEQ_GROUP // SUB_GROUP):
        rows = slice(sub * SUB_GROUP, (sub + 1) * SUB_GROUP)
        for kh in range(N_KV_HEADS):
            def q_piece(t, head):
                src = q_t[t] if head % 2 == 0 else q_sw[t]
                chunk = head // 2
                return jnp.where(low, src[rows, chunk * LANES:(chunk + 1) * LANES], 0.0)

            def kv_first(c):
                return c if kh == 0 else jnp.concatenate([c[HEAD_DIM:], c[:HEAD_DIM]], axis=0)

            def kv_twice(c):
                part = c[kh * HEAD_DIM:(kh + 1) * HEAD_DIM]
                return jnp.concatenate([part, part], axis=0)

            lhs = jnp.concatenate([q_piece(t, kh * HEADS_PER_KV + g)
                                   for g in range(HEADS_PER_KV) for t in range(DEC_SEQ)], axis=0).astype(BF16)
            kcat = jnp.concatenate([kv_first(ckt_ref[sub * SUB_GROUP + b]) for b in range(SUB_GROUP)],
                                   axis=1).astype(BF16)
            vcat = jnp.concatenate([kv_twice(cvt_ref[sub * SUB_GROUP + b]) for b in range(SUB_GROUP)],
                                   axis=1).astype(BF16)
            knew = jnp.concatenate([(k_t[t] if kh == 0 else k_sw[t])[rows] for t in range(DEC_SEQ)],
                                   axis=0).astype(BF16)
            vnew = jnp.concatenate([(jnp.where(low, v_t[t], v_sw[t]) if kh == 0 else
                                     jnp.where(low, v_sw[t], v_t[t]))[rows] for t in range(DEC_SEQ)],
                                   axis=0).astype(BF16)
            s_c = _dot(lhs, kcat) + bias_c
            s_n = _dot_nt(lhs, knew) + bias_n
            sink = jnp.zeros(head_of_row.shape, F32)
            for g in range(HEADS_PER_KV):
                sink = jnp.where(head_of_row == g, sinks_ref[kh * HEADS_PER_KV + g], sink)
            m = jnp.maximum(jnp.maximum(jnp.max(s_c, axis=-1, keepdims=True),
                                        jnp.max(s_n, axis=-1, keepdims=True)), sink)
            p_c = jnp.exp(s_c - m)
            p_n = jnp.exp(s_n - m)
            denom = (jnp.sum(p_c, axis=-1, keepdims=True) + jnp.sum(p_n, axis=-1, keepdims=True)
                     + jnp.exp(sink - m))
            o = (_dot_nt(p_c.astype(BF16), vcat) + _dot(p_n.astype(BF16), vnew)) / denom
            for t in range(DEC_SEQ):
                for pair in range(HEADS_PER_KV // 2):
                    piece = lambda g: o[(g * DEC_SEQ + t) * SUB_GROUP:(g * DEC_SEQ + t + 1) * SUB_GROUP]
                    c0 = (kh * HEADS_PER_KV // 2 + pair) * LANES
                    attn_ref[t, rows, c0:c0 + LANES] = jnp.where(low, piece(2 * pair), piece(2 * pair + 1))

    if DEC_SEQ * SEQ_GROUP < LANES:
        zeros = jnp.zeros((LANES - DEC_SEQ * SEQ_GROUP, KV_WIDTH), F32)
        kbt_s[DEC_SEQ * SEQ_GROUP:, :] = zeros
        vbt_s[DEC_SEQ * SEQ_GROUP:, :] = zeros
    for t in range(DEC_SEQ):
        kbt_s[pl.ds(t, SEQ_GROUP, stride=DEC_SEQ), :] = k_t[t]
        vbt_s[pl.ds(t, SEQ_GROUP, stride=DEC_SEQ), :] = v_t[t]
    knew_t = kbt_s[...].T
    vnew_t = vbt_s[...].T
    keep = lax.broadcasted_iota(jnp.int32, (1, WINDOW), 1) < WINDOW - DEC_SEQ
    for b in range(SEQ_GROUP):
        shift_new = WINDOW - DEC_SEQ - DEC_SEQ * b
        kt_out[b] = jnp.where(keep, pltpu.roll(ckt_ref[b], WINDOW - DEC_SEQ, axis=1),
                              pltpu.roll(knew_t, shift_new, axis=1))
        vt_out[b] = jnp.where(keep, pltpu.roll(cvt_ref[b], WINDOW - DEC_SEQ, axis=1),
                              pltpu.roll(vnew_t, shift_new, axis=1))


def _sample_ctx(us, qs, ks, vs, ckt, cvt, cu, sinks):
    groups = DEC_BATCH // SEQ_GROUP

    def token_rows(t, width):
        return pl.BlockSpec((SEQ_GROUP, width), lambda i: (t * groups + i, 0))

    def per_token(width):
        return [token_rows(t, width) for t in range(DEC_SEQ)]

    cache_spec = pl.BlockSpec((SEQ_GROUP, KV_WIDTH, WINDOW), lambda i: (i, 0, 0))
    pool_rows_spec = pl.BlockSpec((POOL_BUF, SEQ_GROUP, POOL_WIDTH), lambda i: (0, i, 0))
    by_token = lambda width: pl.BlockSpec((DEC_SEQ, SEQ_GROUP, width), lambda i: (0, i, 0))
    bias_c, bias_n = _sample_bias()
    return pl.pallas_call(
        _sample_ctx_kernel,
        out_shape=(jax.ShapeDtypeStruct((DEC_SEQ, DEC_BATCH, Q_WIDTH), F32),
                   jax.ShapeDtypeStruct((DEC_SEQ, DEC_BATCH, POOL_WIDTH), F32),
                   jax.ShapeDtypeStruct((DEC_BATCH, KV_WIDTH, WINDOW), F32),
                   jax.ShapeDtypeStruct((DEC_BATCH, KV_WIDTH, WINDOW), F32),
                   jax.ShapeDtypeStruct((POOL_BUF, DEC_BATCH, POOL_WIDTH), F32)),
        grid=(groups,),
        in_specs=[pl.BlockSpec(memory_space=pltpu.SMEM)]
                 + per_token(Q_WIDTH) + per_token(KV_WIDTH) + per_token(KV_WIDTH) + per_token(POOL_WIDTH)
                 + [cache_spec, cache_spec, pool_rows_spec, _resident(bias_c.shape), _resident(bias_n.shape)],
        out_specs=(by_token(Q_WIDTH), by_token(POOL_WIDTH), cache_spec, cache_spec, pool_rows_spec),
        scratch_shapes=[pltpu.VMEM((LANES, KV_WIDTH), F32), pltpu.VMEM((LANES, KV_WIDTH), F32)],
        compiler_params=pltpu.CompilerParams(dimension_semantics=("parallel",),
                                             vmem_limit_bytes=VMEM_LIMIT_BYTES),
        name="sample_ctx",
    )(sinks, *([qs] * DEC_SEQ), *([ks] * DEC_SEQ), *([vs] * DEC_SEQ), *([us] * DEC_SEQ),
      ckt, cvt, cu, jnp.asarray(bias_c), jnp.asarray(bias_n))


def _back_kernel(h_ref, pa_ref, pools_ref, attns_ref,
                 win_hbm, wgrp_hbm, scale_ref, wpo_hbm, wao_hbm, wout_hbm, g2_ref, b2_ref,
                 w1_hbm, w3_hbm, w2_hbm, g3_ref, b3_ref, yp_ref, ys_ref, z2_c, z3_c, h_s,
                 wg_ref, wgrp_ref, wpo_ref, wao_ref, wout_ref, w1_ref, w3_ref, w2_ref,
                 buf_ff, buf_group, sem_ff, sem_group, sem_model):
    t = pl.program_id(0)

    def branch_outputs():
        is_sample = t >= PROMPT_TILES
        pool_in = jnp.where(is_sample, pools_ref[...].astype(BF16), pa_ref[:, 0:POOL_WIDTH])
        attn_o = jnp.where(is_sample, attns_ref[...].astype(BF16), pa_ref[:, POOL_WIDTH:POOL_WIDTH + Q_WIDTH])
        zs = [_dot(pool_in[:, g * POOL_GROUP:(g + 1) * POOL_GROUP],
                   wgrp_ref[g * POOL_GROUP:(g + 1) * POOL_GROUP, :])
              for g in range(len(POOL_WINDOWS))]
        pool_z = jnp.concatenate(zs, axis=1) * scale_ref[...]
        return _dot(pool_z.astype(BF16), wpo_ref[...]), _dot(attn_o, wao_ref[...])

    def gate_and_merge(h1b, a, b):
        gates = jax.nn.sigmoid(_dot(h1b, wg_ref[...]))
        return (gates[:, :D_MODEL] * a + gates[:, D_MODEL:] * b).astype(BF16)

    def project_out(h1, merged):
        z2_c[...] = ALPHA * h1 + _dot(merged, wout_ref[...])

    @pl.when(jnp.logical_and(t >= 1, t <= ROW_TILES))
    def _():
        h1 = h_ref[...]
        a, b = branch_outputs()
        y = _layer_norm(z3_c[...], g3_ref[...], b3_ref[...])
        yp_ref[...] = y
        h2 = _layer_norm(z2_c[...], g2_ref[...], b2_ref[...])
        h1b = jnp.where(t <= ROW_TILES, h1.astype(BF16), y.astype(BF16))
        merged = []
        side = {1: lambda: merged.append(gate_and_merge(h1b, a, b)), 9: lambda: project_out(h1, merged[0])}
        _swiglu_residual(h2, w1_ref, w3_ref, w2_ref, h_s, side, z3_c)

    @pl.when(t == ROW_TILES + 1)
    def _():
        ys_ref[...] = _layer_norm(z3_c[...], g3_ref[...], b3_ref[...])

    @pl.when(t == 0)
    def _():
        ring_ff = _ring(buf_ff, sem_ff)
        ring_group = _ring(buf_group, sem_group)
        ring_model = _ring(z3_c, sem_model, rows=MODEL_RING_ROWS)
        mixer = (_row_chunks(wgrp_hbm, 0, wgrp_ref, ring_group)
                 + _row_chunks(wpo_hbm, 0, wpo_ref, ring_model)
                 + _row_chunks(wao_hbm, 0, wao_ref, ring_model)
                 + _row_chunks(win_hbm, UQKV_WIDTH, wg_ref.at[:, 0:D_MODEL], ring_model)
                 + _row_chunks(win_hbm, UQKV_WIDTH + D_MODEL, wg_ref.at[:, D_MODEL:2 * D_MODEL], ring_model)
                 + _row_chunks(wout_hbm, 0, wout_ref, ring_model))
        swiglu = (_row_chunks(w1_hbm, 0, w1_ref, ring_ff) + _row_chunks(w3_hbm, 0, w3_ref, ring_ff)
                  + _row_chunks(w2_hbm, 0, w2_ref, ring_model))
        stager = _WeightStager(mixer + swiglu)
        stager.run(len(mixer))
        h1 = h_ref[...]
        a, b = branch_outputs()
        stager.run(len(swiglu) // 3)
        merged = gate_and_merge(h1.astype(BF16), a, b)
        stager.run(len(swiglu) // 3)
        project_out(h1, merged)
        stager.run()
        z3_c[...] = jnp.zeros(z3_c.shape, z3_c.dtype)


def _back(h1, pool_attn, pool_s, attn_s, w_in, wgrp, scale, wpo, wao, wout, g2, b2, w1, w3, w2, g3, b3):
    lagged = pl.BlockSpec((ROW_TILE, D_MODEL), lambda t: (jnp.clip(t - 2, 0, PROMPT_TILES - 1), 0))
    hbm = pl.BlockSpec(memory_space=pl.ANY)
    return pl.pallas_call(
        _back_kernel,
        out_shape=(jax.ShapeDtypeStruct((PROMPT_ROWS, D_MODEL), F32),
                   jax.ShapeDtypeStruct((SAMPLE_ROWS, D_MODEL), F32)),
        grid=(ROW_TILES + 2,),
        in_specs=[pl.BlockSpec((ROW_TILE, D_MODEL), lambda t: (jnp.minimum(t, PROMPT_TILES), 0)),
                  pl.BlockSpec((ROW_TILE, POOL_WIDTH + Q_WIDTH), lambda t: (jnp.minimum(t, PROMPT_TILES - 1), 0)),
                  _resident((SAMPLE_ROWS, POOL_WIDTH)), _resident((SAMPLE_ROWS, Q_WIDTH)),
                  hbm, hbm, _resident((1, POOL_WIDTH)), hbm, hbm, hbm,
                  _resident((1, D_MODEL)), _resident((1, D_MODEL)),
                  hbm, hbm, hbm, _resident((1, D_MODEL)), _resident((1, D_MODEL))],
        out_specs=(lagged, pl.BlockSpec((ROW_TILE, D_MODEL), lambda r: (0, 0))),
        scratch_shapes=[pltpu.VMEM((ROW_TILE, D_MODEL), F32), pltpu.VMEM((ROW_TILE, D_MODEL), F32),
                        pltpu.VMEM((ROW_TILE, D_FF), BF16),
                        pltpu.VMEM((D_MODEL, 2 * D_MODEL), BF16),
                        pltpu.VMEM((len(POOL_WINDOWS) * POOL_GROUP, POOL_GROUP), BF16),
                        pltpu.VMEM((POOL_WIDTH, D_MODEL), BF16), pltpu.VMEM((Q_WIDTH, D_MODEL), BF16),
                        pltpu.VMEM((D_MODEL, D_MODEL), BF16),
                        pltpu.VMEM((D_MODEL, D_FF), BF16), pltpu.VMEM((D_MODEL, D_FF), BF16),
                        pltpu.VMEM((D_FF, D_MODEL), BF16)]
                       + _ring_scratch(RING_FF, RING_GROUP),
        compiler_params=pltpu.CompilerParams(dimension_semantics=("arbitrary",),
                                             vmem_limit_bytes=VMEM_LIMIT_BYTES),
        name="back",
    )(h1, pool_attn, pool_s, attn_s, w_in, wgrp, scale, wpo, wao, wout, g2, b2, w1, w3, w2, g3, b3)


def kernel(x_prompt, x_sample, cache_pool_u, cache_k_win, cache_v_win, w_in, pool_w_grp, pool_scale,
           attn_sinks, w_pool_out, w_attn_out, w_out, ffn1_w1, ffn1_w3, ffn1_w2, ffn2_w1, ffn2_w3,
           ffn2_w2, ln1_g, ln1_b, ln2_g, ln2_b, ln3_g, ln3_b):
    assert DEPTH == 1 and w_in.shape[0] == 1
    l = 0
    vec = lambda p: p[l].reshape(1, -1)
    sinks = attn_sinks[l]

    freq = jnp.tile(ROPE_THETA ** (-2.0 * jnp.arange(HEAD_DIM // 2, dtype=F32) / HEAD_DIM), 4).reshape(1, LANES)

    xp = x_prompt.reshape(PROMPT_ROWS, D_MODEL)
    xs = jnp.transpose(x_sample, (1, 0, 2)).reshape(SAMPLE_ROWS, D_MODEL)
    (h1, pool_attn, us, qs, ks, vs, kt_last, vt_last, u_last) = _front(
        xp, xs, ffn1_w1, ffn1_w3, ffn1_w2, w_in, vec(ln1_g), vec(ln1_b), freq, sinks)

    to_t = lambda c: jnp.transpose(c[l], (0, 2, 3, 1)).reshape(DEC_BATCH, KV_WIDTH, WINDOW)
    cu = jnp.transpose(cache_pool_u[l], (1, 0, 2))
    attn_s, pool_s, kt_s, vt_s, pu_s = _sample_ctx(us, qs, ks, vs, to_t(cache_k_win), to_t(cache_v_win), cu, sinks)

    wgrp = pool_w_grp.reshape(DEPTH, len(POOL_WINDOWS) * POOL_GROUP, POOL_GROUP)
    yp, ys = _back(h1, pool_attn, pool_s.reshape(SAMPLE_ROWS, POOL_WIDTH), attn_s.reshape(SAMPLE_ROWS, Q_WIDTH),
                   w_in, wgrp, vec(pool_scale), w_pool_out, w_attn_out, w_out, vec(ln2_g), vec(ln2_b),
                   ffn2_w1, ffn2_w3, ffn2_w2, vec(ln3_g), vec(ln3_b))
    yp = yp.reshape(BATCH, SEQ, D_MODEL)
    ys = jnp.transpose(ys.reshape(DEC_SEQ, DEC_BATCH, D_MODEL), (1, 0, 2))

    from_t = lambda c, n: jnp.transpose(c.reshape(n, N_KV_HEADS, HEAD_DIM, WINDOW), (0, 3, 1, 2))[None]
    pool_u_prompt = u_last[None, :, POOL_PAD - POOL_BUF:]
    pool_u_sample = jnp.transpose(pu_s, (1, 0, 2))[None]
    return (yp, ys, pool_u_prompt, from_t(kt_last, BATCH), from_t(vt_last, BATCH),
            pool_u_sample, from_t(kt_s, DEC_BATCH), from_t(vt_s, DEC_BATCH))
```

```python
import jax
import jax.numpy as jnp
import numpy as np
from jax import lax
from jax.experimental import pallas as pl
from jax.experimental.pallas import tpu as pltpu

D_MODEL = 1024
BATCH = 8
SEQ = 2048
DEC_BATCH = 128
DEC_SEQ = 4
PAST_LEN = 8192
POOL_WINDOWS = (2, 4, 8, 16)
POOL_GROUP = 128
POOL_WIDTH = 512
POOL_BUF = 15
N_HEADS = 8
N_KV_HEADS = 2
HEADS_PER_KV = N_HEADS // N_KV_HEADS
HEAD_DIM = 64
Q_WIDTH = 512
KV_WIDTH = 128
WINDOW = 128
ROPE_THETA = 10000.0
D_FF = 2816
DEPTH = 1
ALPHA = (2.0 * DEPTH) ** 0.25
LN_EPS = 1e-5
NEG_INF = -1e30
UQKV_WIDTH = POOL_WIDTH + Q_WIDTH + 2 * KV_WIDTH

LANES = 128
KEY_SPAN = 2 * WINDOW
VMEM_LIMIT_BYTES = 61 * 1024 * 1024

ROW_TILE = 512
TILES_PER_SEQ = SEQ // ROW_TILE
PROMPT_ROWS = BATCH * SEQ
PROMPT_TILES = PROMPT_ROWS // ROW_TILE
SAMPLE_ROWS = DEC_BATCH * DEC_SEQ
ROW_TILES = PROMPT_TILES + 1
POOL_PAD = 16
SEQ_GROUP = 32
SUB_GROUP = 8
FF_CHUNK = 256
OUT_BLOCK = 256

BF16 = jnp.bfloat16
F32 = jnp.float32


def _dot(a, b):
    return jnp.dot(a, b, preferred_element_type=F32)


def _dot_nt(a, b):
    return lax.dot_general(a, b, (((1,), (1,)), ((), ())), preferred_element_type=F32)


def _layer_norm(y, g, b):
    mu = jnp.mean(y, axis=-1, keepdims=True)
    yc = y - mu
    var = jnp.mean(yc * yc, axis=-1, keepdims=True)
    return yc * lax.rsqrt(var + LN_EPS) * g + b


def _resident(shape):
    nd = len(shape)
    return pl.BlockSpec(shape, lambda *_: (0,) * nd, pipeline_mode=pl.Buffered(1))


class _WeightStager:
    def __init__(self, jobs):
        uses, self.staged = {}, []
        for src, ring, dst in jobs:
            slot, sem = ring[uses.get(id(ring), 0) % len(ring)]
            uses[id(ring)] = uses.get(id(ring), 0) + 1
            self.staged.append((pltpu.make_async_copy(src, slot, sem), slot, dst))
        assert all(len(ring) >= min(STAGE_IN_FLIGHT, uses[id(ring)]) for _, ring, _ in jobs)
        self.done = 0
        for copy, _, _ in self.staged[:STAGE_IN_FLIGHT]:
            copy.start()

    def run(self, count=None):
        end = len(self.staged) if count is None else min(self.done + count, len(self.staged))
        for i in range(self.done, end):
            copy, slot, dst = self.staged[i]
            copy.wait()
            dst[...] = slot[...].astype(BF16)
            if i + STAGE_IN_FLIGHT < len(self.staged):
                self.staged[i + STAGE_IN_FLIGHT][0].start()
        self.done = end


def _ring(buf, sems, rows=None):
    if rows is None:
        return [(buf.at[i], sems.at[i]) for i in range(buf.shape[0])]
    return [(buf.at[pl.ds(i * rows, rows), :], sems.at[i]) for i in range(buf.shape[0] // rows)]


def _row_chunks(w_hbm, col0, dst, ring):
    rows, cols = dst.shape
    step, width = ring[0][0].shape
    assert rows % step == 0 and width == cols
    return [(w_hbm.at[0, pl.ds(r0, step), pl.ds(col0, cols)], ring, dst.at[pl.ds(r0, step), :])
            for r0 in range(0, rows, step)]


STAGE_IN_FLIGHT = 4
RING_FF = (STAGE_IN_FLIGHT, 64, D_FF)
MODEL_RING_ROWS = ROW_TILE // STAGE_IN_FLIGHT
RING_NARROW = (STAGE_IN_FLIGHT, 128, UQKV_WIDTH - D_MODEL)
RING_GROUP = (1, len(POOL_WINDOWS) * POOL_GROUP, POOL_GROUP)


def _ring_scratch(*rings):
    return ([pltpu.VMEM(ring, F32) for ring in rings]
            + [pltpu.SemaphoreType.DMA((ring[0],)) for ring in rings]
            + [pltpu.SemaphoreType.DMA((STAGE_IN_FLIGHT,))])


FF_SLOTS = 2 * (D_FF // FF_CHUNK)


def _swiglu_residual(x, w1_ref, w3_ref, w2_ref, h_s, side_work, out_ref):
    assert all(0 <= slot <= FF_SLOTS for slot in side_work)
    run = lambda slot: side_work.get(slot, lambda: None)()
    xb = x.astype(BF16)
    for j in range(D_FF // FF_CHUNK):
        cols = slice(j * FF_CHUNK, (j + 1) * FF_CHUNK)
        a = _dot(xb, w1_ref[:, cols])
        run(2 * j)
        b = _dot(xb, w3_ref[:, cols])
        h_s[:, cols] = ((a * jax.nn.sigmoid(a)) * b).astype(BF16)
        run(2 * j + 1)
    run(FF_SLOTS)
    h = h_s[...]
    for c0 in range(0, D_MODEL, OUT_BLOCK):
        cols = slice(c0, c0 + OUT_BLOCK)
        out_ref[:, cols] = ALPHA * x[:, cols] + 0.5 * _dot(h, w2_ref[:, cols])


def _rope(x, cos, sin_signed, first_half):
    fwd = pltpu.roll(x, LANES - HEAD_DIM // 2, axis=1)
    bwd = pltpu.roll(x, HEAD_DIM // 2, axis=1)
    return x * cos + jnp.where(first_half, fwd, bwd) * sin_signed


def _swap_halves(x):
    return pltpu.roll(x, HEAD_DIM, axis=1)


def _swap_halves_wide(x):
    return jnp.concatenate([_swap_halves(x[:, c:c + LANES]) for c in range(0, x.shape[1], LANES)], axis=1)


def _lane_split(x, x_sw, kh):
    low = lax.broadcasted_iota(jnp.int32, (1, LANES), 1) < HEAD_DIM
    lo, hi = (x, x_sw) if kh == 0 else (x_sw, x)
    return jnp.concatenate([jnp.where(low, lo, 0.0), jnp.where(low, 0.0, hi)], axis=0).astype(BF16)


def _sink_softmax(q_pairs, keys, kh, bias, sinks_ref):
    s = _dot_nt(q_pairs, _lane_split(keys, _swap_halves(keys), kh)) + bias
    second_pair = lax.broadcasted_iota(jnp.int32, (s.shape[0], 1), 0) >= WINDOW
    probs, denoms = [], []
    for c in range(2):
        sc = s[:, c * KEY_SPAN:(c + 1) * KEY_SPAN]
        head = kh * HEADS_PER_KV + c
        sink = jnp.where(second_pair, sinks_ref[head + 2], sinks_ref[head])
        m = jnp.maximum(jnp.max(sc, axis=-1, keepdims=True), sink)
        p = jnp.exp(sc - m)
        denoms.append(jnp.sum(p, axis=-1, keepdims=True) + jnp.exp(sink - m))
        probs.append(p.astype(BF16))
    return jnp.concatenate(probs, axis=1), denoms


def _weighted_values(probs, denoms, vals, kh):
    low = lax.broadcasted_iota(jnp.int32, (1, LANES), 1) < HEAD_DIM
    o = _dot(probs, _lane_split(vals, _swap_halves(vals), kh))
    return o / jnp.where(low, denoms[0], denoms[1])


def _prompt_context_work(tile, q_c, k_c, v_c, u_c, bias_ref, sinks_ref,
                         pool_ref, attn_ref, kt_ref, vt_ref, ulast_ref):
    seq_tile = (tile + TILES_PER_SEQ) % TILES_PER_SEQ
    first_tile = seq_tile == 0
    softmaxed = {}

    def scores(unit, blk, kh):
        r0 = blk * WINDOW
        bias = bias_ref[jnp.where(first_tile, 1, 0)] if blk == 0 else bias_ref[0]
        c0 = 2 * kh * LANES
        q_pairs = jnp.concatenate([q_c[r0:r0 + WINDOW, c0:c0 + LANES],
                                   q_c[r0:r0 + WINDOW, c0 + LANES:c0 + 2 * LANES]], axis=0)
        softmaxed[unit] = _sink_softmax(q_pairs, k_c[r0:r0 + KEY_SPAN, :], kh, bias, sinks_ref)

    def values(unit, blk, kh):
        r0 = blk * WINDOW
        c0 = 2 * kh * LANES
        o = _weighted_values(*softmaxed.pop(unit), v_c[r0:r0 + KEY_SPAN, :], kh)
        attn_ref[r0:r0 + WINDOW, c0:c0 + LANES] = o[:WINDOW].astype(BF16)
        attn_ref[r0:r0 + WINDOW, c0 + LANES:c0 + 2 * LANES] = o[WINDOW:].astype(BF16)

    def pool(groups):
        pos = seq_tile * ROW_TILE + lax.broadcasted_iota(jnp.int32, (ROW_TILE, 1), 0)
        for g in groups:
            w = POOL_WINDOWS[g]
            cols = slice(g * POOL_GROUP, (g + 1) * POOL_GROUP)
            rows = u_c[:, cols]
            acc, span = rows, 1
            while span < w:
                acc = acc + pltpu.roll(acc, span, axis=0)
                span *= 2
            cur = rows[POOL_PAD:]
            cnt = jnp.minimum(pos + 1, w).astype(F32)
            pool_ref[:, cols] = (acc[POOL_PAD:] / cnt - cur).astype(BF16)

    def sequence_state():
        kt_ref[0] = k_c[ROW_TILE:ROW_TILE + WINDOW, :].T
        vt_ref[0] = v_c[ROW_TILE:ROW_TILE + WINDOW, :].T
        ulast_ref[0] = u_c[ROW_TILE:ROW_TILE + POOL_PAD, :]

    work = {"pool_wide": lambda: pool((3,)), "pool_narrow": lambda: pool((0, 1, 2)),
            "sequence_state": sequence_state}
    for blk in range(ROW_TILE // WINDOW):
        for kh in range(N_KV_HEADS):
            unit = blk * N_KV_HEADS + kh
            work["scores", unit] = lambda unit=unit, blk=blk, kh=kh: scores(unit, blk, kh)
            work["values", unit] = lambda unit=unit, blk=blk, kh=kh: values(unit, blk, kh)
    return work


def _prompt_bias():
    r = np.arange(2 * WINDOW)[:, None] % WINDOW
    c = np.arange(2 * KEY_SPAN)[None, :] % KEY_SPAN
    valid = (r <= c) & (c <= r + WINDOW)
    first = valid & (c >= WINDOW)
    return np.where(np.stack([valid, first]), 0.0, NEG_INF).astype(np.float32)


def _front_kernel(sinks_ref, xp_ref, xs_ref, w1_hbm, w3_hbm, w2_hbm, win_hbm, g_ref, b_ref, freq_ref, bias_ref,
                  h_ref, pa_ref, us_ref, qs_ref, ks_ref, vs_ref, kt_ref, vt_ref, ulast_ref,
                  z_c, q_c, k_c, v_c, u_c, h_s, rope_ref, w1_ref, w3_ref, w2_ref, wu_ref,
                  buf_ff, buf_narrow, sem_ff, sem_narrow, sem_model):
    r = pl.program_id(0) - 1
    prev = r - 1
    pool_ref = pa_ref.at[:, 0:POOL_WIDTH]
    attn_ref = pa_ref.at[:, POOL_WIDTH:POOL_WIDTH + Q_WIDTH]

    def stage_and_reset():
        ring_ff = _ring(buf_ff, sem_ff)
        ring_narrow = _ring(buf_narrow, sem_narrow)
        ring_model = _ring(z_c, sem_model, rows=MODEL_RING_ROWS)
        stager = _WeightStager(_row_chunks(w1_hbm, 0, w1_ref, ring_ff)
                               + _row_chunks(w3_hbm, 0, w3_ref, ring_ff)
                               + _row_chunks(win_hbm, 0, wu_ref.at[:, 0:D_MODEL], ring_model)
                               + _row_chunks(win_hbm, D_MODEL, wu_ref.at[:, D_MODEL:UQKV_WIDTH], ring_narrow)
                               + _row_chunks(w2_hbm, 0, w2_ref, ring_model))
        freq = freq_ref[...]
        lane = lax.broadcasted_iota(jnp.int32, (1, LANES), 1)
        sign = jnp.where((lane & (HEAD_DIM // 2)) == 0, -1.0, 1.0)
        row = lax.broadcasted_iota(jnp.int32, (ROW_TILE, 1), 0)
        per_tile = -(-len(stager.staged) // (TILES_PER_SEQ + 1))
        for tile in range(TILES_PER_SEQ + 1):
            pos = tile * ROW_TILE + row if tile < TILES_PER_SEQ else PAST_LEN + row // DEC_BATCH
            ang = pos.astype(F32) * freq
            rows = slice(tile * ROW_TILE, (tile + 1) * ROW_TILE)
            rope_ref[rows, 0:LANES] = jnp.cos(ang)
            rope_ref[rows, LANES:2 * LANES] = jnp.sin(ang) * sign
            stager.run(per_tile)
        stager.run()
        for ref in (z_c, q_c, k_c, v_c, u_c):
            ref[...] = jnp.zeros(ref.shape, ref.dtype)

    def norm_previous():
        h1 = _layer_norm(z_c[...], g_ref[...], b_ref[...])
        h_ref[...] = h1
        return h1.astype(BF16)

    def rope_tables():
        tile = jnp.clip(prev, 0, PROMPT_TILES)
        table = jnp.where(tile == PROMPT_TILES, TILES_PER_SEQ, tile % TILES_PER_SEQ)
        rows = pl.ds(pl.multiple_of(table * ROW_TILE, ROW_TILE), ROW_TILE)
        cos = rope_ref[rows, 0:LANES]
        lane = lax.broadcasted_iota(jnp.int32, cos.shape, 1)
        return cos, rope_ref[rows, LANES:2 * LANES], (lane & (HEAD_DIM // 2)) == 0

    def project_u(h1b):
        return _dot(h1b, wu_ref[:, 0:POOL_WIDTH])

    def project_q(h1b):
        z = _dot(h1b, wu_ref[:, POOL_WIDTH:POOL_WIDTH + Q_WIDTH])
        tables = rope_tables()
        return [(_rope(z[:, c:c + LANES], *tables) * (HEAD_DIM ** -0.5)).astype(BF16)
                for c in range(0, Q_WIDTH, LANES)]

    def project_kv(h1b):
        z = _dot(h1b, wu_ref[:, POOL_WIDTH + Q_WIDTH:UQKV_WIDTH])
        return _rope(z[:, :KV_WIDTH], *rope_tables()), z[:, KV_WIDTH:]

    def tile_step(x):
        normed = []

        def norm():
            normed.append(norm_previous())

        projected = []

        def project():
            projected.extend([project_u(normed[0]), project_q(normed[0]), *project_kv(normed[0])])

        def carry():
            u, q, k, v = projected
            starts_seq = (prev + TILES_PER_SEQ) % TILES_PER_SEQ == 0
            u_c[0:POOL_PAD, :] = jnp.where(starts_seq, 0.0, u_c[ROW_TILE:ROW_TILE + POOL_PAD, :])
            u_c[POOL_PAD:POOL_PAD + ROW_TILE, :] = u
            for c, qc in enumerate(q):
                q_c[:, c * LANES:(c + 1) * LANES] = qc
            k_c[0:WINDOW, :] = k_c[ROW_TILE:ROW_TILE + WINDOW, :]
            v_c[0:WINDOW, :] = v_c[ROW_TILE:ROW_TILE + WINDOW, :]
            k_c[WINDOW:WINDOW + ROW_TILE, :] = k
            v_c[WINDOW:WINDOW + ROW_TILE, :] = v

        side = {0: norm, FF_SLOTS - 3: project, FF_SLOTS - 1: carry}
        stages = context_stages()
        assert len(stages) < FF_SLOTS - 3
        side.update(enumerate(stages, start=1))
        _swiglu_residual(x, w1_ref, w3_ref, w2_ref, h_s, side, z_c)

    def context_stages():
        context = _prompt_context_work(r - 2, q_c, k_c, v_c, u_c, bias_ref, sinks_ref,
                                       pool_ref, attn_ref, kt_ref, vt_ref, ulast_ref)
        stages = [("scores", 0)]
        for unit in range(1, 8):
            stages += [("scores", unit), ("values", unit - 1)]
        stages.append(("values", 7))

        def pooling_and_state():
            context["pool_wide"]()
            context["pool_narrow"]()
            context["sequence_state"]()

        return [context[stage] for stage in stages] + [pooling_and_state]

    @pl.when(jnp.logical_and(r >= 0, r <= PROMPT_TILES))
    def _():
        tile_step(jnp.where(r == PROMPT_TILES, xs_ref[...], xp_ref[...]))

    @pl.when(r == ROW_TILES)
    def _():
        for work in context_stages():
            work()
        h1b = norm_previous()
        us_ref[...] = project_u(h1b)
        for c, qc in enumerate(project_q(h1b)):
            qs_ref[:, c * LANES:(c + 1) * LANES] = qc
        ks_ref[...], vs_ref[...] = project_kv(h1b)

    pl.when(r < 0)(stage_and_reset)


def _front(xp, xs, w1, w3, w2, w_in, g, b, freq, sinks):
    rows = ROW_TILES * ROW_TILE

    def lagged(width):
        return pl.BlockSpec((ROW_TILE, width), lambda s: (jnp.clip(s - 3, 0, PROMPT_TILES - 1), 0))

    def seq_of_lagged(shape):
        return pl.BlockSpec(shape, lambda s: (jnp.clip(s - 3, 0, PROMPT_TILES - 1) // TILES_PER_SEQ, 0, 0))

    sample = lambda width: pl.BlockSpec((ROW_TILE, width), lambda r: (0, 0))
    hbm = pl.BlockSpec(memory_space=pl.ANY)
    return pl.pallas_call(
        _front_kernel,
        out_shape=(jax.ShapeDtypeStruct((rows, D_MODEL), F32),
                   jax.ShapeDtypeStruct((PROMPT_ROWS, POOL_WIDTH + Q_WIDTH), BF16),
                   jax.ShapeDtypeStruct((SAMPLE_ROWS, POOL_WIDTH), F32),
                   jax.ShapeDtypeStruct((SAMPLE_ROWS, Q_WIDTH), BF16),
                   jax.ShapeDtypeStruct((SAMPLE_ROWS, KV_WIDTH), F32),
                   jax.ShapeDtypeStruct((SAMPLE_ROWS, KV_WIDTH), F32),
                   jax.ShapeDtypeStruct((BATCH, KV_WIDTH, WINDOW), F32),
                   jax.ShapeDtypeStruct((BATCH, KV_WIDTH, WINDOW), F32),
                   jax.ShapeDtypeStruct((BATCH, POOL_PAD, POOL_WIDTH), F32)),
        grid=(ROW_TILES + 2,),
        in_specs=[pl.BlockSpec(memory_space=pltpu.SMEM),
                  pl.BlockSpec((ROW_TILE, D_MODEL), lambda s: (jnp.clip(s - 1, 0, PROMPT_TILES - 1), 0)),
                  _resident((SAMPLE_ROWS, D_MODEL)),
                  hbm, hbm, hbm, hbm, _resident((1, D_MODEL)), _resident((1, D_MODEL)),
                  _resident((1, LANES)), _resident((2, 2 * WINDOW, 2 * KEY_SPAN))],
        out_specs=(pl.BlockSpec((ROW_TILE, D_MODEL), lambda s: (jnp.clip(s - 2, 0, PROMPT_TILES), 0)),
                   lagged(POOL_WIDTH + Q_WIDTH),
                   sample(POOL_WIDTH), sample(Q_WIDTH), sample(KV_WIDTH), sample(KV_WIDTH),
                   seq_of_lagged((1, KV_WIDTH, WINDOW)), seq_of_lagged((1, KV_WIDTH, WINDOW)),
                   seq_of_lagged((1, POOL_PAD, POOL_WIDTH))),
        scratch_shapes=[pltpu.VMEM((ROW_TILE, D_MODEL), F32),
                        pltpu.VMEM((ROW_TILE, Q_WIDTH), BF16),
                        pltpu.VMEM((WINDOW + ROW_TILE, KV_WIDTH), F32),
                        pltpu.VMEM((WINDOW + ROW_TILE, KV_WIDTH), F32),
                        pltpu.VMEM((POOL_PAD + ROW_TILE, POOL_WIDTH), F32),
                        pltpu.VMEM((ROW_TILE, D_FF), BF16),
                        pltpu.VMEM(((TILES_PER_SEQ + 1) * ROW_TILE, 2 * LANES), F32),
                        pltpu.VMEM((D_MODEL, D_FF), BF16), pltpu.VMEM((D_MODEL, D_FF), BF16),
                        pltpu.VMEM((D_FF, D_MODEL), BF16), pltpu.VMEM((D_MODEL, UQKV_WIDTH), BF16)]
                       + _ring_scratch(RING_FF, RING_NARROW),
        compiler_params=pltpu.CompilerParams(dimension_semantics=("arbitrary",),
                                             vmem_limit_bytes=VMEM_LIMIT_BYTES),
        name="front",
    )(sinks, xp, xs, w1, w3, w2, w_in, g, b, freq, jnp.asarray(_prompt_bias()))


def _sample_bias():
    row = np.arange(HEADS_PER_KV * DEC_SEQ * SUB_GROUP)
    row_t = (row // SUB_GROUP) % DEC_SEQ
    row_b = row % SUB_GROUP
    col = np.arange(SUB_GROUP * WINDOW)
    ok_c = (col[None, :] // WINDOW == row_b[:, None]) & (col[None, :] % WINDOW >= row_t[:, None])
    new = np.arange(DEC_SEQ * SUB_GROUP)
    ok_n = (new[None, :] % SUB_GROUP == row_b[:, None]) & (new[None, :] // SUB_GROUP <= row_t[:, None])
    to_bias = lambda ok: np.where(ok, 0.0, NEG_INF).astype(np.float32)
    return to_bias(ok_c), to_bias(ok_n)


def _sample_ctx_kernel(sinks_ref, q0, q1, q2, q3, k0, k1, k2, k3, v0, v1, v2, v3, u0, u1, u2, u3,
                       ckt_ref, cvt_ref, cu_ref, bias_c_ref, bias_n_ref,
                       attn_ref, pool_ref, kt_out, vt_out, pu_out, kbt_s, vbt_s):
    q_t = [q[...].astype(F32) for q in (q0, q1, q2, q3)]
    k_t = [k[...] for k in (k0, k1, k2, k3)]
    v_t = [v[...] for v in (v0, v1, v2, v3)]

    rows_u = [cu_ref[i] for i in range(POOL_BUF)] + [u[...] for u in (u0, u1, u2, u3)]
    for t in range(DEC_SEQ):
        pooled = []
        for g, w in enumerate(POOL_WINDOWS):
            cols = slice(g * POOL_GROUP, (g + 1) * POOL_GROUP)
            cur = rows_u[POOL_BUF + t][:, cols]
            acc = cur
            for j in range(1, w):
                acc = acc + rows_u[POOL_BUF + t - j][:, cols]
            pooled.append(acc / float(w) - cur)
        pool_ref[t] = jnp.concatenate(pooled, axis=1)
    for i in range(POOL_BUF):
        pu_out[i] = rows_u[i + DEC_SEQ]

    bias_c = bias_c_ref[...]
    bias_n = bias_n_ref[...]
    head_of_row = lax.broadcasted_iota(jnp.int32, (bias_c.shape[0], 1), 0) // (DEC_SEQ * SUB_GROUP)
    low = lax.broadcasted_iota(jnp.int32, (1, LANES), 1) < HEAD_DIM
    q_sw = [_swap_halves_wide(q) for q in q_t]
    k_sw = [_swap_halves(k) for k in k_t]
    v_sw = [_swap_halves(v) for v in v_t]
    def scores(sub, kh):
        rows = slice(sub * SUB_GROUP, (sub + 1) * SUB_GROUP)

        def q_piece(t, head):
            src = q_t[t] if head % 2 == 0 else q_sw[t]
            chunk = head // 2
            return jnp.where(low, src[rows, chunk * LANES:(chunk + 1) * LANES], 0.0)

        def kv_first(c):
            return c if kh == 0 else jnp.concatenate([c[HEAD_DIM:], c[:HEAD_DIM]], axis=0)

        lhs = jnp.concatenate([q_piece(t, kh * HEADS_PER_KV + g)
                               for g in range(HEADS_PER_KV) for t in range(DEC_SEQ)], axis=0).astype(BF16)
        kcat = jnp.concatenate([kv_first(ckt_ref[sub * SUB_GROUP + b]) for b in range(SUB_GROUP)],
                               axis=1).astype(BF16)
        knew = jnp.concatenate([(k_t[t] if kh == 0 else k_sw[t])[rows] for t in range(DEC_SEQ)],
                               axis=0).astype(BF16)
        s_c = _dot(lhs, kcat) + bias_c
        s_n = _dot_nt(lhs, knew) + bias_n
        sink = jnp.zeros(head_of_row.shape, F32)
        for g in range(HEADS_PER_KV):
            sink = jnp.where(head_of_row == g, sinks_ref[kh * HEADS_PER_KV + g], sink)
        m = jnp.maximum(jnp.maximum(jnp.max(s_c, axis=-1, keepdims=True),
                                    jnp.max(s_n, axis=-1, keepdims=True)), sink)
        p_c = jnp.exp(s_c - m)
        p_n = jnp.exp(s_n - m)
        denom = (jnp.sum(p_c, axis=-1, keepdims=True) + jnp.sum(p_n, axis=-1, keepdims=True)
                 + jnp.exp(sink - m))
        return p_c.astype(BF16), p_n.astype(BF16), denom

    def values(sub, kh, p_c, p_n, denom):
        rows = slice(sub * SUB_GROUP, (sub + 1) * SUB_GROUP)

        def kv_twice(c):
            part = c[kh * HEAD_DIM:(kh + 1) * HEAD_DIM]
            return jnp.concatenate([part, part], axis=0)

        vcat = jnp.concatenate([kv_twice(cvt_ref[sub * SUB_GROUP + b]) for b in range(SUB_GROUP)],
                               axis=1).astype(BF16)
        vnew = jnp.concatenate([(jnp.where(low, v_t[t], v_sw[t]) if kh == 0 else
                                 jnp.where(low, v_sw[t], v_t[t]))[rows] for t in range(DEC_SEQ)],
                               axis=0).astype(BF16)
        o = (_dot_nt(p_c, vcat) + _dot(p_n, vnew)) / denom
        for t in range(DEC_SEQ):
            for pair in range(HEADS_PER_KV // 2):
                piece = lambda g: o[(g * DEC_SEQ + t) * SUB_GROUP:(g * DEC_SEQ + t + 1) * SUB_GROUP]
                c0 = (kh * HEADS_PER_KV // 2 + pair) * LANES
                attn_ref[t, rows, c0:c0 + LANES] = jnp.where(low, piece(2 * pair), piece(2 * pair + 1))

    units = [(sub, kh) for sub in range(SEQ_GROUP // SUB_GROUP) for kh in range(N_KV_HEADS)]
    pending = None
    for unit in units:
        softmaxed = scores(*unit)
        if pending is not None:
            values(*pending)
        pending = (*unit, *softmaxed)
    values(*pending)

    if DEC_SEQ * SEQ_GROUP < LANES:
        zeros = jnp.zeros((LANES - DEC_SEQ * SEQ_GROUP, KV_WIDTH), F32)
        kbt_s[DEC_SEQ * SEQ_GROUP:, :] = zeros
        vbt_s[DEC_SEQ * SEQ_GROUP:, :] = zeros
    for t in range(DEC_SEQ):
        kbt_s[pl.ds(t, SEQ_GROUP, stride=DEC_SEQ), :] = k_t[t]
        vbt_s[pl.ds(t, SEQ_GROUP, stride=DEC_SEQ), :] = v_t[t]
    knew_t = kbt_s[...].T
    vnew_t = vbt_s[...].T
    keep = lax.broadcasted_iota(jnp.int32, (1, WINDOW), 1) < WINDOW - DEC_SEQ
    for b in range(SEQ_GROUP):
        shift_new = WINDOW - DEC_SEQ - DEC_SEQ * b
        kt_out[b] = jnp.where(keep, pltpu.roll(ckt_ref[b], WINDOW - DEC_SEQ, axis=1),
                              pltpu.roll(knew_t, shift_new, axis=1))
        vt_out[b] = jnp.where(keep, pltpu.roll(cvt_ref[b], WINDOW - DEC_SEQ, axis=1),
                              pltpu.roll(vnew_t, shift_new, axis=1))


def _sample_ctx(us, qs, ks, vs, ckt, cvt, cu, sinks):
    groups = DEC_BATCH // SEQ_GROUP

    def token_rows(t, width):
        return pl.BlockSpec((SEQ_GROUP, width), lambda i: (t * groups + i, 0))

    def per_token(width):
        return [token_rows(t, width) for t in range(DEC_SEQ)]

    cache_spec = pl.BlockSpec((SEQ_GROUP, KV_WIDTH, WINDOW), lambda i: (i, 0, 0))
    pool_rows_spec = pl.BlockSpec((POOL_BUF, SEQ_GROUP, POOL_WIDTH), lambda i: (0, i, 0))
    by_token = lambda width: pl.BlockSpec((DEC_SEQ, SEQ_GROUP, width), lambda i: (0, i, 0))
    bias_c, bias_n = _sample_bias()
    return pl.pallas_call(
        _sample_ctx_kernel,
        out_shape=(jax.ShapeDtypeStruct((DEC_SEQ, DEC_BATCH, Q_WIDTH), F32),
                   jax.ShapeDtypeStruct((DEC_SEQ, DEC_BATCH, POOL_WIDTH), F32),
                   jax.ShapeDtypeStruct((DEC_BATCH, KV_WIDTH, WINDOW), F32),
                   jax.ShapeDtypeStruct((DEC_BATCH, KV_WIDTH, WINDOW), F32),
                   jax.ShapeDtypeStruct((POOL_BUF, DEC_BATCH, POOL_WIDTH), F32)),
        grid=(groups,),
        in_specs=[pl.BlockSpec(memory_space=pltpu.SMEM)]
                 + per_token(Q_WIDTH) + per_token(KV_WIDTH) + per_token(KV_WIDTH) + per_token(POOL_WIDTH)
                 + [cache_spec, cache_spec, pool_rows_spec, _resident(bias_c.shape), _resident(bias_n.shape)],
        out_specs=(by_token(Q_WIDTH), by_token(POOL_WIDTH), cache_spec, cache_spec, pool_rows_spec),
        scratch_shapes=[pltpu.VMEM((LANES, KV_WIDTH), F32), pltpu.VMEM((LANES, KV_WIDTH), F32)],
        compiler_params=pltpu.CompilerParams(dimension_semantics=("parallel",),
                                             vmem_limit_bytes=VMEM_LIMIT_BYTES),
        name="sample_ctx",
    )(sinks, *([qs] * DEC_SEQ), *([ks] * DEC_SEQ), *([vs] * DEC_SEQ), *([us] * DEC_SEQ),
      ckt, cvt, cu, jnp.asarray(bias_c), jnp.asarray(bias_n))


def _back_kernel(h_ref, pa_ref, pools_ref, attns_ref,
                 win_hbm, wgrp_hbm, scale_ref, wpo_hbm, wao_hbm, wout_hbm, g2_ref, b2_ref,
                 w1_hbm, w3_hbm, w2_hbm, g3_ref, b3_ref, yp_ref, ys_ref, z2_c, z3_c, h_s,
                 wg_ref, wgrp_ref, wpo_ref, wao_ref, wout_ref, w1_ref, w3_ref, w2_ref,
                 buf_ff, buf_group, sem_ff, sem_group, sem_model):
    t = pl.program_id(0)

    def branch_outputs():
        is_sample = t >= PROMPT_TILES
        pool_in = jnp.where(is_sample, pools_ref[...].astype(BF16), pa_ref[:, 0:POOL_WIDTH])
        attn_o = jnp.where(is_sample, attns_ref[...].astype(BF16), pa_ref[:, POOL_WIDTH:POOL_WIDTH + Q_WIDTH])
        zs = [_dot(pool_in[:, g * POOL_GROUP:(g + 1) * POOL_GROUP],
                   wgrp_ref[g * POOL_GROUP:(g + 1) * POOL_GROUP, :])
              for g in range(len(POOL_WINDOWS))]
        pool_z = jnp.concatenate(zs, axis=1) * scale_ref[...]
        return _dot(pool_z.astype(BF16), wpo_ref[...]), _dot(attn_o, wao_ref[...])

    def gate_and_merge(h1b, a, b):
        gates = jax.nn.sigmoid(_dot(h1b, wg_ref[...]))
        return (gates[:, :D_MODEL] * a + gates[:, D_MODEL:] * b).astype(BF16)

    def project_out(h1, merged):
        z2_c[...] = ALPHA * h1 + _dot(merged, wout_ref[...])

    @pl.when(jnp.logical_and(t >= 1, t <= ROW_TILES))
    def _():
        h1 = h_ref[...]
        a, b = branch_outputs()
        y = _layer_norm(z3_c[...], g3_ref[...], b3_ref[...])
        yp_ref[...] = y
        h2 = _layer_norm(z2_c[...], g2_ref[...], b2_ref[...])
        h1b = jnp.where(t <= ROW_TILES, h1.astype(BF16), y.astype(BF16))
        merged = []
        side = {6: lambda: merged.append(gate_and_merge(h1b, a, b)), 16: lambda: project_out(h1, merged[0])}
        _swiglu_residual(h2, w1_ref, w3_ref, w2_ref, h_s, side, z3_c)

    @pl.when(t == ROW_TILES + 1)
    def _():
        ys_ref[...] = _layer_norm(z3_c[...], g3_ref[...], b3_ref[...])

    @pl.when(t == 0)
    def _():
        ring_ff = _ring(buf_ff, sem_ff)
        ring_group = _ring(buf_group, sem_group)
        ring_model = _ring(z3_c, sem_model, rows=MODEL_RING_ROWS)
        mixer = (_row_chunks(wgrp_hbm, 0, wgrp_ref, ring_group)
                 + _row_chunks(wpo_hbm, 0, wpo_ref, ring_model)
                 + _row_chunks(wao_hbm, 0, wao_ref, ring_model)
                 + _row_chunks(win_hbm, UQKV_WIDTH, wg_ref.at[:, 0:D_MODEL], ring_model)
                 + _row_chunks(win_hbm, UQKV_WIDTH + D_MODEL, wg_ref.at[:, D_MODEL:2 * D_MODEL], ring_model)
                 + _row_chunks(wout_hbm, 0, wout_ref, ring_model))
        swiglu = (_row_chunks(w1_hbm, 0, w1_ref, ring_ff) + _row_chunks(w3_hbm, 0, w3_ref, ring_ff)
                  + _row_chunks(w2_hbm, 0, w2_ref, ring_model))
        stager = _WeightStager(mixer + swiglu)
        stager.run(len(mixer))
        h1 = h_ref[...]
        a, b = branch_outputs()
        stager.run(len(swiglu) // 3)
        merged = gate_and_merge(h1.astype(BF16), a, b)
        stager.run(len(swiglu) // 3)
        project_out(h1, merged)
        stager.run()
        z3_c[...] = jnp.zeros(z3_c.shape, z3_c.dtype)


def _back(h1, pool_attn, pool_s, attn_s, w_in, wgrp, scale, wpo, wao, wout, g2, b2, w1, w3, w2, g3, b3):
    lagged = pl.BlockSpec((ROW_TILE, D_MODEL), lambda t: (jnp.clip(t - 2, 0, PROMPT_TILES - 1), 0))
    hbm = pl.BlockSpec(memory_space=pl.ANY)
    return pl.pallas_call(
        _back_kernel,
        out_shape=(jax.ShapeDtypeStruct((PROMPT_ROWS, D_MODEL), F32),
                   jax.ShapeDtypeStruct((SAMPLE_ROWS, D_MODEL), F32)),
        grid=(ROW_TILES + 2,),
        in_specs=[pl.BlockSpec((ROW_TILE, D_MODEL), lambda t: (jnp.minimum(t, PROMPT_TILES), 0)),
                  pl.BlockSpec((ROW_TILE, POOL_WIDTH + Q_WIDTH), lambda t: (jnp.minimum(t, PROMPT_TILES - 1), 0)),
                  _resident((SAMPLE_ROWS, POOL_WIDTH)), _resident((SAMPLE_ROWS, Q_WIDTH)),
                  hbm, hbm, _resident((1, POOL_WIDTH)), hbm, hbm, hbm,
                  _resident((1, D_MODEL)), _resident((1, D_MODEL)),
                  hbm, hbm, hbm, _resident((1, D_MODEL)), _resident((1, D_MODEL))],
        out_specs=(lagged, pl.BlockSpec((ROW_TILE, D_MODEL), lambda r: (0, 0))),
        scratch_shapes=[pltpu.VMEM((ROW_TILE, D_MODEL), F32), pltpu.VMEM((ROW_TILE, D_MODEL), F32),
                        pltpu.VMEM((ROW_TILE, D_FF), BF16),
                        pltpu.VMEM((D_MODEL, 2 * D_MODEL), BF16),
                        pltpu.VMEM((len(POOL_WINDOWS) * POOL_GROUP, POOL_GROUP), BF16),
                        pltpu.VMEM((POOL_WIDTH, D_MODEL), BF16), pltpu.VMEM((Q_WIDTH, D_MODEL), BF16),
                        pltpu.VMEM((D_MODEL, D_MODEL), BF16),
                        pltpu.VMEM((D_MODEL, D_FF), BF16), pltpu.VMEM((D_MODEL, D_FF), BF16),
                        pltpu.VMEM((D_FF, D_MODEL), BF16)]
                       + _ring_scratch(RING_FF, RING_GROUP),
        compiler_params=pltpu.CompilerParams(dimension_semantics=("arbitrary",),
                                             vmem_limit_bytes=VMEM_LIMIT_BYTES),
        name="back",
    )(h1, pool_attn, pool_s, attn_s, w_in, wgrp, scale, wpo, wao, wout, g2, b2, w1, w3, w2, g3, b3)


def kernel(x_prompt, x_sample, cache_pool_u, cache_k_win, cache_v_win, w_in, pool_w_grp, pool_scale,
           attn_sinks, w_pool_out, w_attn_out, w_out, ffn1_w1, ffn1_w3, ffn1_w2, ffn2_w1, ffn2_w3,
           ffn2_w2, ln1_g, ln1_b, ln2_g, ln2_b, ln3_g, ln3_b):
    assert DEPTH == 1 and w_in.shape[0] == 1
    l = 0
    vec = lambda p: p[l].reshape(1, -1)
    sinks = attn_sinks[l]

    freq = jnp.tile(ROPE_THETA ** (-2.0 * jnp.arange(HEAD_DIM // 2, dtype=F32) / HEAD_DIM), 4).reshape(1, LANES)

    xp = x_prompt.reshape(PROMPT_ROWS, D_MODEL)
    xs = jnp.transpose(x_sample, (1, 0, 2)).reshape(SAMPLE_ROWS, D_MODEL)
    (h1, pool_attn, us, qs, ks, vs, kt_last, vt_last, u_last) = _front(
        xp, xs, ffn1_w1, ffn1_w3, ffn1_w2, w_in, vec(ln1_g), vec(ln1_b), freq, sinks)

    to_t = lambda c: jnp.transpose(c[l], (0, 2, 3, 1)).reshape(DEC_BATCH, KV_WIDTH, WINDOW)
    cu = jnp.transpose(cache_pool_u[l], (1, 0, 2))
    attn_s, pool_s, kt_s, vt_s, pu_s = _sample_ctx(us, qs, ks, vs, to_t(cache_k_win), to_t(cache_v_win), cu, sinks)

    wgrp = pool_w_grp.reshape(DEPTH, len(POOL_WINDOWS) * POOL_GROUP, POOL_GROUP)
    yp, ys = _back(h1, pool_attn, pool_s.reshape(SAMPLE_ROWS, POOL_WIDTH), attn_s.reshape(SAMPLE_ROWS, Q_WIDTH),
                   w_in, wgrp, vec(pool_scale), w_pool_out, w_attn_out, w_out, vec(ln2_g), vec(ln2_b),
                   ffn2_w1, ffn2_w3, ffn2_w2, vec(ln3_g), vec(ln3_b))
    yp = yp.reshape(BATCH, SEQ, D_MODEL)
    ys = jnp.transpose(ys.reshape(DEC_SEQ, DEC_BATCH, D_MODEL), (1, 0, 2))

    from_t = lambda c, n: jnp.transpose(c.reshape(n, N_KV_HEADS, HEAD_DIM, WINDOW), (0, 3, 1, 2))[None]
    pool_u_prompt = u_last[None, :, POOL_PAD - POOL_BUF:]
    pool_u_sample = jnp.transpose(pu_s, (1, 0, 2))[None]
    return (yp, ys, pool_u_prompt, from_t(kt_last, BATCH), from_t(vt_last, BATCH),
            pool_u_sample, from_t(kt_s, DEC_BATCH), from_t(vt_s, DEC_BATCH))
```

```python
import jax
import jax.numpy as jnp
import numpy as np
from jax import lax
from jax.experimental import pallas as pl
from jax.experimental.pallas import tpu as pltpu

D_MODEL = 1024
BATCH = 8
SEQ = 2048
DEC_BATCH = 128
DEC_SEQ = 4
PAST_LEN = 8192
POOL_WINDOWS = (2, 4, 8, 16)
POOL_GROUP = 128
POOL_WIDTH = 512
POOL_BUF = 15
N_HEADS = 8
N_KV_HEADS = 2
HEADS_PER_KV = N_HEADS // N_KV_HEADS
HEAD_DIM = 64
Q_WIDTH = 512
KV_WIDTH = 128
WINDOW = 128
ROPE_THETA = 10000.0
D_FF = 2816
DEPTH = 1
ALPHA = (2.0 * DEPTH) ** 0.25
LN_EPS = 1e-5
NEG_INF = -1e30
UQKV_WIDTH = POOL_WIDTH + Q_WIDTH + 2 * KV_WIDTH

LANES = 128
KEY_SPAN = 2 * WINDOW
VMEM_LIMIT_BYTES = 61 * 1024 * 1024

ROW_TILE = 512
TILES_PER_SEQ = SEQ // ROW_TILE
PROMPT_ROWS = BATCH * SEQ
PROMPT_TILES = PROMPT_ROWS // ROW_TILE
SAMPLE_ROWS = DEC_BATCH * DEC_SEQ
ROW_TILES = PROMPT_TILES + 1
POOL_PAD = 16
SEQ_GROUP = 32
SUB_GROUP = 8
FF_CHUNK = 256
OUT_BLOCK = 256

BF16 = jnp.bfloat16
F32 = jnp.float32


def _dot(a, b):
    return jnp.dot(a, b, preferred_element_type=F32)


def _dot_nt(a, b):
    return lax.dot_general(a, b, (((1,), (1,)), ((), ())), preferred_element_type=F32)


def _layer_norm(y, g, b):
    mu = jnp.mean(y, axis=-1, keepdims=True)
    yc = y - mu
    var = jnp.mean(yc * yc, axis=-1, keepdims=True)
    return yc * lax.rsqrt(var + LN_EPS) * g + b


def _resident(shape):
    nd = len(shape)
    return pl.BlockSpec(shape, lambda *_: (0,) * nd, pipeline_mode=pl.Buffered(1))


class _WeightStager:
    def __init__(self, jobs):
        self.staged, self.next_in_slot, first, users = [], {}, [], {}
        for src, ring, dst in jobs:
            mine = users.setdefault(id(ring), [])
            slot, sem = ring[len(mine) % len(ring)]
            if len(mine) < len(ring):
                first.append(len(self.staged))
            else:
                self.next_in_slot[mine[-len(ring)]] = len(self.staged)
            mine.append(len(self.staged))
            self.staged.append((pltpu.make_async_copy(src, slot, sem), slot, dst))
        self.done = 0
        for i in first:
            self.staged[i][0].start()

    def run(self, count=None):
        end = len(self.staged) if count is None else min(self.done + count, len(self.staged))
        for i in range(self.done, end):
            copy, slot, dst = self.staged[i]
            copy.wait()
            dst[...] = slot[...].astype(BF16)
            if i in self.next_in_slot:
                self.staged[self.next_in_slot[i]][0].start()
        self.done = end


def _interleave(*job_lists):
    keyed = [((i + 0.5) / len(jobs), n, job) for n, jobs in enumerate(job_lists) for i, job in enumerate(jobs)]
    return [job for _, _, job in sorted(keyed, key=lambda entry: entry[:2])]


def _ring(buf, sems, rows=None):
    if rows is None:
        return [(buf.at[i], sems.at[i]) for i in range(buf.shape[0])]
    return [(buf.at[pl.ds(i * rows, rows), :], sems.at[i]) for i in range(buf.shape[0] // rows)]


def _row_chunks(w_hbm, col0, dst, ring):
    rows, cols = dst.shape
    step, width = ring[0][0].shape
    assert rows % step == 0 and width == cols
    return [(w_hbm.at[0, pl.ds(r0, step), pl.ds(col0, cols)], ring, dst.at[pl.ds(r0, step), :])
            for r0 in range(0, rows, step)]


STAGE_IN_FLIGHT = 4
RING_FF = (STAGE_IN_FLIGHT, 64, D_FF)
MODEL_RING_ROWS = ROW_TILE // STAGE_IN_FLIGHT
RING_NARROW = (STAGE_IN_FLIGHT, 128, UQKV_WIDTH - D_MODEL)
RING_GROUP = (1, len(POOL_WINDOWS) * POOL_GROUP, POOL_GROUP)


def _ring_scratch(*rings):
    return ([pltpu.VMEM(ring, F32) for ring in rings]
            + [pltpu.SemaphoreType.DMA((ring[0],)) for ring in rings]
            + [pltpu.SemaphoreType.DMA((STAGE_IN_FLIGHT,))])


FF_SLOTS = 2 * (D_FF // FF_CHUNK)


def _swiglu_residual(x, w1_ref, w3_ref, w2_ref, h_s, side_work, out_ref):
    assert all(0 <= slot <= FF_SLOTS for slot in side_work)
    run = lambda slot: side_work.get(slot, lambda: None)()
    xb = x.astype(BF16)
    for j in range(D_FF // FF_CHUNK):
        cols = slice(j * FF_CHUNK, (j + 1) * FF_CHUNK)
        a = _dot(xb, w1_ref[:, cols])
        run(2 * j)
        b = _dot(xb, w3_ref[:, cols])
        h_s[:, cols] = ((a * jax.nn.sigmoid(a)) * b).astype(BF16)
        run(2 * j + 1)
    run(FF_SLOTS)
    h = h_s[...]
    for c0 in range(0, D_MODEL, OUT_BLOCK):
        cols = slice(c0, c0 + OUT_BLOCK)
        out_ref[:, cols] = ALPHA * x[:, cols] + 0.5 * _dot(h, w2_ref[:, cols])


def _rope(x, cos, sin_signed, first_half):
    fwd = pltpu.roll(x, LANES - HEAD_DIM // 2, axis=1)
    bwd = pltpu.roll(x, HEAD_DIM // 2, axis=1)
    return x * cos + jnp.where(first_half, fwd, bwd) * sin_signed


def _swap_halves(x):
    return pltpu.roll(x, HEAD_DIM, axis=1)


def _swap_halves_wide(x):
    return jnp.concatenate([_swap_halves(x[:, c:c + LANES]) for c in range(0, x.shape[1], LANES)], axis=1)


def _lane_split(x, x_sw, kh):
    low = lax.broadcasted_iota(jnp.int32, (1, LANES), 1) < HEAD_DIM
    lo, hi = (x, x_sw) if kh == 0 else (x_sw, x)
    return jnp.concatenate([jnp.where(low, lo, 0.0), jnp.where(low, 0.0, hi)], axis=0).astype(BF16)


def _sink_softmax(q_pairs, keys, kh, bias, sinks_ref):
    s = _dot_nt(q_pairs, _lane_split(keys, _swap_halves(keys), kh)) + bias
    second_pair = lax.broadcasted_iota(jnp.int32, (s.shape[0], 1), 0) >= WINDOW
    probs, denoms = [], []
    for c in range(2):
        sc = s[:, c * KEY_SPAN:(c + 1) * KEY_SPAN]
        head = kh * HEADS_PER_KV + c
        sink = jnp.where(second_pair, sinks_ref[head + 2], sinks_ref[head])
        m = jnp.maximum(jnp.max(sc, axis=-1, keepdims=True), sink)
        p = jnp.exp(sc - m)
        denoms.append(jnp.sum(p, axis=-1, keepdims=True) + jnp.exp(sink - m))
        probs.append(p.astype(BF16))
    return jnp.concatenate(probs, axis=1), denoms


def _weighted_values(probs, denoms, vals, kh):
    low = lax.broadcasted_iota(jnp.int32, (1, LANES), 1) < HEAD_DIM
    o = _dot(probs, _lane_split(vals, _swap_halves(vals), kh))
    return o / jnp.where(low, denoms[0], denoms[1])


def _prompt_context_work(tile, q_c, k_c, v_c, u_c, bias_ref, sinks_ref,
                         pool_ref, attn_ref, kt_ref, vt_ref, ulast_ref):
    seq_tile = (tile + TILES_PER_SEQ) % TILES_PER_SEQ
    first_tile = seq_tile == 0
    softmaxed = {}

    def scores(unit, blk, kh):
        r0 = blk * WINDOW
        bias = bias_ref[jnp.where(first_tile, 1, 0)] if blk == 0 else bias_ref[0]
        c0 = 2 * kh * LANES
        q_pairs = jnp.concatenate([q_c[r0:r0 + WINDOW, c0:c0 + LANES],
                                   q_c[r0:r0 + WINDOW, c0 + LANES:c0 + 2 * LANES]], axis=0)
        softmaxed[unit] = _sink_softmax(q_pairs, k_c[r0:r0 + KEY_SPAN, :], kh, bias, sinks_ref)

    def values(unit, blk, kh):
        r0 = blk * WINDOW
        c0 = 2 * kh * LANES
        o = _weighted_values(*softmaxed.pop(unit), v_c[r0:r0 + KEY_SPAN, :], kh)
        attn_ref[r0:r0 + WINDOW, c0:c0 + LANES] = o[:WINDOW].astype(BF16)
        attn_ref[r0:r0 + WINDOW, c0 + LANES:c0 + 2 * LANES] = o[WINDOW:].astype(BF16)

    def pool(groups):
        pos = seq_tile * ROW_TILE + lax.broadcasted_iota(jnp.int32, (ROW_TILE, 1), 0)
        for g in groups:
            w = POOL_WINDOWS[g]
            cols = slice(g * POOL_GROUP, (g + 1) * POOL_GROUP)
            rows = u_c[:, cols]
            acc, span = rows, 1
            while span < w:
                acc = acc + pltpu.roll(acc, span, axis=0)
                span *= 2
            cur = rows[POOL_PAD:]
            cnt = jnp.minimum(pos + 1, w).astype(F32)
            pool_ref[:, cols] = (acc[POOL_PAD:] / cnt - cur).astype(BF16)

    def sequence_state():
        kt_ref[0] = k_c[ROW_TILE:ROW_TILE + WINDOW, :].T
        vt_ref[0] = v_c[ROW_TILE:ROW_TILE + WINDOW, :].T
        ulast_ref[0] = u_c[ROW_TILE:ROW_TILE + POOL_PAD, :]

    work = {"pool_wide": lambda: pool((3,)), "pool_narrow": lambda: pool((0, 1, 2)),
            "sequence_state": sequence_state}
    for blk in range(ROW_TILE // WINDOW):
        for kh in range(N_KV_HEADS):
            unit = blk * N_KV_HEADS + kh
            work["scores", unit] = lambda unit=unit, blk=blk, kh=kh: scores(unit, blk, kh)
            work["values", unit] = lambda unit=unit, blk=blk, kh=kh: values(unit, blk, kh)
    return work


def _prompt_bias():
    r = np.arange(2 * WINDOW)[:, None] % WINDOW
    c = np.arange(2 * KEY_SPAN)[None, :] % KEY_SPAN
    valid = (r <= c) & (c <= r + WINDOW)
    first = valid & (c >= WINDOW)
    return np.where(np.stack([valid, first]), 0.0, NEG_INF).astype(np.float32)


def _front_kernel(sinks_ref, xp_ref, xs_ref, w1_hbm, w3_hbm, w2_hbm, win_hbm, g_ref, b_ref, freq_ref, bias_ref,
                  h_ref, pa_ref, us_ref, qs_ref, ks_ref, vs_ref, kt_ref, vt_ref, ulast_ref,
                  z_c, q_c, k_c, v_c, u_c, h_s, rope_ref, w1_ref, w3_ref, w2_ref, wu_ref,
                  buf_ff, buf_narrow, sem_ff, sem_narrow, sem_model):
    r = pl.program_id(0) - 1
    prev = r - 1
    pool_ref = pa_ref.at[:, 0:POOL_WIDTH]
    attn_ref = pa_ref.at[:, POOL_WIDTH:POOL_WIDTH + Q_WIDTH]

    def stage_and_reset():
        ring_ff = _ring(buf_ff, sem_ff)
        ring_narrow = _ring(buf_narrow, sem_narrow)
        ring_model = _ring(z_c, sem_model, rows=MODEL_RING_ROWS)
        stager = _WeightStager(_interleave(
            _row_chunks(w1_hbm, 0, w1_ref, ring_ff) + _row_chunks(w3_hbm, 0, w3_ref, ring_ff),
            _row_chunks(win_hbm, 0, wu_ref.at[:, 0:D_MODEL], ring_model) + _row_chunks(w2_hbm, 0, w2_ref, ring_model),
            _row_chunks(win_hbm, D_MODEL, wu_ref.at[:, D_MODEL:UQKV_WIDTH], ring_narrow)))
        freq = freq_ref[...]
        lane = lax.broadcasted_iota(jnp.int32, (1, LANES), 1)
        sign = jnp.where((lane & (HEAD_DIM // 2)) == 0, -1.0, 1.0)
        row = lax.broadcasted_iota(jnp.int32, (ROW_TILE, 1), 0)
        per_tile = -(-len(stager.staged) // (TILES_PER_SEQ + 1))
        for tile in range(TILES_PER_SEQ + 1):
            pos = tile * ROW_TILE + row if tile < TILES_PER_SEQ else PAST_LEN + row // DEC_BATCH
            ang = pos.astype(F32) * freq
            rows = slice(tile * ROW_TILE, (tile + 1) * ROW_TILE)
            rope_ref[rows, 0:LANES] = jnp.cos(ang)
            rope_ref[rows, LANES:2 * LANES] = jnp.sin(ang) * sign
            stager.run(per_tile)
        stager.run()
        for ref in (z_c, q_c, k_c, v_c, u_c):
            ref[...] = jnp.zeros(ref.shape, ref.dtype)

    def norm_previous():
        h1 = _layer_norm(z_c[...], g_ref[...], b_ref[...])
        h_ref[...] = h1
        return h1.astype(BF16)

    def rope_tables():
        tile = jnp.clip(prev, 0, PROMPT_TILES)
        table = jnp.where(tile == PROMPT_TILES, TILES_PER_SEQ, tile % TILES_PER_SEQ)
        rows = pl.ds(pl.multiple_of(table * ROW_TILE, ROW_TILE), ROW_TILE)
        cos = rope_ref[rows, 0:LANES]
        lane = lax.broadcasted_iota(jnp.int32, cos.shape, 1)
        return cos, rope_ref[rows, LANES:2 * LANES], (lane & (HEAD_DIM // 2)) == 0

    def project_u(h1b):
        return _dot(h1b, wu_ref[:, 0:POOL_WIDTH])

    def project_q(h1b):
        z = _dot(h1b, wu_ref[:, POOL_WIDTH:POOL_WIDTH + Q_WIDTH])
        tables = rope_tables()
        return [(_rope(z[:, c:c + LANES], *tables) * (HEAD_DIM ** -0.5)).astype(BF16)
                for c in range(0, Q_WIDTH, LANES)]

    def project_kv(h1b):
        z = _dot(h1b, wu_ref[:, POOL_WIDTH + Q_WIDTH:UQKV_WIDTH])
        return _rope(z[:, :KV_WIDTH], *rope_tables()), z[:, KV_WIDTH:]

    def tile_step(x):
        normed = []

        def norm():
            normed.append(norm_previous())

        projected = []

        def project():
            projected.extend([project_u(normed[0]), project_q(normed[0]), *project_kv(normed[0])])

        def carry():
            u, q, k, v = projected
            starts_seq = (prev + TILES_PER_SEQ) % TILES_PER_SEQ == 0
            u_c[0:POOL_PAD, :] = jnp.where(starts_seq, 0.0, u_c[ROW_TILE:ROW_TILE + POOL_PAD, :])
            u_c[POOL_PAD:POOL_PAD + ROW_TILE, :] = u
            for c, qc in enumerate(q):
                q_c[:, c * LANES:(c + 1) * LANES] = qc
            k_c[0:WINDOW, :] = k_c[ROW_TILE:ROW_TILE + WINDOW, :]
            v_c[0:WINDOW, :] = v_c[ROW_TILE:ROW_TILE + WINDOW, :]
            k_c[WINDOW:WINDOW + ROW_TILE, :] = k
            v_c[WINDOW:WINDOW + ROW_TILE, :] = v

        side = {0: norm, FF_SLOTS - 3: project, FF_SLOTS - 1: carry}
        stages = context_stages()
        assert len(stages) < FF_SLOTS - 3
        side.update(enumerate(stages, start=1))
        _swiglu_residual(x, w1_ref, w3_ref, w2_ref, h_s, side, z_c)

    def context_stages():
        context = _prompt_context_work(r - 2, q_c, k_c, v_c, u_c, bias_ref, sinks_ref,
                                       pool_ref, attn_ref, kt_ref, vt_ref, ulast_ref)
        stages = [("scores", 0)]
        for unit in range(1, 8):
            stages += [("scores", unit), ("values", unit - 1)]
        stages.append(("values", 7))

        def pooling_and_state():
            context["pool_wide"]()
            context["pool_narrow"]()
            context["sequence_state"]()

        return [context[stage] for stage in stages] + [pooling_and_state]

    @pl.when(jnp.logical_and(r >= 0, r <= PROMPT_TILES))
    def _():
        tile_step(jnp.where(r == PROMPT_TILES, xs_ref[...], xp_ref[...]))

    @pl.when(r == ROW_TILES)
    def _():
        for work in context_stages():
            work()
        h1b = norm_previous()
        us_ref[...] = project_u(h1b)
        for c, qc in enumerate(project_q(h1b)):
            qs_ref[:, c * LANES:(c + 1) * LANES] = qc
        ks_ref[...], vs_ref[...] = project_kv(h1b)

    pl.when(r < 0)(stage_and_reset)


def _front(xp, xs, w1, w3, w2, w_in, g, b, freq, sinks):
    rows = ROW_TILES * ROW_TILE

    def lagged(width):
        return pl.BlockSpec((ROW_TILE, width), lambda s: (jnp.clip(s - 3, 0, PROMPT_TILES - 1), 0))

    def seq_of_lagged(shape):
        return pl.BlockSpec(shape, lambda s: (jnp.clip(s - 3, 0, PROMPT_TILES - 1) // TILES_PER_SEQ, 0, 0))

    sample = lambda width: pl.BlockSpec((ROW_TILE, width), lambda r: (0, 0))
    hbm = pl.BlockSpec(memory_space=pl.ANY)
    return pl.pallas_call(
        _front_kernel,
        out_shape=(jax.ShapeDtypeStruct((rows, D_MODEL), F32),
                   jax.ShapeDtypeStruct((PROMPT_ROWS, POOL_WIDTH + Q_WIDTH), BF16),
                   jax.ShapeDtypeStruct((SAMPLE_ROWS, POOL_WIDTH), F32),
                   jax.ShapeDtypeStruct((SAMPLE_ROWS, Q_WIDTH), BF16),
                   jax.ShapeDtypeStruct((SAMPLE_ROWS, KV_WIDTH), F32),
                   jax.ShapeDtypeStruct((SAMPLE_ROWS, KV_WIDTH), F32),
                   jax.ShapeDtypeStruct((BATCH, KV_WIDTH, WINDOW), F32),
                   jax.ShapeDtypeStruct((BATCH, KV_WIDTH, WINDOW), F32),
                   jax.ShapeDtypeStruct((BATCH, POOL_PAD, POOL_WIDTH), F32)),
        grid=(ROW_TILES + 2,),
        in_specs=[pl.BlockSpec(memory_space=pltpu.SMEM),
                  pl.BlockSpec((ROW_TILE, D_MODEL), lambda s: (jnp.clip(s - 1, 0, PROMPT_TILES - 1), 0)),
                  _resident((SAMPLE_ROWS, D_MODEL)),
                  hbm, hbm, hbm, hbm, _resident((1, D_MODEL)), _resident((1, D_MODEL)),
                  _resident((1, LANES)), _resident((2, 2 * WINDOW, 2 * KEY_SPAN))],
        out_specs=(pl.BlockSpec((ROW_TILE, D_MODEL), lambda s: (jnp.clip(s - 2, 0, PROMPT_TILES), 0)),
                   lagged(POOL_WIDTH + Q_WIDTH),
                   sample(POOL_WIDTH), sample(Q_WIDTH), sample(KV_WIDTH), sample(KV_WIDTH),
                   seq_of_lagged((1, KV_WIDTH, WINDOW)), seq_of_lagged((1, KV_WIDTH, WINDOW)),
                   seq_of_lagged((1, POOL_PAD, POOL_WIDTH))),
        scratch_shapes=[pltpu.VMEM((ROW_TILE, D_MODEL), F32),
                        pltpu.VMEM((ROW_TILE, Q_WIDTH), BF16),
                        pltpu.VMEM((WINDOW + ROW_TILE, KV_WIDTH), F32),
                        pltpu.VMEM((WINDOW + ROW_TILE, KV_WIDTH), F32),
                        pltpu.VMEM((POOL_PAD + ROW_TILE, POOL_WIDTH), F32),
                        pltpu.VMEM((ROW_TILE, D_FF), BF16),
                        pltpu.VMEM(((TILES_PER_SEQ + 1) * ROW_TILE, 2 * LANES), F32),
                        pltpu.VMEM((D_MODEL, D_FF), BF16), pltpu.VMEM((D_MODEL, D_FF), BF16),
                        pltpu.VMEM((D_FF, D_MODEL), BF16), pltpu.VMEM((D_MODEL, UQKV_WIDTH), BF16)]
                       + _ring_scratch(RING_FF, RING_NARROW),
        compiler_params=pltpu.CompilerParams(dimension_semantics=("arbitrary",),
                                             vmem_limit_bytes=VMEM_LIMIT_BYTES),
        name="front",
    )(sinks, xp, xs, w1, w3, w2, w_in, g, b, freq, jnp.asarray(_prompt_bias()))


def _sample_bias():
    row = np.arange(HEADS_PER_KV * DEC_SEQ * SUB_GROUP)
    row_t = (row // SUB_GROUP) % DEC_SEQ
    row_b = row % SUB_GROUP
    col = np.arange(SUB_GROUP * WINDOW)
    ok_c = (col[None, :] // WINDOW == row_b[:, None]) & (col[None, :] % WINDOW >= row_t[:, None])
    new = np.arange(DEC_SEQ * SUB_GROUP)
    ok_n = (new[None, :] % SUB_GROUP == row_b[:, None]) & (new[None, :] // SUB_GROUP <= row_t[:, None])
    to_bias = lambda ok: np.where(ok, 0.0, NEG_INF).astype(np.float32)
    return to_bias(ok_c), to_bias(ok_n)


def _sample_ctx_kernel(sinks_ref, q0, q1, q2, q3, k0, k1, k2, k3, v0, v1, v2, v3, u0, u1, u2, u3,
                       ckt_ref, cvt_ref, cu_ref, bias_c_ref, bias_n_ref,
                       attn_ref, pool_ref, kt_out, vt_out, pu_out, kbt_s, vbt_s):
    q_t = [q[...].astype(F32) for q in (q0, q1, q2, q3)]
    k_t = [k[...] for k in (k0, k1, k2, k3)]
    v_t = [v[...] for v in (v0, v1, v2, v3)]

    rows_u = [cu_ref[i] for i in range(POOL_BUF)] + [u[...] for u in (u0, u1, u2, u3)]
    for t in range(DEC_SEQ):
        pooled = []
        for g, w in enumerate(POOL_WINDOWS):
            cols = slice(g * POOL_GROUP, (g + 1) * POOL_GROUP)
            cur = rows_u[POOL_BUF + t][:, cols]
            acc = cur
            for j in range(1, w):
                acc = acc + rows_u[POOL_BUF + t - j][:, cols]
            pooled.append(acc / float(w) - cur)
        pool_ref[t] = jnp.concatenate(pooled, axis=1)
    for i in range(POOL_BUF):
        pu_out[i] = rows_u[i + DEC_SEQ]

    bias_c = bias_c_ref[...]
    bias_n = bias_n_ref[...]
    head_of_row = lax.broadcasted_iota(jnp.int32, (bias_c.shape[0], 1), 0) // (DEC_SEQ * SUB_GROUP)
    low = lax.broadcasted_iota(jnp.int32, (1, LANES), 1) < HEAD_DIM
    q_sw = [_swap_halves_wide(q) for q in q_t]
    k_sw = [_swap_halves(k) for k in k_t]
    v_sw = [_swap_halves(v) for v in v_t]
    def scores(sub, kh):
        rows = slice(sub * SUB_GROUP, (sub + 1) * SUB_GROUP)

        def q_piece(t, head):
            src = q_t[t] if head % 2 == 0 else q_sw[t]
            chunk = head // 2
            return jnp.where(low, src[rows, chunk * LANES:(chunk + 1) * LANES], 0.0)

        def kv_first(c):
            return c if kh == 0 else jnp.concatenate([c[HEAD_DIM:], c[:HEAD_DIM]], axis=0)

        lhs = jnp.concatenate([q_piece(t, kh * HEADS_PER_KV + g)
                               for g in range(HEADS_PER_KV) for t in range(DEC_SEQ)], axis=0).astype(BF16)
        kcat = jnp.concatenate([kv_first(ckt_ref[sub * SUB_GROUP + b]) for b in range(SUB_GROUP)],
                               axis=1).astype(BF16)
        knew = jnp.concatenate([(k_t[t] if kh == 0 else k_sw[t])[rows] for t in range(DEC_SEQ)],
                               axis=0).astype(BF16)
        s_c = _dot(lhs, kcat) + bias_c
        s_n = _dot_nt(lhs, knew) + bias_n
        sink = jnp.zeros(head_of_row.shape, F32)
        for g in range(HEADS_PER_KV):
            sink = jnp.where(head_of_row == g, sinks_ref[kh * HEADS_PER_KV + g], sink)
        m = jnp.maximum(jnp.maximum(jnp.max(s_c, axis=-1, keepdims=True),
                                    jnp.max(s_n, axis=-1, keepdims=True)), sink)
        p_c = jnp.exp(s_c - m)
        p_n = jnp.exp(s_n - m)
        denom = (jnp.sum(p_c, axis=-1, keepdims=True) + jnp.sum(p_n, axis=-1, keepdims=True)
                 + jnp.exp(sink - m))
        return p_c.astype(BF16), p_n.astype(BF16), denom

    def values(sub, kh, p_c, p_n, denom):
        rows = slice(sub * SUB_GROUP, (sub + 1) * SUB_GROUP)

        def kv_twice(c):
            part = c[kh * HEAD_DIM:(kh + 1) * HEAD_DIM]
            return jnp.concatenate([part, part], axis=0)

        vcat = jnp.concatenate([kv_twice(cvt_ref[sub * SUB_GROUP + b]) for b in range(SUB_GROUP)],
                               axis=1).astype(BF16)
        vnew = jnp.concatenate([(jnp.where(low, v_t[t], v_sw[t]) if kh == 0 else
                                 jnp.where(low, v_sw[t], v_t[t]))[rows] for t in range(DEC_SEQ)],
                               axis=0).astype(BF16)
        o = (_dot_nt(p_c, vcat) + _dot(p_n, vnew)) / denom
        for t in range(DEC_SEQ):
            for pair in range(HEADS_PER_KV // 2):
                piece = lambda g: o[(g * DEC_SEQ + t) * SUB_GROUP:(g * DEC_SEQ + t + 1) * SUB_GROUP]
                c0 = (kh * HEADS_PER_KV // 2 + pair) * LANES
                attn_ref[t, rows, c0:c0 + LANES] = jnp.where(low, piece(2 * pair), piece(2 * pair + 1))

    units = [(sub, kh) for sub in range(SEQ_GROUP // SUB_GROUP) for kh in range(N_KV_HEADS)]
    pending = None
    for unit in units:
        softmaxed = scores(*unit)
        if pending is not None:
            values(*pending)
        pending = (*unit, *softmaxed)
    values(*pending)

    if DEC_SEQ * SEQ_GROUP < LANES:
        zeros = jnp.zeros((LANES - DEC_SEQ * SEQ_GROUP, KV_WIDTH), F32)
        kbt_s[DEC_SEQ * SEQ_GROUP:, :] = zeros
        vbt_s[DEC_SEQ * SEQ_GROUP:, :] = zeros
    for t in range(DEC_SEQ):
        kbt_s[pl.ds(t, SEQ_GROUP, stride=DEC_SEQ), :] = k_t[t]
        vbt_s[pl.ds(t, SEQ_GROUP, stride=DEC_SEQ), :] = v_t[t]
    knew_t = kbt_s[...].T
    vnew_t = vbt_s[...].T
    keep = lax.broadcasted_iota(jnp.int32, (1, WINDOW), 1) < WINDOW - DEC_SEQ
    for b in range(SEQ_GROUP):
        shift_new = WINDOW - DEC_SEQ - DEC_SEQ * b
        kt_out[b] = jnp.where(keep, pltpu.roll(ckt_ref[b], WINDOW - DEC_SEQ, axis=1),
                              pltpu.roll(knew_t, shift_new, axis=1))
        vt_out[b] = jnp.where(keep, pltpu.roll(cvt_ref[b], WINDOW - DEC_SEQ, axis=1),
                              pltpu.roll(vnew_t, shift_new, axis=1))


def _sample_ctx(us, qs, ks, vs, ckt, cvt, cu, sinks):
    groups = DEC_BATCH // SEQ_GROUP

    def token_rows(t, width):
        return pl.BlockSpec((SEQ_GROUP, width), lambda i: (t * groups + i, 0))

    def per_token(width):
        return [token_rows(t, width) for t in range(DEC_SEQ)]

    cache_spec = pl.BlockSpec((SEQ_GROUP, KV_WIDTH, WINDOW), lambda i: (i, 0, 0))
    pool_rows_spec = pl.BlockSpec((POOL_BUF, SEQ_GROUP, POOL_WIDTH), lambda i: (0, i, 0))
    by_token = lambda width: pl.BlockSpec((DEC_SEQ, SEQ_GROUP, width), lambda i: (0, i, 0))
    bias_c, bias_n = _sample_bias()
    return pl.pallas_call(
        _sample_ctx_kernel,
        out_shape=(jax.ShapeDtypeStruct((DEC_SEQ, DEC_BATCH, Q_WIDTH), F32),
                   jax.ShapeDtypeStruct((DEC_SEQ, DEC_BATCH, POOL_WIDTH), F32),
                   jax.ShapeDtypeStruct((DEC_BATCH, KV_WIDTH, WINDOW), F32),
                   jax.ShapeDtypeStruct((DEC_BATCH, KV_WIDTH, WINDOW), F32),
                   jax.ShapeDtypeStruct((POOL_BUF, DEC_BATCH, POOL_WIDTH), F32)),
        grid=(groups,),
        in_specs=[pl.BlockSpec(memory_space=pltpu.SMEM)]
                 + per_token(Q_WIDTH) + per_token(KV_WIDTH) + per_token(KV_WIDTH) + per_token(POOL_WIDTH)
                 + [cache_spec, cache_spec, pool_rows_spec, _resident(bias_c.shape), _resident(bias_n.shape)],
        out_specs=(by_token(Q_WIDTH), by_token(POOL_WIDTH), cache_spec, cache_spec, pool_rows_spec),
        scratch_shapes=[pltpu.VMEM((LANES, KV_WIDTH), F32), pltpu.VMEM((LANES, KV_WIDTH), F32)],
        compiler_params=pltpu.CompilerParams(dimension_semantics=("parallel",),
                                             vmem_limit_bytes=VMEM_LIMIT_BYTES),
        name="sample_ctx",
    )(sinks, *([qs] * DEC_SEQ), *([ks] * DEC_SEQ), *([vs] * DEC_SEQ), *([us] * DEC_SEQ),
      ckt, cvt, cu, jnp.asarray(bias_c), jnp.asarray(bias_n))


def _back_kernel(h_ref, pa_ref, pools_ref, attns_ref,
                 win_hbm, wgrp_hbm, scale_ref, wpo_hbm, wao_hbm, wout_hbm, g2_ref, b2_ref,
                 w1_hbm, w3_hbm, w2_hbm, g3_ref, b3_ref, yp_ref, ys_ref, z2_c, z3_c, h_s,
                 wg_ref, wgrp_ref, wpo_ref, wao_ref, wout_ref, w1_ref, w3_ref, w2_ref,
                 buf_ff, buf_group, sem_ff, sem_group, sem_model):
    t = pl.program_id(0)

    def branch_outputs():
        is_sample = t >= PROMPT_TILES
        pool_in = jnp.where(is_sample, pools_ref[...].astype(BF16), pa_ref[:, 0:POOL_WIDTH])
        attn_o = jnp.where(is_sample, attns_ref[...].astype(BF16), pa_ref[:, POOL_WIDTH:POOL_WIDTH + Q_WIDTH])
        zs = [_dot(pool_in[:, g * POOL_GROUP:(g + 1) * POOL_GROUP],
                   wgrp_ref[g * POOL_GROUP:(g + 1) * POOL_GROUP, :])
              for g in range(len(POOL_WINDOWS))]
        pool_z = jnp.concatenate(zs, axis=1) * scale_ref[...]
        return _dot(pool_z.astype(BF16), wpo_ref[...]), _dot(attn_o, wao_ref[...])

    def gate_and_merge(h1b, a, b):
        gates = jax.nn.sigmoid(_dot(h1b, wg_ref[...]))
        return (gates[:, :D_MODEL] * a + gates[:, D_MODEL:] * b).astype(BF16)

    def project_out(h1, merged):
        z2_c[...] = ALPHA * h1 + _dot(merged, wout_ref[...])

    @pl.when(jnp.logical_and(t >= 1, t <= ROW_TILES))
    def _():
        h1 = h_ref[...]
        a, b = branch_outputs()
        y = _layer_norm(z3_c[...], g3_ref[...], b3_ref[...])
        yp_ref[...] = y
        h2 = _layer_norm(z2_c[...], g2_ref[...], b2_ref[...])
        h1b = jnp.where(t <= ROW_TILES, h1.astype(BF16), y.astype(BF16))
        merged = []
        side = {6: lambda: merged.append(gate_and_merge(h1b, a, b)), 16: lambda: project_out(h1, merged[0])}
        _swiglu_residual(h2, w1_ref, w3_ref, w2_ref, h_s, side, z3_c)

    @pl.when(t == ROW_TILES + 1)
    def _():
        ys_ref[...] = _layer_norm(z3_c[...], g3_ref[...], b3_ref[...])

    @pl.when(t == 0)
    def _():
        ring_ff = _ring(buf_ff, sem_ff)
        ring_group = _ring(buf_group, sem_group)
        ring_model = _ring(z3_c, sem_model, rows=MODEL_RING_ROWS)
        mixer = (_row_chunks(wgrp_hbm, 0, wgrp_ref, ring_group)
                 + _row_chunks(wpo_hbm, 0, wpo_ref, ring_model)
                 + _row_chunks(wao_hbm, 0, wao_ref, ring_model)
                 + _row_chunks(win_hbm, UQKV_WIDTH, wg_ref.at[:, 0:D_MODEL], ring_model)
                 + _row_chunks(win_hbm, UQKV_WIDTH + D_MODEL, wg_ref.at[:, D_MODEL:2 * D_MODEL], ring_model)
                 + _row_chunks(wout_hbm, 0, wout_ref, ring_model))
        up = _row_chunks(w1_hbm, 0, w1_ref, ring_ff) + _row_chunks(w3_hbm, 0, w3_ref, ring_ff)
        down = _row_chunks(w2_hbm, 0, w2_ref, ring_model)
        stager = _WeightStager(_interleave(mixer, up) + down)
        stager.run(len(mixer) + len(up))
        h1 = h_ref[...]
        a, b = branch_outputs()
        stager.run(len(down) // 3)
        merged = gate_and_merge(h1.astype(BF16), a, b)
        stager.run(len(down) // 3)
        project_out(h1, merged)
        stager.run()
        z3_c[...] = jnp.zeros(z3_c.shape, z3_c.dtype)


def _back(h1, pool_attn, pool_s, attn_s, w_in, wgrp, scale, wpo, wao, wout, g2, b2, w1, w3, w2, g3, b3):
    lagged = pl.BlockSpec((ROW_TILE, D_MODEL), lambda t: (jnp.clip(t - 2, 0, PROMPT_TILES - 1), 0))
    hbm = pl.BlockSpec(memory_space=pl.ANY)
    return pl.pallas_call(
        _back_kernel,
        out_shape=(jax.ShapeDtypeStruct((PROMPT_ROWS, D_MODEL), F32),
                   jax.ShapeDtypeStruct((SAMPLE_ROWS, D_MODEL), F32)),
        grid=(ROW_TILES + 2,),
        in_specs=[pl.BlockSpec((ROW_TILE, D_MODEL), lambda t: (jnp.minimum(t, PROMPT_TILES), 0)),
                  pl.BlockSpec((ROW_TILE, POOL_WIDTH + Q_WIDTH), lambda t: (jnp.minimum(t, PROMPT_TILES - 1), 0)),
                  _resident((SAMPLE_ROWS, POOL_WIDTH)), _resident((SAMPLE_ROWS, Q_WIDTH)),
                  hbm, hbm, _resident((1, POOL_WIDTH)), hbm, hbm, hbm,
                  _resident((1, D_MODEL)), _resident((1, D_MODEL)),
                  hbm, hbm, hbm, _resident((1, D_MODEL)), _resident((1, D_MODEL))],
        out_specs=(lagged, pl.BlockSpec((ROW_TILE, D_MODEL), lambda r: (0, 0))),
        scratch_shapes=[pltpu.VMEM((ROW_TILE, D_MODEL), F32), pltpu.VMEM((ROW_TILE, D_MODEL), F32),
                        pltpu.VMEM((ROW_TILE, D_FF), BF16),
                        pltpu.VMEM((D_MODEL, 2 * D_MODEL), BF16),
                        pltpu.VMEM((len(POOL_WINDOWS) * POOL_GROUP, POOL_GROUP), BF16),
                        pltpu.VMEM((POOL_WIDTH, D_MODEL), BF16), pltpu.VMEM((Q_WIDTH, D_MODEL), BF16),
                        pltpu.VMEM((D_MODEL, D_MODEL), BF16),
                        pltpu.VMEM((D_MODEL, D_FF), BF16), pltpu.VMEM((D_MODEL, D_FF), BF16),
                        pltpu.VMEM((D_FF, D_MODEL), BF16)]
                       + _ring_scratch(RING_FF, RING_GROUP),
        compiler_params=pltpu.CompilerParams(dimension_semantics=("arbitrary",),
                                             vmem_limit_bytes=VMEM_LIMIT_BYTES),
        name="back",
    )(h1, pool_attn, pool_s, attn_s, w_in, wgrp, scale, wpo, wao, wout, g2, b2, w1, w3, w2, g3, b3)


def kernel(x_prompt, x_sample, cache_pool_u, cache_k_win, cache_v_win, w_in, pool_w_grp, pool_scale,
           attn_sinks, w_pool_out, w_attn_out, w_out, ffn1_w1, ffn1_w3, ffn1_w2, ffn2_w1, ffn2_w3,
           ffn2_w2, ln1_g, ln1_b, ln2_g, ln2_b, ln3_g, ln3_b):
    assert DEPTH == 1 and w_in.shape[0] == 1
    l = 0
    vec = lambda p: p[l].reshape(1, -1)
    sinks = attn_sinks[l]

    freq = jnp.tile(ROPE_THETA ** (-2.0 * jnp.arange(HEAD_DIM // 2, dtype=F32) / HEAD_DIM), 4).reshape(1, LANES)

    xp = x_prompt.reshape(PROMPT_ROWS, D_MODEL)
    xs = jnp.transpose(x_sample, (1, 0, 2)).reshape(SAMPLE_ROWS, D_MODEL)
    (h1, pool_attn, us, qs, ks, vs, kt_last, vt_last, u_last) = _front(
        xp, xs, ffn1_w1, ffn1_w3, ffn1_w2, w_in, vec(ln1_g), vec(ln1_b), freq, sinks)

    to_t = lambda c: jnp.transpose(c[l], (0, 2, 3, 1)).reshape(DEC_BATCH, KV_WIDTH, WINDOW)
    cu = jnp.transpose(cache_pool_u[l], (1, 0, 2))
    attn_s, pool_s, kt_s, vt_s, pu_s = _sample_ctx(us, qs, ks, vs, to_t(cache_k_win), to_t(cache_v_win), cu, sinks)

    wgrp = pool_w_grp.reshape(DEPTH, len(POOL_WINDOWS) * POOL_GROUP, POOL_GROUP)
    yp, ys = _back(h1, pool_attn, pool_s.reshape(SAMPLE_ROWS, POOL_WIDTH), attn_s.reshape(SAMPLE_ROWS, Q_WIDTH),
                   w_in, wgrp, vec(pool_scale), w_pool_out, w_attn_out, w_out, vec(ln2_g), vec(ln2_b),
                   ffn2_w1, ffn2_w3, ffn2_w2, vec(ln3_g), vec(ln3_b))
    yp = yp.reshape(BATCH, SEQ, D_MODEL)
    ys = jnp.transpose(ys.reshape(DEC_SEQ, DEC_BATCH, D_MODEL), (1, 0, 2))

    from_t = lambda c, n: jnp.transpose(c.reshape(n, N_KV_HEADS, HEAD_DIM, WINDOW), (0, 3, 1, 2))[None]
    pool_u_prompt = u_last[None, :, POOL_PAD - POOL_BUF:]
    pool_u_sample = jnp.transpose(pu_s, (1, 0, 2))[None]
    return (yp, ys, pool_u_prompt, from_t(kt_last, BATCH), from_t(vt_last, BATCH),
            pool_u_sample, from_t(kt_s, DEC_BATCH), from_t(vt_s, DEC_BATCH))
```

```python
import jax
import jax.numpy as jnp
import numpy as np
from jax import lax
from jax.experimental import pallas as pl
from jax.experimental.pallas import tpu as pltpu

D_MODEL = 1024
BATCH = 8
SEQ = 2048
DEC_BATCH = 128
DEC_SEQ = 4
PAST_LEN = 8192
POOL_WINDOWS = (2, 4, 8, 16)
POOL_GROUP = 128
POOL_WIDTH = 512
POOL_BUF = 15
N_HEADS = 8
N_KV_HEADS = 2
HEADS_PER_KV = N_HEADS // N_KV_HEADS
HEAD_DIM = 64
Q_WIDTH = 512
KV_WIDTH = 128
WINDOW = 128
ROPE_THETA = 10000.0
D_FF = 2816
DEPTH = 1
ALPHA = (2.0 * DEPTH) ** 0.25
LN_EPS = 1e-5
NEG_INF = -1e30
UQKV_WIDTH = POOL_WIDTH + Q_WIDTH + 2 * KV_WIDTH

LANES = 128
KEY_SPAN = 2 * WINDOW
VMEM_LIMIT_BYTES = 61 * 1024 * 1024

ROW_TILE = 512
TILES_PER_SEQ = SEQ // ROW_TILE
PROMPT_ROWS = BATCH * SEQ
PROMPT_TILES = PROMPT_ROWS // ROW_TILE
SAMPLE_ROWS = DEC_BATCH * DEC_SEQ
ROW_TILES = PROMPT_TILES + 1
POOL_PAD = 16
SEQ_GROUP = 32
SUB_GROUP = 8
FF_CHUNK = 256
OUT_BLOCK = 256

BF16 = jnp.bfloat16
F32 = jnp.float32


def _dot(a, b):
    return jnp.dot(a, b, preferred_element_type=F32)


def _dot_nt(a, b):
    return lax.dot_general(a, b, (((1,), (1,)), ((), ())), preferred_element_type=F32)


def _layer_norm(y, g, b):
    mu = jnp.mean(y, axis=-1, keepdims=True)
    yc = y - mu
    var = jnp.mean(yc * yc, axis=-1, keepdims=True)
    return yc * lax.rsqrt(var + LN_EPS) * g + b


def _resident(shape):
    nd = len(shape)
    return pl.BlockSpec(shape, lambda *_: (0,) * nd, pipeline_mode=pl.Buffered(1))


class _WeightStager:
    def __init__(self, jobs):
        self.staged, self.next_in_slot, first, users = [], {}, [], {}
        for src, ring, dst in jobs:
            mine = users.setdefault(id(ring), [])
            slot, sem = ring[len(mine) % len(ring)]
            if len(mine) < len(ring):
                first.append(len(self.staged))
            else:
                self.next_in_slot[mine[-len(ring)]] = len(self.staged)
            mine.append(len(self.staged))
            self.staged.append((pltpu.make_async_copy(src, slot, sem), slot, dst))
        self.done = 0
        for i in first:
            self.staged[i][0].start()

    def run(self, count=None):
        end = len(self.staged) if count is None else min(self.done + count, len(self.staged))
        for i in range(self.done, end):
            copy, slot, dst = self.staged[i]
            copy.wait()
            dst[...] = slot[...].astype(BF16)
            if i in self.next_in_slot:
                self.staged[self.next_in_slot[i]][0].start()
        self.done = end


def _interleave(*job_lists):
    keyed = [((i + 0.5) / len(jobs), n, job) for n, jobs in enumerate(job_lists) for i, job in enumerate(jobs)]
    return [job for _, _, job in sorted(keyed, key=lambda entry: entry[:2])]


def _ring(buf, sems, rows=None):
    if rows is None:
        return [(buf.at[i], sems.at[i]) for i in range(buf.shape[0])]
    return [(buf.at[pl.ds(i * rows, rows), :], sems.at[i]) for i in range(buf.shape[0] // rows)]


def _row_chunks(w_hbm, col0, dst, ring):
    rows, cols = dst.shape
    step, width = ring[0][0].shape
    assert rows % step == 0 and width == cols
    return [(w_hbm.at[0, pl.ds(r0, step), pl.ds(col0, cols)], ring, dst.at[pl.ds(r0, step), :])
            for r0 in range(0, rows, step)]


STAGE_IN_FLIGHT = 4
RING_FF = (STAGE_IN_FLIGHT, 64, D_FF)
MODEL_RING_ROWS = ROW_TILE // STAGE_IN_FLIGHT
RING_NARROW = (STAGE_IN_FLIGHT, 128, UQKV_WIDTH - D_MODEL)
RING_GROUP = (1, len(POOL_WINDOWS) * POOL_GROUP, POOL_GROUP)


def _ring_scratch(*rings):
    return ([pltpu.VMEM(ring, F32) for ring in rings]
            + [pltpu.SemaphoreType.DMA((ring[0],)) for ring in rings]
            + [pltpu.SemaphoreType.DMA((STAGE_IN_FLIGHT,))])


FF_SLOTS = 2 * (D_FF // FF_CHUNK)


def _swiglu_residual(x, w1_ref, w3_ref, w2_ref, h_s, side_work, out_ref):
    assert all(0 <= slot <= FF_SLOTS for slot in side_work)
    run = lambda slot: side_work.get(slot, lambda: None)()
    xb = x.astype(BF16)
    for j in range(D_FF // FF_CHUNK):
        cols = slice(j * FF_CHUNK, (j + 1) * FF_CHUNK)
        a = _dot(xb, w1_ref[:, cols])
        run(2 * j)
        b = _dot(xb, w3_ref[:, cols])
        h_s[:, cols] = ((a * jax.nn.sigmoid(a)) * b).astype(BF16)
        run(2 * j + 1)
    run(FF_SLOTS)
    h = h_s[...]
    for c0 in range(0, D_MODEL, OUT_BLOCK):
        cols = slice(c0, c0 + OUT_BLOCK)
        out_ref[:, cols] = ALPHA * x[:, cols] + 0.5 * _dot(h, w2_ref[:, cols])


def _rope(x, cos, sin_signed, first_half):
    fwd = pltpu.roll(x, LANES - HEAD_DIM // 2, axis=1)
    bwd = pltpu.roll(x, HEAD_DIM // 2, axis=1)
    return x * cos + jnp.where(first_half, fwd, bwd) * sin_signed


def _swap_halves(x):
    return pltpu.roll(x, HEAD_DIM, axis=1)


def _swap_halves_wide(x):
    return jnp.concatenate([_swap_halves(x[:, c:c + LANES]) for c in range(0, x.shape[1], LANES)], axis=1)


def _lane_split(x, x_sw, kh):
    low = lax.broadcasted_iota(jnp.int32, (1, LANES), 1) < HEAD_DIM
    lo, hi = (x, x_sw) if kh == 0 else (x_sw, x)
    return jnp.concatenate([jnp.where(low, lo, 0.0), jnp.where(low, 0.0, hi)], axis=0).astype(BF16)


def _sink_softmax(q_pairs, keys, kh, bias, sinks_ref):
    s = _dot_nt(q_pairs, _lane_split(keys, _swap_halves(keys), kh)) + bias
    second_pair = lax.broadcasted_iota(jnp.int32, (s.shape[0], 1), 0) >= WINDOW
    probs, denoms = [], []
    for c in range(2):
        sc = s[:, c * KEY_SPAN:(c + 1) * KEY_SPAN]
        head = kh * HEADS_PER_KV + c
        sink = jnp.where(second_pair, sinks_ref[head + 2], sinks_ref[head])
        m = jnp.maximum(jnp.max(sc, axis=-1, keepdims=True), sink)
        p = jnp.exp(sc - m)
        denoms.append(jnp.sum(p, axis=-1, keepdims=True) + jnp.exp(sink - m))
        probs.append(p.astype(BF16))
    return jnp.concatenate(probs, axis=1), denoms


def _weighted_values(probs, denoms, vals, kh):
    low = lax.broadcasted_iota(jnp.int32, (1, LANES), 1) < HEAD_DIM
    o = _dot(probs, _lane_split(vals, _swap_halves(vals), kh))
    return o / jnp.where(low, denoms[0], denoms[1])


def _prompt_context_work(tile, q_c, k_c, v_c, u_c, bias_ref, sinks_ref,
                         pool_ref, attn_ref, kt_ref, vt_ref, ulast_ref):
    seq_tile = (tile + TILES_PER_SEQ) % TILES_PER_SEQ
    first_tile = seq_tile == 0
    softmaxed = {}

    def scores(unit, blk, kh):
        r0 = blk * WINDOW
        bias = bias_ref[jnp.where(first_tile, 1, 0)] if blk == 0 else bias_ref[0]
        c0 = 2 * kh * LANES
        q_pairs = jnp.concatenate([q_c[r0:r0 + WINDOW, c0:c0 + LANES],
                                   q_c[r0:r0 + WINDOW, c0 + LANES:c0 + 2 * LANES]], axis=0)
        softmaxed[unit] = _sink_softmax(q_pairs, k_c[r0:r0 + KEY_SPAN, :], kh, bias, sinks_ref)

    def values(unit, blk, kh):
        r0 = blk * WINDOW
        c0 = 2 * kh * LANES
        o = _weighted_values(*softmaxed.pop(unit), v_c[r0:r0 + KEY_SPAN, :], kh)
        attn_ref[r0:r0 + WINDOW, c0:c0 + LANES] = o[:WINDOW].astype(BF16)
        attn_ref[r0:r0 + WINDOW, c0 + LANES:c0 + 2 * LANES] = o[WINDOW:].astype(BF16)

    def pool(groups):
        pos = seq_tile * ROW_TILE + lax.broadcasted_iota(jnp.int32, (ROW_TILE, 1), 0)
        for g in groups:
            w = POOL_WINDOWS[g]
            cols = slice(g * POOL_GROUP, (g + 1) * POOL_GROUP)
            rows = u_c[:, cols]
            acc, span = rows, 1
            while span < w:
                acc = acc + pltpu.roll(acc, span, axis=0)
                span *= 2
            cur = rows[POOL_PAD:]
            cnt = jnp.minimum(pos + 1, w).astype(F32)
            pool_ref[:, cols] = (acc[POOL_PAD:] / cnt - cur).astype(BF16)

    def sequence_state():
        kt_ref[0] = k_c[ROW_TILE:ROW_TILE + WINDOW, :].T
        vt_ref[0] = v_c[ROW_TILE:ROW_TILE + WINDOW, :].T
        ulast_ref[0] = u_c[ROW_TILE:ROW_TILE + POOL_PAD, :]

    work = {"pool_wide": lambda: pool((3,)), "pool_narrow": lambda: pool((0, 1, 2)),
            "sequence_state": sequence_state}
    for blk in range(ROW_TILE // WINDOW):
        for kh in range(N_KV_HEADS):
            unit = blk * N_KV_HEADS + kh
            work["scores", unit] = lambda unit=unit, blk=blk, kh=kh: scores(unit, blk, kh)
            work["values", unit] = lambda unit=unit, blk=blk, kh=kh: values(unit, blk, kh)
    return work


def _prompt_bias():
    r = np.arange(2 * WINDOW)[:, None] % WINDOW
    c = np.arange(2 * KEY_SPAN)[None, :] % KEY_SPAN
    valid = (r <= c) & (c <= r + WINDOW)
    first = valid & (c >= WINDOW)
    return np.where(np.stack([valid, first]), 0.0, NEG_INF).astype(np.float32)


def _front_kernel(sinks_ref, xp_ref, xs_ref, w1_hbm, w3_hbm, w2_hbm, win_hbm, g_ref, b_ref, freq_ref, bias_ref,
                  h_ref, pa_ref, us_ref, qs_ref, ks_ref, vs_ref, kt_ref, vt_ref, ulast_ref,
                  z_c, q_c, k_c, v_c, u_c, h_s, rope_ref, w1_ref, w3_ref, w2_ref, wu_ref,
                  buf_ff, buf_narrow, sem_ff, sem_narrow, sem_model):
    r = pl.program_id(0) - 1
    prev = r - 1
    pool_ref = pa_ref.at[:, 0:POOL_WIDTH]
    attn_ref = pa_ref.at[:, POOL_WIDTH:POOL_WIDTH + Q_WIDTH]

    def stage_and_reset():
        ring_ff = _ring(buf_ff, sem_ff)
        ring_narrow = _ring(buf_narrow, sem_narrow)
        ring_model = _ring(z_c, sem_model, rows=MODEL_RING_ROWS)
        stager = _WeightStager(_interleave(
            _row_chunks(w1_hbm, 0, w1_ref, ring_ff) + _row_chunks(w3_hbm, 0, w3_ref, ring_ff),
            _row_chunks(win_hbm, 0, wu_ref.at[:, 0:D_MODEL], ring_model) + _row_chunks(w2_hbm, 0, w2_ref, ring_model),
            _row_chunks(win_hbm, D_MODEL, wu_ref.at[:, D_MODEL:UQKV_WIDTH], ring_narrow)))
        freq = freq_ref[...]
        lane = lax.broadcasted_iota(jnp.int32, (1, LANES), 1)
        sign = jnp.where((lane & (HEAD_DIM // 2)) == 0, -1.0, 1.0)
        row = lax.broadcasted_iota(jnp.int32, (ROW_TILE, 1), 0)
        per_tile = -(-len(stager.staged) // (TILES_PER_SEQ + 1))
        for tile in range(TILES_PER_SEQ + 1):
            pos = tile * ROW_TILE + row if tile < TILES_PER_SEQ else PAST_LEN + row // DEC_BATCH
            ang = pos.astype(F32) * freq
            rows = slice(tile * ROW_TILE, (tile + 1) * ROW_TILE)
            rope_ref[rows, 0:LANES] = jnp.cos(ang)
            rope_ref[rows, LANES:2 * LANES] = jnp.sin(ang) * sign
            stager.run(per_tile)
        stager.run()
        for ref in (z_c, q_c, k_c, v_c, u_c):
            ref[...] = jnp.zeros(ref.shape, ref.dtype)

    def norm_previous():
        h1 = _layer_norm(z_c[...], g_ref[...], b_ref[...])
        h_ref[...] = h1
        return h1.astype(BF16)

    def rope_tables():
        tile = jnp.clip(prev, 0, PROMPT_TILES)
        table = jnp.where(tile == PROMPT_TILES, TILES_PER_SEQ, tile % TILES_PER_SEQ)
        rows = pl.ds(pl.multiple_of(table * ROW_TILE, ROW_TILE), ROW_TILE)
        cos = rope_ref[rows, 0:LANES]
        lane = lax.broadcasted_iota(jnp.int32, cos.shape, 1)
        return cos, rope_ref[rows, LANES:2 * LANES], (lane & (HEAD_DIM // 2)) == 0

    def project_u(h1b):
        return _dot(h1b, wu_ref[:, 0:POOL_WIDTH])

    def project_q(h1b):
        z = _dot(h1b, wu_ref[:, POOL_WIDTH:POOL_WIDTH + Q_WIDTH])
        tables = rope_tables()
        return [(_rope(z[:, c:c + LANES], *tables) * (HEAD_DIM ** -0.5)).astype(BF16)
                for c in range(0, Q_WIDTH, LANES)]

    def project_kv(h1b):
        z = _dot(h1b, wu_ref[:, POOL_WIDTH + Q_WIDTH:UQKV_WIDTH])
        return _rope(z[:, :KV_WIDTH], *rope_tables()), z[:, KV_WIDTH:]

    def tile_step(x):
        normed = []

        def norm():
            normed.append(norm_previous())

        projected = []

        def project():
            projected.extend([project_u(normed[0]), project_q(normed[0]), *project_kv(normed[0])])

        def carry():
            u, q, k, v = projected
            starts_seq = (prev + TILES_PER_SEQ) % TILES_PER_SEQ == 0
            u_c[0:POOL_PAD, :] = jnp.where(starts_seq, 0.0, u_c[ROW_TILE:ROW_TILE + POOL_PAD, :])
            u_c[POOL_PAD:POOL_PAD + ROW_TILE, :] = u
            for c, qc in enumerate(q):
                q_c[:, c * LANES:(c + 1) * LANES] = qc
            k_c[0:WINDOW, :] = k_c[ROW_TILE:ROW_TILE + WINDOW, :]
            v_c[0:WINDOW, :] = v_c[ROW_TILE:ROW_TILE + WINDOW, :]
            k_c[WINDOW:WINDOW + ROW_TILE, :] = k
            v_c[WINDOW:WINDOW + ROW_TILE, :] = v

        side = {0: norm, FF_SLOTS - 3: project, FF_SLOTS - 1: carry}
        stages = context_stages()
        assert len(stages) < FF_SLOTS - 3
        side.update(enumerate(stages, start=1))
        _swiglu_residual(x, w1_ref, w3_ref, w2_ref, h_s, side, z_c)

    def context_stages():
        context = _prompt_context_work(r - 2, q_c, k_c, v_c, u_c, bias_ref, sinks_ref,
                                       pool_ref, attn_ref, kt_ref, vt_ref, ulast_ref)
        stages = [("scores", 0)]
        for unit in range(1, 8):
            stages += [("scores", unit), ("values", unit - 1)]
        stages.append(("values", 7))

        def pooling_and_state():
            context["pool_wide"]()
            context["pool_narrow"]()
            context["sequence_state"]()

        return [context[stage] for stage in stages] + [pooling_and_state]

    def drain():
        for work in context_stages():
            work()
        h1b = norm_previous()
        us_ref[...] = project_u(h1b)
        for c, qc in enumerate(project_q(h1b)):
            qs_ref[:, c * LANES:(c + 1) * LANES] = qc
        ks_ref[...], vs_ref[...] = project_kv(h1b)

    is_tile_step = jnp.logical_and(r >= 0, r <= PROMPT_TILES)

    @pl.when(is_tile_step)
    def _():
        tile_step(jnp.where(r == PROMPT_TILES, xs_ref[...], xp_ref[...]))

    @pl.when(jnp.logical_not(is_tile_step))
    def _():
        pl.when(r == ROW_TILES)(drain)
        pl.when(r < 0)(stage_and_reset)


def _front(xp, xs, w1, w3, w2, w_in, g, b, freq, sinks):
    rows = ROW_TILES * ROW_TILE

    def lagged(width):
        return pl.BlockSpec((ROW_TILE, width), lambda s: (jnp.clip(s - 3, 0, PROMPT_TILES - 1), 0))

    def seq_of_lagged(shape):
        return pl.BlockSpec(shape, lambda s: (jnp.clip(s - 3, 0, PROMPT_TILES - 1) // TILES_PER_SEQ, 0, 0))

    sample = lambda width: pl.BlockSpec((ROW_TILE, width), lambda r: (0, 0))
    hbm = pl.BlockSpec(memory_space=pl.ANY)
    return pl.pallas_call(
        _front_kernel,
        out_shape=(jax.ShapeDtypeStruct((rows, D_MODEL), F32),
                   jax.ShapeDtypeStruct((PROMPT_ROWS, POOL_WIDTH + Q_WIDTH), BF16),
                   jax.ShapeDtypeStruct((SAMPLE_ROWS, POOL_WIDTH), F32),
                   jax.ShapeDtypeStruct((SAMPLE_ROWS, Q_WIDTH), BF16),
                   jax.ShapeDtypeStruct((SAMPLE_ROWS, KV_WIDTH), F32),
                   jax.ShapeDtypeStruct((SAMPLE_ROWS, KV_WIDTH), F32),
                   jax.ShapeDtypeStruct((BATCH, KV_WIDTH, WINDOW), F32),
                   jax.ShapeDtypeStruct((BATCH, KV_WIDTH, WINDOW), F32),
                   jax.ShapeDtypeStruct((BATCH, POOL_PAD, POOL_WIDTH), F32)),
        grid=(ROW_TILES + 2,),
        in_specs=[pl.BlockSpec(memory_space=pltpu.SMEM),
                  pl.BlockSpec((ROW_TILE, D_MODEL), lambda s: (jnp.clip(s - 1, 0, PROMPT_TILES - 1), 0)),
                  _resident((SAMPLE_ROWS, D_MODEL)),
                  hbm, hbm, hbm, hbm, _resident((1, D_MODEL)), _resident((1, D_MODEL)),
                  _resident((1, LANES)), _resident((2, 2 * WINDOW, 2 * KEY_SPAN))],
        out_specs=(pl.BlockSpec((ROW_TILE, D_MODEL), lambda s: (jnp.clip(s - 2, 0, PROMPT_TILES), 0)),
                   lagged(POOL_WIDTH + Q_WIDTH),
                   sample(POOL_WIDTH), sample(Q_WIDTH), sample(KV_WIDTH), sample(KV_WIDTH),
                   seq_of_lagged((1, KV_WIDTH, WINDOW)), seq_of_lagged((1, KV_WIDTH, WINDOW)),
                   seq_of_lagged((1, POOL_PAD, POOL_WIDTH))),
        scratch_shapes=[pltpu.VMEM((ROW_TILE, D_MODEL), F32),
                        pltpu.VMEM((ROW_TILE, Q_WIDTH), BF16),
                        pltpu.VMEM((WINDOW + ROW_TILE, KV_WIDTH), F32),
                        pltpu.VMEM((WINDOW + ROW_TILE, KV_WIDTH), F32),
                        pltpu.VMEM((POOL_PAD + ROW_TILE, POOL_WIDTH), F32),
                        pltpu.VMEM((ROW_TILE, D_FF), BF16),
                        pltpu.VMEM(((TILES_PER_SEQ + 1) * ROW_TILE, 2 * LANES), F32),
                        pltpu.VMEM((D_MODEL, D_FF), BF16), pltpu.VMEM((D_MODEL, D_FF), BF16),
                        pltpu.VMEM((D_FF, D_MODEL), BF16), pltpu.VMEM((D_MODEL, UQKV_WIDTH), BF16)]
                       + _ring_scratch(RING_FF, RING_NARROW),
        compiler_params=pltpu.CompilerParams(dimension_semantics=("arbitrary",),
                                             vmem_limit_bytes=VMEM_LIMIT_BYTES),
        name="front",
    )(sinks, xp, xs, w1, w3, w2, w_in, g, b, freq, jnp.asarray(_prompt_bias()))


def _sample_bias():
    row = np.arange(HEADS_PER_KV * DEC_SEQ * SUB_GROUP)
    row_t = (row // SUB_GROUP) % DEC_SEQ
    row_b = row % SUB_GROUP
    col = np.arange(SUB_GROUP * WINDOW)
    ok_c = (col[None, :] // WINDOW == row_b[:, None]) & (col[None, :] % WINDOW >= row_t[:, None])
    new = np.arange(DEC_SEQ * SUB_GROUP)
    ok_n = (new[None, :] % SUB_GROUP == row_b[:, None]) & (new[None, :] // SUB_GROUP <= row_t[:, None])
    to_bias = lambda ok: np.where(ok, 0.0, NEG_INF).astype(np.float32)
    return to_bias(ok_c), to_bias(ok_n)


def _sample_ctx_kernel(sinks_ref, q0, q1, q2, q3, k0, k1, k2, k3, v0, v1, v2, v3, u0, u1, u2, u3,
                       ckt_ref, cvt_ref, cu_ref, bias_c_ref, bias_n_ref,
                       attn_ref, pool_ref, kt_out, vt_out, pu_out, kbt_s, vbt_s):
    q_t = [q[...].astype(F32) for q in (q0, q1, q2, q3)]
    k_t = [k[...] for k in (k0, k1, k2, k3)]
    v_t = [v[...] for v in (v0, v1, v2, v3)]

    rows_u = [cu_ref[i] for i in range(POOL_BUF)] + [u[...] for u in (u0, u1, u2, u3)]
    for t in range(DEC_SEQ):
        pooled = []
        for g, w in enumerate(POOL_WINDOWS):
            cols = slice(g * POOL_GROUP, (g + 1) * POOL_GROUP)
            cur = rows_u[POOL_BUF + t][:, cols]
            acc = cur
            for j in range(1, w):
                acc = acc + rows_u[POOL_BUF + t - j][:, cols]
            pooled.append(acc / float(w) - cur)
        pool_ref[t] = jnp.concatenate(pooled, axis=1)
    for i in range(POOL_BUF):
        pu_out[i] = rows_u[i + DEC_SEQ]

    bias_c = bias_c_ref[...]
    bias_n = bias_n_ref[...]
    head_of_row = lax.broadcasted_iota(jnp.int32, (bias_c.shape[0], 1), 0) // (DEC_SEQ * SUB_GROUP)
    low = lax.broadcasted_iota(jnp.int32, (1, LANES), 1) < HEAD_DIM
    q_sw = [_swap_halves_wide(q) for q in q_t]
    k_sw = [_swap_halves(k) for k in k_t]
    v_sw = [_swap_halves(v) for v in v_t]
    def scores(sub, kh):
        rows = slice(sub * SUB_GROUP, (sub + 1) * SUB_GROUP)

        def q_piece(t, head):
            src = q_t[t] if head % 2 == 0 else q_sw[t]
            chunk = head // 2
            return jnp.where(low, src[rows, chunk * LANES:(chunk + 1) * LANES], 0.0)

        def kv_first(c):
            return c if kh == 0 else jnp.concatenate([c[HEAD_DIM:], c[:HEAD_DIM]], axis=0)

        lhs = jnp.concatenate([q_piece(t, kh * HEADS_PER_KV + g)
                               for g in range(HEADS_PER_KV) for t in range(DEC_SEQ)], axis=0).astype(BF16)
        kcat = jnp.concatenate([kv_first(ckt_ref[sub * SUB_GROUP + b]) for b in range(SUB_GROUP)],
                               axis=1).astype(BF16)
        knew = jnp.concatenate([(k_t[t] if kh == 0 else k_sw[t])[rows] for t in range(DEC_SEQ)],
                               axis=0).astype(BF16)
        s_c = _dot(lhs, kcat) + bias_c
        s_n = _dot_nt(lhs, knew) + bias_n
        sink = jnp.zeros(head_of_row.shape, F32)
        for g in range(HEADS_PER_KV):
            sink = jnp.where(head_of_row == g, sinks_ref[kh * HEADS_PER_KV + g], sink)
        m = jnp.maximum(jnp.maximum(jnp.max(s_c, axis=-1, keepdims=True),
                                    jnp.max(s_n, axis=-1, keepdims=True)), sink)
        p_c = jnp.exp(s_c - m)
        p_n = jnp.exp(s_n - m)
        denom = (jnp.sum(p_c, axis=-1, keepdims=True) + jnp.sum(p_n, axis=-1, keepdims=True)
                 + jnp.exp(sink - m))
        return p_c.astype(BF16), p_n.astype(BF16), denom

    def values(sub, kh, p_c, p_n, denom):
        rows = slice(sub * SUB_GROUP, (sub + 1) * SUB_GROUP)

        def kv_twice(c):
            part = c[kh * HEAD_DIM:(kh + 1) * HEAD_DIM]
            return jnp.concatenate([part, part], axis=0)

        vcat = jnp.concatenate([kv_twice(cvt_ref[sub * SUB_GROUP + b]) for b in range(SUB_GROUP)],
                               axis=1).astype(BF16)
        vnew = jnp.concatenate([(jnp.where(low, v_t[t], v_sw[t]) if kh == 0 else
                                 jnp.where(low, v_sw[t], v_t[t]))[rows] for t in range(DEC_SEQ)],
                               axis=0).astype(BF16)
        o = (_dot_nt(p_c, vcat) + _dot(p_n, vnew)) / denom
        for t in range(DEC_SEQ):
            for pair in range(HEADS_PER_KV // 2):
                piece = lambda g: o[(g * DEC_SEQ + t) * SUB_GROUP:(g * DEC_SEQ + t + 1) * SUB_GROUP]
                c0 = (kh * HEADS_PER_KV // 2 + pair) * LANES
                attn_ref[t, rows, c0:c0 + LANES] = jnp.where(low, piece(2 * pair), piece(2 * pair + 1))

    units = [(sub, kh) for sub in range(SEQ_GROUP // SUB_GROUP) for kh in range(N_KV_HEADS)]
    pending = None
    for unit in units:
        softmaxed = scores(*unit)
        if pending is not None:
            values(*pending)
        pending = (*unit, *softmaxed)
    values(*pending)

    if DEC_SEQ * SEQ_GROUP < LANES:
        zeros = jnp.zeros((LANES - DEC_SEQ * SEQ_GROUP, KV_WIDTH), F32)
        kbt_s[DEC_SEQ * SEQ_GROUP:, :] = zeros
        vbt_s[DEC_SEQ * SEQ_GROUP:, :] = zeros
    for t in range(DEC_SEQ):
        kbt_s[pl.ds(t, SEQ_GROUP, stride=DEC_SEQ), :] = k_t[t]
        vbt_s[pl.ds(t, SEQ_GROUP, stride=DEC_SEQ), :] = v_t[t]
    knew_t = kbt_s[...].T
    vnew_t = vbt_s[...].T
    keep = lax.broadcasted_iota(jnp.int32, (1, WINDOW), 1) < WINDOW - DEC_SEQ
    for b in range(SEQ_GROUP):
        shift_new = WINDOW - DEC_SEQ - DEC_SEQ * b
        kt_out[b] = jnp.where(keep, pltpu.roll(ckt_ref[b], WINDOW - DEC_SEQ, axis=1),
                              pltpu.roll(knew_t, shift_new, axis=1))
        vt_out[b] = jnp.where(keep, pltpu.roll(cvt_ref[b], WINDOW - DEC_SEQ, axis=1),
                              pltpu.roll(vnew_t, shift_new, axis=1))


def _sample_ctx(us, qs, ks, vs, ckt, cvt, cu, sinks):
    groups = DEC_BATCH // SEQ_GROUP

    def token_rows(t, width):
        return pl.BlockSpec((SEQ_GROUP, width), lambda i: (t * groups + i, 0))

    def per_token(width):
        return [token_rows(t, width) for t in range(DEC_SEQ)]

    cache_spec = pl.BlockSpec((SEQ_GROUP, KV_WIDTH, WINDOW), lambda i: (i, 0, 0))
    pool_rows_spec = pl.BlockSpec((POOL_BUF, SEQ_GROUP, POOL_WIDTH), lambda i: (0, i, 0))
    by_token = lambda width: pl.BlockSpec((DEC_SEQ, SEQ_GROUP, width), lambda i: (0, i, 0))
    bias_c, bias_n = _sample_bias()
    return pl.pallas_call(
        _sample_ctx_kernel,
        out_shape=(jax.ShapeDtypeStruct((DEC_SEQ, DEC_BATCH, Q_WIDTH), F32),
                   jax.ShapeDtypeStruct((DEC_SEQ, DEC_BATCH, POOL_WIDTH), F32),
                   jax.ShapeDtypeStruct((DEC_BATCH, KV_WIDTH, WINDOW), F32),
                   jax.ShapeDtypeStruct((DEC_BATCH, KV_WIDTH, WINDOW), F32),
                   jax.ShapeDtypeStruct((POOL_BUF, DEC_BATCH, POOL_WIDTH), F32)),
        grid=(groups,),
        in_specs=[pl.BlockSpec(memory_space=pltpu.SMEM)]
                 + per_token(Q_WIDTH) + per_token(KV_WIDTH) + per_token(KV_WIDTH) + per_token(POOL_WIDTH)
                 + [cache_spec, cache_spec, pool_rows_spec, _resident(bias_c.shape), _resident(bias_n.shape)],
        out_specs=(by_token(Q_WIDTH), by_token(POOL_WIDTH), cache_spec, cache_spec, pool_rows_spec),
        scratch_shapes=[pltpu.VMEM((LANES, KV_WIDTH), F32), pltpu.VMEM((LANES, KV_WIDTH), F32)],
        compiler_params=pltpu.CompilerParams(dimension_semantics=("parallel",),
                                             vmem_limit_bytes=VMEM_LIMIT_BYTES),
        name="sample_ctx",
    )(sinks, *([qs] * DEC_SEQ), *([ks] * DEC_SEQ), *([vs] * DEC_SEQ), *([us] * DEC_SEQ),
      ckt, cvt, cu, jnp.asarray(bias_c), jnp.asarray(bias_n))


def _back_kernel(h_ref, pa_ref, pools_ref, attns_ref,
                 win_hbm, wgrp_hbm, scale_ref, wpo_hbm, wao_hbm, wout_hbm, g2_ref, b2_ref,
                 w1_hbm, w3_hbm, w2_hbm, g3_ref, b3_ref, yp_ref, ys_ref, z2_c, z3_c, h_s,
                 wg_ref, wgrp_ref, wpo_ref, wao_ref, wout_ref, w1_ref, w3_ref, w2_ref,
                 buf_ff, buf_group, sem_ff, sem_group, sem_model):
    t = pl.program_id(0)

    def branch_outputs():
        is_sample = t >= PROMPT_TILES
        pool_in = jnp.where(is_sample, pools_ref[...].astype(BF16), pa_ref[:, 0:POOL_WIDTH])
        attn_o = jnp.where(is_sample, attns_ref[...].astype(BF16), pa_ref[:, POOL_WIDTH:POOL_WIDTH + Q_WIDTH])
        zs = [_dot(pool_in[:, g * POOL_GROUP:(g + 1) * POOL_GROUP],
                   wgrp_ref[g * POOL_GROUP:(g + 1) * POOL_GROUP, :])
              for g in range(len(POOL_WINDOWS))]
        pool_z = jnp.concatenate(zs, axis=1) * scale_ref[...]
        return _dot(pool_z.astype(BF16), wpo_ref[...]), _dot(attn_o, wao_ref[...])

    def gate_and_merge(h1b, a, b):
        gates = jax.nn.sigmoid(_dot(h1b, wg_ref[...]))
        return (gates[:, :D_MODEL] * a + gates[:, D_MODEL:] * b).astype(BF16)

    def project_out(h1, merged):
        z2_c[...] = ALPHA * h1 + _dot(merged, wout_ref[...])

    is_tile_step = jnp.logical_and(t >= 1, t <= ROW_TILES)

    @pl.when(is_tile_step)
    def _():
        h1 = h_ref[...]
        a, b = branch_outputs()
        y = _layer_norm(z3_c[...], g3_ref[...], b3_ref[...])
        yp_ref[...] = y
        h2 = _layer_norm(z2_c[...], g2_ref[...], b2_ref[...])
        h1b = jnp.where(t <= ROW_TILES, h1.astype(BF16), y.astype(BF16))
        merged = []
        side = {6: lambda: merged.append(gate_and_merge(h1b, a, b)), 16: lambda: project_out(h1, merged[0])}
        _swiglu_residual(h2, w1_ref, w3_ref, w2_ref, h_s, side, z3_c)

    def last_norm():
        ys_ref[...] = _layer_norm(z3_c[...], g3_ref[...], b3_ref[...])

    def stage_and_first_tile():
        ring_ff = _ring(buf_ff, sem_ff)
        ring_group = _ring(buf_group, sem_group)
        ring_model = _ring(z3_c, sem_model, rows=MODEL_RING_ROWS)
        mixer = (_row_chunks(wgrp_hbm, 0, wgrp_ref, ring_group)
                 + _row_chunks(wpo_hbm, 0, wpo_ref, ring_model)
                 + _row_chunks(wao_hbm, 0, wao_ref, ring_model)
                 + _row_chunks(win_hbm, UQKV_WIDTH, wg_ref.at[:, 0:D_MODEL], ring_model)
                 + _row_chunks(win_hbm, UQKV_WIDTH + D_MODEL, wg_ref.at[:, D_MODEL:2 * D_MODEL], ring_model)
                 + _row_chunks(wout_hbm, 0, wout_ref, ring_model))
        up = _row_chunks(w1_hbm, 0, w1_ref, ring_ff) + _row_chunks(w3_hbm, 0, w3_ref, ring_ff)
        down = _row_chunks(w2_hbm, 0, w2_ref, ring_model)
        stager = _WeightStager(_interleave(mixer, up) + down)
        stager.run(len(mixer) + len(up))
        h1 = h_ref[...]
        a, b = branch_outputs()
        stager.run(len(down) // 3)
        merged = gate_and_merge(h1.astype(BF16), a, b)
        stager.run(len(down) // 3)
        project_out(h1, merged)
        stager.run()
        z3_c[...] = jnp.zeros(z3_c.shape, z3_c.dtype)

    @pl.when(jnp.logical_not(is_tile_step))
    def _():
        pl.when(t == ROW_TILES + 1)(last_norm)
        pl.when(t == 0)(stage_and_first_tile)


def _back(h1, pool_attn, pool_s, attn_s, w_in, wgrp, scale, wpo, wao, wout, g2, b2, w1, w3, w2, g3, b3):
    lagged = pl.BlockSpec((ROW_TILE, D_MODEL), lambda t: (jnp.clip(t - 2, 0, PROMPT_TILES - 1), 0))
    hbm = pl.BlockSpec(memory_space=pl.ANY)
    return pl.pallas_call(
        _back_kernel,
        out_shape=(jax.ShapeDtypeStruct((PROMPT_ROWS, D_MODEL), F32),
                   jax.ShapeDtypeStruct((SAMPLE_ROWS, D_MODEL), F32)),
        grid=(ROW_TILES + 2,),
        in_specs=[pl.BlockSpec((ROW_TILE, D_MODEL), lambda t: (jnp.minimum(t, PROMPT_TILES), 0)),
                  pl.BlockSpec((ROW_TILE, POOL_WIDTH + Q_WIDTH), lambda t: (jnp.minimum(t, PROMPT_TILES - 1), 0)),
                  _resident((SAMPLE_ROWS, POOL_WIDTH)), _resident((SAMPLE_ROWS, Q_WIDTH)),
                  hbm, hbm, _resident((1, POOL_WIDTH)), hbm, hbm, hbm,
                  _resident((1, D_MODEL)), _resident((1, D_MODEL)),
                  hbm, hbm, hbm, _resident((1, D_MODEL)), _resident((1, D_MODEL))],
        out_specs=(lagged, pl.BlockSpec((ROW_TILE, D_MODEL), lambda r: (0, 0))),
        scratch_shapes=[pltpu.VMEM((ROW_TILE, D_MODEL), F32), pltpu.VMEM((ROW_TILE, D_MODEL), F32),
                        pltpu.VMEM((ROW_TILE, D_FF), BF16),
                        pltpu.VMEM((D_MODEL, 2 * D_MODEL), BF16),
                        pltpu.VMEM((len(POOL_WINDOWS) * POOL_GROUP, POOL_GROUP), BF16),
                        pltpu.VMEM((POOL_WIDTH, D_MODEL), BF16), pltpu.VMEM((Q_WIDTH, D_MODEL), BF16),
                        pltpu.VMEM((D_MODEL, D_MODEL), BF16),
                        pltpu.VMEM((D_MODEL, D_FF), BF16), pltpu.VMEM((D_MODEL, D_FF), BF16),
                        pltpu.VMEM((D_FF, D_MODEL), BF16)]
                       + _ring_scratch(RING_FF, RING_GROUP),
        compiler_params=pltpu.CompilerParams(dimension_semantics=("arbitrary",),
                                             vmem_limit_bytes=VMEM_LIMIT_BYTES),
        name="back",
    )(h1, pool_attn, pool_s, attn_s, w_in, wgrp, scale, wpo, wao, wout, g2, b2, w1, w3, w2, g3, b3)


def kernel(x_prompt, x_sample, cache_pool_u, cache_k_win, cache_v_win, w_in, pool_w_grp, pool_scale,
           attn_sinks, w_pool_out, w_attn_out, w_out, ffn1_w1, ffn1_w3, ffn1_w2, ffn2_w1, ffn2_w3,
           ffn2_w2, ln1_g, ln1_b, ln2_g, ln2_b, ln3_g, ln3_b):
    assert DEPTH == 1 and w_in.shape[0] == 1
    l = 0
    vec = lambda p: p[l].reshape(1, -1)
    sinks = attn_sinks[l]

    freq = jnp.tile(ROPE_THETA ** (-2.0 * jnp.arange(HEAD_DIM // 2, dtype=F32) / HEAD_DIM), 4).reshape(1, LANES)

    xp = x_prompt.reshape(PROMPT_ROWS, D_MODEL)
    xs = jnp.transpose(x_sample, (1, 0, 2)).reshape(SAMPLE_ROWS, D_MODEL)
    (h1, pool_attn, us, qs, ks, vs, kt_last, vt_last, u_last) = _front(
        xp, xs, ffn1_w1, ffn1_w3, ffn1_w2, w_in, vec(ln1_g), vec(ln1_b), freq, sinks)

    to_t = lambda c: jnp.transpose(c[l], (0, 2, 3, 1)).reshape(DEC_BATCH, KV_WIDTH, WINDOW)
    cu = jnp.transpose(cache_pool_u[l], (1, 0, 2))
    attn_s, pool_s, kt_s, vt_s, pu_s = _sample_ctx(us, qs, ks, vs, to_t(cache_k_win), to_t(cache_v_win), cu, sinks)

    wgrp = pool_w_grp.reshape(DEPTH, len(POOL_WINDOWS) * POOL_GROUP, POOL_GROUP)
    yp, ys = _back(h1, pool_attn, pool_s.reshape(SAMPLE_ROWS, POOL_WIDTH), attn_s.reshape(SAMPLE_ROWS, Q_WIDTH),
                   w_in, wgrp, vec(pool_scale), w_pool_out, w_attn_out, w_out, vec(ln2_g), vec(ln2_b),
                   ffn2_w1, ffn2_w3, ffn2_w2, vec(ln3_g), vec(ln3_b))
    yp = yp.reshape(BATCH, SEQ, D_MODEL)
    ys = jnp.transpose(ys.reshape(DEC_SEQ, DEC_BATCH, D_MODEL), (1, 0, 2))

    from_t = lambda c, n: jnp.transpose(c.reshape(n, N_KV_HEADS, HEAD_DIM, WINDOW), (0, 3, 1, 2))[None]
    pool_u_prompt = u_last[None, :, POOL_PAD - POOL_BUF:]
    pool_u_sample = jnp.transpose(pu_s, (1, 0, 2))[None]
    return (yp, ys, pool_u_prompt, from_t(kt_last, BATCH), from_t(vt_last, BATCH),
            pool_u_sample, from_t(kt_s, DEC_BATCH), from_t(vt_s, DEC_BATCH))
```

```python
import jax
import jax.numpy as jnp
import numpy as np
from jax import lax
from jax.experimental import pallas as pl
from jax.experimental.pallas import tpu as pltpu

D_MODEL = 1024
BATCH = 8
SEQ = 2048
DEC_BATCH = 128
DEC_SEQ = 4
PAST_LEN = 8192
POOL_WINDOWS = (2, 4, 8, 16)
POOL_GROUP = 128
POOL_WIDTH = 512
POOL_BUF = 15
N_HEADS = 8
N_KV_HEADS = 2
HEADS_PER_KV = N_HEADS // N_KV_HEADS
HEAD_DIM = 64
Q_WIDTH = 512
KV_WIDTH = 128
WINDOW = 128
ROPE_THETA = 10000.0
D_FF = 2816
DEPTH = 1
ALPHA = (2.0 * DEPTH) ** 0.25
LN_EPS = 1e-5
NEG_INF = -1e30
UQKV_WIDTH = POOL_WIDTH + Q_WIDTH + 2 * KV_WIDTH

LANES = 128
KEY_SPAN = 2 * WINDOW
VMEM_LIMIT_BYTES = 61 * 1024 * 1024

ROW_TILE = 512
TILES_PER_SEQ = SEQ // ROW_TILE
PROMPT_ROWS = BATCH * SEQ
PROMPT_TILES = PROMPT_ROWS // ROW_TILE
SAMPLE_ROWS = DEC_BATCH * DEC_SEQ
ROW_TILES = PROMPT_TILES + 1
POOL_PAD = 16
SEQ_GROUP = 32
SUB_GROUP = 8
FF_CHUNK = 256
OUT_BLOCK = 256

BF16 = jnp.bfloat16
F32 = jnp.float32


def _dot(a, b):
    return jnp.dot(a, b, preferred_element_type=F32)


def _dot_nt(a, b):
    return lax.dot_general(a, b, (((1,), (1,)), ((), ())), preferred_element_type=F32)


def _layer_norm(y, g, b):
    mu = jnp.mean(y, axis=-1, keepdims=True)
    yc = y - mu
    var = jnp.mean(yc * yc, axis=-1, keepdims=True)
    return yc * lax.rsqrt(var + LN_EPS) * g + b


def _resident(shape):
    nd = len(shape)
    return pl.BlockSpec(shape, lambda *_: (0,) * nd, pipeline_mode=pl.Buffered(1))


class _WeightStager:
    def __init__(self, jobs):
        self.staged, self.next_in_slot, first, users = [], {}, [], {}
        for src, ring, dst in jobs:
            mine = users.setdefault(id(ring), [])
            slot, sem = ring[len(mine) % len(ring)]
            if len(mine) < len(ring):
                first.append(len(self.staged))
            else:
                self.next_in_slot[mine[-len(ring)]] = len(self.staged)
            mine.append(len(self.staged))
            if slot.shape[1] > dst.shape[1]:
                slot = slot.at[:, pl.ds(0, dst.shape[1])]
            self.staged.append((pltpu.make_async_copy(src, slot, sem), slot, dst))
        self.done = 0
        for i in first:
            self.staged[i][0].start()

    def run(self, count=None):
        end = len(self.staged) if count is None else min(self.done + count, len(self.staged))
        for i in range(self.done, end):
            copy, slot, dst = self.staged[i]
            copy.wait()
            dst[...] = slot[...].astype(BF16)
            if i in self.next_in_slot:
                self.staged[self.next_in_slot[i]][0].start()
        self.done = end


def _interleave(*job_lists):
    keyed = [((i + 0.5) / len(jobs), n, job) for n, jobs in enumerate(job_lists) for i, job in enumerate(jobs)]
    return [job for _, _, job in sorted(keyed, key=lambda entry: entry[:2])]


def _ring(buf, sems, rows=None):
    if rows is None:
        return [(buf.at[i], sems.at[i]) for i in range(buf.shape[0])]
    return [(buf.at[pl.ds(i * rows, rows), :], sems.at[i]) for i in range(buf.shape[0] // rows)]


def _row_chunks(w_hbm, col0, dst, ring):
    rows, cols = dst.shape
    step, width = ring[0][0].shape
    assert rows % step == 0 and width >= cols
    return [(w_hbm.at[0, pl.ds(r0, step), pl.ds(col0, cols)], ring, dst.at[pl.ds(r0, step), :])
            for r0 in range(0, rows, step)]


STAGE_IN_FLIGHT = 4
RING_FF = (STAGE_IN_FLIGHT, 64, D_FF)
MODEL_RING_ROWS = ROW_TILE // STAGE_IN_FLIGHT
RING_NARROW = (STAGE_IN_FLIGHT, 128, UQKV_WIDTH - D_MODEL)
RING_GROUP = (1, len(POOL_WINDOWS) * POOL_GROUP, POOL_GROUP)


def _ring_scratch(*rings):
    return ([pltpu.VMEM(ring, F32) for ring in rings]
            + [pltpu.SemaphoreType.DMA((ring[0],)) for ring in rings]
            + [pltpu.SemaphoreType.DMA((STAGE_IN_FLIGHT,))])


FF_SLOTS = 2 * (D_FF // FF_CHUNK)


def _swiglu_residual(x, w1_ref, w3_ref, w2_ref, h_s, side_work, out_ref):
    assert all(0 <= slot <= FF_SLOTS for slot in side_work)
    run = lambda slot: side_work.get(slot, lambda: None)()
    xb = x.astype(BF16)
    for j in range(D_FF // FF_CHUNK):
        cols = slice(j * FF_CHUNK, (j + 1) * FF_CHUNK)
        a = _dot(xb, w1_ref[:, cols])
        run(2 * j)
        b = _dot(xb, w3_ref[:, cols])
        h_s[:, cols] = ((a * jax.nn.sigmoid(a)) * b).astype(BF16)
        run(2 * j + 1)
    run(FF_SLOTS)
    h = h_s[...]
    for c0 in range(0, D_MODEL, OUT_BLOCK):
        cols = slice(c0, c0 + OUT_BLOCK)
        out_ref[:, cols] = ALPHA * x[:, cols] + 0.5 * _dot(h, w2_ref[:, cols])


def _rope(x, cos, sin_signed, first_half):
    fwd = pltpu.roll(x, LANES - HEAD_DIM // 2, axis=1)
    bwd = pltpu.roll(x, HEAD_DIM // 2, axis=1)
    return x * cos + jnp.where(first_half, fwd, bwd) * sin_signed


def _swap_halves(x):
    return pltpu.roll(x, HEAD_DIM, axis=1)


def _swap_halves_wide(x):
    return jnp.concatenate([_swap_halves(x[:, c:c + LANES]) for c in range(0, x.shape[1], LANES)], axis=1)


def _lane_split(x, x_sw, kh):
    low = lax.broadcasted_iota(jnp.int32, (1, LANES), 1) < HEAD_DIM
    lo, hi = (x, x_sw) if kh == 0 else (x_sw, x)
    return jnp.concatenate([jnp.where(low, lo, 0.0), jnp.where(low, 0.0, hi)], axis=0).astype(BF16)


def _sink_softmax(q_pairs, keys, kh, bias, sinks_ref):
    s = _dot_nt(q_pairs, _lane_split(keys, _swap_halves(keys), kh)) + bias
    second_pair = lax.broadcasted_iota(jnp.int32, (s.shape[0], 1), 0) >= WINDOW
    probs, denoms = [], []
    for c in range(2):
        sc = s[:, c * KEY_SPAN:(c + 1) * KEY_SPAN]
        head = kh * HEADS_PER_KV + c
        sink = jnp.where(second_pair, sinks_ref[head + 2], sinks_ref[head])
        m = jnp.maximum(jnp.max(sc, axis=-1, keepdims=True), sink)
        p = jnp.exp(sc - m)
        denoms.append(jnp.sum(p, axis=-1, keepdims=True) + jnp.exp(sink - m))
        probs.append(p.astype(BF16))
    return jnp.concatenate(probs, axis=1), denoms


def _weighted_values(probs, denoms, vals, kh):
    low = lax.broadcasted_iota(jnp.int32, (1, LANES), 1) < HEAD_DIM
    o = _dot(probs, _lane_split(vals, _swap_halves(vals), kh))
    return o / jnp.where(low, denoms[0], denoms[1])


def _prompt_context_work(tile, q_c, k_c, v_c, u_c, bias_ref, sinks_ref,
                         pool_ref, attn_ref, kt_ref, vt_ref, ulast_ref):
    seq_tile = (tile + TILES_PER_SEQ) % TILES_PER_SEQ
    first_tile = seq_tile == 0
    softmaxed = {}

    def scores(unit, blk, kh):
        r0 = blk * WINDOW
        bias = bias_ref[jnp.where(first_tile, 1, 0)] if blk == 0 else bias_ref[0]
        c0 = 2 * kh * LANES
        q_pairs = jnp.concatenate([q_c[r0:r0 + WINDOW, c0:c0 + LANES],
                                   q_c[r0:r0 + WINDOW, c0 + LANES:c0 + 2 * LANES]], axis=0)
        softmaxed[unit] = _sink_softmax(q_pairs, k_c[r0:r0 + KEY_SPAN, :], kh, bias, sinks_ref)

    def values(unit, blk, kh):
        r0 = blk * WINDOW
        c0 = 2 * kh * LANES
        o = _weighted_values(*softmaxed.pop(unit), v_c[r0:r0 + KEY_SPAN, :], kh)
        attn_ref[r0:r0 + WINDOW, c0:c0 + LANES] = o[:WINDOW].astype(BF16)
        attn_ref[r0:r0 + WINDOW, c0 + LANES:c0 + 2 * LANES] = o[WINDOW:].astype(BF16)

    def pool(groups):
        pos = seq_tile * ROW_TILE + lax.broadcasted_iota(jnp.int32, (ROW_TILE, 1), 0)
        for g in groups:
            w = POOL_WINDOWS[g]
            cols = slice(g * POOL_GROUP, (g + 1) * POOL_GROUP)
            rows = u_c[:, cols]
            acc, span = rows, 1
            while span < w:
                acc = acc + pltpu.roll(acc, span, axis=0)
                span *= 2
            cur = rows[POOL_PAD:]
            cnt = jnp.minimum(pos + 1, w).astype(F32)
            pool_ref[:, cols] = (acc[POOL_PAD:] / cnt - cur).astype(BF16)

    def sequence_state():
        kt_ref[0] = k_c[ROW_TILE:ROW_TILE + WINDOW, :].T
        vt_ref[0] = v_c[ROW_TILE:ROW_TILE + WINDOW, :].T
        ulast_ref[0] = u_c[ROW_TILE:ROW_TILE + POOL_PAD, :]

    work = {"pool_wide": lambda: pool((3,)), "pool_narrow": lambda: pool((0, 1, 2)),
            "sequence_state": sequence_state}
    for blk in range(ROW_TILE // WINDOW):
        for kh in range(N_KV_HEADS):
            unit = blk * N_KV_HEADS + kh
            work["scores", unit] = lambda unit=unit, blk=blk, kh=kh: scores(unit, blk, kh)
            work["values", unit] = lambda unit=unit, blk=blk, kh=kh: values(unit, blk, kh)
    return work


def _prompt_bias():
    r = np.arange(2 * WINDOW)[:, None] % WINDOW
    c = np.arange(2 * KEY_SPAN)[None, :] % KEY_SPAN
    valid = (r <= c) & (c <= r + WINDOW)
    first = valid & (c >= WINDOW)
    return np.where(np.stack([valid, first]), 0.0, NEG_INF).astype(np.float32)


def _front_kernel(sinks_ref, xp_ref, xs_ref, w1_hbm, w3_hbm, w2_hbm, win_hbm, g_ref, b_ref, freq_ref, bias_ref,
                  h_ref, pa_ref, us_ref, qs_ref, ks_ref, vs_ref, kt_ref, vt_ref, ulast_ref,
                  z_c, q_c, k_c, v_c, u_c, h_s, rope_ref, w1_ref, w3_ref, w2_ref, wu_ref,
                  buf_ff, buf_narrow, sem_ff, sem_narrow, sem_model):
    r = pl.program_id(0) - 1
    prev = r - 1
    pool_ref = pa_ref.at[:, 0:POOL_WIDTH]
    attn_ref = pa_ref.at[:, POOL_WIDTH:POOL_WIDTH + Q_WIDTH]

    def stage_and_reset():
        ring_ff = _ring(buf_ff, sem_ff)
        ring_narrow = _ring(buf_narrow, sem_narrow)
        ring_model = _ring(z_c, sem_model, rows=MODEL_RING_ROWS)
        stager = _WeightStager(_interleave(
            _row_chunks(w1_hbm, 0, w1_ref, ring_ff) + _row_chunks(w3_hbm, 0, w3_ref, ring_ff),
            _row_chunks(win_hbm, 0, wu_ref.at[:, 0:D_MODEL], ring_model) + _row_chunks(w2_hbm, 0, w2_ref, ring_model),
            _row_chunks(win_hbm, D_MODEL, wu_ref.at[:, D_MODEL:UQKV_WIDTH], ring_narrow)))
        freq = freq_ref[...]
        lane = lax.broadcasted_iota(jnp.int32, (1, LANES), 1)
        sign = jnp.where((lane & (HEAD_DIM // 2)) == 0, -1.0, 1.0)
        row = lax.broadcasted_iota(jnp.int32, (ROW_TILE, 1), 0)
        per_tile = -(-len(stager.staged) // (TILES_PER_SEQ + 1))
        for tile in range(TILES_PER_SEQ + 1):
            pos = tile * ROW_TILE + row if tile < TILES_PER_SEQ else PAST_LEN + row // DEC_BATCH
            ang = pos.astype(F32) * freq
            rows = slice(tile * ROW_TILE, (tile + 1) * ROW_TILE)
            rope_ref[rows, 0:LANES] = jnp.cos(ang)
            rope_ref[rows, LANES:2 * LANES] = jnp.sin(ang) * sign
            stager.run(per_tile)
        stager.run()
        for ref in (z_c, q_c, k_c, v_c, u_c):
            ref[...] = jnp.zeros(ref.shape, ref.dtype)

    def norm_previous():
        h1 = _layer_norm(z_c[...], g_ref[...], b_ref[...])
        h_ref[...] = h1
        return h1.astype(BF16)

    def rope_tables():
        tile = jnp.clip(prev, 0, PROMPT_TILES)
        table = jnp.where(tile == PROMPT_TILES, TILES_PER_SEQ, tile % TILES_PER_SEQ)
        rows = pl.ds(pl.multiple_of(table * ROW_TILE, ROW_TILE), ROW_TILE)
        cos = rope_ref[rows, 0:LANES]
        lane = lax.broadcasted_iota(jnp.int32, cos.shape, 1)
        return cos, rope_ref[rows, LANES:2 * LANES], (lane & (HEAD_DIM // 2)) == 0

    def project_u(h1b):
        return _dot(h1b, wu_ref[:, 0:POOL_WIDTH])

    def project_q(h1b):
        z = _dot(h1b, wu_ref[:, POOL_WIDTH:POOL_WIDTH + Q_WIDTH])
        tables = rope_tables()
        return [(_rope(z[:, c:c + LANES], *tables) * (HEAD_DIM ** -0.5)).astype(BF16)
                for c in range(0, Q_WIDTH, LANES)]

    def project_kv(h1b):
        z = _dot(h1b, wu_ref[:, POOL_WIDTH + Q_WIDTH:UQKV_WIDTH])
        return _rope(z[:, :KV_WIDTH], *rope_tables()), z[:, KV_WIDTH:]

    def tile_step(x):
        normed = []

        def norm():
            normed.append(norm_previous())

        projected = []

        def project():
            projected.extend([project_u(normed[0]), project_q(normed[0]), *project_kv(normed[0])])

        def carry():
            u, q, k, v = projected
            starts_seq = (prev + TILES_PER_SEQ) % TILES_PER_SEQ == 0
            u_c[0:POOL_PAD, :] = jnp.where(starts_seq, 0.0, u_c[ROW_TILE:ROW_TILE + POOL_PAD, :])
            u_c[POOL_PAD:POOL_PAD + ROW_TILE, :] = u
            for c, qc in enumerate(q):
                q_c[:, c * LANES:(c + 1) * LANES] = qc
            k_c[0:WINDOW, :] = k_c[ROW_TILE:ROW_TILE + WINDOW, :]
            v_c[0:WINDOW, :] = v_c[ROW_TILE:ROW_TILE + WINDOW, :]
            k_c[WINDOW:WINDOW + ROW_TILE, :] = k
            v_c[WINDOW:WINDOW + ROW_TILE, :] = v

        side = {0: norm, FF_SLOTS - 3: project, FF_SLOTS - 1: carry}
        stages = context_stages()
        assert len(stages) < FF_SLOTS - 3
        side.update(enumerate(stages, start=1))
        _swiglu_residual(x, w1_ref, w3_ref, w2_ref, h_s, side, z_c)

    def context_stages():
        context = _prompt_context_work(r - 2, q_c, k_c, v_c, u_c, bias_ref, sinks_ref,
                                       pool_ref, attn_ref, kt_ref, vt_ref, ulast_ref)
        stages = [("scores", 0)]
        for unit in range(1, 8):
            stages += [("scores", unit), ("values", unit - 1)]
        stages.append(("values", 7))

        def pooling_and_state():
            context["pool_wide"]()
            context["pool_narrow"]()
            context["sequence_state"]()

        return [context[stage] for stage in stages] + [pooling_and_state]

    def drain():
        stages = context_stages()
        third = len(stages) // 3
        h1b = norm_previous()
        for work in stages[:third]:
            work()
        us_ref[...] = project_u(h1b)
        for work in stages[third:2 * third]:
            work()
        for c, qc in enumerate(project_q(h1b)):
            qs_ref[:, c * LANES:(c + 1) * LANES] = qc
        for work in stages[2 * third:]:
            work()
        ks_ref[...], vs_ref[...] = project_kv(h1b)

    is_tile_step = jnp.logical_and(r >= 0, r <= PROMPT_TILES)

    @pl.when(is_tile_step)
    def _():
        tile_step(jnp.where(r == PROMPT_TILES, xs_ref[...], xp_ref[...]))

    @pl.when(jnp.logical_not(is_tile_step))
    def _():
        pl.when(r == ROW_TILES)(drain)
        pl.when(r < 0)(stage_and_reset)


def _front(xp, xs, w1, w3, w2, w_in, g, b, freq, sinks):
    rows = ROW_TILES * ROW_TILE

    def lagged(width):
        return pl.BlockSpec((ROW_TILE, width), lambda s: (jnp.clip(s - 3, 0, PROMPT_TILES - 1), 0))

    def seq_of_lagged(shape):
        return pl.BlockSpec(shape, lambda s: (jnp.clip(s - 3, 0, PROMPT_TILES - 1) // TILES_PER_SEQ, 0, 0))

    sample = lambda width: pl.BlockSpec((ROW_TILE, width), lambda r: (0, 0))
    hbm = pl.BlockSpec(memory_space=pl.ANY)
    return pl.pallas_call(
        _front_kernel,
        out_shape=(jax.ShapeDtypeStruct((rows, D_MODEL), F32),
                   jax.ShapeDtypeStruct((PROMPT_ROWS, POOL_WIDTH + Q_WIDTH), BF16),
                   jax.ShapeDtypeStruct((SAMPLE_ROWS, POOL_WIDTH), F32),
                   jax.ShapeDtypeStruct((SAMPLE_ROWS, Q_WIDTH), BF16),
                   jax.ShapeDtypeStruct((SAMPLE_ROWS, KV_WIDTH), F32),
                   jax.ShapeDtypeStruct((SAMPLE_ROWS, KV_WIDTH), F32),
                   jax.ShapeDtypeStruct((BATCH, KV_WIDTH, WINDOW), F32),
                   jax.ShapeDtypeStruct((BATCH, KV_WIDTH, WINDOW), F32),
                   jax.ShapeDtypeStruct((BATCH, POOL_PAD, POOL_WIDTH), F32)),
        grid=(ROW_TILES + 2,),
        in_specs=[pl.BlockSpec(memory_space=pltpu.SMEM),
                  pl.BlockSpec((ROW_TILE, D_MODEL), lambda s: (jnp.clip(s - 1, 0, PROMPT_TILES - 1), 0)),
                  _resident((SAMPLE_ROWS, D_MODEL)),
                  hbm, hbm, hbm, hbm, _resident((1, D_MODEL)), _resident((1, D_MODEL)),
                  _resident((1, LANES)), _resident((2, 2 * WINDOW, 2 * KEY_SPAN))],
        out_specs=(pl.BlockSpec((ROW_TILE, D_MODEL), lambda s: (jnp.clip(s - 2, 0, PROMPT_TILES), 0)),
                   lagged(POOL_WIDTH + Q_WIDTH),
                   sample(POOL_WIDTH), sample(Q_WIDTH), sample(KV_WIDTH), sample(KV_WIDTH),
                   seq_of_lagged((1, KV_WIDTH, WINDOW)), seq_of_lagged((1, KV_WIDTH, WINDOW)),
                   seq_of_lagged((1, POOL_PAD, POOL_WIDTH))),
        scratch_shapes=[pltpu.VMEM((ROW_TILE, D_MODEL), F32),
                        pltpu.VMEM((ROW_TILE, Q_WIDTH), BF16),
                        pltpu.VMEM((WINDOW + ROW_TILE, KV_WIDTH), F32),
                        pltpu.VMEM((WINDOW + ROW_TILE, KV_WIDTH), F32),
                        pltpu.VMEM((POOL_PAD + ROW_TILE, POOL_WIDTH), F32),
                        pltpu.VMEM((ROW_TILE, D_FF), BF16),
                        pltpu.VMEM(((TILES_PER_SEQ + 1) * ROW_TILE, 2 * LANES), F32),
                        pltpu.VMEM((D_MODEL, D_FF), BF16), pltpu.VMEM((D_MODEL, D_FF), BF16),
                        pltpu.VMEM((D_FF, D_MODEL), BF16), pltpu.VMEM((D_MODEL, UQKV_WIDTH), BF16)]
                       + _ring_scratch(RING_FF, RING_NARROW),
        compiler_params=pltpu.CompilerParams(dimension_semantics=("arbitrary",),
                                             vmem_limit_bytes=VMEM_LIMIT_BYTES),
        name="front",
    )(sinks, xp, xs, w1, w3, w2, w_in, g, b, freq, jnp.asarray(_prompt_bias()))


def _sample_bias():
    row = np.arange(HEADS_PER_KV * DEC_SEQ * SUB_GROUP)
    row_t = (row // SUB_GROUP) % DEC_SEQ
    row_b = row % SUB_GROUP
    col = np.arange(SUB_GROUP * WINDOW)
    ok_c = (col[None, :] // WINDOW == row_b[:, None]) & (col[None, :] % WINDOW >= row_t[:, None])
    new = np.arange(DEC_SEQ * SUB_GROUP)
    ok_n = (new[None, :] % SUB_GROUP == row_b[:, None]) & (new[None, :] // SUB_GROUP <= row_t[:, None])
    to_bias = lambda ok: np.where(ok, 0.0, NEG_INF).astype(np.float32)
    return to_bias(ok_c), to_bias(ok_n)


def _sample_ctx_kernel(sinks_ref, q0, q1, q2, q3, k0, k1, k2, k3, v0, v1, v2, v3, u0, u1, u2, u3,
                       ckt_ref, cvt_ref, cu_ref, bias_c_ref, bias_n_ref,
                       attn_ref, pool_ref, kt_out, vt_out, pu_out, kbt_s, vbt_s):
    q_t = [q[...].astype(F32) for q in (q0, q1, q2, q3)]
    k_t = [k[...] for k in (k0, k1, k2, k3)]
    v_t = [v[...] for v in (v0, v1, v2, v3)]

    rows_u = [cu_ref[i] for i in range(POOL_BUF)] + [u[...] for u in (u0, u1, u2, u3)]
    for t in range(DEC_SEQ):
        pooled = []
        for g, w in enumerate(POOL_WINDOWS):
            cols = slice(g * POOL_GROUP, (g + 1) * POOL_GROUP)
            cur = rows_u[POOL_BUF + t][:, cols]
            acc = cur
            for j in range(1, w):
                acc = acc + rows_u[POOL_BUF + t - j][:, cols]
            pooled.append(acc / float(w) - cur)
        pool_ref[t] = jnp.concatenate(pooled, axis=1)
    for i in range(POOL_BUF):
        pu_out[i] = rows_u[i + DEC_SEQ]

    bias_c = bias_c_ref[...]
    bias_n = bias_n_ref[...]
    head_of_row = lax.broadcasted_iota(jnp.int32, (bias_c.shape[0], 1), 0) // (DEC_SEQ * SUB_GROUP)
    low = lax.broadcasted_iota(jnp.int32, (1, LANES), 1) < HEAD_DIM
    q_sw = [_swap_halves_wide(q) for q in q_t]
    k_sw = [_swap_halves(k) for k in k_t]
    v_sw = [_swap_halves(v) for v in v_t]
    def scores(sub, kh):
        rows = slice(sub * SUB_GROUP, (sub + 1) * SUB_GROUP)

        def q_piece(t, head):
            src = q_t[t] if head % 2 == 0 else q_sw[t]
            chunk = head // 2
            return jnp.where(low, src[rows, chunk * LANES:(chunk + 1) * LANES], 0.0)

        def kv_first(c):
            return c if kh == 0 else jnp.concatenate([c[HEAD_DIM:], c[:HEAD_DIM]], axis=0)

        lhs = jnp.concatenate([q_piece(t, kh * HEADS_PER_KV + g)
                               for g in range(HEADS_PER_KV) for t in range(DEC_SEQ)], axis=0).astype(BF16)
        kcat = jnp.concatenate([kv_first(ckt_ref[sub * SUB_GROUP + b]) for b in range(SUB_GROUP)],
                               axis=1).astype(BF16)
        knew = jnp.concatenate([(k_t[t] if kh == 0 else k_sw[t])[rows] for t in range(DEC_SEQ)],
                               axis=0).astype(BF16)
        s_c = _dot(lhs, kcat) + bias_c
        s_n = _dot_nt(lhs, knew) + bias_n
        sink = jnp.zeros(head_of_row.shape, F32)
        for g in range(HEADS_PER_KV):
            sink = jnp.where(head_of_row == g, sinks_ref[kh * HEADS_PER_KV + g], sink)
        m = jnp.maximum(jnp.maximum(jnp.max(s_c, axis=-1, keepdims=True),
                                    jnp.max(s_n, axis=-1, keepdims=True)), sink)
        p_c = jnp.exp(s_c - m)
        p_n = jnp.exp(s_n - m)
        denom = (jnp.sum(p_c, axis=-1, keepdims=True) + jnp.sum(p_n, axis=-1, keepdims=True)
                 + jnp.exp(sink - m))
        return p_c.astype(BF16), p_n.astype(BF16), denom

    def values(sub, kh, p_c, p_n, denom):
        rows = slice(sub * SUB_GROUP, (sub + 1) * SUB_GROUP)

        def kv_twice(c):
            part = c[kh * HEAD_DIM:(kh + 1) * HEAD_DIM]
            return jnp.concatenate([part, part], axis=0)

        vcat = jnp.concatenate([kv_twice(cvt_ref[sub * SUB_GROUP + b]) for b in range(SUB_GROUP)],
                               axis=1).astype(BF16)
        vnew = jnp.concatenate([(jnp.where(low, v_t[t], v_sw[t]) if kh == 0 else
                                 jnp.where(low, v_sw[t], v_t[t]))[rows] for t in range(DEC_SEQ)],
                               axis=0).astype(BF16)
        o = (_dot_nt(p_c, vcat) + _dot(p_n, vnew)) / denom
        for t in range(DEC_SEQ):
            for pair in range(HEADS_PER_KV // 2):
                piece = lambda g: o[(g * DEC_SEQ + t) * SUB_GROUP:(g * DEC_SEQ + t + 1) * SUB_GROUP]
                c0 = (kh * HEADS_PER_KV // 2 + pair) * LANES
                attn_ref[t, rows, c0:c0 + LANES] = jnp.where(low, piece(2 * pair), piece(2 * pair + 1))

    units = [(sub, kh) for sub in range(SEQ_GROUP // SUB_GROUP) for kh in range(N_KV_HEADS)]
    pending = None
    for unit in units:
        softmaxed = scores(*unit)
        if pending is not None:
            values(*pending)
        pending = (*unit, *softmaxed)
    values(*pending)

    if DEC_SEQ * SEQ_GROUP < LANES:
        zeros = jnp.zeros((LANES - DEC_SEQ * SEQ_GROUP, KV_WIDTH), F32)
        kbt_s[DEC_SEQ * SEQ_GROUP:, :] = zeros
        vbt_s[DEC_SEQ * SEQ_GROUP:, :] = zeros
    for t in range(DEC_SEQ):
        kbt_s[pl.ds(t, SEQ_GROUP, stride=DEC_SEQ), :] = k_t[t]
        vbt_s[pl.ds(t, SEQ_GROUP, stride=DEC_SEQ), :] = v_t[t]
    knew_t = kbt_s[...].T
    vnew_t = vbt_s[...].T
    keep = lax.broadcasted_iota(jnp.int32, (1, WINDOW), 1) < WINDOW - DEC_SEQ
    for b in range(SEQ_GROUP):
        shift_new = WINDOW - DEC_SEQ - DEC_SEQ * b
        kt_out[b] = jnp.where(keep, pltpu.roll(ckt_ref[b], WINDOW - DEC_SEQ, axis=1),
                              pltpu.roll(knew_t, shift_new, axis=1))
        vt_out[b] = jnp.where(keep, pltpu.roll(cvt_ref[b], WINDOW - DEC_SEQ, axis=1),
                              pltpu.roll(vnew_t, shift_new, axis=1))


def _sample_ctx(us, qs, ks, vs, ckt, cvt, cu, sinks):
    groups = DEC_BATCH // SEQ_GROUP

    def token_rows(t, width):
        return pl.BlockSpec((SEQ_GROUP, width), lambda i: (t * groups + i, 0))

    def per_token(width):
        return [token_rows(t, width) for t in range(DEC_SEQ)]

    cache_spec = pl.BlockSpec((SEQ_GROUP, KV_WIDTH, WINDOW), lambda i: (i, 0, 0))
    pool_rows_spec = pl.BlockSpec((POOL_BUF, SEQ_GROUP, POOL_WIDTH), lambda i: (0, i, 0))
    by_token = lambda width: pl.BlockSpec((DEC_SEQ, SEQ_GROUP, width), lambda i: (0, i, 0))
    bias_c, bias_n = _sample_bias()
    return pl.pallas_call(
        _sample_ctx_kernel,
        out_shape=(jax.ShapeDtypeStruct((DEC_SEQ, DEC_BATCH, Q_WIDTH), F32),
                   jax.ShapeDtypeStruct((DEC_SEQ, DEC_BATCH, POOL_WIDTH), F32),
                   jax.ShapeDtypeStruct((DEC_BATCH, KV_WIDTH, WINDOW), F32),
                   jax.ShapeDtypeStruct((DEC_BATCH, KV_WIDTH, WINDOW), F32),
                   jax.ShapeDtypeStruct((POOL_BUF, DEC_BATCH, POOL_WIDTH), F32)),
        grid=(groups,),
        in_specs=[pl.BlockSpec(memory_space=pltpu.SMEM)]
                 + per_token(Q_WIDTH) + per_token(KV_WIDTH) + per_token(KV_WIDTH) + per_token(POOL_WIDTH)
                 + [cache_spec, cache_spec, pool_rows_spec, _resident(bias_c.shape), _resident(bias_n.shape)],
        out_specs=(by_token(Q_WIDTH), by_token(POOL_WIDTH), cache_spec, cache_spec, pool_rows_spec),
        scratch_shapes=[pltpu.VMEM((LANES, KV_WIDTH), F32), pltpu.VMEM((LANES, KV_WIDTH), F32)],
        compiler_params=pltpu.CompilerParams(dimension_semantics=("parallel",),
                                             vmem_limit_bytes=VMEM_LIMIT_BYTES),
        name="sample_ctx",
    )(sinks, *([qs] * DEC_SEQ), *([ks] * DEC_SEQ), *([vs] * DEC_SEQ), *([us] * DEC_SEQ),
      ckt, cvt, cu, jnp.asarray(bias_c), jnp.asarray(bias_n))


def _back_kernel(h_ref, pa_ref, pools_ref, attns_ref,
                 win_hbm, wgrp_hbm, scale_ref, wpo_hbm, wao_hbm, wout_hbm, g2_ref, b2_ref,
                 w1_hbm, w3_hbm, w2_hbm, g3_ref, b3_ref, yp_ref, ys_ref, z2_c, z3_c, h_s,
                 wg_ref, wgrp_ref, wpo_ref, wao_ref, wout_ref, w1_ref, w3_ref, w2_ref,
                 buf_ff, buf_group, sem_ff, sem_group, sem_model):
    t = pl.program_id(0)

    def branch_outputs():
        is_sample = t >= PROMPT_TILES
        pool_in = jnp.where(is_sample, pools_ref[...].astype(BF16), pa_ref[:, 0:POOL_WIDTH])
        attn_o = jnp.where(is_sample, attns_ref[...].astype(BF16), pa_ref[:, POOL_WIDTH:POOL_WIDTH + Q_WIDTH])
        zs = [_dot(pool_in[:, g * POOL_GROUP:(g + 1) * POOL_GROUP],
                   wgrp_ref[g * POOL_GROUP:(g + 1) * POOL_GROUP, :])
              for g in range(len(POOL_WINDOWS))]
        pool_z = jnp.concatenate(zs, axis=1) * scale_ref[...]
        return _dot(pool_z.astype(BF16), wpo_ref[...]), _dot(attn_o, wao_ref[...])

    def gate_and_merge(h1b, a, b):
        gates = jax.nn.sigmoid(_dot(h1b, wg_ref[...]))
        return (gates[:, :D_MODEL] * a + gates[:, D_MODEL:] * b).astype(BF16)

    def project_out(h1, merged):
        z2_c[...] = ALPHA * h1 + _dot(merged, wout_ref[...])

    is_tile_step = jnp.logical_and(t >= 1, t <= ROW_TILES)

    @pl.when(is_tile_step)
    def _():
        h1 = h_ref[...]
        a, b = branch_outputs()
        y = _layer_norm(z3_c[...], g3_ref[...], b3_ref[...])
        yp_ref[...] = y
        h2 = _layer_norm(z2_c[...], g2_ref[...], b2_ref[...])
        h1b = jnp.where(t <= ROW_TILES, h1.astype(BF16), y.astype(BF16))
        merged = []
        side = {6: lambda: merged.append(gate_and_merge(h1b, a, b)), 16: lambda: project_out(h1, merged[0])}
        _swiglu_residual(h2, w1_ref, w3_ref, w2_ref, h_s, side, z3_c)

    def last_norm():
        ys_ref[...] = _layer_norm(z3_c[...], g3_ref[...], b3_ref[...])

    def stage_and_first_tile():
        ring_ff = _ring(buf_ff, sem_ff)
        ring_group = _ring(buf_group, sem_group)
        ring_model = _ring(z3_c, sem_model, rows=MODEL_RING_ROWS)
        mixer = _interleave(_row_chunks(wgrp_hbm, 0, wgrp_ref, ring_group)
                            + _row_chunks(wpo_hbm, 0, wpo_ref, ring_model)
                            + _row_chunks(wao_hbm, 0, wao_ref, ring_model)
                            + _row_chunks(wout_hbm, 0, wout_ref, ring_model),
                            _row_chunks(win_hbm, UQKV_WIDTH, wg_ref, ring_ff))
        swiglu = _interleave(_row_chunks(w1_hbm, 0, w1_ref, ring_ff) + _row_chunks(w3_hbm, 0, w3_ref, ring_ff),
                             _row_chunks(w2_hbm, 0, w2_ref, ring_model))
        stager = _WeightStager(mixer + swiglu)
        stager.run(len(mixer))
        h1 = h_ref[...]
        a, b = branch_outputs()
        stager.run(len(swiglu) // 3)
        merged = gate_and_merge(h1.astype(BF16), a, b)
        stager.run(len(swiglu) // 3)
        project_out(h1, merged)
        stager.run()
        z3_c[...] = jnp.zeros(z3_c.shape, z3_c.dtype)

    @pl.when(jnp.logical_not(is_tile_step))
    def _():
        pl.when(t == ROW_TILES + 1)(last_norm)
        pl.when(t == 0)(stage_and_first_tile)


def _back(h1, pool_attn, pool_s, attn_s, w_in, wgrp, scale, wpo, wao, wout, g2, b2, w1, w3, w2, g3, b3):
    lagged = pl.BlockSpec((ROW_TILE, D_MODEL), lambda t: (jnp.clip(t - 2, 0, PROMPT_TILES - 1), 0))
    hbm = pl.BlockSpec(memory_space=pl.ANY)
    return pl.pallas_call(
        _back_kernel,
        out_shape=(jax.ShapeDtypeStruct((PROMPT_ROWS, D_MODEL), F32),
                   jax.ShapeDtypeStruct((SAMPLE_ROWS, D_MODEL), F32)),
        grid=(ROW_TILES + 2,),
        in_specs=[pl.BlockSpec((ROW_TILE, D_MODEL), lambda t: (jnp.minimum(t, PROMPT_TILES), 0)),
                  pl.BlockSpec((ROW_TILE, POOL_WIDTH + Q_WIDTH), lambda t: (jnp.minimum(t, PROMPT_TILES - 1), 0)),
                  _resident((SAMPLE_ROWS, POOL_WIDTH)), _resident((SAMPLE_ROWS, Q_WIDTH)),
                  hbm, hbm, _resident((1, POOL_WIDTH)), hbm, hbm, hbm,
                  _resident((1, D_MODEL)), _resident((1, D_MODEL)),
                  hbm, hbm, hbm, _resident((1, D_MODEL)), _resident((1, D_MODEL))],
        out_specs=(lagged, pl.BlockSpec((ROW_TILE, D_MODEL), lambda r: (0, 0))),
        scratch_shapes=[pltpu.VMEM((ROW_TILE, D_MODEL), F32), pltpu.VMEM((ROW_TILE, D_MODEL), F32),
                        pltpu.VMEM((ROW_TILE, D_FF), BF16),
                        pltpu.VMEM((D_MODEL, 2 * D_MODEL), BF16),
                        pltpu.VMEM((len(POOL_WINDOWS) * POOL_GROUP, POOL_GROUP), BF16),
                        pltpu.VMEM((POOL_WIDTH, D_MODEL), BF16), pltpu.VMEM((Q_WIDTH, D_MODEL), BF16),
                        pltpu.VMEM((D_MODEL, D_MODEL), BF16),
                        pltpu.VMEM((D_MODEL, D_FF), BF16), pltpu.VMEM((D_MODEL, D_FF), BF16),
                        pltpu.VMEM((D_FF, D_MODEL), BF16)]
                       + _ring_scratch(RING_FF, RING_GROUP),
        compiler_params=pltpu.CompilerParams(dimension_semantics=("arbitrary",),
                                             vmem_limit_bytes=VMEM_LIMIT_BYTES),
        name="back",
    )(h1, pool_attn, pool_s, attn_s, w_in, wgrp, scale, wpo, wao, wout, g2, b2, w1, w3, w2, g3, b3)


def kernel(x_prompt, x_sample, cache_pool_u, cache_k_win, cache_v_win, w_in, pool_w_grp, pool_scale,
           attn_sinks, w_pool_out, w_attn_out, w_out, ffn1_w1, ffn1_w3, ffn1_w2, ffn2_w1, ffn2_w3,
           ffn2_w2, ln1_g, ln1_b, ln2_g, ln2_b, ln3_g, ln3_b):
    assert DEPTH == 1 and w_in.shape[0] == 1
    l = 0
    vec = lambda p: p[l].reshape(1, -1)
    sinks = attn_sinks[l]

    freq = jnp.tile(ROPE_THETA ** (-2.0 * jnp.arange(HEAD_DIM // 2, dtype=F32) / HEAD_DIM), 4).reshape(1, LANES)

    xp = x_prompt.reshape(PROMPT_ROWS, D_MODEL)
    xs = jnp.transpose(x_sample, (1, 0, 2)).reshape(SAMPLE_ROWS, D_MODEL)
    (h1, pool_attn, us, qs, ks, vs, kt_last, vt_last, u_last) = _front(
        xp, xs, ffn1_w1, ffn1_w3, ffn1_w2, w_in, vec(ln1_g), vec(ln1_b), freq, sinks)

    to_t = lambda c: jnp.transpose(c[l], (0, 2, 3, 1)).reshape(DEC_BATCH, KV_WIDTH, WINDOW)
    cu = jnp.transpose(cache_pool_u[l], (1, 0, 2))
    attn_s, pool_s, kt_s, vt_s, pu_s = _sample_ctx(us, qs, ks, vs, to_t(cache_k_win), to_t(cache_v_win), cu, sinks)

    wgrp = pool_w_grp.reshape(DEPTH, len(POOL_WINDOWS) * POOL_GROUP, POOL_GROUP)
    yp, ys = _back(h1, pool_attn, pool_s.reshape(SAMPLE_ROWS, POOL_WIDTH), attn_s.reshape(SAMPLE_ROWS, Q_WIDTH),
                   w_in, wgrp, vec(pool_scale), w_pool_out, w_attn_out, w_out, vec(ln2_g), vec(ln2_b),
                   ffn2_w1, ffn2_w3, ffn2_w2, vec(ln3_g), vec(ln3_b))
    yp = yp.reshape(BATCH, SEQ, D_MODEL)
    ys = jnp.transpose(ys.reshape(DEC_SEQ, DEC_BATCH, D_MODEL), (1, 0, 2))

    from_t = lambda c, n: jnp.transpose(c.reshape(n, N_KV_HEADS, HEAD_DIM, WINDOW), (0, 3, 1, 2))[None]
    pool_u_prompt = u_last[None, :, POOL_PAD - POOL_BUF:]
    pool_u_sample = jnp.transpose(pu_s, (1, 0, 2))[None]
    return (yp, ys, pool_u_prompt, from_t(kt_last, BATCH), from_t(vt_last, BATCH),
            pool_u_sample, from_t(kt_s, DEC_BATCH), from_t(vt_s, DEC_BATCH))
```

```python
import jax
import jax.numpy as jnp
import numpy as np
from jax import lax
from jax.experimental import pallas as pl
from jax.experimental.pallas import tpu as pltpu

D_MODEL = 1024
BATCH = 8
SEQ = 2048
DEC_BATCH = 128
DEC_SEQ = 4
PAST_LEN = 8192
POOL_WINDOWS = (2, 4, 8, 16)
POOL_GROUP = 128
POOL_WIDTH = 512
POOL_BUF = 15
N_HEADS = 8
N_KV_HEADS = 2
HEADS_PER_KV = N_HEADS // N_KV_HEADS
HEAD_DIM = 64
Q_WIDTH = 512
KV_WIDTH = 128
WINDOW = 128
ROPE_THETA = 10000.0
D_FF = 2816
DEPTH = 1
ALPHA = (2.0 * DEPTH) ** 0.25
LN_EPS = 1e-5
NEG_INF = -1e30
UQKV_WIDTH = POOL_WIDTH + Q_WIDTH + 2 * KV_WIDTH

LANES = 128
KEY_SPAN = 2 * WINDOW
VMEM_LIMIT_BYTES = 61 * 1024 * 1024

ROW_TILE = 512
TILES_PER_SEQ = SEQ // ROW_TILE
PROMPT_ROWS = BATCH * SEQ
PROMPT_TILES = PROMPT_ROWS // ROW_TILE
SAMPLE_ROWS = DEC_BATCH * DEC_SEQ
ROW_TILES = PROMPT_TILES + 1
POOL_PAD = 16
SEQ_GROUP = 32
SUB_GROUP = 8
FF_CHUNK = 256
OUT_BLOCK = 256

BF16 = jnp.bfloat16
F32 = jnp.float32


def _dot(a, b):
    return jnp.dot(a, b, preferred_element_type=F32)


def _dot_nt(a, b):
    return lax.dot_general(a, b, (((1,), (1,)), ((), ())), preferred_element_type=F32)


def _layer_norm(y, g, b):
    mu = jnp.mean(y, axis=-1, keepdims=True)
    yc = y - mu
    var = jnp.mean(yc * yc, axis=-1, keepdims=True)
    return yc * lax.rsqrt(var + LN_EPS) * g + b


def _resident(shape):
    nd = len(shape)
    return pl.BlockSpec(shape, lambda *_: (0,) * nd, pipeline_mode=pl.Buffered(1))


class _WeightStager:
    def __init__(self, jobs):
        self.staged, self.next_in_slot, first, users = [], {}, [], {}
        for src, ring, dst in jobs:
            mine = users.setdefault(id(ring), [])
            slot, sem = ring[len(mine) % len(ring)]
            if len(mine) < len(ring):
                first.append(len(self.staged))
            else:
                self.next_in_slot[mine[-len(ring)]] = len(self.staged)
            mine.append(len(self.staged))
            if slot.shape[1] > dst.shape[1]:
                slot = slot.at[:, pl.ds(0, dst.shape[1])]
            self.staged.append((pltpu.make_async_copy(src, slot, sem), slot, dst))
        self.done = 0
        for i in first:
            self.staged[i][0].start()

    def run(self, count=None):
        end = len(self.staged) if count is None else min(self.done + count, len(self.staged))
        for i in range(self.done, end):
            copy, slot, dst = self.staged[i]
            copy.wait()
            dst[...] = slot[...].astype(BF16)
            if i in self.next_in_slot:
                self.staged[self.next_in_slot[i]][0].start()
        self.done = end


def _interleave(*job_lists):
    keyed = [((i + 0.5) / len(jobs), n, job) for n, jobs in enumerate(job_lists) for i, job in enumerate(jobs)]
    return [job for _, _, job in sorted(keyed, key=lambda entry: entry[:2])]


def _ring(buf, sems, rows=None):
    if rows is None:
        return [(buf.at[i], sems.at[i]) for i in range(buf.shape[0])]
    return [(buf.at[pl.ds(i * rows, rows), :], sems.at[i]) for i in range(buf.shape[0] // rows)]


def _row_chunks(w_hbm, col0, dst, ring):
    rows, cols = dst.shape
    step, width = ring[0][0].shape
    assert rows % step == 0 and width >= cols
    return [(w_hbm.at[0, pl.ds(r0, step), pl.ds(col0, cols)], ring, dst.at[pl.ds(r0, step), :])
            for r0 in range(0, rows, step)]


STAGE_IN_FLIGHT = 4
RING_FF = (STAGE_IN_FLIGHT, 64, D_FF)
MODEL_RING_ROWS = ROW_TILE // STAGE_IN_FLIGHT
RING_NARROW = (STAGE_IN_FLIGHT, 128, UQKV_WIDTH - D_MODEL)
RING_GROUP = (1, len(POOL_WINDOWS) * POOL_GROUP, POOL_GROUP)


def _ring_scratch(*rings):
    return ([pltpu.VMEM(ring, F32) for ring in rings]
            + [pltpu.SemaphoreType.DMA((ring[0],)) for ring in rings]
            + [pltpu.SemaphoreType.DMA((STAGE_IN_FLIGHT,))])


FF_SLOTS = 2 * (D_FF // FF_CHUNK)


def _swiglu_residual(x, w1_ref, w3_ref, w2_ref, h_s, side_work, out_ref):
    assert all(0 <= slot <= FF_SLOTS for slot in side_work)
    run = lambda slot: side_work.get(slot, lambda: None)()
    xb = x.astype(BF16)
    for j in range(D_FF // FF_CHUNK):
        cols = slice(j * FF_CHUNK, (j + 1) * FF_CHUNK)
        a = _dot(xb, w1_ref[:, cols])
        run(2 * j)
        b = _dot(xb, w3_ref[:, cols])
        h_s[:, cols] = ((a * jax.nn.sigmoid(a)) * b).astype(BF16)
        run(2 * j + 1)
    run(FF_SLOTS)
    h = h_s[...]
    for c0 in range(0, D_MODEL, OUT_BLOCK):
        cols = slice(c0, c0 + OUT_BLOCK)
        out_ref[:, cols] = ALPHA * x[:, cols] + 0.5 * _dot(h, w2_ref[:, cols])


def _rope(x, cos, sin_signed, first_half):
    fwd = pltpu.roll(x, LANES - HEAD_DIM // 2, axis=1)
    bwd = pltpu.roll(x, HEAD_DIM // 2, axis=1)
    return x * cos + jnp.where(first_half, fwd, bwd) * sin_signed


def _swap_halves(x):
    return pltpu.roll(x, HEAD_DIM, axis=1)


def _swap_halves_wide(x):
    return jnp.concatenate([_swap_halves(x[:, c:c + LANES]) for c in range(0, x.shape[1], LANES)], axis=1)


def _lane_split(x, x_sw, kh):
    low = lax.broadcasted_iota(jnp.int32, (1, LANES), 1) < HEAD_DIM
    lo, hi = (x, x_sw) if kh == 0 else (x_sw, x)
    return jnp.concatenate([jnp.where(low, lo, 0.0), jnp.where(low, 0.0, hi)], axis=0).astype(BF16)


def _sink_softmax(q_pairs, keys, kh, bias, sinks_ref):
    s = _dot_nt(q_pairs, _lane_split(keys, _swap_halves(keys), kh)) + bias
    second_pair = lax.broadcasted_iota(jnp.int32, (s.shape[0], 1), 0) >= WINDOW
    probs, denoms = [], []
    for c in range(2):
        sc = s[:, c * KEY_SPAN:(c + 1) * KEY_SPAN]
        head = kh * HEADS_PER_KV + c
        sink = jnp.where(second_pair, sinks_ref[head + 2], sinks_ref[head])
        m = jnp.maximum(jnp.max(sc, axis=-1, keepdims=True), sink)
        p = jnp.exp(sc - m)
        denoms.append(jnp.sum(p, axis=-1, keepdims=True) + jnp.exp(sink - m))
        probs.append(p.astype(BF16))
    return jnp.concatenate(probs, axis=1), denoms


def _weighted_values(probs, denoms, vals, kh):
    low = lax.broadcasted_iota(jnp.int32, (1, LANES), 1) < HEAD_DIM
    o = _dot(probs, _lane_split(vals, _swap_halves(vals), kh))
    return o / jnp.where(low, denoms[0], denoms[1])


def _prompt_context_work(tile, q_c, k_c, v_c, u_c, bias_ref, sinks_ref,
                         pool_ref, attn_ref, kt_ref, vt_ref, ulast_ref):
    seq_tile = (tile + TILES_PER_SEQ) % TILES_PER_SEQ
    first_tile = seq_tile == 0
    softmaxed = {}

    def scores(unit, blk, kh):
        r0 = blk * WINDOW
        bias = bias_ref[jnp.where(first_tile, 1, 0)] if blk == 0 else bias_ref[0]
        c0 = 2 * kh * LANES
        q_pairs = jnp.concatenate([q_c[r0:r0 + WINDOW, c0:c0 + LANES],
                                   q_c[r0:r0 + WINDOW, c0 + LANES:c0 + 2 * LANES]], axis=0)
        softmaxed[unit] = _sink_softmax(q_pairs, k_c[r0:r0 + KEY_SPAN, :], kh, bias, sinks_ref)

    def values(unit, blk, kh):
        r0 = blk * WINDOW
        c0 = 2 * kh * LANES
        o = _weighted_values(*softmaxed.pop(unit), v_c[r0:r0 + KEY_SPAN, :], kh)
        attn_ref[r0:r0 + WINDOW, c0:c0 + LANES] = o[:WINDOW].astype(BF16)
        attn_ref[r0:r0 + WINDOW, c0 + LANES:c0 + 2 * LANES] = o[WINDOW:].astype(BF16)

    def pool(groups):
        pos = seq_tile * ROW_TILE + lax.broadcasted_iota(jnp.int32, (ROW_TILE, 1), 0)
        for g in groups:
            w = POOL_WINDOWS[g]
            cols = slice(g * POOL_GROUP, (g + 1) * POOL_GROUP)
            rows = u_c[:, cols]
            acc, span = rows, 1
            while span < w:
                acc = acc + pltpu.roll(acc, span, axis=0)
                span *= 2
            cur = rows[POOL_PAD:]
            cnt = jnp.minimum(pos + 1, w).astype(F32)
            pool_ref[:, cols] = (acc[POOL_PAD:] / cnt - cur).astype(BF16)

    def sequence_state():
        kt_ref[0] = k_c[ROW_TILE:ROW_TILE + WINDOW, :].T
        vt_ref[0] = v_c[ROW_TILE:ROW_TILE + WINDOW, :].T
        ulast_ref[0] = u_c[ROW_TILE:ROW_TILE + POOL_PAD, :]

    work = {"pool_wide": lambda: pool((3,)), "pool_narrow": lambda: pool((0, 1, 2)),
            "sequence_state": sequence_state}
    for blk in range(ROW_TILE // WINDOW):
        for kh in range(N_KV_HEADS):
            unit = blk * N_KV_HEADS + kh
            work["scores", unit] = lambda unit=unit, blk=blk, kh=kh: scores(unit, blk, kh)
            work["values", unit] = lambda unit=unit, blk=blk, kh=kh: values(unit, blk, kh)
    return work


def _prompt_bias():
    r = np.arange(2 * WINDOW)[:, None] % WINDOW
    c = np.arange(2 * KEY_SPAN)[None, :] % KEY_SPAN
    valid = (r <= c) & (c <= r + WINDOW)
    first = valid & (c >= WINDOW)
    return np.where(np.stack([valid, first]), 0.0, NEG_INF).astype(np.float32)


def _front_kernel(sinks_ref, xp_ref, xs_hbm, w1_hbm, w3_hbm, w2_hbm, win_hbm, g_ref, b_ref, freq_ref, bias_ref,
                  h_ref, pa_ref, us_ref, qs_ref, ks_ref, vs_ref, kt_ref, vt_ref, ulast_ref,
                  z_c, q_c, k_c, v_c, u_c, h_s, rope_ref, w1_ref, w3_ref, w2_ref, wu_ref,
                  buf_ff, buf_narrow, sem_ff, sem_narrow, sem_model, xs_s, sem_x):
    r = pl.program_id(0) - 1
    prev = r - 1
    pool_ref = pa_ref.at[:, 0:POOL_WIDTH]
    attn_ref = pa_ref.at[:, POOL_WIDTH:POOL_WIDTH + Q_WIDTH]

    def stage_and_reset():
        gather = [pltpu.make_async_copy(xs_hbm.at[:, t, :], xs_s.at[pl.ds(t * DEC_BATCH, DEC_BATCH), :], sem_x.at[t])
                  for t in range(DEC_SEQ)]
        for copy in gather:
            copy.start()
        ring_ff = _ring(buf_ff, sem_ff)
        ring_narrow = _ring(buf_narrow, sem_narrow)
        ring_model = _ring(z_c, sem_model, rows=MODEL_RING_ROWS)
        stager = _WeightStager(_interleave(
            _row_chunks(w1_hbm, 0, w1_ref, ring_ff) + _row_chunks(w3_hbm, 0, w3_ref, ring_ff),
            _row_chunks(win_hbm, 0, wu_ref.at[:, 0:D_MODEL], ring_model) + _row_chunks(w2_hbm, 0, w2_ref, ring_model),
            _row_chunks(win_hbm, D_MODEL, wu_ref.at[:, D_MODEL:UQKV_WIDTH], ring_narrow)))
        freq = freq_ref[...]
        lane = lax.broadcasted_iota(jnp.int32, (1, LANES), 1)
        sign = jnp.where((lane & (HEAD_DIM // 2)) == 0, -1.0, 1.0)
        row = lax.broadcasted_iota(jnp.int32, (ROW_TILE, 1), 0)
        per_tile = -(-len(stager.staged) // (TILES_PER_SEQ + 1))
        for tile in range(TILES_PER_SEQ + 1):
            pos = tile * ROW_TILE + row if tile < TILES_PER_SEQ else PAST_LEN + row // DEC_BATCH
            ang = pos.astype(F32) * freq
            rows = slice(tile * ROW_TILE, (tile + 1) * ROW_TILE)
            rope_ref[rows, 0:LANES] = jnp.cos(ang)
            rope_ref[rows, LANES:2 * LANES] = jnp.sin(ang) * sign
            stager.run(per_tile)
        stager.run()
        for copy in gather:
            copy.wait()
        for ref in (z_c, q_c, k_c, v_c, u_c):
            ref[...] = jnp.zeros(ref.shape, ref.dtype)

    def norm_previous():
        h1 = _layer_norm(z_c[...], g_ref[...], b_ref[...])
        h_ref[...] = h1
        return h1.astype(BF16)

    def rope_tables():
        tile = jnp.clip(prev, 0, PROMPT_TILES)
        table = jnp.where(tile == PROMPT_TILES, TILES_PER_SEQ, tile % TILES_PER_SEQ)
        rows = pl.ds(pl.multiple_of(table * ROW_TILE, ROW_TILE), ROW_TILE)
        cos = rope_ref[rows, 0:LANES]
        lane = lax.broadcasted_iota(jnp.int32, cos.shape, 1)
        return cos, rope_ref[rows, LANES:2 * LANES], (lane & (HEAD_DIM // 2)) == 0

    def project_u(h1b):
        return _dot(h1b, wu_ref[:, 0:POOL_WIDTH])

    def project_q(h1b):
        z = _dot(h1b, wu_ref[:, POOL_WIDTH:POOL_WIDTH + Q_WIDTH])
        tables = rope_tables()
        return [(_rope(z[:, c:c + LANES], *tables) * (HEAD_DIM ** -0.5)).astype(BF16)
                for c in range(0, Q_WIDTH, LANES)]

    def project_kv(h1b):
        z = _dot(h1b, wu_ref[:, POOL_WIDTH + Q_WIDTH:UQKV_WIDTH])
        return _rope(z[:, :KV_WIDTH], *rope_tables()), z[:, KV_WIDTH:]

    def tile_step(x):
        normed = []

        def norm():
            normed.append(norm_previous())

        projected = []

        def project():
            projected.extend([project_u(normed[0]), project_q(normed[0]), *project_kv(normed[0])])

        def carry():
            u, q, k, v = projected
            starts_seq = (prev + TILES_PER_SEQ) % TILES_PER_SEQ == 0
            u_c[0:POOL_PAD, :] = jnp.where(starts_seq, 0.0, u_c[ROW_TILE:ROW_TILE + POOL_PAD, :])
            u_c[POOL_PAD:POOL_PAD + ROW_TILE, :] = u
            for c, qc in enumerate(q):
                q_c[:, c * LANES:(c + 1) * LANES] = qc
            k_c[0:WINDOW, :] = k_c[ROW_TILE:ROW_TILE + WINDOW, :]
            v_c[0:WINDOW, :] = v_c[ROW_TILE:ROW_TILE + WINDOW, :]
            k_c[WINDOW:WINDOW + ROW_TILE, :] = k
            v_c[WINDOW:WINDOW + ROW_TILE, :] = v

        side = {0: norm, FF_SLOTS - 3: project, FF_SLOTS - 1: carry}
        stages = context_stages()
        assert len(stages) < FF_SLOTS - 3
        side.update(enumerate(stages, start=1))
        _swiglu_residual(x, w1_ref, w3_ref, w2_ref, h_s, side, z_c)

    def context_stages():
        context = _prompt_context_work(r - 2, q_c, k_c, v_c, u_c, bias_ref, sinks_ref,
                                       pool_ref, attn_ref, kt_ref, vt_ref, ulast_ref)
        stages = [("scores", 0)]
        for unit in range(1, 8):
            stages += [("scores", unit), ("values", unit - 1)]
        stages.append(("values", 7))

        def pooling_and_state():
            context["pool_wide"]()
            context["pool_narrow"]()
            context["sequence_state"]()

        return [context[stage] for stage in stages] + [pooling_and_state]

    def drain():
        stages = context_stages()
        third = len(stages) // 3
        h1b = norm_previous()
        for work in stages[:third]:
            work()
        us_ref[...] = project_u(h1b)
        for work in stages[third:2 * third]:
            work()
        for c, qc in enumerate(project_q(h1b)):
            qs_ref[:, c * LANES:(c + 1) * LANES] = qc
        for work in stages[2 * third:]:
            work()
        ks_ref[...], vs_ref[...] = project_kv(h1b)

    is_tile_step = jnp.logical_and(r >= 0, r <= PROMPT_TILES)

    @pl.when(is_tile_step)
    def _():
        tile_step(jnp.where(r == PROMPT_TILES, xs_s[...], xp_ref[...]))

    @pl.when(jnp.logical_not(is_tile_step))
    def _():
        pl.when(r == ROW_TILES)(drain)
        pl.when(r < 0)(stage_and_reset)


def _front(xp, xs, w1, w3, w2, w_in, g, b, freq, sinks):
    rows = ROW_TILES * ROW_TILE

    def lagged(width):
        return pl.BlockSpec((ROW_TILE, width), lambda s: (jnp.clip(s - 3, 0, PROMPT_TILES - 1), 0))

    def seq_of_lagged(shape):
        return pl.BlockSpec(shape, lambda s: (jnp.clip(s - 3, 0, PROMPT_TILES - 1) // TILES_PER_SEQ, 0, 0))

    sample = lambda width: pl.BlockSpec((ROW_TILE, width), lambda r: (0, 0))
    hbm = pl.BlockSpec(memory_space=pl.ANY)
    return pl.pallas_call(
        _front_kernel,
        out_shape=(jax.ShapeDtypeStruct((rows, D_MODEL), F32),
                   jax.ShapeDtypeStruct((PROMPT_ROWS, POOL_WIDTH + Q_WIDTH), BF16),
                   jax.ShapeDtypeStruct((SAMPLE_ROWS, POOL_WIDTH), F32),
                   jax.ShapeDtypeStruct((SAMPLE_ROWS, Q_WIDTH), BF16),
                   jax.ShapeDtypeStruct((SAMPLE_ROWS, KV_WIDTH), F32),
                   jax.ShapeDtypeStruct((SAMPLE_ROWS, KV_WIDTH), F32),
                   jax.ShapeDtypeStruct((BATCH, KV_WIDTH, WINDOW), F32),
                   jax.ShapeDtypeStruct((BATCH, KV_WIDTH, WINDOW), F32),
                   jax.ShapeDtypeStruct((BATCH, POOL_PAD, POOL_WIDTH), F32)),
        grid=(ROW_TILES + 2,),
        in_specs=[pl.BlockSpec(memory_space=pltpu.SMEM),
                  pl.BlockSpec((ROW_TILE, D_MODEL), lambda s: (jnp.clip(s - 1, 0, PROMPT_TILES - 1), 0)),
                  hbm, hbm, hbm, hbm, hbm, _resident((1, D_MODEL)), _resident((1, D_MODEL)),
                  _resident((1, LANES)), _resident((2, 2 * WINDOW, 2 * KEY_SPAN))],
        out_specs=(pl.BlockSpec((ROW_TILE, D_MODEL), lambda s: (jnp.clip(s - 2, 0, PROMPT_TILES), 0)),
                   lagged(POOL_WIDTH + Q_WIDTH),
                   sample(POOL_WIDTH), sample(Q_WIDTH), sample(KV_WIDTH), sample(KV_WIDTH),
                   seq_of_lagged((1, KV_WIDTH, WINDOW)), seq_of_lagged((1, KV_WIDTH, WINDOW)),
                   seq_of_lagged((1, POOL_PAD, POOL_WIDTH))),
        scratch_shapes=[pltpu.VMEM((ROW_TILE, D_MODEL), F32),
                        pltpu.VMEM((ROW_TILE, Q_WIDTH), BF16),
                        pltpu.VMEM((WINDOW + ROW_TILE, KV_WIDTH), F32),
                        pltpu.VMEM((WINDOW + ROW_TILE, KV_WIDTH), F32),
                        pltpu.VMEM((POOL_PAD + ROW_TILE, POOL_WIDTH), F32),
                        pltpu.VMEM((ROW_TILE, D_FF), BF16),
                        pltpu.VMEM(((TILES_PER_SEQ + 1) * ROW_TILE, 2 * LANES), F32),
                        pltpu.VMEM((D_MODEL, D_FF), BF16), pltpu.VMEM((D_MODEL, D_FF), BF16),
                        pltpu.VMEM((D_FF, D_MODEL), BF16), pltpu.VMEM((D_MODEL, UQKV_WIDTH), BF16)]
                       + _ring_scratch(RING_FF, RING_NARROW)
                       + [pltpu.VMEM((SAMPLE_ROWS, D_MODEL), F32), pltpu.SemaphoreType.DMA((DEC_SEQ,))],
        compiler_params=pltpu.CompilerParams(dimension_semantics=("arbitrary",),
                                             vmem_limit_bytes=VMEM_LIMIT_BYTES),
        name="front",
    )(sinks, xp, xs, w1, w3, w2, w_in, g, b, freq, jnp.asarray(_prompt_bias()))


def _sample_bias():
    row = np.arange(HEADS_PER_KV * DEC_SEQ * SUB_GROUP)
    row_t = (row // SUB_GROUP) % DEC_SEQ
    row_b = row % SUB_GROUP
    col = np.arange(SUB_GROUP * WINDOW)
    ok_c = (col[None, :] // WINDOW == row_b[:, None]) & (col[None, :] % WINDOW >= row_t[:, None])
    new = np.arange(DEC_SEQ * SUB_GROUP)
    ok_n = (new[None, :] % SUB_GROUP == row_b[:, None]) & (new[None, :] // SUB_GROUP <= row_t[:, None])
    to_bias = lambda ok: np.where(ok, 0.0, NEG_INF).astype(np.float32)
    return to_bias(ok_c), to_bias(ok_n)


def _sample_ctx_kernel(sinks_ref, q0, q1, q2, q3, k0, k1, k2, k3, v0, v1, v2, v3, u0, u1, u2, u3,
                       ckt_ref, cvt_ref, cu_ref, bias_c_ref, bias_n_ref,
                       attn_ref, pool_ref, kt_out, vt_out, pu_out, kbt_s, vbt_s):
    q_t = [q[...].astype(F32) for q in (q0, q1, q2, q3)]
    k_t = [k[...] for k in (k0, k1, k2, k3)]
    v_t = [v[...] for v in (v0, v1, v2, v3)]

    rows_u = [cu_ref[i] for i in range(POOL_BUF)] + [u[...] for u in (u0, u1, u2, u3)]
    for t in range(DEC_SEQ):
        pooled = []
        for g, w in enumerate(POOL_WINDOWS):
            cols = slice(g * POOL_GROUP, (g + 1) * POOL_GROUP)
            cur = rows_u[POOL_BUF + t][:, cols]
            acc = cur
            for j in range(1, w):
                acc = acc + rows_u[POOL_BUF + t - j][:, cols]
            pooled.append(acc / float(w) - cur)
        pool_ref[t] = jnp.concatenate(pooled, axis=1)
    for i in range(POOL_BUF):
        pu_out[i] = rows_u[i + DEC_SEQ]

    bias_c = bias_c_ref[...]
    bias_n = bias_n_ref[...]
    head_of_row = lax.broadcasted_iota(jnp.int32, (bias_c.shape[0], 1), 0) // (DEC_SEQ * SUB_GROUP)
    low = lax.broadcasted_iota(jnp.int32, (1, LANES), 1) < HEAD_DIM
    q_sw = [_swap_halves_wide(q) for q in q_t]
    k_sw = [_swap_halves(k) for k in k_t]
    v_sw = [_swap_halves(v) for v in v_t]
    def scores(sub, kh):
        rows = slice(sub * SUB_GROUP, (sub + 1) * SUB_GROUP)

        def q_piece(t, head):
            src = q_t[t] if head % 2 == 0 else q_sw[t]
            chunk = head // 2
            return jnp.where(low, src[rows, chunk * LANES:(chunk + 1) * LANES], 0.0)

        def kv_first(c):
            return c if kh == 0 else jnp.concatenate([c[HEAD_DIM:], c[:HEAD_DIM]], axis=0)

        lhs = jnp.concatenate([q_piece(t, kh * HEADS_PER_KV + g)
                               for g in range(HEADS_PER_KV) for t in range(DEC_SEQ)], axis=0).astype(BF16)
        kcat = jnp.concatenate([kv_first(ckt_ref[sub * SUB_GROUP + b]) for b in range(SUB_GROUP)],
                               axis=1).astype(BF16)
        knew = jnp.concatenate([(k_t[t] if kh == 0 else k_sw[t])[rows] for t in range(DEC_SEQ)],
                               axis=0).astype(BF16)
        s_c = _dot(lhs, kcat) + bias_c
        s_n = _dot_nt(lhs, knew) + bias_n
        sink = jnp.zeros(head_of_row.shape, F32)
        for g in range(HEADS_PER_KV):
            sink = jnp.where(head_of_row == g, sinks_ref[kh * HEADS_PER_KV + g], sink)
        m = jnp.maximum(jnp.maximum(jnp.max(s_c, axis=-1, keepdims=True),
                                    jnp.max(s_n, axis=-1, keepdims=True)), sink)
        p_c = jnp.exp(s_c - m)
        p_n = jnp.exp(s_n - m)
        denom = (jnp.sum(p_c, axis=-1, keepdims=True) + jnp.sum(p_n, axis=-1, keepdims=True)
                 + jnp.exp(sink - m))
        return p_c.astype(BF16), p_n.astype(BF16), denom

    def values(sub, kh, p_c, p_n, denom):
        rows = slice(sub * SUB_GROUP, (sub + 1) * SUB_GROUP)

        def kv_twice(c):
            part = c[kh * HEAD_DIM:(kh + 1) * HEAD_DIM]
            return jnp.concatenate([part, part], axis=0)

        vcat = jnp.concatenate([kv_twice(cvt_ref[sub * SUB_GROUP + b]) for b in range(SUB_GROUP)],
                               axis=1).astype(BF16)
        vnew = jnp.concatenate([(jnp.where(low, v_t[t], v_sw[t]) if kh == 0 else
                                 jnp.where(low, v_sw[t], v_t[t]))[rows] for t in range(DEC_SEQ)],
                               axis=0).astype(BF16)
        o = (_dot_nt(p_c, vcat) + _dot(p_n, vnew)) / denom
        for t in range(DEC_SEQ):
            for pair in range(HEADS_PER_KV // 2):
                piece = lambda g: o[(g * DEC_SEQ + t) * SUB_GROUP:(g * DEC_SEQ + t + 1) * SUB_GROUP]
                c0 = (kh * HEADS_PER_KV // 2 + pair) * LANES
                attn_ref[t, rows, c0:c0 + LANES] = jnp.where(low, piece(2 * pair), piece(2 * pair + 1))

    units = [(sub, kh) for sub in range(SEQ_GROUP // SUB_GROUP) for kh in range(N_KV_HEADS)]
    pending = None
    for unit in units:
        softmaxed = scores(*unit)
        if pending is not None:
            values(*pending)
        pending = (*unit, *softmaxed)
    values(*pending)

    if DEC_SEQ * SEQ_GROUP < LANES:
        zeros = jnp.zeros((LANES - DEC_SEQ * SEQ_GROUP, KV_WIDTH), F32)
        kbt_s[DEC_SEQ * SEQ_GROUP:, :] = zeros
        vbt_s[DEC_SEQ * SEQ_GROUP:, :] = zeros
    for t in range(DEC_SEQ):
        kbt_s[pl.ds(t, SEQ_GROUP, stride=DEC_SEQ), :] = k_t[t]
        vbt_s[pl.ds(t, SEQ_GROUP, stride=DEC_SEQ), :] = v_t[t]
    knew_t = kbt_s[...].T
    vnew_t = vbt_s[...].T
    keep = lax.broadcasted_iota(jnp.int32, (1, WINDOW), 1) < WINDOW - DEC_SEQ
    for b in range(SEQ_GROUP):
        shift_new = WINDOW - DEC_SEQ - DEC_SEQ * b
        kt_out[b] = jnp.where(keep, pltpu.roll(ckt_ref[b], WINDOW - DEC_SEQ, axis=1),
                              pltpu.roll(knew_t, shift_new, axis=1))
        vt_out[b] = jnp.where(keep, pltpu.roll(cvt_ref[b], WINDOW - DEC_SEQ, axis=1),
                              pltpu.roll(vnew_t, shift_new, axis=1))


def _sample_ctx(us, qs, ks, vs, ckt, cvt, cu, sinks):
    groups = DEC_BATCH // SEQ_GROUP

    def token_rows(t, width):
        return pl.BlockSpec((SEQ_GROUP, width), lambda i: (t * groups + i, 0))

    def per_token(width):
        return [token_rows(t, width) for t in range(DEC_SEQ)]

    cache_spec = pl.BlockSpec((SEQ_GROUP, KV_WIDTH, WINDOW), lambda i: (i, 0, 0))
    pool_rows_spec = pl.BlockSpec((POOL_BUF, SEQ_GROUP, POOL_WIDTH), lambda i: (0, i, 0))
    by_token = lambda width: pl.BlockSpec((DEC_SEQ, SEQ_GROUP, width), lambda i: (0, i, 0))
    bias_c, bias_n = _sample_bias()
    return pl.pallas_call(
        _sample_ctx_kernel,
        out_shape=(jax.ShapeDtypeStruct((DEC_SEQ, DEC_BATCH, Q_WIDTH), F32),
                   jax.ShapeDtypeStruct((DEC_SEQ, DEC_BATCH, POOL_WIDTH), F32),
                   jax.ShapeDtypeStruct((DEC_BATCH, KV_WIDTH, WINDOW), F32),
                   jax.ShapeDtypeStruct((DEC_BATCH, KV_WIDTH, WINDOW), F32),
                   jax.ShapeDtypeStruct((POOL_BUF, DEC_BATCH, POOL_WIDTH), F32)),
        grid=(groups,),
        in_specs=[pl.BlockSpec(memory_space=pltpu.SMEM)]
                 + per_token(Q_WIDTH) + per_token(KV_WIDTH) + per_token(KV_WIDTH) + per_token(POOL_WIDTH)
                 + [cache_spec, cache_spec, pool_rows_spec, _resident(bias_c.shape), _resident(bias_n.shape)],
        out_specs=(by_token(Q_WIDTH), by_token(POOL_WIDTH), cache_spec, cache_spec, pool_rows_spec),
        scratch_shapes=[pltpu.VMEM((LANES, KV_WIDTH), F32), pltpu.VMEM((LANES, KV_WIDTH), F32)],
        compiler_params=pltpu.CompilerParams(dimension_semantics=("parallel",),
                                             vmem_limit_bytes=VMEM_LIMIT_BYTES),
        name="sample_ctx",
    )(sinks, *([qs] * DEC_SEQ), *([ks] * DEC_SEQ), *([vs] * DEC_SEQ), *([us] * DEC_SEQ),
      ckt, cvt, cu, jnp.asarray(bias_c), jnp.asarray(bias_n))


def _back_kernel(h_ref, pa_ref, pools_ref, attns_ref,
                 win_hbm, wgrp_hbm, scale_ref, wpo_hbm, wao_hbm, wout_hbm, g2_ref, b2_ref,
                 w1_hbm, w3_hbm, w2_hbm, g3_ref, b3_ref, yp_ref, ys_hbm, z2_c, z3_c, h_s,
                 wg_ref, wgrp_ref, wpo_ref, wao_ref, wout_ref, w1_ref, w3_ref, w2_ref,
                 buf_ff, buf_group, sem_ff, sem_group, sem_model, ys_s, sem_y):
    t = pl.program_id(0)

    def branch_outputs():
        is_sample = t >= PROMPT_TILES
        pool_in = jnp.where(is_sample, pools_ref[...].astype(BF16), pa_ref[:, 0:POOL_WIDTH])
        attn_o = jnp.where(is_sample, attns_ref[...].astype(BF16), pa_ref[:, POOL_WIDTH:POOL_WIDTH + Q_WIDTH])
        zs = [_dot(pool_in[:, g * POOL_GROUP:(g + 1) * POOL_GROUP],
                   wgrp_ref[g * POOL_GROUP:(g + 1) * POOL_GROUP, :])
              for g in range(len(POOL_WINDOWS))]
        pool_z = jnp.concatenate(zs, axis=1) * scale_ref[...]
        return _dot(pool_z.astype(BF16), wpo_ref[...]), _dot(attn_o, wao_ref[...])

    def gate_and_merge(h1b, a, b):
        gates = jax.nn.sigmoid(_dot(h1b, wg_ref[...]))
        return (gates[:, :D_MODEL] * a + gates[:, D_MODEL:] * b).astype(BF16)

    def project_out(h1, merged):
        z2_c[...] = ALPHA * h1 + _dot(merged, wout_ref[...])

    is_tile_step = jnp.logical_and(t >= 1, t <= ROW_TILES)

    @pl.when(is_tile_step)
    def _():
        h1 = h_ref[...]
        a, b = branch_outputs()
        y = _layer_norm(z3_c[...], g3_ref[...], b3_ref[...])
        yp_ref[...] = y
        h2 = _layer_norm(z2_c[...], g2_ref[...], b2_ref[...])
        h1b = jnp.where(t <= ROW_TILES, h1.astype(BF16), y.astype(BF16))
        merged = []
        side = {6: lambda: merged.append(gate_and_merge(h1b, a, b)), 16: lambda: project_out(h1, merged[0])}
        _swiglu_residual(h2, w1_ref, w3_ref, w2_ref, h_s, side, z3_c)

    def last_norm():
        ys_s[...] = _layer_norm(z3_c[...], g3_ref[...], b3_ref[...])
        scatter = [pltpu.make_async_copy(ys_s.at[pl.ds(t * DEC_BATCH, DEC_BATCH), :], ys_hbm.at[:, t, :], sem_y.at[t])
                   for t in range(DEC_SEQ)]
        for copy in scatter:
            copy.start()
        for copy in scatter:
            copy.wait()

    def stage_and_first_tile():
        ring_ff = _ring(buf_ff, sem_ff)
        ring_group = _ring(buf_group, sem_group)
        ring_model = _ring(z3_c, sem_model, rows=MODEL_RING_ROWS)
        mixer = _interleave(_row_chunks(wgrp_hbm, 0, wgrp_ref, ring_group)
                            + _row_chunks(wpo_hbm, 0, wpo_ref, ring_model)
                            + _row_chunks(wao_hbm, 0, wao_ref, ring_model)
                            + _row_chunks(wout_hbm, 0, wout_ref, ring_model),
                            _row_chunks(win_hbm, UQKV_WIDTH, wg_ref, ring_ff))
        swiglu = _interleave(_row_chunks(w1_hbm, 0, w1_ref, ring_ff) + _row_chunks(w3_hbm, 0, w3_ref, ring_ff),
                             _row_chunks(w2_hbm, 0, w2_ref, ring_model))
        stager = _WeightStager(mixer + swiglu)
        stager.run(len(mixer))
        h1 = h_ref[...]
        a, b = branch_outputs()
        stager.run(len(swiglu) // 3)
        merged = gate_and_merge(h1.astype(BF16), a, b)
        stager.run(len(swiglu) // 3)
        project_out(h1, merged)
        stager.run()
        z3_c[...] = jnp.zeros(z3_c.shape, z3_c.dtype)

    @pl.when(jnp.logical_not(is_tile_step))
    def _():
        pl.when(t == ROW_TILES + 1)(last_norm)
        pl.when(t == 0)(stage_and_first_tile)


def _back(h1, pool_attn, pool_s, attn_s, w_in, wgrp, scale, wpo, wao, wout, g2, b2, w1, w3, w2, g3, b3):
    lagged = pl.BlockSpec((ROW_TILE, D_MODEL), lambda t: (jnp.clip(t - 2, 0, PROMPT_TILES - 1), 0))
    hbm = pl.BlockSpec(memory_space=pl.ANY)
    return pl.pallas_call(
        _back_kernel,
        out_shape=(jax.ShapeDtypeStruct((PROMPT_ROWS, D_MODEL), F32),
                   jax.ShapeDtypeStruct((DEC_BATCH, DEC_SEQ, D_MODEL), F32)),
        grid=(ROW_TILES + 2,),
        in_specs=[pl.BlockSpec((ROW_TILE, D_MODEL), lambda t: (jnp.minimum(t, PROMPT_TILES), 0)),
                  pl.BlockSpec((ROW_TILE, POOL_WIDTH + Q_WIDTH), lambda t: (jnp.minimum(t, PROMPT_TILES - 1), 0)),
                  _resident((SAMPLE_ROWS, POOL_WIDTH)), _resident((SAMPLE_ROWS, Q_WIDTH)),
                  hbm, hbm, _resident((1, POOL_WIDTH)), hbm, hbm, hbm,
                  _resident((1, D_MODEL)), _resident((1, D_MODEL)),
                  hbm, hbm, hbm, _resident((1, D_MODEL)), _resident((1, D_MODEL))],
        out_specs=(lagged, hbm),
        scratch_shapes=[pltpu.VMEM((ROW_TILE, D_MODEL), F32), pltpu.VMEM((ROW_TILE, D_MODEL), F32),
                        pltpu.VMEM((ROW_TILE, D_FF), BF16),
                        pltpu.VMEM((D_MODEL, 2 * D_MODEL), BF16),
                        pltpu.VMEM((len(POOL_WINDOWS) * POOL_GROUP, POOL_GROUP), BF16),
                        pltpu.VMEM((POOL_WIDTH, D_MODEL), BF16), pltpu.VMEM((Q_WIDTH, D_MODEL), BF16),
                        pltpu.VMEM((D_MODEL, D_MODEL), BF16),
                        pltpu.VMEM((D_MODEL, D_FF), BF16), pltpu.VMEM((D_MODEL, D_FF), BF16),
                        pltpu.VMEM((D_FF, D_MODEL), BF16)]
                       + _ring_scratch(RING_FF, RING_GROUP)
                       + [pltpu.VMEM((SAMPLE_ROWS, D_MODEL), F32), pltpu.SemaphoreType.DMA((DEC_SEQ,))],
        compiler_params=pltpu.CompilerParams(dimension_semantics=("arbitrary",),
                                             vmem_limit_bytes=VMEM_LIMIT_BYTES),
        name="back",
    )(h1, pool_attn, pool_s, attn_s, w_in, wgrp, scale, wpo, wao, wout, g2, b2, w1, w3, w2, g3, b3)


def kernel(x_prompt, x_sample, cache_pool_u, cache_k_win, cache_v_win, w_in, pool_w_grp, pool_scale,
           attn_sinks, w_pool_out, w_attn_out, w_out, ffn1_w1, ffn1_w3, ffn1_w2, ffn2_w1, ffn2_w3,
           ffn2_w2, ln1_g, ln1_b, ln2_g, ln2_b, ln3_g, ln3_b):
    assert DEPTH == 1 and w_in.shape[0] == 1
    l = 0
    vec = lambda p: p[l].reshape(1, -1)
    sinks = attn_sinks[l]

    freq = jnp.tile(ROPE_THETA ** (-2.0 * jnp.arange(HEAD_DIM // 2, dtype=F32) / HEAD_DIM), 4).reshape(1, LANES)

    xp = x_prompt.reshape(PROMPT_ROWS, D_MODEL)
    (h1, pool_attn, us, qs, ks, vs, kt_last, vt_last, u_last) = _front(
        xp, x_sample, ffn1_w1, ffn1_w3, ffn1_w2, w_in, vec(ln1_g), vec(ln1_b), freq, sinks)

    to_t = lambda c: jnp.transpose(c[l], (0, 2, 3, 1)).reshape(DEC_BATCH, KV_WIDTH, WINDOW)
    cu = jnp.transpose(cache_pool_u[l], (1, 0, 2))
    attn_s, pool_s, kt_s, vt_s, pu_s = _sample_ctx(us, qs, ks, vs, to_t(cache_k_win), to_t(cache_v_win), cu, sinks)

    wgrp = pool_w_grp.reshape(DEPTH, len(POOL_WINDOWS) * POOL_GROUP, POOL_GROUP)
    yp, ys = _back(h1, pool_attn, pool_s.reshape(SAMPLE_ROWS, POOL_WIDTH), attn_s.reshape(SAMPLE_ROWS, Q_WIDTH),
                   w_in, wgrp, vec(pool_scale), w_pool_out, w_attn_out, w_out, vec(ln2_g), vec(ln2_b),
                   ffn2_w1, ffn2_w3, ffn2_w2, vec(ln3_g), vec(ln3_b))
    yp = yp.reshape(BATCH, SEQ, D_MODEL)

    from_t = lambda c, n: jnp.transpose(c.reshape(n, N_KV_HEADS, HEAD_DIM, WINDOW), (0, 3, 1, 2))[None]
    pool_u_prompt = u_last[None, :, POOL_PAD - POOL_BUF:]
    pool_u_sample = jnp.transpose(pu_s, (1, 0, 2))[None]
    return (yp, ys, pool_u_prompt, from_t(kt_last, BATCH), from_t(vt_last, BATCH),
            pool_u_sample, from_t(kt_s, DEC_BATCH), from_t(vt_s, DEC_BATCH))
```

```python
import jax
import jax.numpy as jnp
import numpy as np
from jax import lax
from jax.experimental import pallas as pl
from jax.experimental.pallas import tpu as pltpu

D_MODEL = 1024
BATCH = 8
SEQ = 2048
DEC_BATCH = 128
DEC_SEQ = 4
PAST_LEN = 8192
POOL_WINDOWS = (2, 4, 8, 16)
POOL_GROUP = 128
POOL_WIDTH = 512
POOL_BUF = 15
N_HEADS = 8
N_KV_HEADS = 2
HEADS_PER_KV = N_HEADS // N_KV_HEADS
HEAD_DIM = 64
Q_WIDTH = 512
KV_WIDTH = 128
WINDOW = 128
ROPE_THETA = 10000.0
D_FF = 2816
DEPTH = 1
ALPHA = (2.0 * DEPTH) ** 0.25
LN_EPS = 1e-5
NEG_INF = -1e30
UQKV_WIDTH = POOL_WIDTH + Q_WIDTH + 2 * KV_WIDTH

LANES = 128
KEY_SPAN = 2 * WINDOW
VMEM_LIMIT_BYTES = 61 * 1024 * 1024

ROW_TILE = 512
TILES_PER_SEQ = SEQ // ROW_TILE
PROMPT_ROWS = BATCH * SEQ
PROMPT_TILES = PROMPT_ROWS // ROW_TILE
SAMPLE_ROWS = DEC_BATCH * DEC_SEQ
ROW_TILES = PROMPT_TILES + 1
POOL_PAD = 16
SEQ_GROUP = 32
SUB_GROUP = 8
FF_CHUNK = 256
OUT_BLOCK = 256

BF16 = jnp.bfloat16
F32 = jnp.float32


def _dot(a, b):
    return jnp.dot(a, b, preferred_element_type=F32)


def _dot_nt(a, b):
    return lax.dot_general(a, b, (((1,), (1,)), ((), ())), preferred_element_type=F32)


def _layer_norm(y, g, b):
    mu = jnp.mean(y, axis=-1, keepdims=True)
    yc = y - mu
    var = jnp.mean(yc * yc, axis=-1, keepdims=True)
    return yc * lax.rsqrt(var + LN_EPS) * g + b


def _resident(shape):
    nd = len(shape)
    return pl.BlockSpec(shape, lambda *_: (0,) * nd, pipeline_mode=pl.Buffered(1))


class _WeightStager:
    def __init__(self, jobs):
        self.staged, self.next_in_slot, first, users = [], {}, [], {}
        for src, ring, dst in jobs:
            mine = users.setdefault(id(ring), [])
            slot, sem = ring[len(mine) % len(ring)]
            if len(mine) < len(ring):
                first.append(len(self.staged))
            else:
                self.next_in_slot[mine[-len(ring)]] = len(self.staged)
            mine.append(len(self.staged))
            if slot.shape[1] > dst.shape[1]:
                slot = slot.at[:, pl.ds(0, dst.shape[1])]
            self.staged.append((pltpu.make_async_copy(src, slot, sem), slot, dst))
        self.done = 0
        for i in first:
            self.staged[i][0].start()

    def run(self, count=None):
        end = len(self.staged) if count is None else min(self.done + count, len(self.staged))
        for i in range(self.done, end):
            copy, slot, dst = self.staged[i]
            copy.wait()
            dst[...] = slot[...].astype(BF16)
            if i in self.next_in_slot:
                self.staged[self.next_in_slot[i]][0].start()
        self.done = end


def _interleave(*job_lists):
    keyed = [((i + 0.5) / len(jobs), n, job) for n, jobs in enumerate(job_lists) for i, job in enumerate(jobs)]
    return [job for _, _, job in sorted(keyed, key=lambda entry: entry[:2])]


def _ring(buf, sems, rows=None):
    if rows is None:
        return [(buf.at[i], sems.at[i]) for i in range(buf.shape[0])]
    return [(buf.at[pl.ds(i * rows, rows), :], sems.at[i]) for i in range(buf.shape[0] // rows)]


def _row_chunks(w_hbm, col0, dst, ring):
    rows, cols = dst.shape
    step, width = ring[0][0].shape
    assert rows % step == 0 and width >= cols
    return [(w_hbm.at[0, pl.ds(r0, step), pl.ds(col0, cols)], ring, dst.at[pl.ds(r0, step), :])
            for r0 in range(0, rows, step)]


STAGE_IN_FLIGHT = 4
RING_FF = (STAGE_IN_FLIGHT, 64, D_FF)
RING_FF_DEEP = (2 * STAGE_IN_FLIGHT, 64, D_FF)
MODEL_RING_ROWS = ROW_TILE // STAGE_IN_FLIGHT
RING_NARROW = (STAGE_IN_FLIGHT, 128, UQKV_WIDTH - D_MODEL)
RING_GROUP = (1, len(POOL_WINDOWS) * POOL_GROUP, POOL_GROUP)


def _ring_scratch(*rings):
    return ([pltpu.VMEM(ring, F32) for ring in rings]
            + [pltpu.SemaphoreType.DMA((ring[0],)) for ring in rings]
            + [pltpu.SemaphoreType.DMA((STAGE_IN_FLIGHT,))])


FF_SLOTS = 2 * (D_FF // FF_CHUNK)


def _swiglu_residual(x, w1_ref, w3_ref, w2_ref, h_s, side_work, out_ref):
    assert all(0 <= slot <= FF_SLOTS for slot in side_work)
    run = lambda slot: side_work.get(slot, lambda: None)()
    xb = x.astype(BF16)
    for j in range(D_FF // FF_CHUNK):
        cols = slice(j * FF_CHUNK, (j + 1) * FF_CHUNK)
        a = _dot(xb, w1_ref[:, cols])
        run(2 * j)
        b = _dot(xb, w3_ref[:, cols])
        h_s[:, cols] = ((a * jax.nn.sigmoid(a)) * b).astype(BF16)
        run(2 * j + 1)
    run(FF_SLOTS)
    h = h_s[...]
    for c0 in range(0, D_MODEL, OUT_BLOCK):
        cols = slice(c0, c0 + OUT_BLOCK)
        out_ref[:, cols] = ALPHA * x[:, cols] + 0.5 * _dot(h, w2_ref[:, cols])


def _rope(x, cos, sin_signed, first_half):
    fwd = pltpu.roll(x, LANES - HEAD_DIM // 2, axis=1)
    bwd = pltpu.roll(x, HEAD_DIM // 2, axis=1)
    return x * cos + jnp.where(first_half, fwd, bwd) * sin_signed


def _swap_halves(x):
    return pltpu.roll(x, HEAD_DIM, axis=1)


def _swap_halves_wide(x):
    return jnp.concatenate([_swap_halves(x[:, c:c + LANES]) for c in range(0, x.shape[1], LANES)], axis=1)


def _lane_split(x, x_sw, kh):
    low = lax.broadcasted_iota(jnp.int32, (1, LANES), 1) < HEAD_DIM
    lo, hi = (x, x_sw) if kh == 0 else (x_sw, x)
    return jnp.concatenate([jnp.where(low, lo, 0.0), jnp.where(low, 0.0, hi)], axis=0).astype(BF16)


def _sink_softmax(q_pairs, keys, kh, bias, sinks_ref):
    s = _dot_nt(q_pairs, _lane_split(keys, _swap_halves(keys), kh)) + bias
    second_pair = lax.broadcasted_iota(jnp.int32, (s.shape[0], 1), 0) >= WINDOW
    probs, denoms = [], []
    for c in range(2):
        sc = s[:, c * KEY_SPAN:(c + 1) * KEY_SPAN]
        head = kh * HEADS_PER_KV + c
        sink = jnp.where(second_pair, sinks_ref[head + 2], sinks_ref[head])
        m = jnp.maximum(jnp.max(sc, axis=-1, keepdims=True), sink)
        p = jnp.exp(sc - m)
        denoms.append(jnp.sum(p, axis=-1, keepdims=True) + jnp.exp(sink - m))
        probs.append(p.astype(BF16))
    return jnp.concatenate(probs, axis=1), denoms


def _weighted_values(probs, denoms, vals, kh):
    low = lax.broadcasted_iota(jnp.int32, (1, LANES), 1) < HEAD_DIM
    o = _dot(probs, _lane_split(vals, _swap_halves(vals), kh))
    return o / jnp.where(low, denoms[0], denoms[1])


def _prompt_context_work(tile, q_c, k_c, v_c, u_c, bias_ref, sinks_ref,
                         pool_ref, attn_ref, kt_ref, vt_ref, ulast_ref):
    seq_tile = (tile + TILES_PER_SEQ) % TILES_PER_SEQ
    first_tile = seq_tile == 0
    softmaxed = {}

    def scores(unit, blk, kh):
        r0 = blk * WINDOW
        bias = bias_ref[jnp.where(first_tile, 1, 0)] if blk == 0 else bias_ref[0]
        c0 = 2 * kh * LANES
        q_pairs = jnp.concatenate([q_c[r0:r0 + WINDOW, c0:c0 + LANES],
                                   q_c[r0:r0 + WINDOW, c0 + LANES:c0 + 2 * LANES]], axis=0)
        softmaxed[unit] = _sink_softmax(q_pairs, k_c[r0:r0 + KEY_SPAN, :], kh, bias, sinks_ref)

    def values(unit, blk, kh):
        r0 = blk * WINDOW
        c0 = 2 * kh * LANES
        o = _weighted_values(*softmaxed.pop(unit), v_c[r0:r0 + KEY_SPAN, :], kh)
        attn_ref[r0:r0 + WINDOW, c0:c0 + LANES] = o[:WINDOW].astype(BF16)
        attn_ref[r0:r0 + WINDOW, c0 + LANES:c0 + 2 * LANES] = o[WINDOW:].astype(BF16)

    def pool(groups):
        pos = seq_tile * ROW_TILE + lax.broadcasted_iota(jnp.int32, (ROW_TILE, 1), 0)
        for g in groups:
            w = POOL_WINDOWS[g]
            cols = slice(g * POOL_GROUP, (g + 1) * POOL_GROUP)
            rows = u_c[:, cols]
            acc, span = rows, 1
            while span < w:
                acc = acc + pltpu.roll(acc, span, axis=0)
                span *= 2
            cur = rows[POOL_PAD:]
            cnt = jnp.minimum(pos + 1, w).astype(F32)
            pool_ref[:, cols] = (acc[POOL_PAD:] / cnt - cur).astype(BF16)

    def sequence_state():
        kt_ref[0] = k_c[ROW_TILE:ROW_TILE + WINDOW, :].T
        vt_ref[0] = v_c[ROW_TILE:ROW_TILE + WINDOW, :].T
        ulast_ref[0] = u_c[ROW_TILE:ROW_TILE + POOL_PAD, :]

    work = {"pool_wide": lambda: pool((3,)), "pool_narrow": lambda: pool((0, 1, 2)),
            "sequence_state": sequence_state}
    for blk in range(ROW_TILE // WINDOW):
        for kh in range(N_KV_HEADS):
            unit = blk * N_KV_HEADS + kh
            work["scores", unit] = lambda unit=unit, blk=blk, kh=kh: scores(unit, blk, kh)
            work["values", unit] = lambda unit=unit, blk=blk, kh=kh: values(unit, blk, kh)
    return work


def _prompt_bias():
    r = np.arange(2 * WINDOW)[:, None] % WINDOW
    c = np.arange(2 * KEY_SPAN)[None, :] % KEY_SPAN
    valid = (r <= c) & (c <= r + WINDOW)
    first = valid & (c >= WINDOW)
    return np.where(np.stack([valid, first]), 0.0, NEG_INF).astype(np.float32)


def _front_kernel(sinks_ref, xp_ref, xs_hbm, w1_hbm, w3_hbm, w2_hbm, win_hbm, g_ref, b_ref, freq_ref, bias_ref,
                  h_ref, pa_ref, us_ref, qs_ref, ks_ref, vs_ref, kt_ref, vt_ref, ulast_ref,
                  z_c, q_c, k_c, v_c, u_c, h_s, rope_ref, w1_ref, w3_ref, w2_ref, wu_ref,
                  buf_ff, buf_narrow, sem_ff, sem_narrow, sem_model, xs_s, sem_x):
    r = pl.program_id(0) - 1
    prev = r - 1
    pool_ref = pa_ref.at[:, 0:POOL_WIDTH]
    attn_ref = pa_ref.at[:, POOL_WIDTH:POOL_WIDTH + Q_WIDTH]

    def stage_and_reset():
        gather = [pltpu.make_async_copy(xs_hbm.at[:, t, :], xs_s.at[pl.ds(t * DEC_BATCH, DEC_BATCH), :], sem_x.at[t])
                  for t in range(DEC_SEQ)]
        for copy in gather:
            copy.start()
        ring_ff = _ring(buf_ff, sem_ff)
        ring_narrow = _ring(buf_narrow, sem_narrow)
        ring_model = _ring(z_c, sem_model, rows=MODEL_RING_ROWS)
        stager = _WeightStager(_interleave(
            _row_chunks(w1_hbm, 0, w1_ref, ring_ff) + _row_chunks(w3_hbm, 0, w3_ref, ring_ff),
            _row_chunks(win_hbm, 0, wu_ref.at[:, 0:D_MODEL], ring_model) + _row_chunks(w2_hbm, 0, w2_ref, ring_model),
            _row_chunks(win_hbm, D_MODEL, wu_ref.at[:, D_MODEL:UQKV_WIDTH], ring_narrow)))
        freq = freq_ref[...]
        lane = lax.broadcasted_iota(jnp.int32, (1, LANES), 1)
        sign = jnp.where((lane & (HEAD_DIM // 2)) == 0, -1.0, 1.0)
        row = lax.broadcasted_iota(jnp.int32, (ROW_TILE, 1), 0)
        per_tile = -(-len(stager.staged) // (TILES_PER_SEQ + 1))
        for tile in range(TILES_PER_SEQ + 1):
            pos = tile * ROW_TILE + row if tile < TILES_PER_SEQ else PAST_LEN + row // DEC_BATCH
            ang = pos.astype(F32) * freq
            rows = slice(tile * ROW_TILE, (tile + 1) * ROW_TILE)
            rope_ref[rows, 0:LANES] = jnp.cos(ang)
            rope_ref[rows, LANES:2 * LANES] = jnp.sin(ang) * sign
            stager.run(per_tile)
        stager.run()
        for copy in gather:
            copy.wait()
        for ref in (z_c, q_c, k_c, v_c, u_c):
            ref[...] = jnp.zeros(ref.shape, ref.dtype)

    def norm_previous():
        h1 = _layer_norm(z_c[...], g_ref[...], b_ref[...])
        h_ref[...] = h1
        return h1.astype(BF16)

    def rope_tables():
        tile = jnp.clip(prev, 0, PROMPT_TILES)
        table = jnp.where(tile == PROMPT_TILES, TILES_PER_SEQ, tile % TILES_PER_SEQ)
        rows = pl.ds(pl.multiple_of(table * ROW_TILE, ROW_TILE), ROW_TILE)
        cos = rope_ref[rows, 0:LANES]
        lane = lax.broadcasted_iota(jnp.int32, cos.shape, 1)
        return cos, rope_ref[rows, LANES:2 * LANES], (lane & (HEAD_DIM // 2)) == 0

    def project_u(h1b):
        return _dot(h1b, wu_ref[:, 0:POOL_WIDTH])

    def project_q(h1b):
        z = _dot(h1b, wu_ref[:, POOL_WIDTH:POOL_WIDTH + Q_WIDTH])
        tables = rope_tables()
        return [(_rope(z[:, c:c + LANES], *tables) * (HEAD_DIM ** -0.5)).astype(BF16)
                for c in range(0, Q_WIDTH, LANES)]

    def project_kv(h1b):
        z = _dot(h1b, wu_ref[:, POOL_WIDTH + Q_WIDTH:UQKV_WIDTH])
        return _rope(z[:, :KV_WIDTH], *rope_tables()), z[:, KV_WIDTH:]

    def tile_step(x):
        normed = []

        def norm():
            normed.append(norm_previous())

        projected = []

        def project():
            projected.extend([project_u(normed[0]), project_q(normed[0]), *project_kv(normed[0])])

        def carry():
            u, q, k, v = projected
            starts_seq = (prev + TILES_PER_SEQ) % TILES_PER_SEQ == 0
            u_c[0:POOL_PAD, :] = jnp.where(starts_seq, 0.0, u_c[ROW_TILE:ROW_TILE + POOL_PAD, :])
            u_c[POOL_PAD:POOL_PAD + ROW_TILE, :] = u
            for c, qc in enumerate(q):
                q_c[:, c * LANES:(c + 1) * LANES] = qc
            k_c[0:WINDOW, :] = k_c[ROW_TILE:ROW_TILE + WINDOW, :]
            v_c[0:WINDOW, :] = v_c[ROW_TILE:ROW_TILE + WINDOW, :]
            k_c[WINDOW:WINDOW + ROW_TILE, :] = k
            v_c[WINDOW:WINDOW + ROW_TILE, :] = v

        side = {0: norm, FF_SLOTS - 3: project, FF_SLOTS - 1: carry}
        stages = context_stages()
        assert len(stages) < FF_SLOTS - 3
        side.update(enumerate(stages, start=1))
        _swiglu_residual(x, w1_ref, w3_ref, w2_ref, h_s, side, z_c)

    def context_stages():
        context = _prompt_context_work(r - 2, q_c, k_c, v_c, u_c, bias_ref, sinks_ref,
                                       pool_ref, attn_ref, kt_ref, vt_ref, ulast_ref)
        stages = [("scores", 0)]
        for unit in range(1, 8):
            stages += [("scores", unit), ("values", unit - 1)]
        stages.append(("values", 7))

        def pooling_and_state():
            context["pool_wide"]()
            context["pool_narrow"]()
            context["sequence_state"]()

        return [context[stage] for stage in stages] + [pooling_and_state]

    def drain():
        stages = context_stages()
        third = len(stages) // 3
        h1b = norm_previous()
        for work in stages[:third]:
            work()
        us_ref[...] = project_u(h1b)
        for work in stages[third:2 * third]:
            work()
        for c, qc in enumerate(project_q(h1b)):
            qs_ref[:, c * LANES:(c + 1) * LANES] = qc
        for work in stages[2 * third:]:
            work()
        ks_ref[...], vs_ref[...] = project_kv(h1b)

    is_tile_step = jnp.logical_and(r >= 0, r <= PROMPT_TILES)

    @pl.when(is_tile_step)
    def _():
        tile_step(jnp.where(r == PROMPT_TILES, xs_s[...], xp_ref[...]))

    @pl.when(jnp.logical_not(is_tile_step))
    def _():
        pl.when(r == ROW_TILES)(drain)
        pl.when(r < 0)(stage_and_reset)


def _front(xp, xs, w1, w3, w2, w_in, g, b, freq, sinks):
    rows = ROW_TILES * ROW_TILE

    def lagged(width):
        return pl.BlockSpec((ROW_TILE, width), lambda s: (jnp.clip(s - 3, 0, PROMPT_TILES - 1), 0))

    def seq_of_lagged(shape):
        return pl.BlockSpec(shape, lambda s: (jnp.clip(s - 3, 0, PROMPT_TILES - 1) // TILES_PER_SEQ, 0, 0))

    sample = lambda width: pl.BlockSpec((ROW_TILE, width), lambda r: (0, 0))
    hbm = pl.BlockSpec(memory_space=pl.ANY)
    return pl.pallas_call(
        _front_kernel,
        out_shape=(jax.ShapeDtypeStruct((rows, D_MODEL), F32),
                   jax.ShapeDtypeStruct((PROMPT_ROWS, POOL_WIDTH + Q_WIDTH), BF16),
                   jax.ShapeDtypeStruct((SAMPLE_ROWS, POOL_WIDTH), F32),
                   jax.ShapeDtypeStruct((SAMPLE_ROWS, Q_WIDTH), BF16),
                   jax.ShapeDtypeStruct((SAMPLE_ROWS, KV_WIDTH), F32),
                   jax.ShapeDtypeStruct((SAMPLE_ROWS, KV_WIDTH), F32),
                   jax.ShapeDtypeStruct((BATCH, KV_WIDTH, WINDOW), F32),
                   jax.ShapeDtypeStruct((BATCH, KV_WIDTH, WINDOW), F32),
                   jax.ShapeDtypeStruct((BATCH, POOL_PAD, POOL_WIDTH), F32)),
        grid=(ROW_TILES + 2,),
        in_specs=[pl.BlockSpec(memory_space=pltpu.SMEM),
                  pl.BlockSpec((ROW_TILE, D_MODEL), lambda s: (jnp.clip(s - 1, 0, PROMPT_TILES - 1), 0)),
                  hbm, hbm, hbm, hbm, hbm, _resident((1, D_MODEL)), _resident((1, D_MODEL)),
                  _resident((1, LANES)), _resident((2, 2 * WINDOW, 2 * KEY_SPAN))],
        out_specs=(pl.BlockSpec((ROW_TILE, D_MODEL), lambda s: (jnp.clip(s - 2, 0, PROMPT_TILES), 0)),
                   lagged(POOL_WIDTH + Q_WIDTH),
                   sample(POOL_WIDTH), sample(Q_WIDTH), sample(KV_WIDTH), sample(KV_WIDTH),
                   seq_of_lagged((1, KV_WIDTH, WINDOW)), seq_of_lagged((1, KV_WIDTH, WINDOW)),
                   seq_of_lagged((1, POOL_PAD, POOL_WIDTH))),
        scratch_shapes=[pltpu.VMEM((ROW_TILE, D_MODEL), F32),
                        pltpu.VMEM((ROW_TILE, Q_WIDTH), BF16),
                        pltpu.VMEM((WINDOW + ROW_TILE, KV_WIDTH), F32),
                        pltpu.VMEM((WINDOW + ROW_TILE, KV_WIDTH), F32),
                        pltpu.VMEM((POOL_PAD + ROW_TILE, POOL_WIDTH), F32),
                        pltpu.VMEM((ROW_TILE, D_FF), BF16),
                        pltpu.VMEM(((TILES_PER_SEQ + 1) * ROW_TILE, 2 * LANES), F32),
                        pltpu.VMEM((D_MODEL, D_FF), BF16), pltpu.VMEM((D_MODEL, D_FF), BF16),
                        pltpu.VMEM((D_FF, D_MODEL), BF16), pltpu.VMEM((D_MODEL, UQKV_WIDTH), BF16)]
                       + _ring_scratch(RING_FF, RING_NARROW)
                       + [pltpu.VMEM((SAMPLE_ROWS, D_MODEL), F32), pltpu.SemaphoreType.DMA((DEC_SEQ,))],
        compiler_params=pltpu.CompilerParams(dimension_semantics=("arbitrary",),
                                             vmem_limit_bytes=VMEM_LIMIT_BYTES),
        name="front",
    )(sinks, xp, xs, w1, w3, w2, w_in, g, b, freq, jnp.asarray(_prompt_bias()))


def _sample_bias():
    row = np.arange(HEADS_PER_KV * DEC_SEQ * SUB_GROUP)
    row_t = (row // SUB_GROUP) % DEC_SEQ
    row_b = row % SUB_GROUP
    col = np.arange(SUB_GROUP * WINDOW)
    ok_c = (col[None, :] // WINDOW == row_b[:, None]) & (col[None, :] % WINDOW >= row_t[:, None])
    new = np.arange(DEC_SEQ * SUB_GROUP)
    ok_n = (new[None, :] % SUB_GROUP == row_b[:, None]) & (new[None, :] // SUB_GROUP <= row_t[:, None])
    to_bias = lambda ok: np.where(ok, 0.0, NEG_INF).astype(np.float32)
    return to_bias(ok_c), to_bias(ok_n)


def _sample_ctx_kernel(sinks_ref, q0, q1, q2, q3, k0, k1, k2, k3, v0, v1, v2, v3, u0, u1, u2, u3,
                       ckt_ref, cvt_ref, cu_ref, bias_c_ref, bias_n_ref,
                       attn_ref, pool_ref, kt_out, vt_out, pu_out, kbt_s, vbt_s):
    q_t = [q[...].astype(F32) for q in (q0, q1, q2, q3)]
    k_t = [k[...] for k in (k0, k1, k2, k3)]
    v_t = [v[...] for v in (v0, v1, v2, v3)]

    rows_u = [cu_ref[i] for i in range(POOL_BUF)] + [u[...] for u in (u0, u1, u2, u3)]
    for t in range(DEC_SEQ):
        pooled = []
        for g, w in enumerate(POOL_WINDOWS):
            cols = slice(g * POOL_GROUP, (g + 1) * POOL_GROUP)
            cur = rows_u[POOL_BUF + t][:, cols]
            acc = cur
            for j in range(1, w):
                acc = acc + rows_u[POOL_BUF + t - j][:, cols]
            pooled.append(acc / float(w) - cur)
        pool_ref[t] = jnp.concatenate(pooled, axis=1)
    for i in range(POOL_BUF):
        pu_out[i] = rows_u[i + DEC_SEQ]

    bias_c = bias_c_ref[...]
    bias_n = bias_n_ref[...]
    head_of_row = lax.broadcasted_iota(jnp.int32, (bias_c.shape[0], 1), 0) // (DEC_SEQ * SUB_GROUP)
    low = lax.broadcasted_iota(jnp.int32, (1, LANES), 1) < HEAD_DIM
    q_sw = [_swap_halves_wide(q) for q in q_t]
    k_sw = [_swap_halves(k) for k in k_t]
    v_sw = [_swap_halves(v) for v in v_t]
    def scores(sub, kh):
        rows = slice(sub * SUB_GROUP, (sub + 1) * SUB_GROUP)

        def q_piece(t, head):
            src = q_t[t] if head % 2 == 0 else q_sw[t]
            chunk = head // 2
            return jnp.where(low, src[rows, chunk * LANES:(chunk + 1) * LANES], 0.0)

        def kv_first(c):
            return c if kh == 0 else jnp.concatenate([c[HEAD_DIM:], c[:HEAD_DIM]], axis=0)

        lhs = jnp.concatenate([q_piece(t, kh * HEADS_PER_KV + g)
                               for g in range(HEADS_PER_KV) for t in range(DEC_SEQ)], axis=0).astype(BF16)
        kcat = jnp.concatenate([kv_first(ckt_ref[sub * SUB_GROUP + b]) for b in range(SUB_GROUP)],
                               axis=1).astype(BF16)
        knew = jnp.concatenate([(k_t[t] if kh == 0 else k_sw[t])[rows] for t in range(DEC_SEQ)],
                               axis=0).astype(BF16)
        s_c = _dot(lhs, kcat) + bias_c
        s_n = _dot_nt(lhs, knew) + bias_n
        sink = jnp.zeros(head_of_row.shape, F32)
        for g in range(HEADS_PER_KV):
            sink = jnp.where(head_of_row == g, sinks_ref[kh * HEADS_PER_KV + g], sink)
        m = jnp.maximum(jnp.maximum(jnp.max(s_c, axis=-1, keepdims=True),
                                    jnp.max(s_n, axis=-1, keepdims=True)), sink)
        p_c = jnp.exp(s_c - m)
        p_n = jnp.exp(s_n - m)
        denom = (jnp.sum(p_c, axis=-1, keepdims=True) + jnp.sum(p_n, axis=-1, keepdims=True)
                 + jnp.exp(sink - m))
        return p_c.astype(BF16), p_n.astype(BF16), denom

    def values(sub, kh, p_c, p_n, denom):
        rows = slice(sub * SUB_GROUP, (sub + 1) * SUB_GROUP)

        def kv_twice(c):
            part = c[kh * HEAD_DIM:(kh + 1) * HEAD_DIM]
            return jnp.concatenate([part, part], axis=0)

        vcat = jnp.concatenate([kv_twice(cvt_ref[sub * SUB_GROUP + b]) for b in range(SUB_GROUP)],
                               axis=1).astype(BF16)
        vnew = jnp.concatenate([(jnp.where(low, v_t[t], v_sw[t]) if kh == 0 else
                                 jnp.where(low, v_sw[t], v_t[t]))[rows] for t in range(DEC_SEQ)],
                               axis=0).astype(BF16)
        o = (_dot_nt(p_c, vcat) + _dot(p_n, vnew)) / denom
        for t in range(DEC_SEQ):
            for pair in range(HEADS_PER_KV // 2):
                piece = lambda g: o[(g * DEC_SEQ + t) * SUB_GROUP:(g * DEC_SEQ + t + 1) * SUB_GROUP]
                c0 = (kh * HEADS_PER_KV // 2 + pair) * LANES
                attn_ref[t, rows, c0:c0 + LANES] = jnp.where(low, piece(2 * pair), piece(2 * pair + 1))

    units = [(sub, kh) for sub in range(SEQ_GROUP // SUB_GROUP) for kh in range(N_KV_HEADS)]
    pending = None
    for unit in units:
        softmaxed = scores(*unit)
        if pending is not None:
            values(*pending)
        pending = (*unit, *softmaxed)
    values(*pending)

    if DEC_SEQ * SEQ_GROUP < LANES:
        zeros = jnp.zeros((LANES - DEC_SEQ * SEQ_GROUP, KV_WIDTH), F32)
        kbt_s[DEC_SEQ * SEQ_GROUP:, :] = zeros
        vbt_s[DEC_SEQ * SEQ_GROUP:, :] = zeros
    for t in range(DEC_SEQ):
        kbt_s[pl.ds(t, SEQ_GROUP, stride=DEC_SEQ), :] = k_t[t]
        vbt_s[pl.ds(t, SEQ_GROUP, stride=DEC_SEQ), :] = v_t[t]
    knew_t = kbt_s[...].T
    vnew_t = vbt_s[...].T
    keep = lax.broadcasted_iota(jnp.int32, (1, WINDOW), 1) < WINDOW - DEC_SEQ
    for b in range(SEQ_GROUP):
        shift_new = WINDOW - DEC_SEQ - DEC_SEQ * b
        kt_out[b] = jnp.where(keep, pltpu.roll(ckt_ref[b], WINDOW - DEC_SEQ, axis=1),
                              pltpu.roll(knew_t, shift_new, axis=1))
        vt_out[b] = jnp.where(keep, pltpu.roll(cvt_ref[b], WINDOW - DEC_SEQ, axis=1),
                              pltpu.roll(vnew_t, shift_new, axis=1))


def _sample_ctx(us, qs, ks, vs, ckt, cvt, cu, sinks):
    groups = DEC_BATCH // SEQ_GROUP

    def token_rows(t, width):
        return pl.BlockSpec((SEQ_GROUP, width), lambda i: (t * groups + i, 0))

    def per_token(width):
        return [token_rows(t, width) for t in range(DEC_SEQ)]

    cache_spec = pl.BlockSpec((SEQ_GROUP, KV_WIDTH, WINDOW), lambda i: (i, 0, 0))
    pool_rows_spec = pl.BlockSpec((POOL_BUF, SEQ_GROUP, POOL_WIDTH), lambda i: (0, i, 0))
    by_token = lambda width: pl.BlockSpec((DEC_SEQ, SEQ_GROUP, width), lambda i: (0, i, 0))
    bias_c, bias_n = _sample_bias()
    return pl.pallas_call(
        _sample_ctx_kernel,
        out_shape=(jax.ShapeDtypeStruct((DEC_SEQ, DEC_BATCH, Q_WIDTH), F32),
                   jax.ShapeDtypeStruct((DEC_SEQ, DEC_BATCH, POOL_WIDTH), F32),
                   jax.ShapeDtypeStruct((DEC_BATCH, KV_WIDTH, WINDOW), F32),
                   jax.ShapeDtypeStruct((DEC_BATCH, KV_WIDTH, WINDOW), F32),
                   jax.ShapeDtypeStruct((POOL_BUF, DEC_BATCH, POOL_WIDTH), F32)),
        grid=(groups,),
        in_specs=[pl.BlockSpec(memory_space=pltpu.SMEM)]
                 + per_token(Q_WIDTH) + per_token(KV_WIDTH) + per_token(KV_WIDTH) + per_token(POOL_WIDTH)
                 + [cache_spec, cache_spec, pool_rows_spec, _resident(bias_c.shape), _resident(bias_n.shape)],
        out_specs=(by_token(Q_WIDTH), by_token(POOL_WIDTH), cache_spec, cache_spec, pool_rows_spec),
        scratch_shapes=[pltpu.VMEM((LANES, KV_WIDTH), F32), pltpu.VMEM((LANES, KV_WIDTH), F32)],
        compiler_params=pltpu.CompilerParams(dimension_semantics=("parallel",),
                                             vmem_limit_bytes=VMEM_LIMIT_BYTES),
        name="sample_ctx",
    )(sinks, *([qs] * DEC_SEQ), *([ks] * DEC_SEQ), *([vs] * DEC_SEQ), *([us] * DEC_SEQ),
      ckt, cvt, cu, jnp.asarray(bias_c), jnp.asarray(bias_n))


def _back_kernel(h_ref, pa_ref, pools_ref, attns_ref,
                 win_hbm, wgrp_hbm, scale_ref, wpo_hbm, wao_hbm, wout_hbm, g2_ref, b2_ref,
                 w1_hbm, w3_hbm, w2_hbm, g3_ref, b3_ref, yp_ref, ys_hbm, z2_c, z3_c, h_s,
                 wg_ref, wgrp_ref, wpo_ref, wao_ref, wout_ref, w1_ref, w3_ref, w2_ref,
                 buf_ff, buf_group, sem_ff, sem_group, sem_model, ys_s, sem_y):
    t = pl.program_id(0)

    def branch_outputs():
        is_sample = t >= PROMPT_TILES
        pool_in = jnp.where(is_sample, pools_ref[...].astype(BF16), pa_ref[:, 0:POOL_WIDTH])
        attn_o = jnp.where(is_sample, attns_ref[...].astype(BF16), pa_ref[:, POOL_WIDTH:POOL_WIDTH + Q_WIDTH])
        zs = [_dot(pool_in[:, g * POOL_GROUP:(g + 1) * POOL_GROUP],
                   wgrp_ref[g * POOL_GROUP:(g + 1) * POOL_GROUP, :])
              for g in range(len(POOL_WINDOWS))]
        pool_z = jnp.concatenate(zs, axis=1) * scale_ref[...]
        return _dot(pool_z.astype(BF16), wpo_ref[...]), _dot(attn_o, wao_ref[...])

    def gate_and_merge(h1b, a, b):
        gates = jax.nn.sigmoid(_dot(h1b, wg_ref[...]))
        return (gates[:, :D_MODEL] * a + gates[:, D_MODEL:] * b).astype(BF16)

    def project_out(h1, merged):
        z2_c[...] = ALPHA * h1 + _dot(merged, wout_ref[...])

    is_tile_step = jnp.logical_and(t >= 1, t <= ROW_TILES)

    @pl.when(is_tile_step)
    def _():
        h1 = h_ref[...]
        a, b = branch_outputs()
        y = _layer_norm(z3_c[...], g3_ref[...], b3_ref[...])
        yp_ref[...] = y
        h2 = _layer_norm(z2_c[...], g2_ref[...], b2_ref[...])
        h1b = jnp.where(t <= ROW_TILES, h1.astype(BF16), y.astype(BF16))
        merged = []
        side = {6: lambda: merged.append(gate_and_merge(h1b, a, b)), 16: lambda: project_out(h1, merged[0])}
        _swiglu_residual(h2, w1_ref, w3_ref, w2_ref, h_s, side, z3_c)

    def last_norm():
        ys_s[...] = _layer_norm(z3_c[...], g3_ref[...], b3_ref[...])
        scatter = [pltpu.make_async_copy(ys_s.at[pl.ds(t * DEC_BATCH, DEC_BATCH), :], ys_hbm.at[:, t, :], sem_y.at[t])
                   for t in range(DEC_SEQ)]
        for copy in scatter:
            copy.start()
        for copy in scatter:
            copy.wait()

    def stage_and_first_tile():
        ring_ff = _ring(buf_ff, sem_ff)
        ring_group = _ring(buf_group, sem_group)
        ring_model = _ring(z3_c, sem_model, rows=MODEL_RING_ROWS)
        mixer = _interleave(_row_chunks(wgrp_hbm, 0, wgrp_ref, ring_group)
                            + _row_chunks(wpo_hbm, 0, wpo_ref, ring_model)
                            + _row_chunks(wao_hbm, 0, wao_ref, ring_model)
                            + _row_chunks(wout_hbm, 0, wout_ref, ring_model),
                            _row_chunks(win_hbm, UQKV_WIDTH, wg_ref, ring_ff))
        swiglu = _interleave(_row_chunks(w1_hbm, 0, w1_ref, ring_ff) + _row_chunks(w3_hbm, 0, w3_ref, ring_ff),
                             _row_chunks(w2_hbm, 0, w2_ref, ring_model))
        stager = _WeightStager(mixer + swiglu)
        stager.run(len(mixer))
        h1 = h_ref[...]
        a, b = branch_outputs()
        stager.run(len(swiglu) // 3)
        merged = gate_and_merge(h1.astype(BF16), a, b)
        stager.run(len(swiglu) // 3)
        project_out(h1, merged)
        stager.run()
        z3_c[...] = jnp.zeros(z3_c.shape, z3_c.dtype)

    @pl.when(jnp.logical_not(is_tile_step))
    def _():
        pl.when(t == ROW_TILES + 1)(last_norm)
        pl.when(t == 0)(stage_and_first_tile)


def _back(h1, pool_attn, pool_s, attn_s, w_in, wgrp, scale, wpo, wao, wout, g2, b2, w1, w3, w2, g3, b3):
    lagged = pl.BlockSpec((ROW_TILE, D_MODEL), lambda t: (jnp.clip(t - 2, 0, PROMPT_TILES - 1), 0))
    hbm = pl.BlockSpec(memory_space=pl.ANY)
    return pl.pallas_call(
        _back_kernel,
        out_shape=(jax.ShapeDtypeStruct((PROMPT_ROWS, D_MODEL), F32),
                   jax.ShapeDtypeStruct((DEC_BATCH, DEC_SEQ, D_MODEL), F32)),
        grid=(ROW_TILES + 2,),
        in_specs=[pl.BlockSpec((ROW_TILE, D_MODEL), lambda t: (jnp.minimum(t, PROMPT_TILES), 0)),
                  pl.BlockSpec((ROW_TILE, POOL_WIDTH + Q_WIDTH), lambda t: (jnp.minimum(t, PROMPT_TILES - 1), 0)),
                  _resident((SAMPLE_ROWS, POOL_WIDTH)), _resident((SAMPLE_ROWS, Q_WIDTH)),
                  hbm, hbm, _resident((1, POOL_WIDTH)), hbm, hbm, hbm,
                  _resident((1, D_MODEL)), _resident((1, D_MODEL)),
                  hbm, hbm, hbm, _resident((1, D_MODEL)), _resident((1, D_MODEL))],
        out_specs=(lagged, hbm),
        scratch_shapes=[pltpu.VMEM((ROW_TILE, D_MODEL), F32), pltpu.VMEM((ROW_TILE, D_MODEL), F32),
                        pltpu.VMEM((ROW_TILE, D_FF), BF16),
                        pltpu.VMEM((D_MODEL, 2 * D_MODEL), BF16),
                        pltpu.VMEM((len(POOL_WINDOWS) * POOL_GROUP, POOL_GROUP), BF16),
                        pltpu.VMEM((POOL_WIDTH, D_MODEL), BF16), pltpu.VMEM((Q_WIDTH, D_MODEL), BF16),
                        pltpu.VMEM((D_MODEL, D_MODEL), BF16),
                        pltpu.VMEM((D_MODEL, D_FF), BF16), pltpu.VMEM((D_MODEL, D_FF), BF16),
                        pltpu.VMEM((D_FF, D_MODEL), BF16)]
                       + _ring_scratch(RING_FF_DEEP, RING_GROUP)
                       + [pltpu.VMEM((SAMPLE_ROWS, D_MODEL), F32), pltpu.SemaphoreType.DMA((DEC_SEQ,))],
        compiler_params=pltpu.CompilerParams(dimension_semantics=("arbitrary",),
                                             vmem_limit_bytes=VMEM_LIMIT_BYTES),
        name="back",
    )(h1, pool_attn, pool_s, attn_s, w_in, wgrp, scale, wpo, wao, wout, g2, b2, w1, w3, w2, g3, b3)


def kernel(x_prompt, x_sample, cache_pool_u, cache_k_win, cache_v_win, w_in, pool_w_grp, pool_scale,
           attn_sinks, w_pool_out, w_attn_out, w_out, ffn1_w1, ffn1_w3, ffn1_w2, ffn2_w1, ffn2_w3,
           ffn2_w2, ln1_g, ln1_b, ln2_g, ln2_b, ln3_g, ln3_b):
    assert DEPTH == 1 and w_in.shape[0] == 1
    l = 0
    vec = lambda p: p[l].reshape(1, -1)
    sinks = attn_sinks[l]

    freq = jnp.tile(ROPE_THETA ** (-2.0 * jnp.arange(HEAD_DIM // 2, dtype=F32) / HEAD_DIM), 4).reshape(1, LANES)

    xp = x_prompt.reshape(PROMPT_ROWS, D_MODEL)
    (h1, pool_attn, us, qs, ks, vs, kt_last, vt_last, u_last) = _front(
        xp, x_sample, ffn1_w1, ffn1_w3, ffn1_w2, w_in, vec(ln1_g), vec(ln1_b), freq, sinks)

    to_t = lambda c: jnp.transpose(c[l], (0, 2, 3, 1)).reshape(DEC_BATCH, KV_WIDTH, WINDOW)
    cu = jnp.transpose(cache_pool_u[l], (1, 0, 2))
    attn_s, pool_s, kt_s, vt_s, pu_s = _sample_ctx(us, qs, ks, vs, to_t(cache_k_win), to_t(cache_v_win), cu, sinks)

    wgrp = pool_w_grp.reshape(DEPTH, len(POOL_WINDOWS) * POOL_GROUP, POOL_GROUP)
    yp, ys = _back(h1, pool_attn, pool_s.reshape(SAMPLE_ROWS, POOL_WIDTH), attn_s.reshape(SAMPLE_ROWS, Q_WIDTH),
                   w_in, wgrp, vec(pool_scale), w_pool_out, w_attn_out, w_out, vec(ln2_g), vec(ln2_b),
                   ffn2_w1, ffn2_w3, ffn2_w2, vec(ln3_g), vec(ln3_b))
    yp = yp.reshape(BATCH, SEQ, D_MODEL)

    from_t = lambda c, n: jnp.transpose(c.reshape(n, N_KV_HEADS, HEAD_DIM, WINDOW), (0, 3, 1, 2))[None]
    pool_u_prompt = u_last[None, :, POOL_PAD - POOL_BUF:]
    pool_u_sample = jnp.transpose(pu_s, (1, 0, 2))[None]
    return (yp, ys, pool_u_prompt, from_t(kt_last, BATCH), from_t(vt_last, BATCH),
            pool_u_sample, from_t(kt_s, DEC_BATCH), from_t(vt_s, DEC_BATCH))
```

```python
import jax
import jax.numpy as jnp
import numpy as np
from jax import lax
from jax.experimental import pallas as pl
from jax.experimental.pallas import tpu as pltpu

D_MODEL = 1024
BATCH = 8
SEQ = 2048
DEC_BATCH = 128
DEC_SEQ = 4
PAST_LEN = 8192
POOL_WINDOWS = (2, 4, 8, 16)
POOL_GROUP = 128
POOL_WIDTH = 512
POOL_BUF = 15
N_HEADS = 8
N_KV_HEADS = 2
HEADS_PER_KV = N_HEADS // N_KV_HEADS
HEAD_DIM = 64
Q_WIDTH = 512
KV_WIDTH = 128
WINDOW = 128
ROPE_THETA = 10000.0
D_FF = 2816
DEPTH = 1
ALPHA = (2.0 * DEPTH) ** 0.25
LN_EPS = 1e-5
NEG_INF = -1e30
UQKV_WIDTH = POOL_WIDTH + Q_WIDTH + 2 * KV_WIDTH

LANES = 128
KEY_SPAN = 2 * WINDOW
VMEM_LIMIT_BYTES = 61 * 1024 * 1024

ROW_TILE = 512
TILES_PER_SEQ = SEQ // ROW_TILE
PROMPT_ROWS = BATCH * SEQ
PROMPT_TILES = PROMPT_ROWS // ROW_TILE
SAMPLE_ROWS = DEC_BATCH * DEC_SEQ
ROW_TILES = PROMPT_TILES + 1
POOL_PAD = 16
SEQ_GROUP = 32
SUB_GROUP = 8
FF_CHUNK = 256
OUT_BLOCK = 256

BF16 = jnp.bfloat16
F32 = jnp.float32


def _dot(a, b):
    return jnp.dot(a, b, preferred_element_type=F32)


def _dot_nt(a, b):
    return lax.dot_general(a, b, (((1,), (1,)), ((), ())), preferred_element_type=F32)


def _layer_norm(y, g, b):
    mu = jnp.mean(y, axis=-1, keepdims=True)
    yc = y - mu
    var = jnp.mean(yc * yc, axis=-1, keepdims=True)
    return yc * lax.rsqrt(var + LN_EPS) * g + b


def _resident(shape):
    nd = len(shape)
    return pl.BlockSpec(shape, lambda *_: (0,) * nd, pipeline_mode=pl.Buffered(1))


class _WeightStager:
    def __init__(self, jobs):
        self.staged, self.next_in_slot, first, users = [], {}, [], {}
        for src, ring, dst in jobs:
            mine = users.setdefault(id(ring), [])
            slot, sem = ring[len(mine) % len(ring)]
            if len(mine) < len(ring):
                first.append(len(self.staged))
            else:
                self.next_in_slot[mine[-len(ring)]] = len(self.staged)
            mine.append(len(self.staged))
            if slot.shape[1] > dst.shape[1]:
                slot = slot.at[:, pl.ds(0, dst.shape[1])]
            self.staged.append((pltpu.make_async_copy(src, slot, sem), slot, dst))
        self.done = 0
        for i in first:
            self.staged[i][0].start()

    def run(self, count=None):
        end = len(self.staged) if count is None else min(self.done + count, len(self.staged))
        for i in range(self.done, end):
            copy, slot, dst = self.staged[i]
            copy.wait()
            dst[...] = slot[...].astype(BF16)
            if i in self.next_in_slot:
                self.staged[self.next_in_slot[i]][0].start()
        self.done = end


def _interleave(*job_lists):
    keyed = [((i + 0.5) / len(jobs), n, job) for n, jobs in enumerate(job_lists) for i, job in enumerate(jobs)]
    return [job for _, _, job in sorted(keyed, key=lambda entry: entry[:2])]


def _ring(buf, sems, rows=None):
    if rows is None:
        return [(buf.at[i], sems.at[i]) for i in range(buf.shape[0])]
    return [(buf.at[pl.ds(i * rows, rows), :], sems.at[i]) for i in range(buf.shape[0] // rows)]


def _row_chunks(w_hbm, col0, dst, ring):
    rows, cols = dst.shape
    step, width = ring[0][0].shape
    assert rows % step == 0 and width >= cols
    return [(w_hbm.at[0, pl.ds(r0, step), pl.ds(col0, cols)], ring, dst.at[pl.ds(r0, step), :])
            for r0 in range(0, rows, step)]


STAGE_IN_FLIGHT = 4
RING_FF = (STAGE_IN_FLIGHT, 64, D_FF)
MODEL_RING_ROWS = ROW_TILE // STAGE_IN_FLIGHT
RING_NARROW = (STAGE_IN_FLIGHT, 128, UQKV_WIDTH - D_MODEL)
RING_GROUP = (1, len(POOL_WINDOWS) * POOL_GROUP, POOL_GROUP)


def _ring_scratch(*rings):
    return ([pltpu.VMEM(ring, F32) for ring in rings]
            + [pltpu.SemaphoreType.DMA((ring[0],)) for ring in rings]
            + [pltpu.SemaphoreType.DMA((STAGE_IN_FLIGHT,))])


FF_SLOTS = 2 * (D_FF // FF_CHUNK)


def _swiglu_residual(x, w1_ref, w3_ref, w2_ref, h_s, side_work, out_ref):
    assert all(0 <= slot <= FF_SLOTS for slot in side_work)
    run = lambda slot: side_work.get(slot, lambda: None)()
    xb = x.astype(BF16)
    for j in range(D_FF // FF_CHUNK):
        cols = slice(j * FF_CHUNK, (j + 1) * FF_CHUNK)
        a = _dot(xb, w1_ref[:, cols])
        run(2 * j)
        b = _dot(xb, w3_ref[:, cols])
        h_s[:, cols] = ((a * jax.nn.sigmoid(a)) * b).astype(BF16)
        run(2 * j + 1)
    run(FF_SLOTS)
    h = h_s[...]
    for c0 in range(0, D_MODEL, OUT_BLOCK):
        cols = slice(c0, c0 + OUT_BLOCK)
        out_ref[:, cols] = ALPHA * x[:, cols] + 0.5 * _dot(h, w2_ref[:, cols])


def _rope(x, cos, sin_signed, first_half):
    fwd = pltpu.roll(x, LANES - HEAD_DIM // 2, axis=1)
    bwd = pltpu.roll(x, HEAD_DIM // 2, axis=1)
    return x * cos + jnp.where(first_half, fwd, bwd) * sin_signed


def _swap_halves(x):
    return pltpu.roll(x, HEAD_DIM, axis=1)


def _swap_halves_wide(x):
    return jnp.concatenate([_swap_halves(x[:, c:c + LANES]) for c in range(0, x.shape[1], LANES)], axis=1)


def _lane_split(x, x_sw, kh):
    low = lax.broadcasted_iota(jnp.int32, (1, LANES), 1) < HEAD_DIM
    lo, hi = (x, x_sw) if kh == 0 else (x_sw, x)
    return jnp.concatenate([jnp.where(low, lo, 0.0), jnp.where(low, 0.0, hi)], axis=0).astype(BF16)


def _sink_softmax(q_pairs, keys, kh, bias, sinks_ref):
    s = _dot_nt(q_pairs, _lane_split(keys, _swap_halves(keys), kh)) + bias
    second_pair = lax.broadcasted_iota(jnp.int32, (s.shape[0], 1), 0) >= WINDOW
    probs, denoms = [], []
    for c in range(2):
        sc = s[:, c * KEY_SPAN:(c + 1) * KEY_SPAN]
        head = kh * HEADS_PER_KV + c
        sink = jnp.where(second_pair, sinks_ref[head + 2], sinks_ref[head])
        m = jnp.maximum(jnp.max(sc, axis=-1, keepdims=True), sink)
        p = jnp.exp(sc - m)
        denoms.append(jnp.sum(p, axis=-1, keepdims=True) + jnp.exp(sink - m))
        probs.append(p.astype(BF16))
    return jnp.concatenate(probs, axis=1), denoms


def _weighted_values(probs, denoms, vals, kh):
    low = lax.broadcasted_iota(jnp.int32, (1, LANES), 1) < HEAD_DIM
    o = _dot(probs, _lane_split(vals, _swap_halves(vals), kh))
    return o / jnp.where(low, denoms[0], denoms[1])


def _prompt_context_work(tile, q_c, k_c, v_c, u_c, bias_ref, sinks_ref,
                         pool_ref, attn_ref, kt_ref, vt_ref, ulast_ref):
    seq_tile = (tile + TILES_PER_SEQ) % TILES_PER_SEQ
    first_tile = seq_tile == 0
    softmaxed = {}

    def scores(unit, blk, kh):
        r0 = blk * WINDOW
        bias = bias_ref[jnp.where(first_tile, 1, 0)] if blk == 0 else bias_ref[0]
        c0 = 2 * kh * LANES
        q_pairs = jnp.concatenate([q_c[r0:r0 + WINDOW, c0:c0 + LANES],
                                   q_c[r0:r0 + WINDOW, c0 + LANES:c0 + 2 * LANES]], axis=0)
        softmaxed[unit] = _sink_softmax(q_pairs, k_c[r0:r0 + KEY_SPAN, :], kh, bias, sinks_ref)

    def values(unit, blk, kh):
        r0 = blk * WINDOW
        c0 = 2 * kh * LANES
        o = _weighted_values(*softmaxed.pop(unit), v_c[r0:r0 + KEY_SPAN, :], kh)
        attn_ref[r0:r0 + WINDOW, c0:c0 + LANES] = o[:WINDOW].astype(BF16)
        attn_ref[r0:r0 + WINDOW, c0 + LANES:c0 + 2 * LANES] = o[WINDOW:].astype(BF16)

    def pool(groups):
        pos = seq_tile * ROW_TILE + lax.broadcasted_iota(jnp.int32, (ROW_TILE, 1), 0)
        for g in groups:
            w = POOL_WINDOWS[g]
            cols = slice(g * POOL_GROUP, (g + 1) * POOL_GROUP)
            rows = u_c[:, cols]
            acc, span = rows, 1
            while span < w:
                acc = acc + pltpu.roll(acc, span, axis=0)
                span *= 2
            cur = rows[POOL_PAD:]
            cnt = jnp.minimum(pos + 1, w).astype(F32)
            pool_ref[:, cols] = (acc[POOL_PAD:] / cnt - cur).astype(BF16)

    def sequence_state():
        kt_ref[0] = k_c[ROW_TILE:ROW_TILE + WINDOW, :].T
        vt_ref[0] = v_c[ROW_TILE:ROW_TILE + WINDOW, :].T
        ulast_ref[0] = u_c[ROW_TILE:ROW_TILE + POOL_PAD, :]

    work = {"pool_wide": lambda: pool((3,)), "pool_narrow": lambda: pool((0, 1, 2)),
            "sequence_state": sequence_state}
    for blk in range(ROW_TILE // WINDOW):
        for kh in range(N_KV_HEADS):
            unit = blk * N_KV_HEADS + kh
            work["scores", unit] = lambda unit=unit, blk=blk, kh=kh: scores(unit, blk, kh)
            work["values", unit] = lambda unit=unit, blk=blk, kh=kh: values(unit, blk, kh)
    return work


def _prompt_bias():
    r = np.arange(2 * WINDOW)[:, None] % WINDOW
    c = np.arange(2 * KEY_SPAN)[None, :] % KEY_SPAN
    valid = (r <= c) & (c <= r + WINDOW)
    first = valid & (c >= WINDOW)
    return np.where(np.stack([valid, first]), 0.0, NEG_INF).astype(np.float32)


def _front_kernel(sinks_ref, xp_ref, xs_hbm, w1_hbm, w3_hbm, w2_hbm, win_hbm, g_ref, b_ref, freq_ref, bias_ref,
                  h_ref, pa_ref, us_ref, qs_ref, ks_ref, vs_ref, kt_ref, vt_ref, ulast_ref,
                  z_c, q_c, k_c, v_c, u_c, h_s, rope_ref, w1_ref, w3_ref, w2_ref, wu_ref,
                  buf_ff, buf_narrow, sem_ff, sem_narrow, sem_model, xs_s, sem_x):
    r = pl.program_id(0) - 1
    prev = r - 1
    pool_ref = pa_ref.at[:, 0:POOL_WIDTH]
    attn_ref = pa_ref.at[:, POOL_WIDTH:POOL_WIDTH + Q_WIDTH]

    def stage_and_reset():
        gather = [pltpu.make_async_copy(xs_hbm.at[:, t, :], xs_s.at[pl.ds(t * DEC_BATCH, DEC_BATCH), :], sem_x.at[t])
                  for t in range(DEC_SEQ)]
        for copy in gather:
            copy.start()
        ring_ff = _ring(buf_ff, sem_ff)
        ring_narrow = _ring(buf_narrow, sem_narrow)
        ring_model = _ring(z_c, sem_model, rows=MODEL_RING_ROWS)
        stager = _WeightStager(_interleave(
            _row_chunks(w1_hbm, 0, w1_ref, ring_ff) + _row_chunks(w3_hbm, 0, w3_ref, ring_ff),
            _row_chunks(win_hbm, 0, wu_ref.at[:, 0:D_MODEL], ring_model) + _row_chunks(w2_hbm, 0, w2_ref, ring_model),
            _row_chunks(win_hbm, D_MODEL, wu_ref.at[:, D_MODEL:UQKV_WIDTH], ring_narrow)))
        freq = freq_ref[...]
        lane = lax.broadcasted_iota(jnp.int32, (1, LANES), 1)
        sign = jnp.where((lane & (HEAD_DIM // 2)) == 0, -1.0, 1.0)
        row = lax.broadcasted_iota(jnp.int32, (ROW_TILE, 1), 0)
        per_tile = -(-len(stager.staged) // (TILES_PER_SEQ + 1))
        for tile in range(TILES_PER_SEQ + 1):
            pos = tile * ROW_TILE + row if tile < TILES_PER_SEQ else PAST_LEN + row // DEC_BATCH
            ang = pos.astype(F32) * freq
            rows = slice(tile * ROW_TILE, (tile + 1) * ROW_TILE)
            rope_ref[rows, 0:LANES] = jnp.cos(ang)
            rope_ref[rows, LANES:2 * LANES] = jnp.sin(ang) * sign
            stager.run(per_tile)
        stager.run()
        for copy in gather:
            copy.wait()
        for ref in (z_c, q_c, k_c, v_c, u_c):
            ref[...] = jnp.zeros(ref.shape, ref.dtype)

    def norm_previous():
        h1 = _layer_norm(z_c[...], g_ref[...], b_ref[...])
        h_ref[...] = h1
        return h1.astype(BF16)

    def rope_tables():
        tile = jnp.clip(prev, 0, PROMPT_TILES)
        table = jnp.where(tile == PROMPT_TILES, TILES_PER_SEQ, tile % TILES_PER_SEQ)
        rows = pl.ds(pl.multiple_of(table * ROW_TILE, ROW_TILE), ROW_TILE)
        cos = rope_ref[rows, 0:LANES]
        lane = lax.broadcasted_iota(jnp.int32, cos.shape, 1)
        return cos, rope_ref[rows, LANES:2 * LANES], (lane & (HEAD_DIM // 2)) == 0

    def project_u(h1b):
        return _dot(h1b, wu_ref[:, 0:POOL_WIDTH])

    def project_q(h1b):
        z = _dot(h1b, wu_ref[:, POOL_WIDTH:POOL_WIDTH + Q_WIDTH])
        tables = rope_tables()
        return [(_rope(z[:, c:c + LANES], *tables) * (HEAD_DIM ** -0.5)).astype(BF16)
                for c in range(0, Q_WIDTH, LANES)]

    def project_kv(h1b):
        z = _dot(h1b, wu_ref[:, POOL_WIDTH + Q_WIDTH:UQKV_WIDTH])
        return _rope(z[:, :KV_WIDTH], *rope_tables()), z[:, KV_WIDTH:]

    def tile_step(x):
        normed = []

        def norm():
            normed.append(norm_previous())

        projected = []

        def project():
            projected.extend([project_u(normed[0]), project_q(normed[0]), *project_kv(normed[0])])

        def carry():
            u, q, k, v = projected
            starts_seq = (prev + TILES_PER_SEQ) % TILES_PER_SEQ == 0
            u_c[0:POOL_PAD, :] = jnp.where(starts_seq, 0.0, u_c[ROW_TILE:ROW_TILE + POOL_PAD, :])
            u_c[POOL_PAD:POOL_PAD + ROW_TILE, :] = u
            for c, qc in enumerate(q):
                q_c[:, c * LANES:(c + 1) * LANES] = qc
            k_c[0:WINDOW, :] = k_c[ROW_TILE:ROW_TILE + WINDOW, :]
            v_c[0:WINDOW, :] = v_c[ROW_TILE:ROW_TILE + WINDOW, :]
            k_c[WINDOW:WINDOW + ROW_TILE, :] = k
            v_c[WINDOW:WINDOW + ROW_TILE, :] = v

        side = {0: norm, FF_SLOTS - 3: project, FF_SLOTS - 1: carry}
        stages = context_stages()
        assert len(stages) < FF_SLOTS - 3
        side.update(enumerate(stages, start=1))
        _swiglu_residual(x, w1_ref, w3_ref, w2_ref, h_s, side, z_c)

    def context_stages():
        context = _prompt_context_work(r - 2, q_c, k_c, v_c, u_c, bias_ref, sinks_ref,
                                       pool_ref, attn_ref, kt_ref, vt_ref, ulast_ref)
        stages = [("scores", 0)]
        for unit in range(1, 8):
            stages += [("scores", unit), ("values", unit - 1)]
        stages.append(("values", 7))

        def pooling_and_state():
            context["pool_wide"]()
            context["pool_narrow"]()
            context["sequence_state"]()

        return [context[stage] for stage in stages] + [pooling_and_state]

    def drain():
        stages = context_stages()
        third = len(stages) // 3
        h1b = norm_previous()
        for work in stages[:third]:
            work()
        us_ref[...] = project_u(h1b)
        for work in stages[third:2 * third]:
            work()
        for c, qc in enumerate(project_q(h1b)):
            qs_ref[:, c * LANES:(c + 1) * LANES] = qc
        for work in stages[2 * third:]:
            work()
        ks_ref[...], vs_ref[...] = project_kv(h1b)

    is_tile_step = jnp.logical_and(r >= 0, r <= PROMPT_TILES)

    @pl.when(is_tile_step)
    def _():
        tile_step(jnp.where(r == PROMPT_TILES, xs_s[...], xp_ref[...]))

    @pl.when(jnp.logical_not(is_tile_step))
    def _():
        pl.when(r == ROW_TILES)(drain)
        pl.when(r < 0)(stage_and_reset)


def _front(xp, xs, w1, w3, w2, w_in, g, b, freq, sinks):
    rows = ROW_TILES * ROW_TILE

    def lagged(width):
        return pl.BlockSpec((ROW_TILE, width), lambda s: (jnp.clip(s - 3, 0, PROMPT_TILES - 1), 0))

    def seq_of_lagged(shape):
        return pl.BlockSpec(shape, lambda s: (jnp.clip(s - 3, 0, PROMPT_TILES - 1) // TILES_PER_SEQ, 0, 0))

    sample = lambda width: pl.BlockSpec((ROW_TILE, width), lambda r: (0, 0))
    hbm = pl.BlockSpec(memory_space=pl.ANY)
    return pl.pallas_call(
        _front_kernel,
        out_shape=(jax.ShapeDtypeStruct((rows, D_MODEL), F32),
                   jax.ShapeDtypeStruct((PROMPT_ROWS, POOL_WIDTH + Q_WIDTH), BF16),
                   jax.ShapeDtypeStruct((SAMPLE_ROWS, POOL_WIDTH), F32),
                   jax.ShapeDtypeStruct((SAMPLE_ROWS, Q_WIDTH), BF16),
                   jax.ShapeDtypeStruct((SAMPLE_ROWS, KV_WIDTH), F32),
                   jax.ShapeDtypeStruct((SAMPLE_ROWS, KV_WIDTH), F32),
                   jax.ShapeDtypeStruct((BATCH, KV_WIDTH, WINDOW), F32),
                   jax.ShapeDtypeStruct((BATCH, KV_WIDTH, WINDOW), F32),
                   jax.ShapeDtypeStruct((BATCH, POOL_PAD, POOL_WIDTH), F32)),
        grid=(ROW_TILES + 2,),
        in_specs=[pl.BlockSpec(memory_space=pltpu.SMEM),
                  pl.BlockSpec((ROW_TILE, D_MODEL), lambda s: (jnp.clip(s - 1, 0, PROMPT_TILES - 1), 0)),
                  hbm, hbm, hbm, hbm, hbm, _resident((1, D_MODEL)), _resident((1, D_MODEL)),
                  _resident((1, LANES)), _resident((2, 2 * WINDOW, 2 * KEY_SPAN))],
        out_specs=(pl.BlockSpec((ROW_TILE, D_MODEL), lambda s: (jnp.clip(s - 2, 0, PROMPT_TILES), 0)),
                   lagged(POOL_WIDTH + Q_WIDTH),
                   sample(POOL_WIDTH), sample(Q_WIDTH), sample(KV_WIDTH), sample(KV_WIDTH),
                   seq_of_lagged((1, KV_WIDTH, WINDOW)), seq_of_lagged((1, KV_WIDTH, WINDOW)),
                   seq_of_lagged((1, POOL_PAD, POOL_WIDTH))),
        scratch_shapes=[pltpu.VMEM((ROW_TILE, D_MODEL), F32),
                        pltpu.VMEM((ROW_TILE, Q_WIDTH), BF16),
                        pltpu.VMEM((WINDOW + ROW_TILE, KV_WIDTH), F32),
                        pltpu.VMEM((WINDOW + ROW_TILE, KV_WIDTH), F32),
                        pltpu.VMEM((POOL_PAD + ROW_TILE, POOL_WIDTH), F32),
                        pltpu.VMEM((ROW_TILE, D_FF), BF16),
                        pltpu.VMEM(((TILES_PER_SEQ + 1) * ROW_TILE, 2 * LANES), F32),
                        pltpu.VMEM((D_MODEL, D_FF), BF16), pltpu.VMEM((D_MODEL, D_FF), BF16),
                        pltpu.VMEM((D_FF, D_MODEL), BF16), pltpu.VMEM((D_MODEL, UQKV_WIDTH), BF16)]
                       + _ring_scratch(RING_FF, RING_NARROW)
                       + [pltpu.VMEM((SAMPLE_ROWS, D_MODEL), F32), pltpu.SemaphoreType.DMA((DEC_SEQ,))],
        compiler_params=pltpu.CompilerParams(dimension_semantics=("arbitrary",),
                                             vmem_limit_bytes=VMEM_LIMIT_BYTES),
        name="front",
    )(sinks, xp, xs, w1, w3, w2, w_in, g, b, freq, jnp.asarray(_prompt_bias()))


def _sample_bias():
    row = np.arange(HEADS_PER_KV * DEC_SEQ * SUB_GROUP)
    row_t = (row // SUB_GROUP) % DEC_SEQ
    row_b = row % SUB_GROUP
    col = np.arange(SUB_GROUP * WINDOW)
    ok_c = (col[None, :] // WINDOW == row_b[:, None]) & (col[None, :] % WINDOW >= row_t[:, None])
    new = np.arange(DEC_SEQ * SUB_GROUP)
    ok_n = (new[None, :] % SUB_GROUP == row_b[:, None]) & (new[None, :] // SUB_GROUP <= row_t[:, None])
    to_bias = lambda ok: np.where(ok, 0.0, NEG_INF).astype(np.float32)
    return to_bias(ok_c), to_bias(ok_n)


def _sample_ctx_kernel(sinks_ref, q0, q1, q2, q3, k0, k1, k2, k3, v0, v1, v2, v3, u0, u1, u2, u3,
                       ckt_ref, cvt_ref, cu_ref, bias_c_ref, bias_n_ref,
                       attn_ref, pool_ref, kt_out, vt_out, pu_out, kbt_s, vbt_s):
    q_t = [q[...].astype(F32) for q in (q0, q1, q2, q3)]
    k_t = [k[...] for k in (k0, k1, k2, k3)]
    v_t = [v[...] for v in (v0, v1, v2, v3)]

    rows_u = [cu_ref[i] for i in range(POOL_BUF)] + [u[...] for u in (u0, u1, u2, u3)]
    for t in range(DEC_SEQ):
        pooled = []
        for g, w in enumerate(POOL_WINDOWS):
            cols = slice(g * POOL_GROUP, (g + 1) * POOL_GROUP)
            cur = rows_u[POOL_BUF + t][:, cols]
            acc = cur
            for j in range(1, w):
                acc = acc + rows_u[POOL_BUF + t - j][:, cols]
            pooled.append(acc / float(w) - cur)
        pool_ref[t] = jnp.concatenate(pooled, axis=1)
    for i in range(POOL_BUF):
        pu_out[i] = rows_u[i + DEC_SEQ]

    bias_c = bias_c_ref[...]
    bias_n = bias_n_ref[...]
    head_of_row = lax.broadcasted_iota(jnp.int32, (bias_c.shape[0], 1), 0) // (DEC_SEQ * SUB_GROUP)
    low = lax.broadcasted_iota(jnp.int32, (1, LANES), 1) < HEAD_DIM
    q_sw = [_swap_halves_wide(q) for q in q_t]
    k_sw = [_swap_halves(k) for k in k_t]
    v_sw = [_swap_halves(v) for v in v_t]
    def scores(sub, kh):
        rows = slice(sub * SUB_GROUP, (sub + 1) * SUB_GROUP)

        def q_piece(t, head):
            src = q_t[t] if head % 2 == 0 else q_sw[t]
            chunk = head // 2
            return jnp.where(low, src[rows, chunk * LANES:(chunk + 1) * LANES], 0.0)

        def kv_first(c):
            return c if kh == 0 else jnp.concatenate([c[HEAD_DIM:], c[:HEAD_DIM]], axis=0)

        lhs = jnp.concatenate([q_piece(t, kh * HEADS_PER_KV + g)
                               for g in range(HEADS_PER_KV) for t in range(DEC_SEQ)], axis=0).astype(BF16)
        kcat = jnp.concatenate([kv_first(ckt_ref[sub * SUB_GROUP + b]) for b in range(SUB_GROUP)],
                               axis=1).astype(BF16)
        knew = jnp.concatenate([(k_t[t] if kh == 0 else k_sw[t])[rows] for t in range(DEC_SEQ)],
                               axis=0).astype(BF16)
        s_c = _dot(lhs, kcat) + bias_c
        s_n = _dot_nt(lhs, knew) + bias_n
        sink = jnp.zeros(head_of_row.shape, F32)
        for g in range(HEADS_PER_KV):
            sink = jnp.where(head_of_row == g, sinks_ref[kh * HEADS_PER_KV + g], sink)
        m = jnp.maximum(jnp.maximum(jnp.max(s_c, axis=-1, keepdims=True),
                                    jnp.max(s_n, axis=-1, keepdims=True)), sink)
        p_c = jnp.exp(s_c - m)
        p_n = jnp.exp(s_n - m)
        denom = (jnp.sum(p_c, axis=-1, keepdims=True) + jnp.sum(p_n, axis=-1, keepdims=True)
                 + jnp.exp(sink - m))
        return p_c.astype(BF16), p_n.astype(BF16), denom

    def values(sub, kh, p_c, p_n, denom):
        rows = slice(sub * SUB_GROUP, (sub + 1) * SUB_GROUP)

        def kv_twice(c):
            part = c[kh * HEAD_DIM:(kh + 1) * HEAD_DIM]
            return jnp.concatenate([part, part], axis=0)

        vcat = jnp.concatenate([kv_twice(cvt_ref[sub * SUB_GROUP + b]) for b in range(SUB_GROUP)],
                               axis=1).astype(BF16)
        vnew = jnp.concatenate([(jnp.where(low, v_t[t], v_sw[t]) if kh == 0 else
                                 jnp.where(low, v_sw[t], v_t[t]))[rows] for t in range(DEC_SEQ)],
                               axis=0).astype(BF16)
        o = (_dot_nt(p_c, vcat) + _dot(p_n, vnew)) / denom
        for t in range(DEC_SEQ):
            for pair in range(HEADS_PER_KV // 2):
                piece = lambda g: o[(g * DEC_SEQ + t) * SUB_GROUP:(g * DEC_SEQ + t + 1) * SUB_GROUP]
                c0 = (kh * HEADS_PER_KV // 2 + pair) * LANES
                attn_ref[t, rows, c0:c0 + LANES] = jnp.where(low, piece(2 * pair), piece(2 * pair + 1))

    units = [(sub, kh) for sub in range(SEQ_GROUP // SUB_GROUP) for kh in range(N_KV_HEADS)]
    pending = None
    for unit in units:
        softmaxed = scores(*unit)
        if pending is not None:
            values(*pending)
        pending = (*unit, *softmaxed)
    values(*pending)

    if DEC_SEQ * SEQ_GROUP < LANES:
        zeros = jnp.zeros((LANES - DEC_SEQ * SEQ_GROUP, KV_WIDTH), F32)
        kbt_s[DEC_SEQ * SEQ_GROUP:, :] = zeros
        vbt_s[DEC_SEQ * SEQ_GROUP:, :] = zeros
    for t in range(DEC_SEQ):
        kbt_s[pl.ds(t, SEQ_GROUP, stride=DEC_SEQ), :] = k_t[t]
        vbt_s[pl.ds(t, SEQ_GROUP, stride=DEC_SEQ), :] = v_t[t]
    knew_t = kbt_s[...].T
    vnew_t = vbt_s[...].T
    keep = lax.broadcasted_iota(jnp.int32, (1, WINDOW), 1) < WINDOW - DEC_SEQ
    for b in range(SEQ_GROUP):
        shift_new = WINDOW - DEC_SEQ - DEC_SEQ * b
        kt_out[b] = jnp.where(keep, pltpu.roll(ckt_ref[b], WINDOW - DEC_SEQ, axis=1),
                              pltpu.roll(knew_t, shift_new, axis=1))
        vt_out[b] = jnp.where(keep, pltpu.roll(cvt_ref[b], WINDOW - DEC_SEQ, axis=1),
                              pltpu.roll(vnew_t, shift_new, axis=1))


def _sample_ctx(us, qs, ks, vs, ckt, cvt, cu, sinks):
    groups = DEC_BATCH // SEQ_GROUP

    def token_rows(t, width):
        return pl.BlockSpec((SEQ_GROUP, width), lambda i: (t * groups + i, 0))

    def per_token(width):
        return [token_rows(t, width) for t in range(DEC_SEQ)]

    cache_spec = pl.BlockSpec((SEQ_GROUP, KV_WIDTH, WINDOW), lambda i: (i, 0, 0))
    pool_rows_spec = pl.BlockSpec((POOL_BUF, SEQ_GROUP, POOL_WIDTH), lambda i: (0, i, 0))
    by_token = lambda width: pl.BlockSpec((DEC_SEQ, SEQ_GROUP, width), lambda i: (0, i, 0))
    bias_c, bias_n = _sample_bias()
    return pl.pallas_call(
        _sample_ctx_kernel,
        out_shape=(jax.ShapeDtypeStruct((DEC_SEQ, DEC_BATCH, Q_WIDTH), F32),
                   jax.ShapeDtypeStruct((DEC_SEQ, DEC_BATCH, POOL_WIDTH), F32),
                   jax.ShapeDtypeStruct((DEC_BATCH, KV_WIDTH, WINDOW), F32),
                   jax.ShapeDtypeStruct((DEC_BATCH, KV_WIDTH, WINDOW), F32),
                   jax.ShapeDtypeStruct((POOL_BUF, DEC_BATCH, POOL_WIDTH), F32)),
        grid=(groups,),
        in_specs=[pl.BlockSpec(memory_space=pltpu.SMEM)]
                 + per_token(Q_WIDTH) + per_token(KV_WIDTH) + per_token(KV_WIDTH) + per_token(POOL_WIDTH)
                 + [cache_spec, cache_spec, pool_rows_spec, _resident(bias_c.shape), _resident(bias_n.shape)],
        out_specs=(by_token(Q_WIDTH), by_token(POOL_WIDTH), cache_spec, cache_spec, pool_rows_spec),
        scratch_shapes=[pltpu.VMEM((LANES, KV_WIDTH), F32), pltpu.VMEM((LANES, KV_WIDTH), F32)],
        compiler_params=pltpu.CompilerParams(dimension_semantics=("parallel",),
                                             vmem_limit_bytes=VMEM_LIMIT_BYTES),
        name="sample_ctx",
    )(sinks, *([qs] * DEC_SEQ), *([ks] * DEC_SEQ), *([vs] * DEC_SEQ), *([us] * DEC_SEQ),
      ckt, cvt, cu, jnp.asarray(bias_c), jnp.asarray(bias_n))


def _back_kernel(h_ref, pa_ref, pools_ref, attns_ref,
                 win_hbm, wgrp_hbm, scale_ref, wpo_hbm, wao_hbm, wout_hbm, g2_ref, b2_ref,
                 w1_hbm, w3_hbm, w2_hbm, g3_ref, b3_ref, yp_ref, ys_hbm, z2_c, z3_c, h_s,
                 wg_ref, wgrp_ref, wpo_ref, wao_ref, wout_ref, w1_ref, w3_ref, w2_ref,
                 buf_ff, buf_group, sem_ff, sem_group, sem_model, ys_s, sem_y):
    t = pl.program_id(0)

    def branch_outputs():
        is_sample = t >= PROMPT_TILES
        pool_in = jnp.where(is_sample, pools_ref[...].astype(BF16), pa_ref[:, 0:POOL_WIDTH])
        attn_o = jnp.where(is_sample, attns_ref[...].astype(BF16), pa_ref[:, POOL_WIDTH:POOL_WIDTH + Q_WIDTH])
        zs = [_dot(pool_in[:, g * POOL_GROUP:(g + 1) * POOL_GROUP],
                   wgrp_ref[g * POOL_GROUP:(g + 1) * POOL_GROUP, :])
              for g in range(len(POOL_WINDOWS))]
        pool_z = jnp.concatenate(zs, axis=1) * scale_ref[...]
        return _dot(pool_z.astype(BF16), wpo_ref[...]), _dot(attn_o, wao_ref[...])

    def gate_and_merge(h1b, a, b):
        gates = jax.nn.sigmoid(_dot(h1b, wg_ref[...]))
        return (gates[:, :D_MODEL] * a + gates[:, D_MODEL:] * b).astype(BF16)

    def project_out(h1, merged):
        z2_c[...] = ALPHA * h1 + _dot(merged, wout_ref[...])

    is_tile_step = jnp.logical_and(t >= 1, t <= ROW_TILES)

    @pl.when(is_tile_step)
    def _():
        h1 = h_ref[...]
        a, b = branch_outputs()
        y = _layer_norm(z3_c[...], g3_ref[...], b3_ref[...])
        yp_ref[...] = y
        h2 = _layer_norm(z2_c[...], g2_ref[...], b2_ref[...])
        h1b = jnp.where(t <= ROW_TILES, h1.astype(BF16), y.astype(BF16))
        merged = []
        side = {6: lambda: merged.append(gate_and_merge(h1b, a, b)), 16: lambda: project_out(h1, merged[0])}
        _swiglu_residual(h2, w1_ref, w3_ref, w2_ref, h_s, side, z3_c)

    def last_norm():
        ys_s[...] = _layer_norm(z3_c[...], g3_ref[...], b3_ref[...])
        scatter = [pltpu.make_async_copy(ys_s.at[pl.ds(t * DEC_BATCH, DEC_BATCH), :], ys_hbm.at[:, t, :], sem_y.at[t])
                   for t in range(DEC_SEQ)]
        for copy in scatter:
            copy.start()
        for copy in scatter:
            copy.wait()

    def stage_and_first_tile():
        ring_ff = _ring(buf_ff, sem_ff)
        ring_group = _ring(buf_group, sem_group)
        ring_model = _ring(z3_c, sem_model, rows=MODEL_RING_ROWS)
        mixer = _interleave(_row_chunks(wgrp_hbm, 0, wgrp_ref, ring_group)
                            + _row_chunks(wpo_hbm, 0, wpo_ref, ring_model)
                            + _row_chunks(wao_hbm, 0, wao_ref, ring_model)
                            + _row_chunks(wout_hbm, 0, wout_ref, ring_model),
                            _row_chunks(win_hbm, UQKV_WIDTH, wg_ref, ring_ff))
        swiglu = _interleave(_row_chunks(w1_hbm, 0, w1_ref, ring_ff) + _row_chunks(w3_hbm, 0, w3_ref, ring_ff),
                             _row_chunks(w2_hbm, 0, w2_ref, ring_model))
        stager = _WeightStager(mixer + swiglu)
        stager.run(len(mixer))
        h1 = h_ref[...]
        a, b = branch_outputs()
        stager.run(len(swiglu) // 3)
        merged = gate_and_merge(h1.astype(BF16), a, b)
        stager.run(len(swiglu) // 3)
        project_out(h1, merged)
        stager.run()
        z3_c[...] = jnp.zeros(z3_c.shape, z3_c.dtype)

    @pl.when(jnp.logical_not(is_tile_step))
    def _():
        pl.when(t == ROW_TILES + 1)(last_norm)
        pl.when(t == 0)(stage_and_first_tile)


def _back(h1, pool_attn, pool_s, attn_s, w_in, wgrp, scale, wpo, wao, wout, g2, b2, w1, w3, w2, g3, b3):
    lagged = pl.BlockSpec((ROW_TILE, D_MODEL), lambda t: (jnp.clip(t - 2, 0, PROMPT_TILES - 1), 0))
    hbm = pl.BlockSpec(memory_space=pl.ANY)
    return pl.pallas_call(
        _back_kernel,
        out_shape=(jax.ShapeDtypeStruct((PROMPT_ROWS, D_MODEL), F32),
                   jax.ShapeDtypeStruct((DEC_BATCH, DEC_SEQ, D_MODEL), F32)),
        grid=(ROW_TILES + 2,),
        in_specs=[pl.BlockSpec((ROW_TILE, D_MODEL), lambda t: (jnp.minimum(t, PROMPT_TILES), 0)),
                  pl.BlockSpec((ROW_TILE, POOL_WIDTH + Q_WIDTH), lambda t: (jnp.minimum(t, PROMPT_TILES - 1), 0)),
                  _resident((SAMPLE_ROWS, POOL_WIDTH)), _resident((SAMPLE_ROWS, Q_WIDTH)),
                  hbm, hbm, _resident((1, POOL_WIDTH)), hbm, hbm, hbm,
                  _resident((1, D_MODEL)), _resident((1, D_MODEL)),
                  hbm, hbm, hbm, _resident((1, D_MODEL)), _resident((1, D_MODEL))],
        out_specs=(lagged, hbm),
        scratch_shapes=[pltpu.VMEM((ROW_TILE, D_MODEL), F32), pltpu.VMEM((ROW_TILE, D_MODEL), F32),
                        pltpu.VMEM((ROW_TILE, D_FF), BF16),
                        pltpu.VMEM((D_MODEL, 2 * D_MODEL), BF16),
                        pltpu.VMEM((len(POOL_WINDOWS) * POOL_GROUP, POOL_GROUP), BF16),
                        pltpu.VMEM((POOL_WIDTH, D_MODEL), BF16), pltpu.VMEM((Q_WIDTH, D_MODEL), BF16),
                        pltpu.VMEM((D_MODEL, D_MODEL), BF16),
                        pltpu.VMEM((D_MODEL, D_FF), BF16), pltpu.VMEM((D_MODEL, D_FF), BF16),
                        pltpu.VMEM((D_FF, D_MODEL), BF16)]
                       + _ring_scratch(RING_FF, RING_GROUP)
                       + [pltpu.VMEM((SAMPLE_ROWS, D_MODEL), F32), pltpu.SemaphoreType.DMA((DEC_SEQ,))],
        compiler_params=pltpu.CompilerParams(dimension_semantics=("arbitrary",),
                                             vmem_limit_bytes=VMEM_LIMIT_BYTES),
        name="back",
    )(h1, pool_attn, pool_s, attn_s, w_in, wgrp, scale, wpo, wao, wout, g2, b2, w1, w3, w2, g3, b3)


def kernel(x_prompt, x_sample, cache_pool_u, cache_k_win, cache_v_win, w_in, pool_w_grp, pool_scale,
           attn_sinks, w_pool_out, w_attn_out, w_out, ffn1_w1, ffn1_w3, ffn1_w2, ffn2_w1, ffn2_w3,
           ffn2_w2, ln1_g, ln1_b, ln2_g, ln2_b, ln3_g, ln3_b):
    assert DEPTH == 1 and w_in.shape[0] == 1
    l = 0
    vec = lambda p: p[l].reshape(1, -1)
    sinks = attn_sinks[l]

    freq = jnp.tile(ROPE_THETA ** (-2.0 * jnp.arange(HEAD_DIM // 2, dtype=F32) / HEAD_DIM), 4).reshape(1, LANES)

    xp = x_prompt.reshape(PROMPT_ROWS, D_MODEL)
    (h1, pool_attn, us, qs, ks, vs, kt_last, vt_last, u_last) = _front(
        xp, x_sample, ffn1_w1, ffn1_w3, ffn1_w2, w_in, vec(ln1_g), vec(ln1_b), freq, sinks)

    to_t = lambda c: jnp.transpose(c[l], (0, 2, 3, 1)).reshape(DEC_BATCH, KV_WIDTH, WINDOW)
    cu = jnp.transpose(cache_pool_u[l], (1, 0, 2))
    attn_s, pool_s, kt_s, vt_s, pu_s = _sample_ctx(us, qs, ks, vs, to_t(cache_k_win), to_t(cache_v_win), cu, sinks)

    wgrp = pool_w_grp.reshape(DEPTH, len(POOL_WINDOWS) * POOL_GROUP, POOL_GROUP)
    yp, ys = _back(h1, pool_attn, pool_s.reshape(SAMPLE_ROWS, POOL_WIDTH), attn_s.reshape(SAMPLE_ROWS, Q_WIDTH),
                   w_in, wgrp, vec(pool_scale), w_pool_out, w_attn_out, w_out, vec(ln2_g), vec(ln2_b),
                   ffn2_w1, ffn2_w3, ffn2_w2, vec(ln3_g), vec(ln3_b))
    yp = yp.reshape(BATCH, SEQ, D_MODEL)

    from_t = lambda c, n: jnp.transpose(c.reshape(n, N_KV_HEADS, HEAD_DIM, WINDOW), (0, 3, 1, 2))[None]
    pool_u_prompt = u_last[None, :, POOL_PAD - POOL_BUF:]
    pool_u_sample = jnp.transpose(pu_s, (1, 0, 2))[None]
    return (yp, ys, pool_u_prompt, from_t(kt_last, BATCH), from_t(vt_last, BATCH),
            pool_u_sample, from_t(kt_s, DEC_BATCH), from_t(vt_s, DEC_BATCH))
```

```python
import jax
import jax.numpy as jnp
import numpy as np
from jax import lax
from jax.experimental import pallas as pl
from jax.experimental.pallas import tpu as pltpu

D_MODEL = 1024
BATCH = 8
SEQ = 2048
DEC_BATCH = 128
DEC_SEQ = 4
PAST_LEN = 8192
POOL_WINDOWS = (2, 4, 8, 16)
POOL_GROUP = 128
POOL_WIDTH = 512
POOL_BUF = 15
N_HEADS = 8
N_KV_HEADS = 2
HEADS_PER_KV = N_HEADS // N_KV_HEADS
HEAD_DIM = 64
Q_WIDTH = 512
KV_WIDTH = 128
WINDOW = 128
ROPE_THETA = 10000.0
D_FF = 2816
DEPTH = 1
ALPHA = (2.0 * DEPTH) ** 0.25
LN_EPS = 1e-5
NEG_INF = -1e30
UQKV_WIDTH = POOL_WIDTH + Q_WIDTH + 2 * KV_WIDTH

LANES = 128
KEY_SPAN = 2 * WINDOW
VMEM_LIMIT_BYTES = 61 * 1024 * 1024

ROW_TILE = 512
TILES_PER_SEQ = SEQ // ROW_TILE
PROMPT_ROWS = BATCH * SEQ
PROMPT_TILES = PROMPT_ROWS // ROW_TILE
SAMPLE_ROWS = DEC_BATCH * DEC_SEQ
ROW_TILES = PROMPT_TILES + 1
POOL_PAD = 16
SEQ_GROUP = 32
SUB_GROUP = 8
FF_CHUNK = 256
OUT_BLOCK = 256

BF16 = jnp.bfloat16
F32 = jnp.float32


def _dot(a, b):
    return jnp.dot(a, b, preferred_element_type=F32)


def _dot_nt(a, b):
    return lax.dot_general(a, b, (((1,), (1,)), ((), ())), preferred_element_type=F32)


def _layer_norm(y, g, b):
    mu = jnp.mean(y, axis=-1, keepdims=True)
    yc = y - mu
    var = jnp.mean(yc * yc, axis=-1, keepdims=True)
    return yc * lax.rsqrt(var + LN_EPS) * g + b


def _resident(shape):
    nd = len(shape)
    return pl.BlockSpec(shape, lambda *_: (0,) * nd, pipeline_mode=pl.Buffered(1))


class _WeightStager:
    def __init__(self, jobs):
        self.staged, self.next_in_slot, first, users = [], {}, [], {}
        for src, ring, dst in jobs:
            mine = users.setdefault(id(ring), [])
            slot, sem = ring[len(mine) % len(ring)]
            if len(mine) < len(ring):
                first.append(len(self.staged))
            else:
                self.next_in_slot[mine[-len(ring)]] = len(self.staged)
            mine.append(len(self.staged))
            if slot.shape[1] > dst.shape[1]:
                slot = slot.at[:, pl.ds(0, dst.shape[1])]
            self.staged.append((pltpu.make_async_copy(src, slot, sem), slot, dst))
        self.done = 0
        for i in first:
            self.staged[i][0].start()

    def run(self, count=None):
        end = len(self.staged) if count is None else min(self.done + count, len(self.staged))
        for i in range(self.done, end):
            copy, slot, dst = self.staged[i]
            copy.wait()
            dst[...] = slot[...].astype(BF16)
            if i in self.next_in_slot:
                self.staged[self.next_in_slot[i]][0].start()
        self.done = end


def _interleave(*job_lists):
    keyed = [((i + 0.5) / len(jobs), n, job) for n, jobs in enumerate(job_lists) for i, job in enumerate(jobs)]
    return [job for _, _, job in sorted(keyed, key=lambda entry: entry[:2])]


def _ring(buf, sems, rows=None):
    if rows is None:
        return [(buf.at[i], sems.at[i]) for i in range(buf.shape[0])]
    return [(buf.at[pl.ds(i * rows, rows), :], sems.at[i]) for i in range(buf.shape[0] // rows)]


def _row_chunks(w_hbm, col0, dst, ring):
    rows, cols = dst.shape
    step, width = ring[0][0].shape
    assert rows % step == 0 and width >= cols
    return [(w_hbm.at[0, pl.ds(r0, step), pl.ds(col0, cols)], ring, dst.at[pl.ds(r0, step), :])
            for r0 in range(0, rows, step)]


STAGE_IN_FLIGHT = 4
RING_FF = (STAGE_IN_FLIGHT, 64, D_FF)
MODEL_RING_ROWS = ROW_TILE // STAGE_IN_FLIGHT
RING_NARROW = (STAGE_IN_FLIGHT, 128, UQKV_WIDTH - D_MODEL)
RING_GROUP = (1, len(POOL_WINDOWS) * POOL_GROUP, POOL_GROUP)


def _ring_scratch(*rings):
    return ([pltpu.VMEM(ring, F32) for ring in rings]
            + [pltpu.SemaphoreType.DMA((ring[0],)) for ring in rings]
            + [pltpu.SemaphoreType.DMA((STAGE_IN_FLIGHT,))])


FF_SLOTS = 2 * (D_FF // FF_CHUNK)


def _swiglu_residual(x, w1_ref, w3_ref, w2_ref, h_s, side_work, out_ref):
    assert all(0 <= slot <= FF_SLOTS for slot in side_work)
    run = lambda slot: side_work.get(slot, lambda: None)()
    xb = x.astype(BF16)
    for j in range(D_FF // FF_CHUNK):
        cols = slice(j * FF_CHUNK, (j + 1) * FF_CHUNK)
        a = _dot(xb, w1_ref[:, cols])
        run(2 * j)
        b = _dot(xb, w3_ref[:, cols])
        h_s[:, cols] = ((a * jax.nn.sigmoid(a)) * b).astype(BF16)
        run(2 * j + 1)
    run(FF_SLOTS)
    h = h_s[...]
    for c0 in range(0, D_MODEL, OUT_BLOCK):
        cols = slice(c0, c0 + OUT_BLOCK)
        out_ref[:, cols] = ALPHA * x[:, cols] + 0.5 * _dot(h, w2_ref[:, cols])


def _rope(x, cos, sin_signed, first_half):
    fwd = pltpu.roll(x, LANES - HEAD_DIM // 2, axis=1)
    bwd = pltpu.roll(x, HEAD_DIM // 2, axis=1)
    return x * cos + jnp.where(first_half, fwd, bwd) * sin_signed


def _swap_halves(x):
    return pltpu.roll(x, HEAD_DIM, axis=1)


def _swap_halves_wide(x):
    return jnp.concatenate([_swap_halves(x[:, c:c + LANES]) for c in range(0, x.shape[1], LANES)], axis=1)


def _lane_split(x, x_sw, kh):
    low = lax.broadcasted_iota(jnp.int32, (1, LANES), 1) < HEAD_DIM
    lo, hi = (x, x_sw) if kh == 0 else (x_sw, x)
    return jnp.concatenate([jnp.where(low, lo, 0.0), jnp.where(low, 0.0, hi)], axis=0).astype(BF16)


def _sink_softmax(q_pairs, keys, kh, bias, sinks_ref):
    s = _dot_nt(q_pairs, _lane_split(keys, _swap_halves(keys), kh)) + bias
    second_pair = lax.broadcasted_iota(jnp.int32, (s.shape[0], 1), 0) >= WINDOW
    probs, denoms = [], []
    for c in range(2):
        sc = s[:, c * KEY_SPAN:(c + 1) * KEY_SPAN]
        head = kh * HEADS_PER_KV + c
        sink = jnp.where(second_pair, sinks_ref[head + 2], sinks_ref[head])
        m = jnp.maximum(jnp.max(sc, axis=-1, keepdims=True), sink)
        p = jnp.exp(sc - m)
        denoms.append(jnp.sum(p, axis=-1, keepdims=True) + jnp.exp(sink - m))
        probs.append(p.astype(BF16))
    return jnp.concatenate(probs, axis=1), denoms


def _weighted_values(probs, denoms, vals, kh):
    low = lax.broadcasted_iota(jnp.int32, (1, LANES), 1) < HEAD_DIM
    o = _dot(probs, _lane_split(vals, _swap_halves(vals), kh))
    return o / jnp.where(low, denoms[0], denoms[1])


def _prompt_context_work(tile, q_c, k_c, v_c, u_c, bias_ref, sinks_ref,
                         pool_ref, attn_ref, kt_ref, vt_ref, ulast_ref):
    seq_tile = (tile + TILES_PER_SEQ) % TILES_PER_SEQ
    first_tile = seq_tile == 0
    softmaxed = {}

    def scores(unit, blk, kh):
        r0 = blk * WINDOW
        bias = bias_ref[jnp.where(first_tile, 1, 0)] if blk == 0 else bias_ref[0]
        c0 = 2 * kh * LANES
        q_pairs = jnp.concatenate([q_c[r0:r0 + WINDOW, c0:c0 + LANES],
                                   q_c[r0:r0 + WINDOW, c0 + LANES:c0 + 2 * LANES]], axis=0)
        softmaxed[unit] = _sink_softmax(q_pairs, k_c[r0:r0 + KEY_SPAN, :], kh, bias, sinks_ref)

    def values(unit, blk, kh):
        r0 = blk * WINDOW
        c0 = 2 * kh * LANES
        o = _weighted_values(*softmaxed.pop(unit), v_c[r0:r0 + KEY_SPAN, :], kh)
        attn_ref[r0:r0 + WINDOW, c0:c0 + LANES] = o[:WINDOW].astype(BF16)
        attn_ref[r0:r0 + WINDOW, c0 + LANES:c0 + 2 * LANES] = o[WINDOW:].astype(BF16)

    def pool(groups):
        pos = seq_tile * ROW_TILE + lax.broadcasted_iota(jnp.int32, (ROW_TILE, 1), 0)
        for g in groups:
            w = POOL_WINDOWS[g]
            cols = slice(g * POOL_GROUP, (g + 1) * POOL_GROUP)
            rows = u_c[:, cols]
            acc, span = rows, 1
            while span < w:
                acc = acc + pltpu.roll(acc, span, axis=0)
                span *= 2
            cur = rows[POOL_PAD:]
            cnt = jnp.minimum(pos + 1, w).astype(F32)
            pool_ref[:, cols] = (acc[POOL_PAD:] / cnt - cur).astype(BF16)

    def sequence_state():
        kt_ref[0] = k_c[ROW_TILE:ROW_TILE + WINDOW, :].T
        vt_ref[0] = v_c[ROW_TILE:ROW_TILE + WINDOW, :].T
        ulast_ref[0] = u_c[ROW_TILE:ROW_TILE + POOL_PAD, :]

    work = {"pool_wide": lambda: pool((3,)), "pool_narrow": lambda: pool((0, 1, 2)),
            "sequence_state": sequence_state}
    for blk in range(ROW_TILE // WINDOW):
        for kh in range(N_KV_HEADS):
            unit = blk * N_KV_HEADS + kh
            work["scores", unit] = lambda unit=unit, blk=blk, kh=kh: scores(unit, blk, kh)
            work["values", unit] = lambda unit=unit, blk=blk, kh=kh: values(unit, blk, kh)
    return work


def _prompt_bias():
    r = np.arange(2 * WINDOW)[:, None] % WINDOW
    c = np.arange(2 * KEY_SPAN)[None, :] % KEY_SPAN
    valid = (r <= c) & (c <= r + WINDOW)
    first = valid & (c >= WINDOW)
    return np.where(np.stack([valid, first]), 0.0, NEG_INF).astype(np.float32)


def _front_kernel(sinks_ref, xp_ref, xs_hbm, w1_hbm, w3_hbm, w2_hbm, win_hbm, g_ref, b_ref, freq_ref, bias_ref,
                  h_ref, pa_ref, us_ref, qs_ref, ks_ref, vs_ref, kt_ref, vt_ref, ulast_ref,
                  z_c, q_c, k_c, v_c, u_c, h_s, rope_ref, w1_ref, w3_ref, w2_ref, wu_ref,
                  buf_ff, buf_narrow, sem_ff, sem_narrow, sem_model, xs_s, sem_x):
    r = pl.program_id(0) - 1
    prev = r - 1
    pool_ref = pa_ref.at[:, 0:POOL_WIDTH]
    attn_ref = pa_ref.at[:, POOL_WIDTH:POOL_WIDTH + Q_WIDTH]

    def stage_and_reset():
        gather = [pltpu.make_async_copy(xs_hbm.at[:, t, :], xs_s.at[pl.ds(t * DEC_BATCH, DEC_BATCH), :], sem_x.at[t])
                  for t in range(DEC_SEQ)]
        for copy in gather:
            copy.start()
        ring_ff = _ring(buf_ff, sem_ff)
        ring_narrow = _ring(buf_narrow, sem_narrow)
        ring_model = _ring(z_c, sem_model, rows=MODEL_RING_ROWS)
        stager = _WeightStager(_interleave(
            _row_chunks(w1_hbm, 0, w1_ref, ring_ff) + _row_chunks(w3_hbm, 0, w3_ref, ring_ff),
            _row_chunks(win_hbm, 0, wu_ref.at[:, 0:D_MODEL], ring_model) + _row_chunks(w2_hbm, 0, w2_ref, ring_model),
            _row_chunks(win_hbm, D_MODEL, wu_ref.at[:, D_MODEL:UQKV_WIDTH], ring_narrow)))
        freq = freq_ref[...]
        lane = lax.broadcasted_iota(jnp.int32, (1, LANES), 1)
        sign = jnp.where((lane & (HEAD_DIM // 2)) == 0, -1.0, 1.0)
        row = lax.broadcasted_iota(jnp.int32, (ROW_TILE, 1), 0)
        per_tile = -(-len(stager.staged) // (TILES_PER_SEQ + 1))
        for tile in range(TILES_PER_SEQ + 1):
            pos = tile * ROW_TILE + row if tile < TILES_PER_SEQ else PAST_LEN + row // DEC_BATCH
            ang = pos.astype(F32) * freq
            rows = slice(tile * ROW_TILE, (tile + 1) * ROW_TILE)
            rope_ref[rows, 0:LANES] = jnp.cos(ang)
            rope_ref[rows, LANES:2 * LANES] = jnp.sin(ang) * sign
            stager.run(per_tile)
        stager.run()
        for copy in gather:
            copy.wait()
        for ref in (z_c, q_c, k_c, v_c, u_c):
            ref[...] = jnp.zeros(ref.shape, ref.dtype)

    def norm_previous():
        h1 = _layer_norm(z_c[...], g_ref[...], b_ref[...])
        h_ref[...] = h1
        return h1.astype(BF16)

    def rope_tables():
        tile = jnp.clip(prev, 0, PROMPT_TILES)
        table = jnp.where(tile == PROMPT_TILES, TILES_PER_SEQ, tile % TILES_PER_SEQ)
        rows = pl.ds(pl.multiple_of(table * ROW_TILE, ROW_TILE), ROW_TILE)
        cos = rope_ref[rows, 0:LANES]
        lane = lax.broadcasted_iota(jnp.int32, cos.shape, 1)
        return cos, rope_ref[rows, LANES:2 * LANES], (lane & (HEAD_DIM // 2)) == 0

    def project_u(h1b):
        return _dot(h1b, wu_ref[:, 0:POOL_WIDTH])

    def project_q(h1b):
        z = _dot(h1b, wu_ref[:, POOL_WIDTH:POOL_WIDTH + Q_WIDTH])
        tables = rope_tables()
        return [(_rope(z[:, c:c + LANES], *tables) * (HEAD_DIM ** -0.5)).astype(BF16)
                for c in range(0, Q_WIDTH, LANES)]

    def project_kv(h1b):
        z = _dot(h1b, wu_ref[:, POOL_WIDTH + Q_WIDTH:UQKV_WIDTH])
        return _rope(z[:, :KV_WIDTH], *rope_tables()), z[:, KV_WIDTH:]

    def tile_step(x):
        normed = []

        def norm():
            normed.append(norm_previous())

        projected = []

        def project():
            projected.extend([project_u(normed[0]), project_q(normed[0]), *project_kv(normed[0])])

        def carry():
            u, q, k, v = projected
            starts_seq = (prev + TILES_PER_SEQ) % TILES_PER_SEQ == 0
            u_c[0:POOL_PAD, :] = jnp.where(starts_seq, 0.0, u_c[ROW_TILE:ROW_TILE + POOL_PAD, :])
            u_c[POOL_PAD:POOL_PAD + ROW_TILE, :] = u
            for c, qc in enumerate(q):
                q_c[:, c * LANES:(c + 1) * LANES] = qc
            k_c[0:WINDOW, :] = k_c[ROW_TILE:ROW_TILE + WINDOW, :]
            v_c[0:WINDOW, :] = v_c[ROW_TILE:ROW_TILE + WINDOW, :]
            k_c[WINDOW:WINDOW + ROW_TILE, :] = k
            v_c[WINDOW:WINDOW + ROW_TILE, :] = v

        side = {0: norm, FF_SLOTS - 3: project, FF_SLOTS - 1: carry}
        stages = context_stages()
        assert len(stages) < FF_SLOTS - 3
        side.update(enumerate(stages, start=1))
        _swiglu_residual(x, w1_ref, w3_ref, w2_ref, h_s, side, z_c)

    def context_stages():
        context = _prompt_context_work(r - 2, q_c, k_c, v_c, u_c, bias_ref, sinks_ref,
                                       pool_ref, attn_ref, kt_ref, vt_ref, ulast_ref)
        stages = [("scores", 0)]
        for unit in range(1, 8):
            stages += [("scores", unit), ("values", unit - 1)]
        stages.append(("values", 7))

        def pooling_and_state():
            context["pool_wide"]()
            context["pool_narrow"]()
            context["sequence_state"]()

        return [context[stage] for stage in stages] + [pooling_and_state]

    def drain():
        stages = context_stages()
        third = len(stages) // 3
        h1b = norm_previous()
        for work in stages[:third]:
            work()
        us_ref[...] = project_u(h1b)
        for work in stages[third:2 * third]:
            work()
        for c, qc in enumerate(project_q(h1b)):
            qs_ref[:, c * LANES:(c + 1) * LANES] = qc
        for work in stages[2 * third:]:
            work()
        ks_ref[...], vs_ref[...] = project_kv(h1b)

    is_tile_step = jnp.logical_and(r >= 0, r <= PROMPT_TILES)

    @pl.when(is_tile_step)
    def _():
        tile_step(jnp.where(r == PROMPT_TILES, xs_s[...], xp_ref[...]))

    @pl.when(jnp.logical_not(is_tile_step))
    def _():
        pl.when(r == ROW_TILES)(drain)
        pl.when(r < 0)(stage_and_reset)


def _front(xp, xs, w1, w3, w2, w_in, g, b, freq, sinks):
    rows = ROW_TILES * ROW_TILE

    def lagged(width):
        return pl.BlockSpec((ROW_TILE, width), lambda s: (jnp.clip(s - 3, 0, PROMPT_TILES - 1), 0))

    def seq_of_lagged(shape):
        return pl.BlockSpec(shape, lambda s: (jnp.clip(s - 3, 0, PROMPT_TILES - 1) // TILES_PER_SEQ, 0, 0))

    sample = lambda width: pl.BlockSpec((ROW_TILE, width), lambda r: (0, 0))
    hbm = pl.BlockSpec(memory_space=pl.ANY)
    return pl.pallas_call(
        _front_kernel,
        out_shape=(jax.ShapeDtypeStruct((rows, D_MODEL), F32),
                   jax.ShapeDtypeStruct((PROMPT_ROWS, POOL_WIDTH + Q_WIDTH), BF16),
                   jax.ShapeDtypeStruct((SAMPLE_ROWS, POOL_WIDTH), F32),
                   jax.ShapeDtypeStruct((SAMPLE_ROWS, Q_WIDTH), BF16),
                   jax.ShapeDtypeStruct((SAMPLE_ROWS, KV_WIDTH), F32),
                   jax.ShapeDtypeStruct((SAMPLE_ROWS, KV_WIDTH), F32),
                   jax.ShapeDtypeStruct((BATCH, KV_WIDTH, WINDOW), F32),
                   jax.ShapeDtypeStruct((BATCH, KV_WIDTH, WINDOW), F32),
                   jax.ShapeDtypeStruct((BATCH, POOL_PAD, POOL_WIDTH), F32)),
        grid=(ROW_TILES + 2,),
        in_specs=[pl.BlockSpec(memory_space=pltpu.SMEM),
                  pl.BlockSpec((ROW_TILE, D_MODEL), lambda s: (jnp.clip(s - 1, 0, PROMPT_TILES - 1), 0)),
                  hbm, hbm, hbm, hbm, hbm, _resident((1, D_MODEL)), _resident((1, D_MODEL)),
                  _resident((1, LANES)), _resident((2, 2 * WINDOW, 2 * KEY_SPAN))],
        out_specs=(pl.BlockSpec((ROW_TILE, D_MODEL), lambda s: (jnp.clip(s - 2, 0, PROMPT_TILES), 0)),
                   lagged(POOL_WIDTH + Q_WIDTH),
                   sample(POOL_WIDTH), sample(Q_WIDTH), sample(KV_WIDTH), sample(KV_WIDTH),
                   seq_of_lagged((1, KV_WIDTH, WINDOW)), seq_of_lagged((1, KV_WIDTH, WINDOW)),
                   seq_of_lagged((1, POOL_PAD, POOL_WIDTH))),
        scratch_shapes=[pltpu.VMEM((ROW_TILE, D_MODEL), F32),
                        pltpu.VMEM((ROW_TILE, Q_WIDTH), BF16),
                        pltpu.VMEM((WINDOW + ROW_TILE, KV_WIDTH), F32),
                        pltpu.VMEM((WINDOW + ROW_TILE, KV_WIDTH), F32),
                        pltpu.VMEM((POOL_PAD + ROW_TILE, POOL_WIDTH), F32),
                        pltpu.VMEM((ROW_TILE, D_FF), BF16),
                        pltpu.VMEM(((TILES_PER_SEQ + 1) * ROW_TILE, 2 * LANES), F32),
                        pltpu.VMEM((D_MODEL, D_FF), BF16), pltpu.VMEM((D_MODEL, D_FF), BF16),
                        pltpu.VMEM((D_FF, D_MODEL), BF16), pltpu.VMEM((D_MODEL, UQKV_WIDTH), BF16)]
                       + _ring_scratch(RING_FF, RING_NARROW)
                       + [pltpu.VMEM((SAMPLE_ROWS, D_MODEL), F32), pltpu.SemaphoreType.DMA((DEC_SEQ,))],
        compiler_params=pltpu.CompilerParams(dimension_semantics=("arbitrary",),
                                             vmem_limit_bytes=VMEM_LIMIT_BYTES),
        name="front",
    )(sinks, xp, xs, w1, w3, w2, w_in, g, b, freq, jnp.asarray(_prompt_bias()))


def _sample_bias():
    row = np.arange(HEADS_PER_KV * DEC_SEQ * SUB_GROUP)
    row_t = (row // SUB_GROUP) % DEC_SEQ
    row_b = row % SUB_GROUP
    col = np.arange(SUB_GROUP * WINDOW)
    ok_c = (col[None, :] // WINDOW == row_b[:, None]) & (col[None, :] % WINDOW >= row_t[:, None])
    new = np.arange(DEC_SEQ * SUB_GROUP)
    ok_n = (new[None, :] % SUB_GROUP == row_b[:, None]) & (new[None, :] // SUB_GROUP <= row_t[:, None])
    to_bias = lambda ok: np.where(ok, 0.0, NEG_INF).astype(np.float32)
    return to_bias(ok_c), to_bias(ok_n)


def _sample_ctx_kernel(sinks_ref, q0, q1, q2, q3, k0, k1, k2, k3, v0, v1, v2, v3, u0, u1, u2, u3,
                       ckt_ref, cvt_ref, cu_ref, bias_c_ref, bias_n_ref,
                       attn_ref, pool_ref, kt_out, vt_out, pu_out, kbt_s, vbt_s):
    q_t = [q[...].astype(F32) for q in (q0, q1, q2, q3)]
    k_t = [k[...] for k in (k0, k1, k2, k3)]
    v_t = [v[...] for v in (v0, v1, v2, v3)]

    rows_u = [cu_ref[i] for i in range(POOL_BUF)] + [u[...] for u in (u0, u1, u2, u3)]
    for t in range(DEC_SEQ):
        pooled = []
        for g, w in enumerate(POOL_WINDOWS):
            cols = slice(g * POOL_GROUP, (g + 1) * POOL_GROUP)
            cur = rows_u[POOL_BUF + t][:, cols]
            acc = cur
            for j in range(1, w):
                acc = acc + rows_u[POOL_BUF + t - j][:, cols]
            pooled.append(acc / float(w) - cur)
        pool_ref[t] = jnp.concatenate(pooled, axis=1)
    for i in range(POOL_BUF):
        pu_out[i] = rows_u[i + DEC_SEQ]

    bias_c = bias_c_ref[...]
    bias_n = bias_n_ref[...]
    head_of_row = lax.broadcasted_iota(jnp.int32, (bias_c.shape[0], 1), 0) // (DEC_SEQ * SUB_GROUP)
    low = lax.broadcasted_iota(jnp.int32, (1, LANES), 1) < HEAD_DIM
    q_sw = [_swap_halves_wide(q) for q in q_t]
    k_sw = [_swap_halves(k) for k in k_t]
    v_sw = [_swap_halves(v) for v in v_t]
    def scores(sub, kh):
        rows = slice(sub * SUB_GROUP, (sub + 1) * SUB_GROUP)

        def q_piece(t, head):
            src = q_t[t] if head % 2 == 0 else q_sw[t]
            chunk = head // 2
            return jnp.where(low, src[rows, chunk * LANES:(chunk + 1) * LANES], 0.0)

        def kv_first(c):
            return c if kh == 0 else jnp.concatenate([c[HEAD_DIM:], c[:HEAD_DIM]], axis=0)

        lhs = jnp.concatenate([q_piece(t, kh * HEADS_PER_KV + g)
                               for g in range(HEADS_PER_KV) for t in range(DEC_SEQ)], axis=0).astype(BF16)
        kcat = jnp.concatenate([kv_first(ckt_ref[sub * SUB_GROUP + b]) for b in range(SUB_GROUP)],
                               axis=1).astype(BF16)
        knew = jnp.concatenate([(k_t[t] if kh == 0 else k_sw[t])[rows] for t in range(DEC_SEQ)],
                               axis=0).astype(BF16)
        s_c = _dot(lhs, kcat) + bias_c
        s_n = _dot_nt(lhs, knew) + bias_n
        sink = jnp.zeros(head_of_row.shape, F32)
        for g in range(HEADS_PER_KV):
            sink = jnp.where(head_of_row == g, sinks_ref[kh * HEADS_PER_KV + g], sink)
        m = jnp.maximum(jnp.maximum(jnp.max(s_c, axis=-1, keepdims=True),
                                    jnp.max(s_n, axis=-1, keepdims=True)), sink)
        p_c = jnp.exp(s_c - m)
        p_n = jnp.exp(s_n - m)
        denom = (jnp.sum(p_c, axis=-1, keepdims=True) + jnp.sum(p_n, axis=-1, keepdims=True)
                 + jnp.exp(sink - m))
        return p_c.astype(BF16), p_n.astype(BF16), denom

    def values(sub, kh, p_c, p_n, denom):
        rows = slice(sub * SUB_GROUP, (sub + 1) * SUB_GROUP)

        def kv_twice(c):
            part = c[kh * HEAD_DIM:(kh + 1) * HEAD_DIM]
            return jnp.concatenate([part, part], axis=0)

        vcat = jnp.concatenate([kv_twice(cvt_ref[sub * SUB_GROUP + b]) for b in range(SUB_GROUP)],
                               axis=1).astype(BF16)
        vnew = jnp.concatenate([(jnp.where(low, v_t[t], v_sw[t]) if kh == 0 else
                                 jnp.where(low, v_sw[t], v_t[t]))[rows] for t in range(DEC_SEQ)],
                               axis=0).astype(BF16)
        o = (_dot_nt(p_c, vcat) + _dot(p_n, vnew)) / denom
        for t in range(DEC_SEQ):
            for pair in range(HEADS_PER_KV // 2):
                piece = lambda g: o[(g * DEC_SEQ + t) * SUB_GROUP:(g * DEC_SEQ + t + 1) * SUB_GROUP]
                c0 = (kh * HEADS_PER_KV // 2 + pair) * LANES
                attn_ref[t, rows, c0:c0 + LANES] = jnp.where(low, piece(2 * pair), piece(2 * pair + 1))

    units = [(sub, kh) for sub in range(SEQ_GROUP // SUB_GROUP) for kh in range(N_KV_HEADS)]
    pending = None
    for unit in units:
        softmaxed = scores(*unit)
        if pending is not None:
            values(*pending)
        pending = (*unit, *softmaxed)
    values(*pending)

    if DEC_SEQ * SEQ_GROUP < LANES:
        zeros = jnp.zeros((LANES - DEC_SEQ * SEQ_GROUP, KV_WIDTH), F32)
        kbt_s[DEC_SEQ * SEQ_GROUP:, :] = zeros
        vbt_s[DEC_SEQ * SEQ_GROUP:, :] = zeros
    for t in range(DEC_SEQ):
        kbt_s[pl.ds(t, SEQ_GROUP, stride=DEC_SEQ), :] = k_t[t]
        vbt_s[pl.ds(t, SEQ_GROUP, stride=DEC_SEQ), :] = v_t[t]
    knew_t = kbt_s[...].T
    vnew_t = vbt_s[...].T
    keep = lax.broadcasted_iota(jnp.int32, (1, WINDOW), 1) < WINDOW - DEC_SEQ
    for b in range(SEQ_GROUP):
        shift_new = WINDOW - DEC_SEQ - DEC_SEQ * b
        kt_out[b] = jnp.where(keep, pltpu.roll(ckt_ref[b], WINDOW - DEC_SEQ, axis=1),
                              pltpu.roll(knew_t, shift_new, axis=1))
        vt_out[b] = jnp.where(keep, pltpu.roll(cvt_ref[b], WINDOW - DEC_SEQ, axis=1),
                              pltpu.roll(vnew_t, shift_new, axis=1))


def _sample_ctx(us, qs, ks, vs, ckt, cvt, cu, sinks):
    groups = DEC_BATCH // SEQ_GROUP

    def token_rows(t, width):
        return pl.BlockSpec((SEQ_GROUP, width), lambda i: (t * groups + i, 0))

    def per_token(width):
        return [token_rows(t, width) for t in range(DEC_SEQ)]

    cache_spec = pl.BlockSpec((SEQ_GROUP, KV_WIDTH, WINDOW), lambda i: (i, 0, 0))
    pool_rows_spec = pl.BlockSpec((POOL_BUF, SEQ_GROUP, POOL_WIDTH), lambda i: (0, i, 0))
    by_token = lambda width: pl.BlockSpec((DEC_SEQ, SEQ_GROUP, width), lambda i: (0, i, 0))
    bias_c, bias_n = _sample_bias()
    return pl.pallas_call(
        _sample_ctx_kernel,
        out_shape=(jax.ShapeDtypeStruct((DEC_SEQ, DEC_BATCH, Q_WIDTH), F32),
                   jax.ShapeDtypeStruct((DEC_SEQ, DEC_BATCH, POOL_WIDTH), F32),
                   jax.ShapeDtypeStruct((DEC_BATCH, KV_WIDTH, WINDOW), F32),
                   jax.ShapeDtypeStruct((DEC_BATCH, KV_WIDTH, WINDOW), F32),
                   jax.ShapeDtypeStruct((POOL_BUF, DEC_BATCH, POOL_WIDTH), F32)),
        grid=(groups,),
        in_specs=[pl.BlockSpec(memory_space=pltpu.SMEM)]
                 + per_token(Q_WIDTH) + per_token(KV_WIDTH) + per_token(KV_WIDTH) + per_token(POOL_WIDTH)
                 + [cache_spec, cache_spec, pool_rows_spec, _resident(bias_c.shape), _resident(bias_n.shape)],
        out_specs=(by_token(Q_WIDTH), by_token(POOL_WIDTH), cache_spec, cache_spec, pool_rows_spec),
        scratch_shapes=[pltpu.VMEM((LANES, KV_WIDTH), F32), pltpu.VMEM((LANES, KV_WIDTH), F32)],
        compiler_params=pltpu.CompilerParams(dimension_semantics=("parallel",),
                                             vmem_limit_bytes=VMEM_LIMIT_BYTES),
        name="sample_ctx",
    )(sinks, *([qs] * DEC_SEQ), *([ks] * DEC_SEQ), *([vs] * DEC_SEQ), *([us] * DEC_SEQ),
      ckt, cvt, cu, jnp.asarray(bias_c), jnp.asarray(bias_n))


def _back_kernel(h_ref, pa_ref, pools_ref, attns_ref,
                 win_hbm, wgrp_hbm, scale_ref, wpo_hbm, wao_hbm, wout_hbm, g2_ref, b2_ref,
                 w1_hbm, w3_hbm, w2_hbm, g3_ref, b3_ref, yp_ref, ys_hbm, z2_c, z3_c, h_s,
                 wg_ref, wgrp_ref, wpo_ref, wao_ref, wout_ref, w1_ref, w3_ref, w2_ref,
                 buf_ff, buf_group, sem_ff, sem_group, sem_model, ys_s, sem_y):
    t = pl.program_id(0)

    def branch_outputs():
        is_sample = t >= PROMPT_TILES
        pool_in = jnp.where(is_sample, pools_ref[...].astype(BF16), pa_ref[:, 0:POOL_WIDTH])
        attn_o = jnp.where(is_sample, attns_ref[...].astype(BF16), pa_ref[:, POOL_WIDTH:POOL_WIDTH + Q_WIDTH])
        zs = [_dot(pool_in[:, g * POOL_GROUP:(g + 1) * POOL_GROUP],
                   wgrp_ref[g * POOL_GROUP:(g + 1) * POOL_GROUP, :])
              for g in range(len(POOL_WINDOWS))]
        pool_z = jnp.concatenate(zs, axis=1) * scale_ref[...]
        return _dot(pool_z.astype(BF16), wpo_ref[...]), _dot(attn_o, wao_ref[...])

    def gate_and_merge(h1b, a, b):
        gates = jax.nn.sigmoid(_dot(h1b, wg_ref[...]))
        return (gates[:, :D_MODEL] * a + gates[:, D_MODEL:] * b).astype(BF16)

    def project_out(h1, merged):
        z2_c[...] = ALPHA * h1 + _dot(merged, wout_ref[...])

    is_tile_step = jnp.logical_and(t >= 1, t <= ROW_TILES)

    @pl.when(is_tile_step)
    def _():
        h1 = h_ref[...]
        a, b = branch_outputs()
        y = _layer_norm(z3_c[...], g3_ref[...], b3_ref[...])
        yp_ref[...] = y
        h2 = _layer_norm(z2_c[...], g2_ref[...], b2_ref[...])
        h1b = jnp.where(t <= ROW_TILES, h1.astype(BF16), y.astype(BF16))
        merged = []
        side = {6: lambda: merged.append(gate_and_merge(h1b, a, b)), 16: lambda: project_out(h1, merged[0])}
        _swiglu_residual(h2, w1_ref, w3_ref, w2_ref, h_s, side, z3_c)

    def last_norm():
        ys_s[...] = _layer_norm(z3_c[...], g3_ref[...], b3_ref[...])
        scatter = [pltpu.make_async_copy(ys_s.at[pl.ds(t * DEC_BATCH, DEC_BATCH), :], ys_hbm.at[:, t, :], sem_y.at[t])
                   for t in range(DEC_SEQ)]
        for copy in scatter:
            copy.start()
        for copy in scatter:
            copy.wait()

    def stage_and_first_tile():
        ring_ff = _ring(buf_ff, sem_ff)
        ring_group = _ring(buf_group, sem_group)
        ring_model = _ring(z3_c, sem_model, rows=MODEL_RING_ROWS)
        mixer = _interleave(_row_chunks(wgrp_hbm, 0, wgrp_ref, ring_group)
                            + _row_chunks(wpo_hbm, 0, wpo_ref, ring_model)
                            + _row_chunks(wao_hbm, 0, wao_ref, ring_model)
                            + _row_chunks(wout_hbm, 0, wout_ref, ring_model),
                            _row_chunks(win_hbm, UQKV_WIDTH, wg_ref, ring_ff))
        swiglu = _interleave(_row_chunks(w1_hbm, 0, w1_ref, ring_ff) + _row_chunks(w3_hbm, 0, w3_ref, ring_ff),
                             _row_chunks(w2_hbm, 0, w2_ref, ring_model))
        stager = _WeightStager(mixer + swiglu)
        stager.run(len(mixer))
        h1 = h_ref[...]
        a, b = branch_outputs()
        stager.run(len(swiglu) // 3)
        merged = gate_and_merge(h1.astype(BF16), a, b)
        stager.run(len(swiglu) // 3)
        project_out(h1, merged)
        stager.run()
        z3_c[...] = jnp.zeros(z3_c.shape, z3_c.dtype)

    @pl.when(jnp.logical_not(is_tile_step))
    def _():
        pl.when(t == ROW_TILES + 1)(last_norm)
        pl.when(t == 0)(stage_and_first_tile)


def _back(h1, pool_attn, pool_s, attn_s, w_in, wgrp, scale, wpo, wao, wout, g2, b2, w1, w3, w2, g3, b3):
    lagged = pl.BlockSpec((ROW_TILE, D_MODEL), lambda t: (jnp.clip(t - 2, 0, PROMPT_TILES - 1), 0))
    hbm = pl.BlockSpec(memory_space=pl.ANY)
    whole = lambda shape: pl.BlockSpec(shape, lambda t: (0,) * len(shape))
    streamed_in = [pl.BlockSpec((ROW_TILE, D_MODEL), lambda t: (jnp.minimum(t, PROMPT_TILES), 0)),
                   pl.BlockSpec((ROW_TILE, POOL_WIDTH + Q_WIDTH), lambda t: (jnp.minimum(t, PROMPT_TILES - 1), 0)),
                   whole((SAMPLE_ROWS, POOL_WIDTH)), whole((SAMPLE_ROWS, Q_WIDTH)), whole((1, POOL_WIDTH)),
                   whole((1, D_MODEL)), whole((1, D_MODEL)), whole((1, D_MODEL)), whole((1, D_MODEL))]

    def pipelined(h_hbm, pa_hbm, pools_hbm, attns_hbm, win_hbm, wgrp_hbm, scale_hbm, wpo_hbm, wao_hbm, wout_hbm,
                  g2_hbm, b2_hbm, w1_hbm, w3_hbm, w2_hbm, g3_hbm, b3_hbm, yp_hbm, ys_hbm, *scratch):
        def step(h_ref, pa_ref, pools_ref, attns_ref, scale_ref, g2_ref, b2_ref, g3_ref, b3_ref, yp_ref):
            _back_kernel(h_ref, pa_ref, pools_ref, attns_ref, win_hbm, wgrp_hbm, scale_ref, wpo_hbm, wao_hbm,
                         wout_hbm, g2_ref, b2_ref, w1_hbm, w3_hbm, w2_hbm, g3_ref, b3_ref, yp_ref, ys_hbm, *scratch)

        pltpu.emit_pipeline(step, grid=(ROW_TILES + 2,), in_specs=streamed_in, out_specs=[lagged])(
            h_hbm, pa_hbm, pools_hbm, attns_hbm, scale_hbm, g2_hbm, b2_hbm, g3_hbm, b3_hbm, yp_hbm)

    return pl.pallas_call(
        pipelined,
        out_shape=(jax.ShapeDtypeStruct((PROMPT_ROWS, D_MODEL), F32),
                   jax.ShapeDtypeStruct((DEC_BATCH, DEC_SEQ, D_MODEL), F32)),
        in_specs=[hbm] * 17,
        out_specs=(hbm, hbm),
        scratch_shapes=[pltpu.VMEM((ROW_TILE, D_MODEL), F32), pltpu.VMEM((ROW_TILE, D_MODEL), F32),
                        pltpu.VMEM((ROW_TILE, D_FF), BF16),
                        pltpu.VMEM((D_MODEL, 2 * D_MODEL), BF16),
                        pltpu.VMEM((len(POOL_WINDOWS) * POOL_GROUP, POOL_GROUP), BF16),
                        pltpu.VMEM((POOL_WIDTH, D_MODEL), BF16), pltpu.VMEM((Q_WIDTH, D_MODEL), BF16),
                        pltpu.VMEM((D_MODEL, D_MODEL), BF16),
                        pltpu.VMEM((D_MODEL, D_FF), BF16), pltpu.VMEM((D_MODEL, D_FF), BF16),
                        pltpu.VMEM((D_FF, D_MODEL), BF16)]
                       + _ring_scratch(RING_FF, RING_GROUP)
                       + [pltpu.VMEM((SAMPLE_ROWS, D_MODEL), F32), pltpu.SemaphoreType.DMA((DEC_SEQ,))],
        compiler_params=pltpu.CompilerParams(vmem_limit_bytes=VMEM_LIMIT_BYTES),
        name="back",
    )(h1, pool_attn, pool_s, attn_s, w_in, wgrp, scale, wpo, wao, wout, g2, b2, w1, w3, w2, g3, b3)


def kernel(x_prompt, x_sample, cache_pool_u, cache_k_win, cache_v_win, w_in, pool_w_grp, pool_scale,
           attn_sinks, w_pool_out, w_attn_out, w_out, ffn1_w1, ffn1_w3, ffn1_w2, ffn2_w1, ffn2_w3,
           ffn2_w2, ln1_g, ln1_b, ln2_g, ln2_b, ln3_g, ln3_b):
    assert DEPTH == 1 and w_in.shape[0] == 1
    l = 0
    vec = lambda p: p[l].reshape(1, -1)
    sinks = attn_sinks[l]

    freq = jnp.tile(ROPE_THETA ** (-2.0 * jnp.arange(HEAD_DIM // 2, dtype=F32) / HEAD_DIM), 4).reshape(1, LANES)

    xp = x_prompt.reshape(PROMPT_ROWS, D_MODEL)
    (h1, pool_attn, us, qs, ks, vs, kt_last, vt_last, u_last) = _front(
        xp, x_sample, ffn1_w1, ffn1_w3, ffn1_w2, w_in, vec(ln1_g), vec(ln1_b), freq, sinks)

    to_t = lambda c: jnp.transpose(c[l], (0, 2, 3, 1)).reshape(DEC_BATCH, KV_WIDTH, WINDOW)
    cu = jnp.transpose(cache_pool_u[l], (1, 0, 2))
    attn_s, pool_s, kt_s, vt_s, pu_s = _sample_ctx(us, qs, ks, vs, to_t(cache_k_win), to_t(cache_v_win), cu, sinks)

    wgrp = pool_w_grp.reshape(DEPTH, len(POOL_WINDOWS) * POOL_GROUP, POOL_GROUP)
    yp, ys = _back(h1, pool_attn, pool_s.reshape(SAMPLE_ROWS, POOL_WIDTH), attn_s.reshape(SAMPLE_ROWS, Q_WIDTH),
                   w_in, wgrp, vec(pool_scale), w_pool_out, w_attn_out, w_out, vec(ln2_g), vec(ln2_b),
                   ffn2_w1, ffn2_w3, ffn2_w2, vec(ln3_g), vec(ln3_b))
    yp = yp.reshape(BATCH, SEQ, D_MODEL)

    from_t = lambda c, n: jnp.transpose(c.reshape(n, N_KV_HEADS, HEAD_DIM, WINDOW), (0, 3, 1, 2))[None]
    pool_u_prompt = u_last[None, :, POOL_PAD - POOL_BUF:]
    pool_u_sample = jnp.transpose(pu_s, (1, 0, 2))[None]
    return (yp, ys, pool_u_prompt, from_t(kt_last, BATCH), from_t(vt_last, BATCH),
            pool_u_sample, from_t(kt_s, DEC_BATCH), from_t(vt_s, DEC_BATCH))
```

```python
import jax
import jax.numpy as jnp
import numpy as np
from jax import lax
from jax.experimental import pallas as pl
from jax.experimental.pallas import tpu as pltpu

D_MODEL = 1024
BATCH = 8
SEQ = 2048
DEC_BATCH = 128
DEC_SEQ = 4
PAST_LEN = 8192
POOL_WINDOWS = (2, 4, 8, 16)
POOL_GROUP = 128
POOL_WIDTH = 512
POOL_BUF = 15
N_HEADS = 8
N_KV_HEADS = 2
HEADS_PER_KV = N_HEADS // N_KV_HEADS
HEAD_DIM = 64
Q_WIDTH = 512
KV_WIDTH = 128
WINDOW = 128
ROPE_THETA = 10000.0
D_FF = 2816
DEPTH = 1
ALPHA = (2.0 * DEPTH) ** 0.25
LN_EPS = 1e-5
NEG_INF = -1e30
UQKV_WIDTH = POOL_WIDTH + Q_WIDTH + 2 * KV_WIDTH

LANES = 128
KEY_SPAN = 2 * WINDOW
VMEM_LIMIT_BYTES = 61 * 1024 * 1024

ROW_TILE = 512
TILES_PER_SEQ = SEQ // ROW_TILE
PROMPT_ROWS = BATCH * SEQ
PROMPT_TILES = PROMPT_ROWS // ROW_TILE
SAMPLE_ROWS = DEC_BATCH * DEC_SEQ
ROW_TILES = PROMPT_TILES + 1
POOL_PAD = 16
SEQ_GROUP = 32
SUB_GROUP = 8
FF_CHUNK = 256
OUT_BLOCK = 256

BF16 = jnp.bfloat16
F32 = jnp.float32


def _dot(a, b):
    return jnp.dot(a, b, preferred_element_type=F32)


def _dot_nt(a, b):
    return lax.dot_general(a, b, (((1,), (1,)), ((), ())), preferred_element_type=F32)


def _layer_norm(y, g, b):
    mu = jnp.mean(y, axis=-1, keepdims=True)
    yc = y - mu
    var = jnp.mean(yc * yc, axis=-1, keepdims=True)
    return yc * lax.rsqrt(var + LN_EPS) * g + b


def _resident(shape):
    nd = len(shape)
    return pl.BlockSpec(shape, lambda *_: (0,) * nd, pipeline_mode=pl.Buffered(1))


class _WeightStager:
    def __init__(self, jobs):
        self.staged, self.priority, self.next_in_slot, first, users = [], [], {}, [], {}
        for src, ring, dst in jobs:
            mine = users.setdefault(id(ring), [])
            slot, sem = ring[len(mine) % len(ring)]
            if len(mine) < len(ring):
                first.append(len(self.staged))
            else:
                self.next_in_slot[mine[-len(ring)]] = len(self.staged)
            mine.append(len(self.staged))
            if slot.shape[1] > dst.shape[1]:
                slot = slot.at[:, pl.ds(0, dst.shape[1])]
            self.staged.append((pltpu.make_async_copy(src, slot, sem), slot, dst))
            self.priority.append(list(users).index(id(ring)) % 2)
        self.done = 0
        for i in first:
            self.staged[i][0].start(priority=self.priority[i])

    def run(self, count=None):
        end = len(self.staged) if count is None else min(self.done + count, len(self.staged))
        for i in range(self.done, end):
            copy, slot, dst = self.staged[i]
            copy.wait()
            dst[...] = slot[...].astype(BF16)
            if i in self.next_in_slot:
                self.staged[self.next_in_slot[i]][0].start(priority=self.priority[self.next_in_slot[i]])
        self.done = end


def _interleave(*job_lists):
    keyed = [((i + 0.5) / len(jobs), n, job) for n, jobs in enumerate(job_lists) for i, job in enumerate(jobs)]
    return [job for _, _, job in sorted(keyed, key=lambda entry: entry[:2])]


def _ring(buf, sems, rows=None):
    if rows is None:
        return [(buf.at[i], sems.at[i]) for i in range(buf.shape[0])]
    return [(buf.at[pl.ds(i * rows, rows), :], sems.at[i]) for i in range(buf.shape[0] // rows)]


def _row_chunks(w_hbm, col0, dst, ring):
    rows, cols = dst.shape
    step, width = ring[0][0].shape
    assert rows % step == 0 and width >= cols
    return [(w_hbm.at[0, pl.ds(r0, step), pl.ds(col0, cols)], ring, dst.at[pl.ds(r0, step), :])
            for r0 in range(0, rows, step)]


STAGE_IN_FLIGHT = 4
RING_FF = (STAGE_IN_FLIGHT, 64, D_FF)
MODEL_RING_ROWS = ROW_TILE // STAGE_IN_FLIGHT
RING_NARROW = (STAGE_IN_FLIGHT, 128, UQKV_WIDTH - D_MODEL)
RING_GROUP = (1, len(POOL_WINDOWS) * POOL_GROUP, POOL_GROUP)


def _ring_scratch(*rings):
    return ([pltpu.VMEM(ring, F32) for ring in rings]
            + [pltpu.SemaphoreType.DMA((ring[0],)) for ring in rings]
            + [pltpu.SemaphoreType.DMA((STAGE_IN_FLIGHT,))])


FF_SLOTS = 2 * (D_FF // FF_CHUNK)


def _swiglu_residual(x, w1_ref, w3_ref, w2_ref, h_s, side_work, out_ref):
    assert all(0 <= slot <= FF_SLOTS for slot in side_work)
    run = lambda slot: side_work.get(slot, lambda: None)()
    xb = x.astype(BF16)
    for j in range(D_FF // FF_CHUNK):
        cols = slice(j * FF_CHUNK, (j + 1) * FF_CHUNK)
        a = _dot(xb, w1_ref[:, cols])
        run(2 * j)
        b = _dot(xb, w3_ref[:, cols])
        h_s[:, cols] = ((a * jax.nn.sigmoid(a)) * b).astype(BF16)
        run(2 * j + 1)
    run(FF_SLOTS)
    h = h_s[...]
    for c0 in range(0, D_MODEL, OUT_BLOCK):
        cols = slice(c0, c0 + OUT_BLOCK)
        out_ref[:, cols] = ALPHA * x[:, cols] + 0.5 * _dot(h, w2_ref[:, cols])


def _rope(x, cos, sin_signed, first_half):
    fwd = pltpu.roll(x, LANES - HEAD_DIM // 2, axis=1)
    bwd = pltpu.roll(x, HEAD_DIM // 2, axis=1)
    return x * cos + jnp.where(first_half, fwd, bwd) * sin_signed


def _swap_halves(x):
    return pltpu.roll(x, HEAD_DIM, axis=1)


def _swap_halves_wide(x):
    return jnp.concatenate([_swap_halves(x[:, c:c + LANES]) for c in range(0, x.shape[1], LANES)], axis=1)


def _lane_split(x, x_sw, kh):
    low = lax.broadcasted_iota(jnp.int32, (1, LANES), 1) < HEAD_DIM
    lo, hi = (x, x_sw) if kh == 0 else (x_sw, x)
    return jnp.concatenate([jnp.where(low, lo, 0.0), jnp.where(low, 0.0, hi)], axis=0).astype(BF16)


def _sink_softmax(q_pairs, keys, kh, bias, sinks_ref):
    s = _dot_nt(q_pairs, _lane_split(keys, _swap_halves(keys), kh)) + bias
    second_pair = lax.broadcasted_iota(jnp.int32, (s.shape[0], 1), 0) >= WINDOW
    probs, denoms = [], []
    for c in range(2):
        sc = s[:, c * KEY_SPAN:(c + 1) * KEY_SPAN]
        head = kh * HEADS_PER_KV + c
        sink = jnp.where(second_pair, sinks_ref[head + 2], sinks_ref[head])
        m = jnp.maximum(jnp.max(sc, axis=-1, keepdims=True), sink)
        p = jnp.exp(sc - m)
        denoms.append(jnp.sum(p, axis=-1, keepdims=True) + jnp.exp(sink - m))
        probs.append(p.astype(BF16))
    return jnp.concatenate(probs, axis=1), denoms


def _weighted_values(probs, denoms, vals, kh):
    low = lax.broadcasted_iota(jnp.int32, (1, LANES), 1) < HEAD_DIM
    o = _dot(probs, _lane_split(vals, _swap_halves(vals), kh))
    return o / jnp.where(low, denoms[0], denoms[1])


def _prompt_context_work(tile, q_c, k_c, v_c, u_c, bias_ref, sinks_ref,
                         pool_ref, attn_ref, kt_ref, vt_ref, ulast_ref):
    seq_tile = (tile + TILES_PER_SEQ) % TILES_PER_SEQ
    first_tile = seq_tile == 0
    softmaxed = {}

    def scores(unit, blk, kh):
        r0 = blk * WINDOW
        bias = bias_ref[jnp.where(first_tile, 1, 0)] if blk == 0 else bias_ref[0]
        c0 = 2 * kh * LANES
        q_pairs = jnp.concatenate([q_c[r0:r0 + WINDOW, c0:c0 + LANES],
                                   q_c[r0:r0 + WINDOW, c0 + LANES:c0 + 2 * LANES]], axis=0)
        softmaxed[unit] = _sink_softmax(q_pairs, k_c[r0:r0 + KEY_SPAN, :], kh, bias, sinks_ref)

    def values(unit, blk, kh):
        r0 = blk * WINDOW
        c0 = 2 * kh * LANES
        o = _weighted_values(*softmaxed.pop(unit), v_c[r0:r0 + KEY_SPAN, :], kh)
        attn_ref[r0:r0 + WINDOW, c0:c0 + LANES] = o[:WINDOW].astype(BF16)
        attn_ref[r0:r0 + WINDOW, c0 + LANES:c0 + 2 * LANES] = o[WINDOW:].astype(BF16)

    def pool(groups):
        pos = seq_tile * ROW_TILE + lax.broadcasted_iota(jnp.int32, (ROW_TILE, 1), 0)
        for g in groups:
            w = POOL_WINDOWS[g]
            cols = slice(g * POOL_GROUP, (g + 1) * POOL_GROUP)
            rows = u_c[:, cols]
            acc, span = rows, 1
            while span < w:
                acc = acc + pltpu.roll(acc, span, axis=0)
                span *= 2
            cur = rows[POOL_PAD:]
            cnt = jnp.minimum(pos + 1, w).astype(F32)
            pool_ref[:, cols] = (acc[POOL_PAD:] / cnt - cur).astype(BF16)

    def sequence_state():
        kt_ref[0] = k_c[ROW_TILE:ROW_TILE + WINDOW, :].T
        vt_ref[0] = v_c[ROW_TILE:ROW_TILE + WINDOW, :].T
        ulast_ref[0] = u_c[ROW_TILE:ROW_TILE + POOL_PAD, :]

    work = {"pool_wide": lambda: pool((3,)), "pool_narrow": lambda: pool((0, 1, 2)),
            "sequence_state": sequence_state}
    for blk in range(ROW_TILE // WINDOW):
        for kh in range(N_KV_HEADS):
            unit = blk * N_KV_HEADS + kh
            work["scores", unit] = lambda unit=unit, blk=blk, kh=kh: scores(unit, blk, kh)
            work["values", unit] = lambda unit=unit, blk=blk, kh=kh: values(unit, blk, kh)
    return work


def _prompt_bias():
    r = np.arange(2 * WINDOW)[:, None] % WINDOW
    c = np.arange(2 * KEY_SPAN)[None, :] % KEY_SPAN
    valid = (r <= c) & (c <= r + WINDOW)
    first = valid & (c >= WINDOW)
    return np.where(np.stack([valid, first]), 0.0, NEG_INF).astype(np.float32)


def _front_kernel(sinks_ref, xp_ref, xs_hbm, w1_hbm, w3_hbm, w2_hbm, win_hbm, g_ref, b_ref, freq_ref, bias_ref,
                  h_ref, pa_ref, us_ref, qs_ref, ks_ref, vs_ref, kt_ref, vt_ref, ulast_ref,
                  z_c, q_c, k_c, v_c, u_c, h_s, rope_ref, w1_ref, w3_ref, w2_ref, wu_ref,
                  buf_ff, buf_narrow, sem_ff, sem_narrow, sem_model, xs_s, sem_x):
    r = pl.program_id(0) - 1
    prev = r - 1
    pool_ref = pa_ref.at[:, 0:POOL_WIDTH]
    attn_ref = pa_ref.at[:, POOL_WIDTH:POOL_WIDTH + Q_WIDTH]

    def stage_and_reset():
        gather = [pltpu.make_async_copy(xs_hbm.at[:, t, :], xs_s.at[pl.ds(t * DEC_BATCH, DEC_BATCH), :], sem_x.at[t])
                  for t in range(DEC_SEQ)]
        for copy in gather:
            copy.start()
        ring_ff = _ring(buf_ff, sem_ff)
        ring_narrow = _ring(buf_narrow, sem_narrow)
        ring_model = _ring(z_c, sem_model, rows=MODEL_RING_ROWS)
        stager = _WeightStager(_interleave(
            _row_chunks(w1_hbm, 0, w1_ref, ring_ff) + _row_chunks(w3_hbm, 0, w3_ref, ring_ff),
            _row_chunks(win_hbm, 0, wu_ref.at[:, 0:D_MODEL], ring_model) + _row_chunks(w2_hbm, 0, w2_ref, ring_model),
            _row_chunks(win_hbm, D_MODEL, wu_ref.at[:, D_MODEL:UQKV_WIDTH], ring_narrow)))
        freq = freq_ref[...]
        lane = lax.broadcasted_iota(jnp.int32, (1, LANES), 1)
        sign = jnp.where((lane & (HEAD_DIM // 2)) == 0, -1.0, 1.0)
        row = lax.broadcasted_iota(jnp.int32, (ROW_TILE, 1), 0)
        per_tile = -(-len(stager.staged) // (TILES_PER_SEQ + 1))
        for tile in range(TILES_PER_SEQ + 1):
            pos = tile * ROW_TILE + row if tile < TILES_PER_SEQ else PAST_LEN + row // DEC_BATCH
            ang = pos.astype(F32) * freq
            rows = slice(tile * ROW_TILE, (tile + 1) * ROW_TILE)
            rope_ref[rows, 0:LANES] = jnp.cos(ang)
            rope_ref[rows, LANES:2 * LANES] = jnp.sin(ang) * sign
            stager.run(per_tile)
        stager.run()
        for copy in gather:
            copy.wait()
        for ref in (z_c, q_c, k_c, v_c, u_c):
            ref[...] = jnp.zeros(ref.shape, ref.dtype)

    def norm_previous():
        h1 = _layer_norm(z_c[...], g_ref[...], b_ref[...])
        h_ref[...] = h1
        return h1.astype(BF16)

    def rope_tables():
        tile = jnp.clip(prev, 0, PROMPT_TILES)
        table = jnp.where(tile == PROMPT_TILES, TILES_PER_SEQ, tile % TILES_PER_SEQ)
        rows = pl.ds(pl.multiple_of(table * ROW_TILE, ROW_TILE), ROW_TILE)
        cos = rope_ref[rows, 0:LANES]
        lane = lax.broadcasted_iota(jnp.int32, cos.shape, 1)
        return cos, rope_ref[rows, LANES:2 * LANES], (lane & (HEAD_DIM // 2)) == 0

    def project_u(h1b):
        return _dot(h1b, wu_ref[:, 0:POOL_WIDTH])

    def project_q(h1b):
        z = _dot(h1b, wu_ref[:, POOL_WIDTH:POOL_WIDTH + Q_WIDTH])
        tables = rope_tables()
        return [(_rope(z[:, c:c + LANES], *tables) * (HEAD_DIM ** -0.5)).astype(BF16)
                for c in range(0, Q_WIDTH, LANES)]

    def project_kv(h1b):
        z = _dot(h1b, wu_ref[:, POOL_WIDTH + Q_WIDTH:UQKV_WIDTH])
        return _rope(z[:, :KV_WIDTH], *rope_tables()), z[:, KV_WIDTH:]

    def tile_step(x):
        normed = []

        def norm():
            normed.append(norm_previous())

        projected = []

        def project():
            projected.extend([project_u(normed[0]), project_q(normed[0]), *project_kv(normed[0])])

        def carry():
            u, q, k, v = projected
            starts_seq = (prev + TILES_PER_SEQ) % TILES_PER_SEQ == 0
            u_c[0:POOL_PAD, :] = jnp.where(starts_seq, 0.0, u_c[ROW_TILE:ROW_TILE + POOL_PAD, :])
            u_c[POOL_PAD:POOL_PAD + ROW_TILE, :] = u
            for c, qc in enumerate(q):
                q_c[:, c * LANES:(c + 1) * LANES] = qc
            k_c[0:WINDOW, :] = k_c[ROW_TILE:ROW_TILE + WINDOW, :]
            v_c[0:WINDOW, :] = v_c[ROW_TILE:ROW_TILE + WINDOW, :]
            k_c[WINDOW:WINDOW + ROW_TILE, :] = k
            v_c[WINDOW:WINDOW + ROW_TILE, :] = v

        side = {0: norm, FF_SLOTS - 3: project, FF_SLOTS - 1: carry}
        stages = context_stages()
        assert len(stages) < FF_SLOTS - 3
        side.update(enumerate(stages, start=1))
        _swiglu_residual(x, w1_ref, w3_ref, w2_ref, h_s, side, z_c)

    def context_stages():
        context = _prompt_context_work(r - 2, q_c, k_c, v_c, u_c, bias_ref, sinks_ref,
                                       pool_ref, attn_ref, kt_ref, vt_ref, ulast_ref)
        stages = [("scores", 0)]
        for unit in range(1, 8):
            stages += [("scores", unit), ("values", unit - 1)]
        stages.append(("values", 7))

        def pooling_and_state():
            context["pool_wide"]()
            context["pool_narrow"]()
            context["sequence_state"]()

        return [context[stage] for stage in stages] + [pooling_and_state]

    def drain():
        stages = context_stages()
        third = len(stages) // 3
        h1b = norm_previous()
        for work in stages[:third]:
            work()
        us_ref[...] = project_u(h1b)
        for work in stages[third:2 * third]:
            work()
        for c, qc in enumerate(project_q(h1b)):
            qs_ref[:, c * LANES:(c + 1) * LANES] = qc
        for work in stages[2 * third:]:
            work()
        ks_ref[...], vs_ref[...] = project_kv(h1b)

    is_tile_step = jnp.logical_and(r >= 0, r <= PROMPT_TILES)

    @pl.when(is_tile_step)
    def _():
        tile_step(jnp.where(r == PROMPT_TILES, xs_s[...], xp_ref[...]))

    @pl.when(jnp.logical_not(is_tile_step))
    def _():
        pl.when(r == ROW_TILES)(drain)
        pl.when(r < 0)(stage_and_reset)


def _front(xp, xs, w1, w3, w2, w_in, g, b, freq, sinks):
    rows = ROW_TILES * ROW_TILE

    def lagged(width):
        return pl.BlockSpec((ROW_TILE, width), lambda s: (jnp.clip(s - 3, 0, PROMPT_TILES - 1), 0))

    def seq_of_lagged(shape):
        return pl.BlockSpec(shape, lambda s: (jnp.clip(s - 3, 0, PROMPT_TILES - 1) // TILES_PER_SEQ, 0, 0))

    sample = lambda width: pl.BlockSpec((ROW_TILE, width), lambda r: (0, 0))
    hbm = pl.BlockSpec(memory_space=pl.ANY)
    return pl.pallas_call(
        _front_kernel,
        out_shape=(jax.ShapeDtypeStruct((rows, D_MODEL), F32),
                   jax.ShapeDtypeStruct((PROMPT_ROWS, POOL_WIDTH + Q_WIDTH), BF16),
                   jax.ShapeDtypeStruct((SAMPLE_ROWS, POOL_WIDTH), F32),
                   jax.ShapeDtypeStruct((SAMPLE_ROWS, Q_WIDTH), BF16),
                   jax.ShapeDtypeStruct((SAMPLE_ROWS, KV_WIDTH), F32),
                   jax.ShapeDtypeStruct((SAMPLE_ROWS, KV_WIDTH), F32),
                   jax.ShapeDtypeStruct((BATCH, KV_WIDTH, WINDOW), F32),
                   jax.ShapeDtypeStruct((BATCH, KV_WIDTH, WINDOW), F32),
                   jax.ShapeDtypeStruct((BATCH, POOL_PAD, POOL_WIDTH), F32)),
        grid=(ROW_TILES + 2,),
        in_specs=[pl.BlockSpec(memory_space=pltpu.SMEM),
                  pl.BlockSpec((ROW_TILE, D_MODEL), lambda s: (jnp.clip(s - 1, 0, PROMPT_TILES - 1), 0)),
                  hbm, hbm, hbm, hbm, hbm, _resident((1, D_MODEL)), _resident((1, D_MODEL)),
                  _resident((1, LANES)), _resident((2, 2 * WINDOW, 2 * KEY_SPAN))],
        out_specs=(pl.BlockSpec((ROW_TILE, D_MODEL), lambda s: (jnp.clip(s - 2, 0, PROMPT_TILES), 0)),
                   lagged(POOL_WIDTH + Q_WIDTH),
                   sample(POOL_WIDTH), sample(Q_WIDTH), sample(KV_WIDTH), sample(KV_WIDTH),
                   seq_of_lagged((1, KV_WIDTH, WINDOW)), seq_of_lagged((1, KV_WIDTH, WINDOW)),
                   seq_of_lagged((1, POOL_PAD, POOL_WIDTH))),
        scratch_shapes=[pltpu.VMEM((ROW_TILE, D_MODEL), F32),
                        pltpu.VMEM((ROW_TILE, Q_WIDTH), BF16),
                        pltpu.VMEM((WINDOW + ROW_TILE, KV_WIDTH), F32),
                        pltpu.VMEM((WINDOW + ROW_TILE, KV_WIDTH), F32),
                        pltpu.VMEM((POOL_PAD + ROW_TILE, POOL_WIDTH), F32),
                        pltpu.VMEM((ROW_TILE, D_FF), BF16),
                        pltpu.VMEM(((TILES_PER_SEQ + 1) * ROW_TILE, 2 * LANES), F32),
                        pltpu.VMEM((D_MODEL, D_FF), BF16), pltpu.VMEM((D_MODEL, D_FF), BF16),
                        pltpu.VMEM((D_FF, D_MODEL), BF16), pltpu.VMEM((D_MODEL, UQKV_WIDTH), BF16)]
                       + _ring_scratch(RING_FF, RING_NARROW)
                       + [pltpu.VMEM((SAMPLE_ROWS, D_MODEL), F32), pltpu.SemaphoreType.DMA((DEC_SEQ,))],
        compiler_params=pltpu.CompilerParams(dimension_semantics=("arbitrary",),
                                             vmem_limit_bytes=VMEM_LIMIT_BYTES),
        name="front",
    )(sinks, xp, xs, w1, w3, w2, w_in, g, b, freq, jnp.asarray(_prompt_bias()))


def _sample_bias():
    row = np.arange(HEADS_PER_KV * DEC_SEQ * SUB_GROUP)
    row_t = (row // SUB_GROUP) % DEC_SEQ
    row_b = row % SUB_GROUP
    col = np.arange(SUB_GROUP * WINDOW)
    ok_c = (col[None, :] // WINDOW == row_b[:, None]) & (col[None, :] % WINDOW >= row_t[:, None])
    new = np.arange(DEC_SEQ * SUB_GROUP)
    ok_n = (new[None, :] % SUB_GROUP == row_b[:, None]) & (new[None, :] // SUB_GROUP <= row_t[:, None])
    to_bias = lambda ok: np.where(ok, 0.0, NEG_INF).astype(np.float32)
    return to_bias(ok_c), to_bias(ok_n)


def _sample_ctx_kernel(sinks_ref, q0, q1, q2, q3, k0, k1, k2, k3, v0, v1, v2, v3, u0, u1, u2, u3,
                       ckt_ref, cvt_ref, cu_ref, bias_c_ref, bias_n_ref,
                       attn_ref, pool_ref, kt_out, vt_out, pu_out, kbt_s, vbt_s):
    q_t = [q[...].astype(F32) for q in (q0, q1, q2, q3)]
    k_t = [k[...] for k in (k0, k1, k2, k3)]
    v_t = [v[...] for v in (v0, v1, v2, v3)]

    rows_u = [cu_ref[i] for i in range(POOL_BUF)] + [u[...] for u in (u0, u1, u2, u3)]
    for t in range(DEC_SEQ):
        pooled = []
        for g, w in enumerate(POOL_WINDOWS):
            cols = slice(g * POOL_GROUP, (g + 1) * POOL_GROUP)
            cur = rows_u[POOL_BUF + t][:, cols]
            acc = cur
            for j in range(1, w):
                acc = acc + rows_u[POOL_BUF + t - j][:, cols]
            pooled.append(acc / float(w) - cur)
        pool_ref[t] = jnp.concatenate(pooled, axis=1)
    for i in range(POOL_BUF):
        pu_out[i] = rows_u[i + DEC_SEQ]

    bias_c = bias_c_ref[...]
    bias_n = bias_n_ref[...]
    head_of_row = lax.broadcasted_iota(jnp.int32, (bias_c.shape[0], 1), 0) // (DEC_SEQ * SUB_GROUP)
    low = lax.broadcasted_iota(jnp.int32, (1, LANES), 1) < HEAD_DIM
    q_sw = [_swap_halves_wide(q) for q in q_t]
    k_sw = [_swap_halves(k) for k in k_t]
    v_sw = [_swap_halves(v) for v in v_t]
    def scores(sub, kh):
        rows = slice(sub * SUB_GROUP, (sub + 1) * SUB_GROUP)

        def q_piece(t, head):
            src = q_t[t] if head % 2 == 0 else q_sw[t]
            chunk = head // 2
            return jnp.where(low, src[rows, chunk * LANES:(chunk + 1) * LANES], 0.0)

        def kv_first(c):
            return c if kh == 0 else jnp.concatenate([c[HEAD_DIM:], c[:HEAD_DIM]], axis=0)

        lhs = jnp.concatenate([q_piece(t, kh * HEADS_PER_KV + g)
                               for g in range(HEADS_PER_KV) for t in range(DEC_SEQ)], axis=0).astype(BF16)
        kcat = jnp.concatenate([kv_first(ckt_ref[sub * SUB_GROUP + b]) for b in range(SUB_GROUP)],
                               axis=1).astype(BF16)
        knew = jnp.concatenate([(k_t[t] if kh == 0 else k_sw[t])[rows] for t in range(DEC_SEQ)],
                               axis=0).astype(BF16)
        s_c = _dot(lhs, kcat) + bias_c
        s_n = _dot_nt(lhs, knew) + bias_n
        sink = jnp.zeros(head_of_row.shape, F32)
        for g in range(HEADS_PER_KV):
            sink = jnp.where(head_of_row == g, sinks_ref[kh * HEADS_PER_KV + g], sink)
        m = jnp.maximum(jnp.maximum(jnp.max(s_c, axis=-1, keepdims=True),
                                    jnp.max(s_n, axis=-1, keepdims=True)), sink)
        p_c = jnp.exp(s_c - m)
        p_n = jnp.exp(s_n - m)
        denom = (jnp.sum(p_c, axis=-1, keepdims=True) + jnp.sum(p_n, axis=-1, keepdims=True)
                 + jnp.exp(sink - m))
        return p_c.astype(BF16), p_n.astype(BF16), denom

    def values(sub, kh, p_c, p_n, denom):
        rows = slice(sub * SUB_GROUP, (sub + 1) * SUB_GROUP)

        def kv_twice(c):
            part = c[kh * HEAD_DIM:(kh + 1) * HEAD_DIM]
            return jnp.concatenate([part, part], axis=0)

        vcat = jnp.concatenate([kv_twice(cvt_ref[sub * SUB_GROUP + b]) for b in range(SUB_GROUP)],
                               axis=1).astype(BF16)
        vnew = jnp.concatenate([(jnp.where(low, v_t[t], v_sw[t]) if kh == 0 else
                                 jnp.where(low, v_sw[t], v_t[t]))[rows] for t in range(DEC_SEQ)],
                               axis=0).astype(BF16)
        o = (_dot_nt(p_c, vcat) + _dot(p_n, vnew)) / denom
        for t in range(DEC_SEQ):
            for pair in range(HEADS_PER_KV // 2):
                piece = lambda g: o[(g * DEC_SEQ + t) * SUB_GROUP:(g * DEC_SEQ + t + 1) * SUB_GROUP]
                c0 = (kh * HEADS_PER_KV // 2 + pair) * LANES
                attn_ref[t, rows, c0:c0 + LANES] = jnp.where(low, piece(2 * pair), piece(2 * pair + 1))

    units = [(sub, kh) for sub in range(SEQ_GROUP // SUB_GROUP) for kh in range(N_KV_HEADS)]
    pending = None
    for unit in units:
        softmaxed = scores(*unit)
        if pending is not None:
            values(*pending)
        pending = (*unit, *softmaxed)
    values(*pending)

    if DEC_SEQ * SEQ_GROUP < LANES:
        zeros = jnp.zeros((LANES - DEC_SEQ * SEQ_GROUP, KV_WIDTH), F32)
        kbt_s[DEC_SEQ * SEQ_GROUP:, :] = zeros
        vbt_s[DEC_SEQ * SEQ_GROUP:, :] = zeros
    for t in range(DEC_SEQ):
        kbt_s[pl.ds(t, SEQ_GROUP, stride=DEC_SEQ), :] = k_t[t]
        vbt_s[pl.ds(t, SEQ_GROUP, stride=DEC_SEQ), :] = v_t[t]
    knew_t = kbt_s[...].T
    vnew_t = vbt_s[...].T
    keep = lax.broadcasted_iota(jnp.int32, (1, WINDOW), 1) < WINDOW - DEC_SEQ
    for b in range(SEQ_GROUP):
        shift_new = WINDOW - DEC_SEQ - DEC_SEQ * b
        kt_out[b] = jnp.where(keep, pltpu.roll(ckt_ref[b], WINDOW - DEC_SEQ, axis=1),
                              pltpu.roll(knew_t, shift_new, axis=1))
        vt_out[b] = jnp.where(keep, pltpu.roll(cvt_ref[b], WINDOW - DEC_SEQ, axis=1),
                              pltpu.roll(vnew_t, shift_new, axis=1))


def _sample_ctx(us, qs, ks, vs, ckt, cvt, cu, sinks):
    groups = DEC_BATCH // SEQ_GROUP

    def token_rows(t, width):
        return pl.BlockSpec((SEQ_GROUP, width), lambda i: (t * groups + i, 0))

    def per_token(width):
        return [token_rows(t, width) for t in range(DEC_SEQ)]

    cache_spec = pl.BlockSpec((SEQ_GROUP, KV_WIDTH, WINDOW), lambda i: (i, 0, 0))
    pool_rows_spec = pl.BlockSpec((POOL_BUF, SEQ_GROUP, POOL_WIDTH), lambda i: (0, i, 0))
    by_token = lambda width: pl.BlockSpec((DEC_SEQ, SEQ_GROUP, width), lambda i: (0, i, 0))
    bias_c, bias_n = _sample_bias()
    return pl.pallas_call(
        _sample_ctx_kernel,
        out_shape=(jax.ShapeDtypeStruct((DEC_SEQ, DEC_BATCH, Q_WIDTH), F32),
                   jax.ShapeDtypeStruct((DEC_SEQ, DEC_BATCH, POOL_WIDTH), F32),
                   jax.ShapeDtypeStruct((DEC_BATCH, KV_WIDTH, WINDOW), F32),
                   jax.ShapeDtypeStruct((DEC_BATCH, KV_WIDTH, WINDOW), F32),
                   jax.ShapeDtypeStruct((POOL_BUF, DEC_BATCH, POOL_WIDTH), F32)),
        grid=(groups,),
        in_specs=[pl.BlockSpec(memory_space=pltpu.SMEM)]
                 + per_token(Q_WIDTH) + per_token(KV_WIDTH) + per_token(KV_WIDTH) + per_token(POOL_WIDTH)
                 + [cache_spec, cache_spec, pool_rows_spec, _resident(bias_c.shape), _resident(bias_n.shape)],
        out_specs=(by_token(Q_WIDTH), by_token(POOL_WIDTH), cache_spec, cache_spec, pool_rows_spec),
        scratch_shapes=[pltpu.VMEM((LANES, KV_WIDTH), F32), pltpu.VMEM((LANES, KV_WIDTH), F32)],
        compiler_params=pltpu.CompilerParams(dimension_semantics=("parallel",),
                                             vmem_limit_bytes=VMEM_LIMIT_BYTES),
        name="sample_ctx",
    )(sinks, *([qs] * DEC_SEQ), *([ks] * DEC_SEQ), *([vs] * DEC_SEQ), *([us] * DEC_SEQ),
      ckt, cvt, cu, jnp.asarray(bias_c), jnp.asarray(bias_n))


def _back_kernel(h_ref, pa_ref, pools_ref, attns_ref,
                 win_hbm, wgrp_hbm, scale_ref, wpo_hbm, wao_hbm, wout_hbm, g2_ref, b2_ref,
                 w1_hbm, w3_hbm, w2_hbm, g3_ref, b3_ref, yp_ref, ys_hbm, z2_c, z3_c, h_s,
                 wg_ref, wgrp_ref, wpo_ref, wao_ref, wout_ref, w1_ref, w3_ref, w2_ref,
                 buf_ff, buf_group, sem_ff, sem_group, sem_model, ys_s, sem_y):
    t = pl.program_id(0)

    def branch_outputs():
        is_sample = t >= PROMPT_TILES
        pool_in = jnp.where(is_sample, pools_ref[...].astype(BF16), pa_ref[:, 0:POOL_WIDTH])
        attn_o = jnp.where(is_sample, attns_ref[...].astype(BF16), pa_ref[:, POOL_WIDTH:POOL_WIDTH + Q_WIDTH])
        zs = [_dot(pool_in[:, g * POOL_GROUP:(g + 1) * POOL_GROUP],
                   wgrp_ref[g * POOL_GROUP:(g + 1) * POOL_GROUP, :])
              for g in range(len(POOL_WINDOWS))]
        pool_z = jnp.concatenate(zs, axis=1) * scale_ref[...]
        return _dot(pool_z.astype(BF16), wpo_ref[...]), _dot(attn_o, wao_ref[...])

    def gate_and_merge(h1b, a, b):
        gates = jax.nn.sigmoid(_dot(h1b, wg_ref[...]))
        return (gates[:, :D_MODEL] * a + gates[:, D_MODEL:] * b).astype(BF16)

    def project_out(h1, merged):
        z2_c[...] = ALPHA * h1 + _dot(merged, wout_ref[...])

    is_tile_step = jnp.logical_and(t >= 1, t <= ROW_TILES)

    @pl.when(is_tile_step)
    def _():
        h1 = h_ref[...]
        a, b = branch_outputs()
        y = _layer_norm(z3_c[...], g3_ref[...], b3_ref[...])
        yp_ref[...] = y
        h2 = _layer_norm(z2_c[...], g2_ref[...], b2_ref[...])
        h1b = jnp.where(t <= ROW_TILES, h1.astype(BF16), y.astype(BF16))
        merged = []
        side = {6: lambda: merged.append(gate_and_merge(h1b, a, b)), 16: lambda: project_out(h1, merged[0])}
        _swiglu_residual(h2, w1_ref, w3_ref, w2_ref, h_s, side, z3_c)

    def last_norm():
        ys_s[...] = _layer_norm(z3_c[...], g3_ref[...], b3_ref[...])
        scatter = [pltpu.make_async_copy(ys_s.at[pl.ds(t * DEC_BATCH, DEC_BATCH), :], ys_hbm.at[:, t, :], sem_y.at[t])
                   for t in range(DEC_SEQ)]
        for copy in scatter:
            copy.start()
        for copy in scatter:
            copy.wait()

    def stage_and_first_tile():
        ring_ff = _ring(buf_ff, sem_ff)
        ring_group = _ring(buf_group, sem_group)
        ring_model = _ring(z3_c, sem_model, rows=MODEL_RING_ROWS)
        mixer = _interleave(_row_chunks(wgrp_hbm, 0, wgrp_ref, ring_group)
                            + _row_chunks(wpo_hbm, 0, wpo_ref, ring_model)
                            + _row_chunks(wao_hbm, 0, wao_ref, ring_model)
                            + _row_chunks(wout_hbm, 0, wout_ref, ring_model),
                            _row_chunks(win_hbm, UQKV_WIDTH, wg_ref, ring_ff))
        swiglu = _interleave(_row_chunks(w1_hbm, 0, w1_ref, ring_ff) + _row_chunks(w3_hbm, 0, w3_ref, ring_ff),
                             _row_chunks(w2_hbm, 0, w2_ref, ring_model))
        stager = _WeightStager(mixer + swiglu)
        stager.run(len(mixer))
        h1 = h_ref[...]
        a, b = branch_outputs()
        stager.run(len(swiglu) // 3)
        merged = gate_and_merge(h1.astype(BF16), a, b)
        stager.run(len(swiglu) // 3)
        project_out(h1, merged)
        stager.run()
        z3_c[...] = jnp.zeros(z3_c.shape, z3_c.dtype)

    @pl.when(jnp.logical_not(is_tile_step))
    def _():
        pl.when(t == ROW_TILES + 1)(last_norm)
        pl.when(t == 0)(stage_and_first_tile)


def _back(h1, pool_attn, pool_s, attn_s, w_in, wgrp, scale, wpo, wao, wout, g2, b2, w1, w3, w2, g3, b3):
    lagged = pl.BlockSpec((ROW_TILE, D_MODEL), lambda t: (jnp.clip(t - 2, 0, PROMPT_TILES - 1), 0))
    hbm = pl.BlockSpec(memory_space=pl.ANY)
    return pl.pallas_call(
        _back_kernel,
        out_shape=(jax.ShapeDtypeStruct((PROMPT_ROWS, D_MODEL), F32),
                   jax.ShapeDtypeStruct((DEC_BATCH, DEC_SEQ, D_MODEL), F32)),
        grid=(ROW_TILES + 2,),
        in_specs=[pl.BlockSpec((ROW_TILE, D_MODEL), lambda t: (jnp.minimum(t, PROMPT_TILES), 0)),
                  pl.BlockSpec((ROW_TILE, POOL_WIDTH + Q_WIDTH), lambda t: (jnp.minimum(t, PROMPT_TILES - 1), 0)),
                  _resident((SAMPLE_ROWS, POOL_WIDTH)), _resident((SAMPLE_ROWS, Q_WIDTH)),
                  hbm, hbm, _resident((1, POOL_WIDTH)), hbm, hbm, hbm,
                  _resident((1, D_MODEL)), _resident((1, D_MODEL)),
                  hbm, hbm, hbm, _resident((1, D_MODEL)), _resident((1, D_MODEL))],
        out_specs=(lagged, hbm),
        scratch_shapes=[pltpu.VMEM((ROW_TILE, D_MODEL), F32), pltpu.VMEM((ROW_TILE, D_MODEL), F32),
                        pltpu.VMEM((ROW_TILE, D_FF), BF16),
                        pltpu.VMEM((D_MODEL, 2 * D_MODEL), BF16),
                        pltpu.VMEM((len(POOL_WINDOWS) * POOL_GROUP, POOL_GROUP), BF16),
                        pltpu.VMEM((POOL_WIDTH, D_MODEL), BF16), pltpu.VMEM((Q_WIDTH, D_MODEL), BF16),
                        pltpu.VMEM((D_MODEL, D_MODEL), BF16),
                        pltpu.VMEM((D_MODEL, D_FF), BF16), pltpu.VMEM((D_MODEL, D_FF), BF16),
                        pltpu.VMEM((D_FF, D_MODEL), BF16)]
                       + _ring_scratch(RING_FF, RING_GROUP)
                       + [pltpu.VMEM((SAMPLE_ROWS, D_MODEL), F32), pltpu.SemaphoreType.DMA((DEC_SEQ,))],
        compiler_params=pltpu.CompilerParams(dimension_semantics=("arbitrary",),
                                             vmem_limit_bytes=VMEM_LIMIT_BYTES),
        name="back",
    )(h1, pool_attn, pool_s, attn_s, w_in, wgrp, scale, wpo, wao, wout, g2, b2, w1, w3, w2, g3, b3)


def kernel(x_prompt, x_sample, cache_pool_u, cache_k_win, cache_v_win, w_in, pool_w_grp, pool_scale,
           attn_sinks, w_pool_out, w_attn_out, w_out, ffn1_w1, ffn1_w3, ffn1_w2, ffn2_w1, ffn2_w3,
           ffn2_w2, ln1_g, ln1_b, ln2_g, ln2_b, ln3_g, ln3_b):
    assert DEPTH == 1 and w_in.shape[0] == 1
    l = 0
    vec = lambda p: p[l].reshape(1, -1)
    sinks = attn_sinks[l]

    freq = jnp.tile(ROPE_THETA ** (-2.0 * jnp.arange(HEAD_DIM // 2, dtype=F32) / HEAD_DIM), 4).reshape(1, LANES)

    xp = x_prompt.reshape(PROMPT_ROWS, D_MODEL)
    (h1, pool_attn, us, qs, ks, vs, kt_last, vt_last, u_last) = _front(
        xp, x_sample, ffn1_w1, ffn1_w3, ffn1_w2, w_in, vec(ln1_g), vec(ln1_b), freq, sinks)

    to_t = lambda c: jnp.transpose(c[l], (0, 2, 3, 1)).reshape(DEC_BATCH, KV_WIDTH, WINDOW)
    cu = jnp.transpose(cache_pool_u[l], (1, 0, 2))
    attn_s, pool_s, kt_s, vt_s, pu_s = _sample_ctx(us, qs, ks, vs, to_t(cache_k_win), to_t(cache_v_win), cu, sinks)

    wgrp = pool_w_grp.reshape(DEPTH, len(POOL_WINDOWS) * POOL_GROUP, POOL_GROUP)
    yp, ys = _back(h1, pool_attn, pool_s.reshape(SAMPLE_ROWS, POOL_WIDTH), attn_s.reshape(SAMPLE_ROWS, Q_WIDTH),
                   w_in, wgrp, vec(pool_scale), w_pool_out, w_attn_out, w_out, vec(ln2_g), vec(ln2_b),
                   ffn2_w1, ffn2_w3, ffn2_w2, vec(ln3_g), vec(ln3_b))
    yp = yp.reshape(BATCH, SEQ, D_MODEL)

    from_t = lambda c, n: jnp.transpose(c.reshape(n, N_KV_HEADS, HEAD_DIM, WINDOW), (0, 3, 1, 2))[None]
    pool_u_prompt = u_last[None, :, POOL_PAD - POOL_BUF:]
    pool_u_sample = jnp.transpose(pu_s, (1, 0, 2))[None]
    return (yp, ys, pool_u_prompt, from_t(kt_last, BATCH), from_t(vt_last, BATCH),
            pool_u_sample, from_t(kt_s, DEC_BATCH), from_t(vt_s, DEC_BATCH))
```
